```python
import jax, jax.numpy as jnp
from jax import lax
import numpy as np

D_MODEL = 2048
BATCH = 1
SEQ = 8192
DEPTH = 2
DEC_BATCH = 32
DEC_SEQ = 1
PAST_LEN = 8192
PAGE_SIZE = 128

N_BRANCH = 4
BR_W = 1024
HG_H = 8
HG_DK = 128
HG_DV = BR_W // HG_H
HG_W = HG_H * HG_DV
HG_CHUNK = 64
F_MIN = 1e-30
CV_W = BR_W
CV_K = 3
NS_H = 8
NS_KV = 2
NS_G = NS_H // NS_KV
NS_DH = BR_W // NS_H
NS_W = NS_H * NS_DH
NS_KVW = NS_KV * NS_DH
CMP_LEN = 32
CMP_STRIDE = 16
SEL_BLOCK = 64
SEL_TOPN = 16
WINDOW = 512
Q_BLOCK = 128
FORCE_BONUS = 100.0
ROPE_THETA = 500000.0
ROT_DIM = NS_DH // 4
MEM_LEN = 256
MEM_H = 4
MEM_DH = BR_W // MEM_H
MEM_W = MEM_H * MEM_DH
NORM_EPS = 1e-6
NEG_INF = -1e30

IN_SPLITS = (
    ('hg_q', HG_H * HG_DK), ('hg_f', HG_H * HG_DK), ('hg_i', HG_W), ('hg_g', HG_W),
    ('cv_u', CV_W), ('cv_b', CV_W), ('cv_c', CV_W), ('cv_g', CV_W),
    ('ns_q', NS_W), ('ns_kc', NS_KVW), ('ns_vc', NS_KVW), ('ns_ks', NS_KVW), ('ns_vs', NS_KVW),
    ('ns_kw', NS_KVW), ('ns_vw', NS_KVW), ('ns_gate', 3 * NS_H), ('ns_g', NS_W),
    ('mm_q', MEM_W), ('mm_g', MEM_W),
    ('merge', N_BRANCH * D_MODEL),
)
IN_WIDTH = 2 * HG_H * HG_DK + 2 * HG_W + 4 * CV_W + 2 * NS_W + 6 * NS_KVW + 3 * NS_H + 2 * MEM_W + N_BRANCH * D_MODEL

kernel_name = 'hybrid_hgrn2_conv_nsa_mem_step'


def rmsnorm(x, g):
    xf = x.astype(jnp.float32)
    y = xf * lax.rsqrt(jnp.mean(xf * xf, axis=-1, keepdims=True) + NORM_EPS)
    return (y * g.astype(jnp.float32)).astype(x.dtype)


def split_columns(z):
    parts = {}
    start = 0
    for name, width in IN_SPLITS:
        parts[name] = z[..., start:start + width]
        start += width
    return parts


def rope_partial(x, pos):
    half = ROT_DIM // 2
    inv = ROPE_THETA ** (-2.0 * jnp.arange(half, dtype=jnp.float32) / ROT_DIM)
    ang = pos.astype(jnp.float32)[:, None] * inv[None, :]
    cos = jnp.cos(ang)[None, :, None, :]
    sin = jnp.sin(ang)[None, :, None, :]
    xr = x[..., :ROT_DIM].astype(jnp.float32)
    x1, x2 = xr[..., :half], xr[..., half:]
    rot = jnp.concatenate([x1 * cos - x2 * sin, x2 * cos + x1 * sin], axis=-1).astype(x.dtype)
    return jnp.concatenate([rot, x[..., ROT_DIM:]], axis=-1)


def masked_softmax(logits, mask):
    return jax.nn.softmax(jnp.where(mask, logits, NEG_INF), axis=-1)


def hgrn2_chunked(q, k, v, log_f, s0):
    B, T, H, DK = q.shape
    DV = v.shape[-1]
    C = min(HG_CHUNK, T)
    n = -(-T // C)
    pad = n * C - T

    def prep(a):
        a = jnp.pad(a.astype(jnp.float32), ((0, 0), (0, pad), (0, 0), (0, 0)))
        return a.reshape(B, n, C, H, a.shape[-1]).transpose(1, 0, 3, 2, 4)

    causal = jnp.tril(jnp.ones((C, C), dtype=bool))[None, None, :, :, None]

    def step(S, inp):
        qi, ki, vi, fi = inp
        b = jnp.cumsum(fi, axis=2)
        inter = jnp.einsum('bhtd,bhde->bhte', qi * jnp.exp(b), S)
        diff = b[:, :, :, None, :] - b[:, :, None, :, :]
        decay = jnp.where(causal, jnp.exp(jnp.where(causal, diff, 0.0)), 0.0)
        att = jnp.einsum('bhtd,bhsd,bhtsd->bhts', qi, ki, decay)
        intra = jnp.einsum('bhts,bhse->bhte', att, vi)
        b_last = b[:, :, -1:, :]
        S = jnp.exp(b_last[:, :, 0, :, None]) * S + jnp.einsum('bhsd,bhse->bhde', ki * jnp.exp(b_last - b), vi)
        return S, inter + intra

    S, o = lax.scan(step, s0.astype(jnp.float32), (prep(q), prep(k), prep(v), prep(log_f)))
    o = o.transpose(1, 0, 3, 2, 4).reshape(B, n * C, H, DV)[:, :T]
    return o, S


def hgrn2_branch(c, lb, norm_g, s0):
    B, T = c['hg_q'].shape[:2]
    q = jax.nn.silu(c['hg_q']).reshape(B, T, HG_H, HG_DK)
    z = c['hg_f'].astype(jnp.float32).reshape(B, T, HG_H, HG_DK)
    lbh = lb.reshape(HG_H, HG_DK)
    f = lbh + (1.0 - lbh) * jax.nn.sigmoid(z)
    log_f = jnp.log(jnp.maximum(f, F_MIN))
    k = (1.0 - lbh) * jax.nn.sigmoid(-z)
    v = c['hg_i'].reshape(B, T, HG_H, HG_DV)
    o, s = hgrn2_chunked(q, k, v, log_f, s0)
    o = rmsnorm(o, norm_g.reshape(HG_H, HG_DV)).reshape(B, T, HG_W)
    return (o * jax.nn.silu(c['hg_g'].astype(jnp.float32))).astype(c['hg_g'].dtype), s


def conv_branch(c, w, prev):
    v = c['cv_c'] * c['cv_u']
    vp = jnp.concatenate([prev.astype(v.dtype), v], axis=1)
    y = lax.conv_general_dilated(vp, w[:, None, :].astype(v.dtype), window_strides=(1,), padding='VALID',
                                 dimension_numbers=('NWC', 'WIO', 'NWC'), feature_group_count=CV_W)
    return c['cv_b'] * y * jax.nn.silu(c['cv_g']), vp[:, -(CV_K - 1):]


def memory_kv(mem, g, w):
    B = mem.shape[0]
    return (rmsnorm(mem, g) @ w).reshape(B, MEM_LEN, 2, MEM_H, MEM_DH)


def memory_branch(c, kv):
    B, T = c['mm_q'].shape[:2]
    q = c['mm_q'].reshape(B, T, MEM_H, MEM_DH)
    s = jnp.einsum('bthd,bmhd->bhtm', q, kv[:, :, 0].astype(q.dtype), preferred_element_type=jnp.float32) * (MEM_DH ** -0.5)
    p = jax.nn.softmax(s, axis=-1).astype(q.dtype)
    o = jnp.einsum('bhtm,bmhd->bthd', p, kv[:, :, 1].astype(q.dtype)).reshape(B, T, MEM_W)
    return o * jax.nn.silu(c['mm_g'])


def compress_blocks(raw, pe, w1, w2):
    B, L = raw.shape[:2]
    nch = L // CMP_STRIDE
    ch = raw[:, :nch * CMP_STRIDE].reshape(B, nch, CMP_STRIDE, NS_KV, NS_DH)
    w1r = w1.reshape(CMP_LEN // CMP_STRIDE, CMP_STRIDE, NS_DH, NS_DH)
    part = jnp.einsum('bnsgd,jsde->jbnge', ch, w1r)
    pe_term = jnp.einsum('sd,sde->e', pe, w1.reshape(CMP_LEN, NS_DH, NS_DH))
    hid = jax.nn.silu(part[0, :, :-1] + part[1, :, 1:] + pe_term)
    return hid @ w2


def nsa_projections(c, pos):
    B, T = c['ns_q'].shape[:2]
    heads = lambda a, n: a.reshape(B, T, n, NS_DH)
    q = heads(c['ns_q'], NS_H)
    q_rot = rope_partial(q, pos)
    rows = jnp.stack([heads(c['ns_kc'], NS_KV), heads(c['ns_vc'], NS_KV),
                      rope_partial(heads(c['ns_ks'], NS_KV), pos), heads(c['ns_vs'], NS_KV)], axis=2)
    win = jnp.stack([rope_partial(heads(c['ns_kw'], NS_KV), pos), heads(c['ns_vw'], NS_KV)], axis=2)
    gates = jax.nn.sigmoid(c['ns_gate'].astype(jnp.float32)).reshape(B, T, NS_H, 3)
    return q, q_rot, rows, win, gates


def nsa_context(rows, pe, w1, w2):
    kcmp = compress_blocks(rows[:, :, 0], pe[0], w1[0], w2[0])
    vcmp = compress_blocks(rows[:, :, 1], pe[1], w1[1], w2[1])
    pad = (-rows.shape[1]) % SEL_BLOCK
    sel = jnp.pad(rows[:, :, 2:4], ((0, 0), (0, pad), (0, 0), (0, 0), (0, 0))).transpose(2, 0, 3, 1, 4)
    return kcmp, vcmp, sel[0], sel[1]


def nsa_attend(q, q_rot, qpos, kcmp, vcmp, sel_k, sel_v, kw, vw, kwpos, gates):
    B, Tq = q.shape[:2]
    f32 = jnp.float32
    scale = NS_DH ** -0.5
    qg = q.reshape(B, Tq, NS_KV, NS_G, NS_DH)
    qr = q_rot.reshape(B, Tq, NS_KV, NS_G, NS_DH)
    n_cmp = kcmp.shape[1]
    cmp_start = jnp.arange(n_cmp, dtype=jnp.int32) * CMP_STRIDE
    cmask = (cmp_start + CMP_LEN - 1)[None, :] <= qpos[:, None]
    s = jnp.einsum('btgrd,bkgd->btgrk', qg, kcmp, preferred_element_type=f32) * scale
    any_c = jnp.any(cmask, axis=-1).astype(f32)[None, :, None, None, None]
    p_cmp = masked_softmax(s, cmask[None, :, None, None, :]) * any_c
    o_cmp = jnp.einsum('btgrk,bkgd->btgrd', p_cmp.astype(vcmp.dtype), vcmp)
    n_sel = sel_k.shape[2] // SEL_BLOCK
    sel_start = jnp.arange(n_sel, dtype=jnp.int32) * SEL_BLOCK
    overlap = (jnp.minimum(cmp_start[:, None] + CMP_LEN, sel_start[None, :] + SEL_BLOCK)
               - jnp.maximum(cmp_start[:, None], sel_start[None, :]))
    cmp_to_sel = jnp.clip(overlap, 0, None).astype(f32) / CMP_LEN
    imp = jnp.einsum('btgk,kj->btgj', p_cmp.sum(axis=3), cmp_to_sel)
    cur = (qpos // SEL_BLOCK)[:, None]
    j = jnp.arange(n_sel, dtype=jnp.int32)[None, :]
    forced = (j == 0) | (j == cur) | (j == cur - 1)
    imp = jnp.where(forced[None, :, None, :], imp + FORCE_BONUS, imp)
    imp = jnp.where((j <= cur)[None, :, None, :], imp, NEG_INF)
    n_top = min(SEL_TOPN, n_sel)
    _, top_idx = lax.top_k(imp, n_top)
    tok = (top_idx[..., None] * SEL_BLOCK + jnp.arange(SEL_BLOCK, dtype=jnp.int32)).reshape(B, Tq, NS_KV, n_top * SEL_BLOCK)
    bi = jnp.arange(B)[:, None, None, None]
    gi = jnp.arange(NS_KV)[None, None, :, None]
    kg = sel_k[bi, gi, tok]
    vg = sel_v[bi, gi, tok]
    smask = (tok <= qpos[None, :, None, None])[:, :, :, None, :]
    s = jnp.einsum('btgrd,btgkd->btgrk', qr, kg, preferred_element_type=f32) * scale
    p = masked_softmax(s, smask)
    o_sel = jnp.einsum('btgrk,btgkd->btgrd', p.astype(vg.dtype), vg)
    wmask = (kwpos[None, :] <= qpos[:, None]) & (kwpos[None, :] > qpos[:, None] - WINDOW) & (kwpos[None, :] >= 0)
    s = jnp.einsum('btgrd,bkgd->btgrk', qr, kw, preferred_element_type=f32) * scale
    p = masked_softmax(s, wmask[None, :, None, None, :])
    o_win = jnp.einsum('btgrk,bkgd->btgrd', p.astype(vw.dtype), vw)
    g = gates.reshape(B, Tq, NS_KV, NS_G, 3)
    o = g[..., 0:1] * o_cmp + g[..., 1:2] * o_sel + g[..., 2:3] * o_win
    return o.reshape(B, Tq, NS_W).astype(q.dtype)


def nsa_prompt(c, pos, pe, w1, w2):
    q, q_rot, rows, win, gates = nsa_projections(c, pos)
    B, T = rows.shape[:2]
    kcmp, vcmp, sel_k, sel_v = nsa_context(rows, pe, w1, w2)
    wpad = jnp.pad(win, ((0, 0), (WINDOW, 0), (0, 0), (0, 0), (0, 0)))

    def block(i):
        t0 = i * Q_BLOCK
        qs = lambda a: lax.dynamic_slice_in_dim(a, t0, Q_BLOCK, axis=1)
        wblk = lax.dynamic_slice_in_dim(wpad, t0, Q_BLOCK + WINDOW, axis=1)
        qpos = t0 + jnp.arange(Q_BLOCK, dtype=jnp.int32)
        kwpos = t0 - WINDOW + jnp.arange(Q_BLOCK + WINDOW, dtype=jnp.int32)
        return nsa_attend(qs(q), qs(q_rot), qpos, kcmp, vcmp, sel_k, sel_v,
                          wblk[:, :, 0], wblk[:, :, 1], kwpos, qs(gates))

    o = lax.map(block, jnp.arange(T // Q_BLOCK, dtype=jnp.int32))
    o = jnp.moveaxis(o, 0, 1).reshape(B, T, NS_W)
    return o * jax.nn.silu(c['ns_g']), rows, win[:, -min(WINDOW, T):]


def nsa_sample(c, pos, pool, page_table, win_buf, pe, w1, w2):
    q, q_rot, rows, win, gates = nsa_projections(c, pos)
    B = rows.shape[0]
    past = pool[page_table]
    past = past.reshape(B, past.shape[1] * past.shape[2], 4, NS_KV, NS_DH)
    past_len = past.shape[1]
    kcmp, vcmp, sel_k, sel_v = nsa_context(jnp.concatenate([past.astype(rows.dtype), rows], axis=1), pe, w1, w2)
    wb = win_buf.shape[1]
    wall = jnp.concatenate([win_buf.astype(win.dtype), win], axis=1)
    kwpos = past_len - wb + jnp.arange(wall.shape[1], dtype=jnp.int32)
    o = nsa_attend(q, q_rot, pos, kcmp, vcmp, sel_k, sel_v, wall[:, :, 0], wall[:, :, 1], kwpos, gates)
    return o * jax.nn.silu(c['ns_g']), rows, wall[:, -wb:]


def merge_branches(c, outs, w_up, w_out):
    B, T = c['merge'].shape[:2]
    br = jnp.stack(outs, axis=2)
    up = jnp.einsum('btnw,nwd->btnd', br, w_up)
    gate = jax.nn.sigmoid(c['merge']).reshape(B, T, N_BRANCH, D_MODEL)
    return jnp.einsum('btnd,de->bte', gate * up, w_out)


def setup_inputs(seed: int = 0) -> dict:
    key = jax.random.key(seed)
    ks = jax.random.split(key, 24)
    nrm = lambda k, shape, s: jax.random.normal(k, shape, jnp.float32) * s
    n_pages = PAST_LEN // PAGE_SIZE
    n_used = DEC_BATCH * n_pages
    n_pool = n_used + max(1, n_used // 4)
    page_table = jax.random.permutation(ks[0], n_pool)[:n_used].reshape(DEC_BATCH, n_pages).astype(jnp.int32)
    return {
        'x_prompt': nrm(ks[1], (BATCH, SEQ, D_MODEL), 1.0),
        'x_sample': nrm(ks[2], (DEC_BATCH, DEC_SEQ, D_MODEL), 1.0),
        'mem_prompt': nrm(ks[3], (BATCH, MEM_LEN, D_MODEL), 1.0),
        'cache_nsa_kv': nrm(ks[4], (DEPTH, n_pool, PAGE_SIZE, 4, NS_KV, NS_DH), 1.0),
        'state_win_kv': nrm(ks[5], (DEPTH, DEC_BATCH, min(WINDOW, PAST_LEN), 2, NS_KV, NS_DH), 1.0),
        'state_hgrn': nrm(ks[6], (DEPTH, DEC_BATCH, HG_H, HG_DK, HG_DV), 0.5),
        'state_conv': nrm(ks[7], (DEPTH, DEC_BATCH, CV_K - 1, CV_W), 1.0),
        'cache_mem_kv': nrm(ks[8], (DEPTH, DEC_BATCH, MEM_LEN, 2, MEM_H, MEM_DH), 1.0),
        'page_table': page_table,
        'norm_g': 1.0 + nrm(ks[9], (DEPTH, D_MODEL), 0.02),
        'final_norm_g': 1.0 + nrm(ks[10], (D_MODEL,), 0.02),
        'w_in': nrm(ks[11], (DEPTH, D_MODEL, IN_WIDTH), D_MODEL ** -0.5),
        'hg_lb_logits': nrm(ks[12], (DEPTH, HG_H * HG_DK), 0.5),
        'hg_norm_g': 1.0 + nrm(ks[13], (DEPTH, HG_W), 0.02),
        'cv_w': nrm(ks[14], (DEPTH, CV_K, CV_W), CV_K ** -0.5),
        'ns_pe': nrm(ks[15], (DEPTH, 2, CMP_LEN, NS_DH), 0.1),
        'ns_cw1': nrm(ks[16], (DEPTH, 2, CMP_LEN * NS_DH, NS_DH), (CMP_LEN * NS_DH) ** -0.5),
        'ns_cw2': nrm(ks[17], (DEPTH, 2, NS_DH, NS_DH), NS_DH ** -0.5),
        'mem_norm_g': 1.0 + nrm(ks[18], (DEPTH, D_MODEL), 0.02),
        'w_mem_kv': nrm(ks[19], (DEPTH, D_MODEL, 2 * MEM_W), D_MODEL ** -0.5),
        'w_up': nrm(ks[20], (DEPTH, N_BRANCH, BR_W, D_MODEL), BR_W ** -0.5),
        'w_out': nrm(ks[21], (DEPTH, D_MODEL, D_MODEL), D_MODEL ** -0.5),
    }


def reference(x_prompt, x_sample, mem_prompt, cache_nsa_kv, state_win_kv, state_hgrn, state_conv, cache_mem_kv,
              page_table, norm_g, final_norm_g, w_in, hg_lb_logits, hg_norm_g, cv_w, ns_pe, ns_cw1, ns_cw2,
              mem_norm_g, w_mem_kv, w_up, w_out):
    f32 = jnp.float32
    s_lb = jax.nn.softmax(hg_lb_logits.astype(f32), axis=0)
    lower_bounds = jnp.cumsum(s_lb, axis=0) - s_lb[0]
    B, T = x_prompt.shape[:2]
    Bs, Ts = x_sample.shape[:2]
    past_len = page_table.shape[1] * PAGE_SIZE
    pos_p = jnp.arange(T, dtype=jnp.int32)
    pos_s = past_len + jnp.arange(Ts, dtype=jnp.int32)
    xp, xs = x_prompt, x_sample
    rows_p, win_p, hg_p, cv_p, mkv_p = [], [], [], [], []
    rows_s, win_s, hg_s, cv_s = [], [], [], []
    for l in range(DEPTH):
        cp = split_columns(rmsnorm(xp, norm_g[l]) @ w_in[l])
        o_hg, st_hg = hgrn2_branch(cp, lower_bounds[l], hg_norm_g[l], jnp.zeros((B, HG_H, HG_DK, HG_DV), f32))
        o_cv, st_cv = conv_branch(cp, cv_w[l], jnp.zeros((B, CV_K - 1, CV_W), xp.dtype))
        o_ns, st_rows, st_win = nsa_prompt(cp, pos_p, ns_pe[l], ns_cw1[l], ns_cw2[l])
        mkv = memory_kv(mem_prompt, mem_norm_g[l], w_mem_kv[l])
        o_mm = memory_branch(cp, mkv)
        xp = xp + merge_branches(cp, (o_hg, o_cv, o_ns, o_mm), w_up[l], w_out[l])
        rows_p.append(st_rows); win_p.append(st_win); hg_p.append(st_hg); cv_p.append(st_cv); mkv_p.append(mkv)
        cs = split_columns(rmsnorm(xs, norm_g[l]) @ w_in[l])
        o_hg, st_hg = hgrn2_branch(cs, lower_bounds[l], hg_norm_g[l], state_hgrn[l])
        o_cv, st_cv = conv_branch(cs, cv_w[l], state_conv[l])
        o_ns, st_rows, st_win = nsa_sample(cs, pos_s, cache_nsa_kv[l], page_table, state_win_kv[l],
                                           ns_pe[l], ns_cw1[l], ns_cw2[l])
        o_mm = memory_branch(cs, cache_mem_kv[l])
        xs = xs + merge_branches(cs, (o_hg, o_cv, o_ns, o_mm), w_up[l], w_out[l])
        rows_s.append(st_rows); win_s.append(st_win); hg_s.append(st_hg); cv_s.append(st_cv)
    y_prompt = rmsnorm(xp, final_norm_g)
    y_sample = rmsnorm(xs, final_norm_g)
    new_nsa_rows_prompt = jnp.stack(rows_p)
    new_win_kv_prompt = jnp.stack(win_p)
    new_hgrn_prompt = jnp.stack(hg_p)
    new_conv_prompt = jnp.stack(cv_p)
    new_mem_kv_prompt = jnp.stack(mkv_p)
    new_nsa_rows_sample = jnp.stack(rows_s)
    new_win_kv_sample = jnp.stack(win_s)
    new_hgrn_sample = jnp.stack(hg_s)
    new_conv_sample = jnp.stack(cv_s)
    return (y_prompt, y_sample, new_nsa_rows_prompt, new_win_kv_prompt, new_hgrn_prompt, new_conv_prompt,
            new_mem_kv_prompt, new_nsa_rows_sample, new_win_kv_sample, new_hgrn_sample, new_conv_sample)
```

```python
import functools

import numpy as np
import jax
import jax.numpy as jnp
from jax import lax
from jax.experimental import pallas as pl
from jax.experimental.pallas import tpu as pltpu

F32 = jnp.float32
BF16 = jnp.bfloat16

BR_W = 1024
N_BRANCH = 4
HG_H = 8
HG_DK = 128
HG_DV = 128
F_MIN = 1e-30
CV_K = 3
NS_H = 8
NS_KV = 2
NS_G = NS_H // NS_KV
NS_DH = 128
CMP_LEN = 32
CMP_STRIDE = 16
SEL_BLOCK = 64
SEL_TOPN = 16
WINDOW = 512
FORCE_BONUS = 100.0
ROPE_THETA = 500000.0
ROT_DIM = NS_DH // 4
MEM_H = 4
MEM_DH = 256
NORM_EPS = 1e-6
NEG_INF = -1e30
REMOVED = -3e38
PAGE_SIZE = 128

COL_A = 10752
COL_GATE = 24
COL_B0 = COL_A + COL_GATE

LANE = 128
HG_CHUNK = 128
Q_TILE = 128
K_TILE = 512
VMEM_LIMIT = 56 * 1024 * 1024


def _cparams(*sem):
    return pltpu.CompilerParams(dimension_semantics=sem, vmem_limit_bytes=VMEM_LIMIT)


def _pick(n, cands):
    for c in cands:
        if n % c == 0:
            return c
    return n


def _silu(x):
    return x * jax.nn.sigmoid(x)


def _dot(a, b):
    return jnp.dot(a, b, preferred_element_type=F32)


def _dot_nt(a, b):
    return lax.dot_general(a, b, (((1,), (1,)), ((), ())), preferred_element_type=F32)


def _split3(x):
    hi = x.astype(BF16)
    r1 = x - hi.astype(F32)
    mid = r1.astype(BF16)
    lo = (r1 - mid.astype(F32)).astype(BF16)
    return hi, mid, lo


def _norm_body(x_ref, g_ref, o_ref):
    x = x_ref[...]
    ms = jnp.mean(x * x, axis=-1, keepdims=True)
    o_ref[...] = (x * lax.rsqrt(ms + NORM_EPS) * g_ref[...]).astype(o_ref.dtype)


def rmsnorm_rows(x, g, out_dtype):
    m, d = x.shape
    tm = _pick(m, (256, 128, 64, 32, 16, 8))
    return pl.pallas_call(
        _norm_body,
        grid=(m // tm,),
        in_specs=[pl.BlockSpec((tm, d), lambda i: (i, 0)), pl.BlockSpec((1, d), lambda i: (0, 0))],
        out_specs=pl.BlockSpec((tm, d), lambda i: (i, 0)),
        out_shape=jax.ShapeDtypeStruct((m, d), out_dtype),
        compiler_params=_cparams("parallel"),
        name="rmsnorm",
    )(x, g.reshape(1, d).astype(F32))


def _mm_body(a_ref, b_ref, o_ref):
    o_ref[...] = _dot(a_ref[...].astype(BF16), b_ref[...]).astype(o_ref.dtype)


def _mm_res_body(a_ref, b_ref, r_ref, o_ref):
    o_ref[...] = r_ref[...] + _dot(a_ref[...].astype(BF16), b_ref[...])


def matmul(a, b, res=None, out_dtype=F32, name="matmul"):
    m, k = a.shape
    n = b.shape[1]
    tm = _pick(m, (1024, 512, 256, 128, 64, 32, 16, 8))
    tn = _pick(n, (512, 256, 128))
    in_specs = [pl.BlockSpec((tm, k), lambda i, j: (i, 0)), pl.BlockSpec((k, tn), lambda i, j: (0, j))]
    args = [a, b]
    body = _mm_body
    if res is not None:
        in_specs.append(pl.BlockSpec((tm, tn), lambda i, j: (i, j)))
        args.append(res)
        body = _mm_res_body
    return pl.pallas_call(
        body,
        grid=(m // tm, n // tn),
        in_specs=in_specs,
        out_specs=pl.BlockSpec((tm, tn), lambda i, j: (i, j)),
        out_shape=jax.ShapeDtypeStruct((m, n), out_dtype),
        compiler_params=_cparams("parallel", "parallel"),
        name=name,
    )(*args)


def _merge_body(b0, b1, b2, b3, g0, g1, g2, g3, w0, w1, w2, w3, o_ref):
    acc = jax.nn.sigmoid(g0[...]) * _dot(b0[...], w0[...])
    acc += jax.nn.sigmoid(g1[...]) * _dot(b1[...], w1[...])
    acc += jax.nn.sigmoid(g2[...]) * _dot(b2[...], w2[...])
    acc += jax.nn.sigmoid(g3[...]) * _dot(b3[...], w3[...])
    o_ref[...] = acc.astype(o_ref.dtype)


def merge_up(branches, z_b, w_up_bf16, d_model):
    m = branches[0].shape[0]
    tm = _pick(m, (512, 256, 128, 64, 32, 16))
    tn = 512
    nj = d_model // tn
    gate_col0 = (z_b.shape[1] - N_BRANCH * d_model) // tn
    br_specs = [pl.BlockSpec((tm, BR_W), lambda i, j: (i, 0)) for _ in range(N_BRANCH)]
    g_specs = [pl.BlockSpec((tm, tn), functools.partial(lambda i, j, n: (i, gate_col0 + n * nj + j), n=n))
               for n in range(N_BRANCH)]
    w_specs = [pl.BlockSpec((None, BR_W, tn), functools.partial(lambda i, j, n: (n, 0, j), n=n))
               for n in range(N_BRANCH)]
    return pl.pallas_call(
        _merge_body,
        grid=(m // tm, nj),
        in_specs=br_specs + g_specs + w_specs,
        out_specs=pl.BlockSpec((tm, tn), lambda i, j: (i, j)),
        out_shape=jax.ShapeDtypeStruct((m, d_model), BF16),
        compiler_params=_cparams("parallel", "parallel"),
        name="merge_up",
    )(*branches, z_b, z_b, z_b, z_b, w_up_bf16, w_up_bf16, w_up_bf16, w_up_bf16)


def _hgrn_consts(c):
    nlev = int(round(np.log2(c)))
    t = np.arange(c)[:, None]
    r = np.arange(c)[None, :]
    blocks = [r <= t]
    masks = [t == r]
    for lv in range(nlev):
        h = c >> (lv + 1)
        mid = (t // (2 * h)) * (2 * h) + h
        blocks.append(np.where(t >= mid, (r >= mid) & (r <= t), (r > t) & (r <= mid - 1)))
        same = (t // (2 * h)) == (r // (2 * h))
        masks.append(same & (t % (2 * h) >= h) & (r % (2 * h) < h))
    blocks.append(r > t)
    l_all = np.concatenate(blocks, axis=0).astype(np.float32)
    return jnp.asarray(l_all, BF16), jnp.asarray(np.stack(masks).astype(np.float32)), nlev


def _hgrn_body(nlev, q_ref, z_ref, v_ref, g_ref, lb_ref, ng_ref, l_ref, mask_ref, o_ref, s_ref, st_scr):
    c_idx = pl.program_id(1)
    c = q_ref.shape[0]

    @pl.when(c_idx == 0)
    def _():
        st_scr[...] = jnp.zeros_like(st_scr)

    lb = lb_ref[...]
    q = _silu(q_ref[...])
    z = z_ref[...]
    f = lb + (1.0 - lb) * jax.nn.sigmoid(z)
    logf = jnp.log(jnp.maximum(f, F_MIN))
    k = (1.0 - lb) * jax.nn.sigmoid(-z)
    v = v_ref[...]

    hi, mid, lo = _split3(logf)
    e3 = _dot(l_ref[...], jnp.concatenate([hi, mid, lo], axis=1))
    dk = q.shape[1]
    x = jnp.exp(e3[:, :dk] + e3[:, dk:2 * dk] + e3[:, 2 * dk:])
    eb = x[0:c]
    est = x[(nlev + 1) * c:(nlev + 2) * c]

    st = st_scr[...]
    inter = _dot_nt((q * eb).astype(BF16), st.astype(BF16))
    att = mask_ref[0] * _dot_nt(q.astype(BF16), k.astype(BF16))
    for lv in range(nlev):
        fac = x[(1 + lv) * c:(2 + lv) * c]
        att += mask_ref[1 + lv] * _dot_nt((q * fac).astype(BF16), (k * fac).astype(BF16))
    o = inter + _dot(att.astype(BF16), v.astype(BF16))
    o = o * lax.rsqrt(jnp.mean(o * o, axis=-1, keepdims=True) + NORM_EPS) * ng_ref[...]
    o_ref[...] = (o * _silu(g_ref[...])).astype(o_ref.dtype)

    st_new = st * eb[c - 1:c, :] + _dot(v.T.astype(BF16), (k * est).astype(BF16))
    st_scr[...] = st_new

    @pl.when(c_idx == pl.num_programs(1) - 1)
    def _():
        s_ref[...] = st_new.T


def hgrn_prompt(z_a, lb, norm_g):
    t = z_a.shape[0]
    c = HG_CHUNK
    l_all, masks, nlev = _hgrn_consts(c)

    def col(k):
        return pl.BlockSpec((c, LANE), functools.partial(lambda h, i, k: (i, k * HG_H + h), k=k))

    vec = pl.BlockSpec((None, 1, LANE), lambda h, i: (h, 0, 0))
    return pl.pallas_call(
        functools.partial(_hgrn_body, nlev),
        grid=(HG_H, t // c),
        in_specs=[col(0), col(1), col(2), col(3), vec, vec,
                  pl.BlockSpec(l_all.shape, lambda h, i: (0, 0)),
                  pl.BlockSpec(masks.shape, lambda h, i: (0, 0, 0))],
        out_specs=[pl.BlockSpec((c, LANE), lambda h, i: (i, h)),
                   pl.BlockSpec((None, HG_DK, HG_DV), lambda h, i: (h, 0, 0))],
        out_shape=[jax.ShapeDtypeStruct((t, BR_W), BF16),
                   jax.ShapeDtypeStruct((HG_H, HG_DK, HG_DV), F32)],
        scratch_shapes=[pltpu.VMEM((HG_DV, HG_DK), F32)],
        compiler_params=_cparams("parallel", "arbitrary"),
        name="hgrn_prompt",
    )(z_a, z_a, z_a, z_a, lb.reshape(HG_H, 1, LANE), norm_g.reshape(HG_H, 1, LANE), l_all, masks)


def _hgrn_step_body(q_ref, z_ref, v_ref, g_ref, lb_ref, ng_ref, s0_ref, o_ref, s_ref):
    lb = lb_ref[...]
    q = _silu(q_ref[...])
    z = z_ref[...]
    f = jnp.maximum(lb + (1.0 - lb) * jax.nn.sigmoid(z), F_MIN)
    k = (1.0 - lb) * jax.nn.sigmoid(-z)
    v = v_ref[...]
    rows = []
    for h in range(HG_H):
        def colb(a):
            return jnp.broadcast_to(a[h:h + 1, :], (HG_DK, HG_DK)).T
        s_new = colb(f) * s0_ref[h] + colb(k) * v[h:h + 1, :]
        s_ref[h] = s_new
        rows.append(jnp.sum(colb(q) * s_new, axis=0, keepdims=True))
    o = jnp.concatenate(rows, axis=0)
    o = o * lax.rsqrt(jnp.mean(o * o, axis=-1, keepdims=True) + NORM_EPS) * ng_ref[...]
    o_ref[...] = o * _silu(g_ref[...])


def hgrn_sample(z_a3, lb, norm_g, s0):
    bs = z_a3.shape[0]

    def grp(k):
        return pl.BlockSpec((None, HG_H, LANE), functools.partial(lambda b, k: (b, k, 0), k=k))

    vec = pl.BlockSpec((HG_H, LANE), lambda b: (0, 0))
    st = pl.BlockSpec((None, HG_H, HG_DK, HG_DV), lambda b: (b, 0, 0, 0))
    return pl.pallas_call(
        _hgrn_step_body,
        grid=(bs,),
        in_specs=[grp(0), grp(1), grp(2), grp(3), vec, vec, st],
        out_specs=[pl.BlockSpec((None, HG_H, LANE), lambda b: (b, 0, 0)), st],
        out_shape=[jax.ShapeDtypeStruct((bs, HG_H, LANE), F32),
                   jax.ShapeDtypeStruct((bs, HG_H, HG_DK, HG_DV), F32)],
        compiler_params=_cparams("parallel"),
        name="hgrn_sample",
    )(z_a3, z_a3, z_a3, z_a3, lb.reshape(HG_H, LANE), norm_g.reshape(HG_H, LANE), s0)


def _conv_body(u_ref, b_ref, c_ref, g_ref, w_ref, prev_ref, o_ref, last_ref, carry):
    @pl.when(pl.program_id(0) == 0)
    def _():
        carry[...] = prev_ref[...]

    v = c_ref[...] * u_ref[...]
    tm = v.shape[0]
    row = lax.broadcasted_iota(jnp.int32, v.shape, 0)
    p1 = carry[7:8, :]
    p2 = carry[6:7, :]
    v1 = jnp.where(row == 0, p1, pltpu.roll(v, 1, 0))
    v2 = jnp.where(row == 0, p2, jnp.where(row == 1, p1, pltpu.roll(v, 2, 0)))
    w = w_ref[...]
    y = w[0:1, :] * v2 + w[1:2, :] * v1 + w[2:3, :] * v
    o_ref[...] = (b_ref[...] * y * _silu(g_ref[...])).astype(o_ref.dtype)
    tail = v[tm - 8:tm, :]
    carry[...] = tail
    last_ref[...] = tail


def conv_prompt(z_a, w, prev):
    t = z_a.shape[0]
    tm = _pick(t, (256, 128, 64, 32, 16, 8))

    def col(k):
        return pl.BlockSpec((tm, BR_W), functools.partial(lambda i, k: (i, 4 + k), k=k))

    w8 = jnp.zeros((8, BR_W), F32).at[:CV_K].set(w.astype(F32))
    prev8 = jnp.zeros((8, BR_W), F32).at[8 - (CV_K - 1):].set(prev.astype(F32))
    full8 = pl.BlockSpec((8, BR_W), lambda i: (0, 0))
    o, last = pl.pallas_call(
        _conv_body,
        grid=(t // tm,),
        in_specs=[col(0), col(1), col(2), col(3), full8, full8],
        out_specs=[pl.BlockSpec((tm, BR_W), lambda i: (i, 0)), full8],
        out_shape=[jax.ShapeDtypeStruct((t, BR_W), BF16), jax.ShapeDtypeStruct((8, BR_W), F32)],
        scratch_shapes=[pltpu.VMEM((8, BR_W), F32)],
        compiler_params=_cparams("arbitrary"),
        name="conv_prompt",
    )(z_a, z_a, z_a, z_a, w8, prev8)
    return o, last[8 - (CV_K - 1):]


def _conv_step_body(u_ref, b_ref, c_ref, g_ref, w_ref, p0_ref, p1_ref, o_ref, v_ref):
    v = c_ref[...] * u_ref[...]
    w = w_ref[...]
    y = w[0:1, :] * p0_ref[...] + w[1:2, :] * p1_ref[...] + w[2:3, :] * v
    o_ref[...] = b_ref[...] * y * _silu(g_ref[...])
    v_ref[...] = v


def conv_sample(z_a, w, prev):
    bs = z_a.shape[0]

    def col(k):
        return pl.BlockSpec((bs, BR_W), functools.partial(lambda i, k: (0, 4 + k), k=k))

    w8 = jnp.zeros((8, BR_W), F32).at[:CV_K].set(w.astype(F32))
    full = pl.BlockSpec((bs, BR_W), lambda i: (0, 0))
    o, v = pl.pallas_call(
        _conv_step_body,
        grid=(1,),
        in_specs=[col(0), col(1), col(2), col(3), pl.BlockSpec((8, BR_W), lambda i: (0, 0)), full, full],
        out_specs=[full, full],
        out_shape=[jax.ShapeDtypeStruct((bs, BR_W), F32), jax.ShapeDtypeStruct((bs, BR_W), F32)],
        compiler_params=_cparams("arbitrary"),
        name="conv_sample",
    )(z_a, z_a, z_a, z_a, w8, prev[:, 0], prev[:, 1])
    return o, jnp.stack([prev[:, 1], v], axis=1)


def _rope_tables(pos):
    half = ROT_DIM // 2
    inv = ROPE_THETA ** (-2.0 * jnp.arange(half, dtype=F32) / ROT_DIM)
    ang = pos.astype(F32)[:, None] * inv[None, :]
    cos, sin = jnp.cos(ang), jnp.sin(ang)
    m = pos.shape[0]
    ones = jnp.ones((m, NS_DH - ROT_DIM), F32)
    zeros = jnp.zeros((m, NS_DH - ROT_DIM), F32)
    zh = jnp.zeros((m, half), F32)
    cos_t = jnp.concatenate([cos, cos, ones], axis=1)
    sin_a = jnp.concatenate([zh, sin, zeros], axis=1)
    sin_b = jnp.concatenate([-sin, zh, zeros], axis=1)
    return cos_t, sin_a, sin_b


def _rope(x, cos_t, sin_a, sin_b):
    n = x.shape[1] // NS_DH
    half = ROT_DIM // 2

    def tile(a):
        return a if n == 1 else jnp.concatenate([a] * n, axis=1)

    return (x * tile(cos_t) + pltpu.roll(x, half, 1) * tile(sin_a)
            + pltpu.roll(x, x.shape[1] - half, 1) * tile(sin_b))


def _nsa_prep_body(q_ref, cv_ref, ks_ref, vs_ref, kw_ref, vw_ref, cos_ref, sa_ref, sb_ref,
                   qc_ref, qr_ref, rows_ref, win_ref):
    cos_t, sin_a, sin_b = cos_ref[...], sa_ref[...], sb_ref[...]
    scale = NS_DH ** -0.5
    q = q_ref[...]
    qc_ref[...] = (q * scale).astype(qc_ref.dtype)
    qr_ref[...] = (_rope(q, cos_t, sin_a, sin_b) * scale).astype(qr_ref.dtype)
    kvw = NS_KV * NS_DH
    rows_ref[:, 0:2 * kvw] = cv_ref[...]
    rows_ref[:, 2 * kvw:3 * kvw] = _rope(ks_ref[...], cos_t, sin_a, sin_b)
    rows_ref[:, 3 * kvw:4 * kvw] = vs_ref[...]
    win_ref[:, 0:kvw] = _rope(kw_ref[...], cos_t, sin_a, sin_b)
    win_ref[:, kvw:2 * kvw] = vw_ref[...]


def nsa_prep(z_a, pos, q_dtype):
    m = z_a.shape[0]
    tm = _pick(m, (256, 128, 64, 32, 16, 8))
    kvw = NS_KV * NS_DH
    c0 = 8 * BR_W // kvw
    tabs = _rope_tables(pos)

    def col(k, width=kvw):
        return pl.BlockSpec((tm, width), lambda i: (i, k))

    tab = pl.BlockSpec((tm, NS_DH), lambda i: (i, 0))
    return pl.pallas_call(
        _nsa_prep_body,
        grid=(m // tm,),
        in_specs=[col(8, BR_W), col((c0 + 4) // 2, 2 * kvw), col(c0 + 6), col(c0 + 7), col(c0 + 8), col(c0 + 9),
                  tab, tab, tab],
        out_specs=[pl.BlockSpec((tm, BR_W), lambda i: (i, 0)), pl.BlockSpec((tm, BR_W), lambda i: (i, 0)),
                   pl.BlockSpec((tm, 4 * kvw), lambda i: (i, 0)), pl.BlockSpec((tm, 2 * kvw), lambda i: (i, 0))],
        out_shape=[jax.ShapeDtypeStruct((m, BR_W), q_dtype), jax.ShapeDtypeStruct((m, BR_W), q_dtype),
                   jax.ShapeDtypeStruct((m, 4 * kvw), F32), jax.ShapeDtypeStruct((m, 2 * kvw), F32)],
        compiler_params=_cparams("parallel"),
        name="nsa_prep",
    )(z_a, z_a, z_a, z_a, z_a, z_a, *tabs)


def _cmp_finish_body(n_valid, p_ref, pe_ref, w1_ref, w2_ref, o_ref):
    p = p_ref[...]
    n = p.shape[0]
    pe_term = _dot(pe_ref[...], w1_ref[...])[0:1, :]
    nxt = pltpu.roll(p[:, NS_DH:], n - 1, 0)
    hid = _silu(p[:, :NS_DH] + nxt + pe_term)
    out = _dot(hid.astype(BF16), w2_ref[...])
    row = lax.broadcasted_iota(jnp.int32, out.shape, 0)
    o_ref[...] = jnp.where(row < n_valid, out, 0.0)


def compress(x_k, x_v, pe, w1, w2, nch):
    kdim = CMP_STRIDE * NS_DH
    w1b = w1.astype(BF16)
    w1cat = jnp.concatenate([w1b[:, :kdim], w1b[:, kdim:]], axis=2)
    p = jnp.stack([matmul(x_k, w1cat[0], name="cmp_partial_k"), matmul(x_v, w1cat[1], name="cmp_partial_v")])
    groups = x_k.shape[0] // nch
    pe8 = jnp.broadcast_to(pe.reshape(2, 1, CMP_LEN * NS_DH), (2, 8, CMP_LEN * NS_DH)).astype(BF16)
    out = pl.pallas_call(
        functools.partial(_cmp_finish_body, nch - 1),
        grid=(2, groups),
        in_specs=[pl.BlockSpec((None, nch, 2 * NS_DH), lambda a, g: (a, g, 0)),
                  pl.BlockSpec((None, 8, CMP_LEN * NS_DH), lambda a, g: (a, 0, 0)),
                  pl.BlockSpec((None, CMP_LEN * NS_DH, NS_DH), lambda a, g: (a, 0, 0)),
                  pl.BlockSpec((None, NS_DH, NS_DH), lambda a, g: (a, 0, 0))],
        out_specs=pl.BlockSpec((None, nch, NS_DH), lambda a, g: (a, g, 0)),
        out_shape=jax.ShapeDtypeStruct((2, groups * nch, NS_DH), F32),
        compiler_params=_cparams("parallel", "parallel"),
        name="cmp_finish",
    )(p, pe8, w1b, w2.astype(BF16))
    return out.reshape(2, groups, nch, NS_DH)


def _cmp_to_sel(n_cmp_pad, n_sel_pad, n_cmp, n_sel):
    cs = np.arange(n_cmp_pad)[:, None] * CMP_STRIDE
    ss = np.arange(n_sel_pad)[None, :] * SEL_BLOCK
    ov = np.minimum(cs + CMP_LEN, ss + SEL_BLOCK) - np.maximum(cs, ss)
    m = np.clip(ov, 0, None).astype(np.float32) / CMP_LEN
    m[n_cmp:, :] = 0.0
    m[:, n_sel:] = 0.0
    return m


def _stack_heads(q):
    return jnp.concatenate([q[:, r * NS_DH:(r + 1) * NS_DH] for r in range(NS_G)], axis=0)


def _tile_heads(a):
    return jnp.concatenate([a] * NS_G, axis=1)


def _store_heads_t(o_ref, o_t):
    tq = o_t.shape[1] // NS_G
    for r in range(NS_G):
        o_ref[:, r * NS_DH:(r + 1) * NS_DH] = o_t[:, r * tq:(r + 1) * tq].T


def _cmp_sel_body(n_cmp, q_ref, kc_ref, vct_ref, msel_ref, o_ref, sel_ref):
    tq = q_ref.shape[0]
    t0 = pl.program_id(0) * tq
    qs = _stack_heads(q_ref[...])
    s_t = _dot_nt(kc_ref[...], qs)
    ncp = s_t.shape[0]
    n_i = lax.broadcasted_iota(jnp.int32, (ncp, tq), 0)
    qpos = t0 + lax.broadcasted_iota(jnp.int32, (ncp, tq), 1)
    valid = _tile_heads((n_i * CMP_STRIDE + CMP_LEN - 1 <= qpos) & (n_i < n_cmp))
    sm = jnp.where(valid, s_t, NEG_INF)
    mx = jnp.max(sm, axis=0, keepdims=True)
    e = jnp.where(valid, jnp.exp(sm - mx), 0.0)
    den = jnp.sum(e, axis=0, keepdims=True)
    p = e * (1.0 / jnp.where(den > 0.0, den, 1.0))
    _store_heads_t(o_ref, _dot(vct_ref[...], p.astype(BF16)))

    psum = p[:, 0:tq]
    for r in range(1, NS_G):
        psum += p[:, r * tq:(r + 1) * tq]
    hi, mid, lo = _split3(psum)
    i3 = _dot(msel_ref[...], jnp.concatenate([hi, mid, lo], axis=1))
    imp = i3[:, :tq] + i3[:, tq:2 * tq] + i3[:, 2 * tq:]
    nb = imp.shape[0]
    j_i = lax.broadcasted_iota(jnp.int32, (nb, tq), 0)
    cur = (t0 + lax.broadcasted_iota(jnp.int32, (nb, tq), 1)) // SEL_BLOCK
    forced = (j_i == 0) | (j_i == cur) | (j_i == cur - 1)
    imp = jnp.where(forced, imp + FORCE_BONUS, imp)
    imp = jnp.where(j_i <= cur, imp, NEG_INF)
    j_f = j_i.astype(F32)
    sel = jnp.zeros((nb, tq), F32)
    for _ in range(min(SEL_TOPN, nb)):
        mx = jnp.max(imp, axis=0, keepdims=True)
        first = jnp.min(jnp.where(imp == mx, j_f, float(nb)), axis=0, keepdims=True)
        hit = j_f == first
        sel = jnp.where(hit, 1.0, sel)
        imp = jnp.where(hit, REMOVED, imp)
    sel_ref[...] = jnp.where(j_i <= cur, sel, 0.0)


def nsa_cmp_select(q_c, kcmp, vcmp_t, n_cmp):
    t = q_c.shape[0]
    ncp = kcmp.shape[1]
    n_sel = t // SEL_BLOCK
    msel_t = jnp.asarray(_cmp_to_sel(ncp, n_sel, n_cmp, n_sel).T, BF16)
    gw = NS_G * NS_DH
    return pl.pallas_call(
        functools.partial(_cmp_sel_body, n_cmp),
        grid=(t // Q_TILE, NS_KV),
        in_specs=[pl.BlockSpec((Q_TILE, gw), lambda i, g: (i, g)),
                  pl.BlockSpec((None, ncp, NS_DH), lambda i, g: (g, 0, 0)),
                  pl.BlockSpec((None, NS_DH, ncp), lambda i, g: (g, 0, 0)),
                  pl.BlockSpec((n_sel, ncp), lambda i, g: (0, 0))],
        out_specs=[pl.BlockSpec((Q_TILE, gw), lambda i, g: (i, g)),
                   pl.BlockSpec((None, n_sel, Q_TILE), lambda i, g: (g, 0, i))],
        out_shape=[jax.ShapeDtypeStruct((t, BR_W), F32), jax.ShapeDtypeStruct((NS_KV, n_sel, t), F32)],
        compiler_params=_cparams("parallel", "parallel"),
        name="nsa_cmp_select",
    )(q_c, kcmp, vcmp_t, msel_t)


def _sel_attn_body(q_ref, k_ref, vt_ref, sel_ref, o_ref, m_scr, l_scr, acc_scr):
    tq = q_ref.shape[0]
    t0 = pl.program_id(0) * tq
    qs = _stack_heads(q_ref[...])
    m_scr[...] = jnp.full_like(m_scr, NEG_INF)
    l_scr[...] = jnp.zeros_like(l_scr)
    acc_scr[...] = jnp.zeros_like(acc_scr)
    bpt = K_TILE // SEL_BLOCK
    n_tiles = (t0 + tq + K_TILE - 1) // K_TILE

    def step(kt, carry):
        k0 = pl.multiple_of(kt * K_TILE, K_TILE)
        s_t = _dot_nt(k_ref[pl.ds(k0, K_TILE), :], qs)
        sel = sel_ref[pl.ds(pl.multiple_of(kt * bpt, bpt), bpt), :]
        sel = jnp.broadcast_to(sel[:, None, :], (bpt, SEL_BLOCK, tq)).reshape(K_TILE, tq)
        key = k0 + lax.broadcasted_iota(jnp.int32, (K_TILE, tq), 0)
        qpos = t0 + lax.broadcasted_iota(jnp.int32, (K_TILE, tq), 1)
        valid = _tile_heads((sel > 0.5) & (key <= qpos))
        m_old = m_scr[...]
        m_new = jnp.maximum(m_old, jnp.max(jnp.where(valid, s_t, NEG_INF), axis=0, keepdims=True))
        alpha = jnp.exp(m_old - m_new)
        p = jnp.where(valid, jnp.exp(s_t - m_new), 0.0)
        l_scr[...] = alpha * l_scr[...] + jnp.sum(p, axis=0, keepdims=True)
        acc_scr[...] = alpha * acc_scr[...] + _dot(vt_ref[:, pl.ds(k0, K_TILE)], p.astype(BF16))
        m_scr[...] = m_new
        return carry

    lax.fori_loop(0, n_tiles, step, 0)
    den = l_scr[...]
    _store_heads_t(o_ref, acc_scr[...] * (1.0 / jnp.where(den > 0.0, den, 1.0)))


def nsa_sel_attn(q_r, ks, vs_t, sel):
    t = q_r.shape[0]
    n_sel = sel.shape[1]
    gw = NS_G * NS_DH
    return pl.pallas_call(
        _sel_attn_body,
        grid=(t // Q_TILE, NS_KV),
        in_specs=[pl.BlockSpec((Q_TILE, gw), lambda i, g: (i, g)),
                  pl.BlockSpec((None, t, NS_DH), lambda i, g: (g, 0, 0)),
                  pl.BlockSpec((None, NS_DH, t), lambda i, g: (g, 0, 0)),
                  pl.BlockSpec((None, n_sel, Q_TILE), lambda i, g: (g, 0, i))],
        out_specs=pl.BlockSpec((Q_TILE, gw), lambda i, g: (i, g)),
        out_shape=jax.ShapeDtypeStruct((t, BR_W), F32),
        scratch_shapes=[pltpu.VMEM((1, NS_G * Q_TILE), F32), pltpu.VMEM((1, NS_G * Q_TILE), F32),
                        pltpu.VMEM((NS_DH, NS_G * Q_TILE), F32)],
        compiler_params=_cparams("parallel", "parallel"),
        name="nsa_sel_attn",
    )(q_r, ks, vs_t, sel)


def _win_attn_body(span, q_ref, k_ref, vt_ref, o_ref):
    tq = q_ref.shape[0]
    t0 = pl.program_id(0) * tq
    qs = _stack_heads(q_ref[...])
    start = pl.multiple_of(jnp.maximum(t0 - WINDOW, 0), Q_TILE)
    s_t = _dot_nt(k_ref[pl.ds(start, span), :], qs)
    key = start + lax.broadcasted_iota(jnp.int32, (span, tq), 0)
    qpos = t0 + lax.broadcasted_iota(jnp.int32, (span, tq), 1)
    valid = _tile_heads((key <= qpos) & (key > qpos - WINDOW))
    sm = jnp.where(valid, s_t, NEG_INF)
    mx = jnp.max(sm, axis=0, keepdims=True)
    e = jnp.where(valid, jnp.exp(sm - mx), 0.0)
    den = jnp.sum(e, axis=0, keepdims=True)
    p = e * (1.0 / den)
    _store_heads_t(o_ref, _dot(vt_ref[:, pl.ds(start, span)], p.astype(BF16)))


def nsa_win_attn(q_r, kw, vw_t):
    t = q_r.shape[0]
    span = min(WINDOW + Q_TILE, t)
    gw = NS_G * NS_DH
    return pl.pallas_call(
        functools.partial(_win_attn_body, span),
        grid=(t // Q_TILE, NS_KV),
        in_specs=[pl.BlockSpec((Q_TILE, gw), lambda i, g: (i, g)),
                  pl.BlockSpec((None, t, NS_DH), lambda i, g: (g, 0, 0)),
                  pl.BlockSpec((None, NS_DH, t), lambda i, g: (g, 0, 0))],
        out_specs=pl.BlockSpec((Q_TILE, gw), lambda i, g: (i, g)),
        out_shape=jax.ShapeDtypeStruct((t, BR_W), F32),
        compiler_params=_cparams("parallel", "parallel"),
        name="nsa_win_attn",
    )(q_r, kw, vw_t)


def _gate_expand():
    e = np.zeros((3, LANE, BR_W), np.float32)
    for h in range(NS_H):
        for c in range(3):
            e[c, h * 3 + c, h * NS_DH:(h + 1) * NS_DH] = 1.0
    return jnp.asarray(e, BF16)


def _combine_body(oc_ref, os_ref, ow_ref, gate_ref, g_ref, e_ref, o_ref):
    gs = jax.nn.sigmoid(gate_ref[...])
    hi = gs.astype(BF16)
    lo = (gs - hi.astype(F32)).astype(BF16)

    def expand(c):
        return _dot(hi, e_ref[c]) + _dot(lo, e_ref[c])

    o = expand(0) * oc_ref[...] + expand(1) * os_ref[...] + expand(2) * ow_ref[...]
    o_ref[...] = (o * _silu(g_ref[...])).astype(o_ref.dtype)


def nsa_combine(o_cmp, o_sel, o_win, z_g, z_b):
    m = o_cmp.shape[0]
    tm = _pick(m, (256, 128, 64, 32, 16))
    row = pl.BlockSpec((tm, BR_W), lambda i: (i, 0))
    return pl.pallas_call(
        _combine_body,
        grid=(m // tm,),
        in_specs=[row, row, row, pl.BlockSpec((tm, LANE), lambda i: (i, 0)), row,
                  pl.BlockSpec((3, LANE, BR_W), lambda i: (0, 0, 0))],
        out_specs=row,
        out_shape=jax.ShapeDtypeStruct((m, BR_W), BF16),
        compiler_params=_cparams("parallel"),
        name="nsa_combine",
    )(o_cmp, o_sel, o_win, z_g, z_b, _gate_expand())


def _mem_attn_body(q_ref, g_ref, kv_ref, o_ref):
    scale = MEM_DH ** -0.5
    q = q_ref[...]
    for h in range(MEM_H):
        k = kv_ref[:, h * MEM_DH:(h + 1) * MEM_DH].astype(BF16)
        v = kv_ref[:, (MEM_H + h) * MEM_DH:(MEM_H + h + 1) * MEM_DH].astype(BF16)
        s = _dot_nt(q[:, h * MEM_DH:(h + 1) * MEM_DH].astype(BF16), k) * scale
        e = jnp.exp(s - jnp.max(s, axis=-1, keepdims=True))
        p = e * (1.0 / jnp.sum(e, axis=-1, keepdims=True))
        o = _dot(p.astype(BF16), v)
        sl = slice(h * MEM_DH, (h + 1) * MEM_DH)
        o_ref[:, sl] = (o * _silu(g_ref[:, sl])).astype(o_ref.dtype)


def mem_attn_prompt(z_b, mkv):
    t = z_b.shape[0]
    tq = _pick(t, (512, 256, 128, 64, 32, 16))
    return pl.pallas_call(
        _mem_attn_body,
        grid=(t // tq,),
        in_specs=[pl.BlockSpec((tq, BR_W), lambda i: (i, 1)), pl.BlockSpec((tq, BR_W), lambda i: (i, 2)),
                  pl.BlockSpec(mkv.shape, lambda i: (0, 0))],
        out_specs=pl.BlockSpec((tq, BR_W), lambda i: (i, 0)),
        out_shape=jax.ShapeDtypeStruct((t, BR_W), BF16),
        compiler_params=_cparams("parallel"),
        name="mem_attn_prompt",
    )(z_b, z_b, mkv)


def _mem_attn_step_body(q_ref, g_ref, kv_ref, o_ref):
    scale = MEM_DH ** -0.5
    q = q_ref[...]
    rows_per = MEM_DH // LANE
    out_rows = []
    for h in range(MEM_H):
        qh = jnp.concatenate([q[h * rows_per + i:h * rows_per + i + 1, :] for i in range(rows_per)], axis=1)
        qh = jnp.broadcast_to(qh, (8, MEM_DH)).astype(BF16)
        k = kv_ref[:, h * MEM_DH:(h + 1) * MEM_DH].astype(BF16)
        v = kv_ref[:, (MEM_H + h) * MEM_DH:(MEM_H + h + 1) * MEM_DH].astype(BF16)
        s = _dot_nt(qh, k) * scale
        e = jnp.exp(s - jnp.max(s, axis=-1, keepdims=True))
        p = e * (1.0 / jnp.sum(e, axis=-1, keepdims=True))
        o = _dot(p.astype(BF16), v)[0:1, :]
        out_rows += [o[:, i * LANE:(i + 1) * LANE] for i in range(rows_per)]
    o_ref[...] = jnp.concatenate(out_rows, axis=0) * _silu(g_ref[...])


def mem_attn_sample(z_b3, kv_cache, layer):
    bs = z_b3.shape[0]
    mem_len, width = kv_cache.shape[2], kv_cache.shape[3]
    return pl.pallas_call(
        _mem_attn_step_body,
        grid=(bs,),
        in_specs=[pl.BlockSpec((None, 8, LANE), lambda b: (b, 1, 0)), pl.BlockSpec((None, 8, LANE), lambda b: (b, 2, 0)),
                  pl.BlockSpec((None, None, mem_len, width), lambda b: (layer, b, 0, 0))],
        out_specs=pl.BlockSpec((None, 8, LANE), lambda b: (b, 0, 0)),
        out_shape=jax.ShapeDtypeStruct((bs, 8, LANE), F32),
        compiler_params=_cparams("parallel"),
        name="mem_attn_sample",
    )(z_b3, z_b3, kv_cache)


def _page_gather_body(pt_ref, x_ref, o_ref):
    del pt_ref
    for c in range(2 * NS_KV):
        o_ref[c] = x_ref[:, c * NS_DH:(c + 1) * NS_DH]


def gather_cmp_rows(cache4, page_table, layer):
    bs, n_pages = page_table.shape
    nslab = 2 * NS_KV
    grid_spec = pltpu.PrefetchScalarGridSpec(
        num_scalar_prefetch=1,
        grid=(bs, n_pages),
        in_specs=[pl.BlockSpec((None, None, PAGE_SIZE, nslab * NS_DH), lambda b, p, pt: (layer, pt[b, p], 0, 0))],
        out_specs=pl.BlockSpec((nslab, None, PAGE_SIZE, NS_DH), lambda b, p, pt: (0, b, p, 0)),
    )
    return pl.pallas_call(
        _page_gather_body,
        grid_spec=grid_spec,
        out_shape=jax.ShapeDtypeStruct((nslab, bs, n_pages * PAGE_SIZE, NS_DH), F32),
        compiler_params=_cparams("parallel", "parallel"),
        name="page_gather",
    )(page_table, cache4)


def _row_group(shape):
    return lax.broadcasted_iota(jnp.int32, shape, 0) // NS_G


def _cmp_step_body(n_cmp, qpos, q_ref, kc_ref, vc_ref, o_ref, ps_ref):
    q = q_ref[...].astype(BF16)
    ncp = kc_ref.shape[1]
    s = [_dot_nt(q, kc_ref[g].astype(BF16)) for g in range(NS_KV)]
    grp = _row_group((NS_H, ncp))
    s = jnp.where(grp == 0, s[0], s[1])
    n_i = lax.broadcasted_iota(jnp.int32, (NS_H, ncp), 1)
    valid = (n_i * CMP_STRIDE + CMP_LEN - 1 <= qpos) & (n_i < n_cmp)
    sm = jnp.where(valid, s, NEG_INF)
    mx = jnp.max(sm, axis=-1, keepdims=True)
    e = jnp.where(valid, jnp.exp(sm - mx), 0.0)
    den = jnp.sum(e, axis=-1, keepdims=True)
    p = e * (1.0 / jnp.where(den > 0.0, den, 1.0))
    pb = p.astype(BF16)
    o = [_dot(pb, vc_ref[g].astype(BF16)) for g in range(NS_KV)]
    o_ref[...] = jnp.where(_row_group((NS_H, NS_DH)) == 0, o[0], o[1])
    ps_ref[...] = jnp.concatenate(
        [jnp.sum(jnp.where(grp == g, p, 0.0), axis=0, keepdims=True) for g in range(NS_KV)], axis=0)


def nsa_cmp_sample(q_c3, kcmp, vcmp, n_cmp, qpos):
    bs = q_c3.shape[0]
    ncp = kcmp.shape[2]
    kv = pl.BlockSpec((NS_KV, None, ncp, NS_DH), lambda b: (0, b, 0, 0))
    return pl.pallas_call(
        functools.partial(_cmp_step_body, n_cmp, qpos),
        grid=(bs,),
        in_specs=[pl.BlockSpec((None, NS_H, NS_DH), lambda b: (b, 0, 0)), kv, kv],
        out_specs=[pl.BlockSpec((None, NS_H, NS_DH), lambda b: (b, 0, 0)),
                   pl.BlockSpec((None, NS_KV, ncp), lambda b: (b, 0, 0))],
        out_shape=[jax.ShapeDtypeStruct((bs, NS_H, NS_DH), F32), jax.ShapeDtypeStruct((bs, NS_KV, ncp), F32)],
        compiler_params=_cparams("parallel"),
        name="nsa_cmp_sample",
    )(q_c3, kcmp, vcmp)


def _select_body(n_sel, qpos, ps_ref, msel_ref, idx_ref):
    hi, mid, lo = _split3(ps_ref[...])
    imp = _dot(hi, msel_ref[...]) + _dot(mid, msel_ref[...]) + _dot(lo, msel_ref[...])
    j_i = lax.broadcasted_iota(jnp.int32, imp.shape, 1)
    cur = qpos // SEL_BLOCK
    forced = (j_i == 0) | (j_i == cur) | (j_i == cur - 1)
    imp = jnp.where(forced, imp + FORCE_BONUS, imp)
    imp = jnp.where(j_i <= cur, imp, NEG_INF)
    imp = jnp.where(j_i < n_sel, imp, REMOVED)
    j_f = j_i.astype(F32)
    col = lax.broadcasted_iota(jnp.int32, idx_ref.shape, 1)
    out = jnp.zeros(idx_ref.shape, F32)
    for it in range(min(SEL_TOPN, n_sel)):
        mx = jnp.max(imp, axis=-1, keepdims=True)
        first = jnp.min(jnp.where(imp == mx, j_f, float(imp.shape[1])), axis=-1, keepdims=True)
        out = jnp.where(col == it, first, out)
        imp = jnp.where(j_f == first, REMOVED, imp)
    idx_ref[...] = out.astype(jnp.int32)


def nsa_select_sample(psum2, n_cmp, n_sel, qpos):
    rows, ncp = psum2.shape
    nsp = -(-n_sel // LANE) * LANE
    msel = jnp.asarray(_cmp_to_sel(ncp, nsp, n_cmp, n_sel), BF16)
    return pl.pallas_call(
        functools.partial(_select_body, n_sel, qpos),
        grid=(1,),
        in_specs=[pl.BlockSpec((rows, ncp), lambda i: (0, 0)), pl.BlockSpec((ncp, nsp), lambda i: (0, 0))],
        out_specs=pl.BlockSpec((rows, LANE), lambda i: (0, 0)),
        out_shape=jax.ShapeDtypeStruct((rows, LANE), jnp.int32),
        compiler_params=_cparams("arbitrary"),
        name="nsa_select_sample",
    )(psum2, msel)


def _sel_step_body(n_past_blocks, qpos, pt_ref, top_ref, q_ref, k0_ref, k1_ref, v0_ref, v1_ref, new_ref,
                   o_ref, m_scr, l_scr, acc_scr):
    del pt_ref
    b = pl.program_id(0)
    j = pl.program_id(1)

    @pl.when(j == 0)
    def _():
        m_scr[...] = jnp.full_like(m_scr, NEG_INF)
        l_scr[...] = jnp.zeros_like(l_scr)
        acc_scr[...] = jnp.zeros_like(acc_scr)

    q = q_ref[...]
    qb = q.astype(BF16)
    grp_k = _row_group((NS_H, SEL_BLOCK))
    grp_d = _row_group((NS_H, NS_DH))
    idx = [top_ref[b, g * SEL_TOPN + j] for g in range(NS_KV)]
    idx_k = jnp.where(grp_k == 0, idx[0], idx[1])
    idx_d = jnp.where(grp_d == 0, idx[0], idx[1])
    s = jnp.where(grp_k == 0, _dot_nt(qb, k0_ref[...].astype(BF16)), _dot_nt(qb, k1_ref[...].astype(BF16)))
    tok = idx_k * SEL_BLOCK + lax.broadcasted_iota(jnp.int32, (NS_H, SEL_BLOCK), 1)
    valid = (idx_k < n_past_blocks) & (tok <= qpos)
    ks_new = jnp.where(grp_d == 0, new_ref[4:5, :], new_ref[5:6, :])
    vs_new = jnp.where(grp_d == 0, new_ref[6:7, :], new_ref[7:8, :])
    is_new = idx_d == n_past_blocks
    s_new = jnp.where(is_new, jnp.sum(q * ks_new, axis=-1, keepdims=True), NEG_INF)
    m_old = m_scr[...]
    m_new = jnp.maximum(jnp.maximum(m_old, jnp.max(jnp.where(valid, s, NEG_INF), axis=-1, keepdims=True)), s_new)
    alpha = jnp.exp(m_old - m_new)
    p = jnp.where(valid, jnp.exp(s - m_new[:, 0:SEL_BLOCK]), 0.0)
    p_new = jnp.where(is_new, jnp.exp(s_new - m_new), 0.0)
    pb = p.astype(BF16)
    pv = jnp.where(grp_d == 0, _dot(pb, v0_ref[...].astype(BF16)), _dot(pb, v1_ref[...].astype(BF16)))
    l_scr[...] = alpha * l_scr[...] + jnp.sum(p, axis=-1, keepdims=True) + p_new
    acc_scr[...] = alpha * acc_scr[...] + pv + p_new * vs_new
    m_scr[...] = m_new

    @pl.when(j == pl.num_programs(1) - 1)
    def _():
        o_ref[...] = acc_scr[...] / l_scr[...]


def nsa_sel_sample(q_r3, rows3, cache4, page_table, top_idx, layer, qpos):
    bs, n_pages = page_table.shape
    halves = PAGE_SIZE // SEL_BLOCK
    n_past_blocks = n_pages * halves

    def blk(col, g):
        def imap(b, j, pt, top):
            i = top[b, g * SEL_TOPN + j]
            return (layer, pt[b, jnp.minimum(i // halves, n_pages - 1)], i % halves, col + g)
        return pl.BlockSpec((None, None, SEL_BLOCK, NS_DH), imap)

    vec = pl.BlockSpec((None, NS_H, NS_DH), lambda b, j, pt, top: (b, 0, 0))
    grid_spec = pltpu.PrefetchScalarGridSpec(
        num_scalar_prefetch=2,
        grid=(bs, SEL_TOPN),
        in_specs=[vec, blk(2 * NS_KV, 0), blk(2 * NS_KV, 1), blk(3 * NS_KV, 0), blk(3 * NS_KV, 1), vec],
        out_specs=vec,
        scratch_shapes=[pltpu.VMEM((NS_H, NS_DH), F32)] * 3,
    )
    return pl.pallas_call(
        functools.partial(_sel_step_body, n_past_blocks, qpos),
        grid_spec=grid_spec,
        out_shape=jax.ShapeDtypeStruct((bs, NS_H, NS_DH), F32),
        compiler_params=_cparams("parallel", "arbitrary"),
        name="nsa_sel_sample",
    )(page_table, top_idx, q_r3, cache4, cache4, cache4, cache4, rows3)


def _win_step_body(pos0, qpos, q_ref, buf_ref, new_ref, o_ref):
    q = q_ref[...]
    qb = q.astype(BF16)
    wb = buf_ref.shape[0]
    grp_k = _row_group((NS_H, wb))
    grp_d = _row_group((NS_H, NS_DH))
    s = jnp.where(grp_k == 0, _dot_nt(qb, buf_ref[:, 0:NS_DH].astype(BF16)),
                  _dot_nt(qb, buf_ref[:, NS_DH:2 * NS_DH].astype(BF16)))
    kpos = pos0 + lax.broadcasted_iota(jnp.int32, (NS_H, wb), 1)
    valid = (kpos <= qpos) & (kpos > qpos - WINDOW) & (kpos >= 0)
    kw_new = jnp.where(grp_d == 0, new_ref[0:1, :], new_ref[1:2, :])
    vw_new = jnp.where(grp_d == 0, new_ref[2:3, :], new_ref[3:4, :])
    s_new = jnp.sum(q * kw_new, axis=-1, keepdims=True)
    mx = jnp.maximum(jnp.max(jnp.where(valid, s, NEG_INF), axis=-1, keepdims=True), s_new)
    p = jnp.where(valid, jnp.exp(s - mx), 0.0)
    p_new = jnp.exp(s_new - mx)
    den = jnp.sum(p, axis=-1, keepdims=True) + p_new
    pb = p.astype(BF16)
    pv = jnp.where(grp_d == 0, _dot(pb, buf_ref[:, 2 * NS_DH:3 * NS_DH].astype(BF16)),
                   _dot(pb, buf_ref[:, 3 * NS_DH:4 * NS_DH].astype(BF16)))
    o_ref[...] = (pv + p_new * vw_new) / den


def nsa_win_sample(q_r3, win_buf4, win3, layer, past_len, qpos):
    bs = q_r3.shape[0]
    wb = win_buf4.shape[2]
    return pl.pallas_call(
        functools.partial(_win_step_body, past_len - wb, qpos),
        grid=(bs,),
        in_specs=[pl.BlockSpec((None, NS_H, NS_DH), lambda b: (b, 0, 0)),
                  pl.BlockSpec((None, None, wb, 2 * NS_KV * NS_DH), lambda b: (layer, b, 0, 0)),
                  pl.BlockSpec((None, 2 * NS_KV, NS_DH), lambda b: (b, 0, 0))],
        out_specs=pl.BlockSpec((None, NS_H, NS_DH), lambda b: (b, 0, 0)),
        out_shape=jax.ShapeDtypeStruct((bs, NS_H, NS_DH), F32),
        compiler_params=_cparams("parallel"),
        name="nsa_win_sample",
    )(q_r3, win_buf4, win3)


def _project(x, norm_g, w_a, w_g, w_b):
    h = rmsnorm_rows(x, norm_g, BF16)
    return (matmul(h, w_a, name="in_proj_a"), matmul(h, w_g, name="in_proj_gate"), matmul(h, w_b, name="in_proj_b"))


def _merge_out(x, branches, z_b, w_up_b, w_out_b):
    d = x.shape[1]
    gated = merge_up(branches, z_b, w_up_b, d)
    return matmul(gated, w_out_b, res=x, name="out_proj")


def _per_group(a, width=NS_DH):
    return jnp.stack([a[:, g * width:(g + 1) * width] for g in range(NS_KV)])


def kernel(x_prompt, x_sample, mem_prompt, cache_nsa_kv, state_win_kv, state_hgrn, state_conv, cache_mem_kv,
           page_table, norm_g, final_norm_g, w_in, hg_lb_logits, hg_norm_g, cv_w, ns_pe, ns_cw1, ns_cw2,
           mem_norm_g, w_mem_kv, w_up, w_out):
    depth = w_in.shape[0]
    b_p, t, d = x_prompt.shape
    bs, ts = x_sample.shape[:2]
    assert b_p == 1 and ts == 1
    n_pages = page_table.shape[1]
    past_len = n_pages * PAGE_SIZE
    wb = state_win_kv.shape[2]
    mem_len = mem_prompt.shape[1]
    kvw = NS_KV * NS_DH

    s_lb = jax.nn.softmax(hg_lb_logits.astype(F32), axis=0)
    lower = jnp.cumsum(s_lb, axis=0) - s_lb[0]

    pos_p = jnp.arange(t, dtype=jnp.int32)
    qpos_s = past_len
    pos_s = jnp.full((bs,), qpos_s, jnp.int32)

    cache4 = cache_nsa_kv.reshape(depth, cache_nsa_kv.shape[1], PAGE_SIZE, 4 * kvw)
    win_buf4 = state_win_kv.reshape(depth, bs, wb, 2 * kvw)
    mem_cache4 = cache_mem_kv.reshape(depth, bs, mem_len, 2 * MEM_H * MEM_DH)

    nch_p = t // CMP_STRIDE
    nch_s = (past_len + 1) // CMP_STRIDE
    n_sel_s = -(-(past_len + 1) // SEL_BLOCK)

    xp = x_prompt.reshape(t, d)
    xs = x_sample.reshape(bs, d)
    rows_p, win_p, hg_p, cv_p, mkv_p = [], [], [], [], []
    rows_s, win_s, hg_s, cv_s = [], [], [], []
    for l in range(depth):
        w_l = w_in[l]
        w_a = w_l[:, :COL_A].astype(BF16)
        w_g = jnp.pad(w_l[:, COL_A:COL_B0], ((0, 0), (0, LANE - COL_GATE))).astype(BF16)
        w_b = w_l[:, COL_B0:].astype(BF16)
        w_up_b = w_up[l].astype(BF16)
        w_out_b = w_out[l].astype(BF16)

        z_a, z_g, z_b = _project(xp, norm_g[l], w_a, w_g, w_b)
        o_hg, st_hg = hgrn_prompt(z_a, lower[l], hg_norm_g[l])
        o_cv, st_cv = conv_prompt(z_a, cv_w[l], jnp.zeros((CV_K - 1, BR_W), F32))
        q_c, q_r, rows, win = nsa_prep(z_a, pos_p, BF16)
        x_k = _per_group(rows[:, 0:kvw]).reshape(NS_KV * nch_p, CMP_STRIDE * NS_DH)
        x_v = _per_group(rows[:, kvw:2 * kvw]).reshape(NS_KV * nch_p, CMP_STRIDE * NS_DH)
        cmp = compress(x_k, x_v, ns_pe[l], ns_cw1[l], ns_cw2[l], nch_p)
        kcmp = cmp[0].astype(BF16)
        vcmp_t = jnp.swapaxes(cmp[1], 1, 2).astype(BF16)
        o_cmp, sel = nsa_cmp_select(q_c, kcmp, vcmp_t, nch_p - 1)
        ks = _per_group(rows[:, 2 * kvw:3 * kvw]).astype(BF16)
        vs_t = jnp.swapaxes(_per_group(rows[:, 3 * kvw:4 * kvw]), 1, 2).astype(BF16)
        o_sel = nsa_sel_attn(q_r, ks, vs_t, sel)
        kw = _per_group(win[:, 0:kvw]).astype(BF16)
        vw_t = jnp.swapaxes(_per_group(win[:, kvw:2 * kvw]), 1, 2).astype(BF16)
        o_win = nsa_win_attn(q_r, kw, vw_t)
        o_ns = nsa_combine(o_cmp, o_sel, o_win, z_g, z_b)
        mem_h = rmsnorm_rows(mem_prompt.reshape(mem_len, d), mem_norm_g[l], BF16)
        mkv = matmul(mem_h, w_mem_kv[l].astype(BF16), name="mem_kv")
        o_mm = mem_attn_prompt(z_b, mkv)
        xp = _merge_out(xp, (o_hg, o_cv, o_ns, o_mm), z_b, w_up_b, w_out_b)
        rows_p.append(rows.reshape(1, t, 4, NS_KV, NS_DH))
        win_p.append(win[t - min(WINDOW, t):].reshape(1, min(WINDOW, t), 2, NS_KV, NS_DH))
        hg_p.append(st_hg.reshape(1, HG_H, HG_DK, HG_DV))
        cv_p.append(st_cv.reshape(1, CV_K - 1, BR_W))
        mkv_p.append(mkv.reshape(1, mem_len, 2, MEM_H, MEM_DH))

        z_a, z_g, z_b = _project(xs, norm_g[l], w_a, w_g, w_b)
        z_a3 = z_a.reshape(bs, COL_A // LANE, LANE)
        z_b3 = z_b.reshape(bs, z_b.shape[1] // LANE, LANE)
        o_hg, st_hg = hgrn_sample(z_a3, lower[l], hg_norm_g[l], state_hgrn[l])
        o_cv, st_cv = conv_sample(z_a, cv_w[l], state_conv[l])
        q_c, q_r, rows, win = nsa_prep(z_a, pos_s, F32)
        past = gather_cmp_rows(cache4, page_table, l)
        x_kv = past.reshape(2, NS_KV * bs * (past_len // CMP_STRIDE), CMP_STRIDE * NS_DH)
        cmp = compress(x_kv[0], x_kv[1], ns_pe[l], ns_cw1[l], ns_cw2[l], nch_s)
        cmp = cmp.reshape(2, NS_KV, bs, nch_s, NS_DH)
        q_c3 = q_c.reshape(bs, NS_H, NS_DH)
        q_r3 = q_r.reshape(bs, NS_H, NS_DH)
        o_cmp, psum = nsa_cmp_sample(q_c3, cmp[0], cmp[1], nch_s - 1, qpos_s)
        top = nsa_select_sample(psum.reshape(bs * NS_KV, nch_s), nch_s - 1, n_sel_s, qpos_s)
        top = top[:, :SEL_TOPN].reshape(bs, NS_KV * SEL_TOPN)
        o_sel = nsa_sel_sample(q_r3, rows.reshape(bs, 4 * NS_KV, NS_DH), cache4, page_table, top, l, qpos_s)
        o_win = nsa_win_sample(q_r3, win_buf4, win.reshape(bs, 2 * NS_KV, NS_DH), l, past_len, qpos_s)
        o_ns = nsa_combine(o_cmp.reshape(bs, BR_W), o_sel.reshape(bs, BR_W), o_win.reshape(bs, BR_W), z_g, z_b)
        o_mm = mem_attn_sample(z_b3, mem_cache4, l)
        branches = (o_hg.reshape(bs, BR_W).astype(BF16), o_cv.astype(BF16), o_ns, o_mm.reshape(bs, BR_W).astype(BF16))
        xs = _merge_out(xs, branches, z_b, w_up_b, w_out_b)
        rows_s.append(rows.reshape(bs, 1, 4, NS_KV, NS_DH))
        win_s.append(jnp.concatenate([state_win_kv[l][:, 1:], win.reshape(bs, 1, 2, NS_KV, NS_DH)], axis=1))
        hg_s.append(st_hg)
        cv_s.append(st_cv)

    y_prompt = rmsnorm_rows(xp, final_norm_g, F32).reshape(1, t, d)
    y_sample = rmsnorm_rows(xs, final_norm_g, F32).reshape(bs, 1, d)
    return (y_prompt, y_sample, jnp.stack(rows_p), jnp.stack(win_p), jnp.stack(hg_p), jnp.stack(cv_p),
            jnp.stack(mkv_p), jnp.stack(rows_s), jnp.stack(win_s), jnp.stack(hg_s), jnp.stack(cv_s))
```

```python
import functools

import numpy as np
import jax
import jax.numpy as jnp
from jax import lax
from jax.experimental import pallas as pl
from jax.experimental.pallas import tpu as pltpu

F32 = jnp.float32
BF16 = jnp.bfloat16

BR_W = 1024
N_BRANCH = 4
HG_H = 8
HG_DK = 128
HG_DV = 128
F_MIN = 1e-30
CV_K = 3
NS_H = 8
NS_KV = 2
NS_G = NS_H // NS_KV
NS_DH = 128
CMP_LEN = 32
CMP_STRIDE = 16
SEL_BLOCK = 64
SEL_TOPN = 16
WINDOW = 512
FORCE_BONUS = 100.0
ROPE_THETA = 500000.0
ROT_DIM = NS_DH // 4
MEM_H = 4
MEM_DH = 256
NORM_EPS = 1e-6
NEG_INF = -1e30
REMOVED = -3e38
LOG2_E = 1.4426950408889634
PAGE_SIZE = 128
ROWS_PER_TOKEN = 4 * NS_KV

COL_A = 10752
COL_GATE = 24
COL_B0 = COL_A + COL_GATE

LANE = 128
HG_CHUNK = 128
Q_TILE = 128
K_TILE = 512
PAGES_PER_STEP = 16
VMEM_LIMIT = 56 * 1024 * 1024


def _cparams(*sem):
    return pltpu.CompilerParams(dimension_semantics=sem, vmem_limit_bytes=VMEM_LIMIT)


def _pick(n, cands):
    for c in cands:
        if n % c == 0:
            return c
    return n


def _silu(x):
    return x * jax.nn.sigmoid(x)


def _dot(a, b):
    return jnp.dot(a, b, preferred_element_type=F32)


def _dot_nt(a, b):
    return lax.dot_general(a, b, (((1,), (1,)), ((), ())), preferred_element_type=F32)


def _split3(x):
    hi = x.astype(BF16)
    r1 = x - hi.astype(F32)
    mid = r1.astype(BF16)
    lo = (r1 - mid.astype(F32)).astype(BF16)
    return hi, mid, lo


def _norm_body(x_ref, g_ref, o_ref):
    x = x_ref[...]
    ms = jnp.mean(x * x, axis=-1, keepdims=True)
    o_ref[...] = (x * lax.rsqrt(ms + NORM_EPS) * g_ref[...]).astype(o_ref.dtype)


def rmsnorm_rows(x, g, out_dtype):
    m, d = x.shape
    tm = _pick(m, (256, 128, 64, 32, 16, 8))
    return pl.pallas_call(
        _norm_body,
        grid=(m // tm,),
        in_specs=[pl.BlockSpec((tm, d), lambda i: (i, 0)), pl.BlockSpec((1, d), lambda i: (0, 0))],
        out_specs=pl.BlockSpec((tm, d), lambda i: (i, 0)),
        out_shape=jax.ShapeDtypeStruct((m, d), out_dtype),
        compiler_params=_cparams("parallel"),
        name="rmsnorm",
    )(x, g.reshape(1, d).astype(F32))


def _mm_body(a_ref, b_ref, o_ref):
    o_ref[...] = _dot(a_ref[...].astype(BF16), b_ref[...]).astype(o_ref.dtype)


def _mm_res_body(a_ref, b_ref, r_ref, o_ref):
    o_ref[...] = r_ref[...] + _dot(a_ref[...].astype(BF16), b_ref[...])


def matmul(a, b, res=None, out_dtype=F32, name="matmul"):
    m, k = a.shape
    n = b.shape[1]
    tm = _pick(m, (1024, 512, 256, 128, 64, 32, 16, 8))
    tn = _pick(n, (512, 256, 128))
    in_specs = [pl.BlockSpec((tm, k), lambda i, j: (i, 0)), pl.BlockSpec((k, tn), lambda i, j: (0, j))]
    args = [a, b]
    body = _mm_body
    if res is not None:
        in_specs.append(pl.BlockSpec((tm, tn), lambda i, j: (i, j)))
        args.append(res)
        body = _mm_res_body
    return pl.pallas_call(
        body,
        grid=(m // tm, n // tn),
        in_specs=in_specs,
        out_specs=pl.BlockSpec((tm, tn), lambda i, j: (i, j)),
        out_shape=jax.ShapeDtypeStruct((m, n), out_dtype),
        compiler_params=_cparams("parallel", "parallel"),
        name=name,
    )(*args)


def _merge_body(b0, b1, b2, b3, g0, g1, g2, g3, w0, w1, w2, w3, o_ref):
    acc = jax.nn.sigmoid(g0[...]) * _dot(b0[...], w0[...])
    acc += jax.nn.sigmoid(g1[...]) * _dot(b1[...], w1[...])
    acc += jax.nn.sigmoid(g2[...]) * _dot(b2[...], w2[...])
    acc += jax.nn.sigmoid(g3[...]) * _dot(b3[...], w3[...])
    o_ref[...] = acc.astype(o_ref.dtype)


def merge_up(branches, z_b, w_up_bf16, d_model):
    m = branches[0].shape[0]
    tm = _pick(m, (512, 256, 128, 64, 32, 16))
    tn = 512
    nj = d_model // tn
    gate_col0 = (z_b.shape[1] - N_BRANCH * d_model) // tn
    br_specs = [pl.BlockSpec((tm, BR_W), lambda i, j: (i, 0)) for _ in range(N_BRANCH)]
    g_specs = [pl.BlockSpec((tm, tn), functools.partial(lambda i, j, n: (i, gate_col0 + n * nj + j), n=n))
               for n in range(N_BRANCH)]
    w_specs = [pl.BlockSpec((None, BR_W, tn), functools.partial(lambda i, j, n: (n, 0, j), n=n))
               for n in range(N_BRANCH)]
    return pl.pallas_call(
        _merge_body,
        grid=(m // tm, nj),
        in_specs=br_specs + g_specs + w_specs,
        out_specs=pl.BlockSpec((tm, tn), lambda i, j: (i, j)),
        out_shape=jax.ShapeDtypeStruct((m, d_model), BF16),
        compiler_params=_cparams("parallel", "parallel"),
        name="merge_up",
    )(*branches, z_b, z_b, z_b, z_b, w_up_bf16, w_up_bf16, w_up_bf16, w_up_bf16)


def _hgrn_consts(c):
    nlev = int(round(np.log2(c)))
    t = np.arange(c)[:, None]
    r = np.arange(c)[None, :]
    blocks = [r <= t]
    masks = [t == r]
    for lv in range(nlev):
        h = c >> (lv + 1)
        mid = (t // (2 * h)) * (2 * h) + h
        blocks.append(np.where(t >= mid, (r >= mid) & (r <= t), (r > t) & (r <= mid - 1)))
        same = (t // (2 * h)) == (r // (2 * h))
        masks.append(same & (t % (2 * h) >= h) & (r % (2 * h) < h))
    blocks.append(r > t)
    l_all = np.concatenate(blocks, axis=0).astype(np.float32)
    return jnp.asarray(l_all, BF16), jnp.asarray(np.stack(masks).astype(np.float32)), nlev


def _hgrn_body(nlev, q_ref, z_ref, v_ref, g_ref, lb_ref, ng_ref, l_ref, mask_ref, o_ref, s_ref, st_scr):
    c_idx = pl.program_id(1)
    c = q_ref.shape[0]

    @pl.when(c_idx == 0)
    def _():
        st_scr[...] = jnp.zeros_like(st_scr)

    lb = lb_ref[...]
    q = _silu(q_ref[...])
    z = z_ref[...]
    f = lb + (1.0 - lb) * jax.nn.sigmoid(z)
    logf = jnp.log(jnp.maximum(f, F_MIN))
    k = (1.0 - lb) * jax.nn.sigmoid(-z)
    v = v_ref[...]

    hi, mid, lo = _split3(logf)
    e3 = _dot(l_ref[...], jnp.concatenate([hi, mid, lo], axis=1))
    dk = q.shape[1]
    x = jnp.exp(e3[:, :dk] + (e3[:, dk:2 * dk] + e3[:, 2 * dk:]))
    eb = x[0:c]
    est = x[(nlev + 1) * c:(nlev + 2) * c]

    st = st_scr[...]
    inter = _dot_nt((q * eb).astype(BF16), st.astype(BF16))
    att = mask_ref[0] * _dot_nt(q.astype(BF16), k.astype(BF16))
    for lv in range(nlev):
        fac = x[(1 + lv) * c:(2 + lv) * c]
        att += mask_ref[1 + lv] * _dot_nt((q * fac).astype(BF16), (k * fac).astype(BF16))
    o = inter + _dot(att.astype(BF16), v.astype(BF16))
    o = o * lax.rsqrt(jnp.mean(o * o, axis=-1, keepdims=True) + NORM_EPS) * ng_ref[...]
    o_ref[...] = (o * _silu(g_ref[...])).astype(o_ref.dtype)

    st_new = st * eb[c - 1:c, :] + _dot(v.T.astype(BF16), (k * est).astype(BF16))
    st_scr[...] = st_new

    @pl.when(c_idx == pl.num_programs(1) - 1)
    def _():
        s_ref[...] = st_new.T


def hgrn_prompt(z_a, lb, norm_g):
    t = z_a.shape[0]
    c = HG_CHUNK
    l_all, masks, nlev = _hgrn_consts(c)

    def col(k):
        return pl.BlockSpec((c, LANE), functools.partial(lambda h, i, k: (i, k * HG_H + h), k=k))

    vec = pl.BlockSpec((None, 1, LANE), lambda h, i: (h, 0, 0))
    return pl.pallas_call(
        functools.partial(_hgrn_body, nlev),
        grid=(HG_H, t // c),
        in_specs=[col(0), col(1), col(2), col(3), vec, vec,
                  pl.BlockSpec(l_all.shape, lambda h, i: (0, 0)),
                  pl.BlockSpec(masks.shape, lambda h, i: (0, 0, 0))],
        out_specs=[pl.BlockSpec((c, LANE), lambda h, i: (i, h)),
                   pl.BlockSpec((None, HG_DK, HG_DV), lambda h, i: (h, 0, 0))],
        out_shape=[jax.ShapeDtypeStruct((t, BR_W), BF16),
                   jax.ShapeDtypeStruct((HG_H, HG_DK, HG_DV), F32)],
        scratch_shapes=[pltpu.VMEM((HG_DV, HG_DK), F32)],
        compiler_params=_cparams("parallel", "arbitrary"),
        name="hgrn_prompt",
    )(z_a, z_a, z_a, z_a, lb.reshape(HG_H, 1, LANE), norm_g.reshape(HG_H, 1, LANE), l_all, masks)


def _hgrn_step_body(q_ref, z_ref, v_ref, g_ref, lb_ref, ng_ref, s0_ref, o_ref, s_ref):
    lb = lb_ref[...]
    q = _silu(q_ref[...])
    z = z_ref[...]
    f = jnp.maximum(lb + (1.0 - lb) * jax.nn.sigmoid(z), F_MIN)
    k = (1.0 - lb) * jax.nn.sigmoid(-z)
    v = v_ref[...]
    rows = []
    for h in range(HG_H):
        def colb(a):
            return jnp.broadcast_to(a[h:h + 1, :], (HG_DK, HG_DK)).T
        s_new = colb(f) * s0_ref[h] + colb(k) * v[h:h + 1, :]
        s_ref[h] = s_new
        rows.append(jnp.sum(colb(q) * s_new, axis=0, keepdims=True))
    o = jnp.concatenate(rows, axis=0)
    o = o * lax.rsqrt(jnp.mean(o * o, axis=-1, keepdims=True) + NORM_EPS) * ng_ref[...]
    o_ref[...] = o * _silu(g_ref[...])


def hgrn_sample(z_a3, lb, norm_g, s0):
    bs = z_a3.shape[0]

    def grp(k):
        return pl.BlockSpec((None, HG_H, LANE), functools.partial(lambda b, k: (b, k, 0), k=k))

    vec = pl.BlockSpec((HG_H, LANE), lambda b: (0, 0))
    st = pl.BlockSpec((None, HG_H, HG_DK, HG_DV), lambda b: (b, 0, 0, 0))
    return pl.pallas_call(
        _hgrn_step_body,
        grid=(bs,),
        in_specs=[grp(0), grp(1), grp(2), grp(3), vec, vec, st],
        out_specs=[pl.BlockSpec((None, HG_H, LANE), lambda b: (b, 0, 0)), st],
        out_shape=[jax.ShapeDtypeStruct((bs, HG_H, LANE), F32),
                   jax.ShapeDtypeStruct((bs, HG_H, HG_DK, HG_DV), F32)],
        compiler_params=_cparams("parallel"),
        name="hgrn_sample",
    )(z_a3, z_a3, z_a3, z_a3, lb.reshape(HG_H, LANE), norm_g.reshape(HG_H, LANE), s0)


def _conv_body(u_ref, b_ref, c_ref, g_ref, w_ref, prev_ref, o_ref, last_ref, carry):
    @pl.when(pl.program_id(0) == 0)
    def _():
        carry[...] = prev_ref[...]

    v = c_ref[...] * u_ref[...]
    tm = v.shape[0]
    row = lax.broadcasted_iota(jnp.int32, v.shape, 0)
    p1 = carry[7:8, :]
    p2 = carry[6:7, :]
    v1 = jnp.where(row == 0, p1, pltpu.roll(v, 1, 0))
    v2 = jnp.where(row == 0, p2, jnp.where(row == 1, p1, pltpu.roll(v, 2, 0)))
    w = w_ref[...]
    y = w[0:1, :] * v2 + w[1:2, :] * v1 + w[2:3, :] * v
    o_ref[...] = (b_ref[...] * y * _silu(g_ref[...])).astype(o_ref.dtype)
    tail = v[tm - 8:tm, :]
    carry[...] = tail
    last_ref[...] = tail


def conv_prompt(z_a, w, prev):
    t = z_a.shape[0]
    tm = _pick(t, (256, 128, 64, 32, 16, 8))

    def col(k):
        return pl.BlockSpec((tm, BR_W), functools.partial(lambda i, k: (i, 4 + k), k=k))

    w8 = jnp.zeros((8, BR_W), F32).at[:CV_K].set(w.astype(F32))
    prev8 = jnp.zeros((8, BR_W), F32).at[8 - (CV_K - 1):].set(prev.astype(F32))
    full8 = pl.BlockSpec((8, BR_W), lambda i: (0, 0))
    o, last = pl.pallas_call(
        _conv_body,
        grid=(t // tm,),
        in_specs=[col(0), col(1), col(2), col(3), full8, full8],
        out_specs=[pl.BlockSpec((tm, BR_W), lambda i: (i, 0)), full8],
        out_shape=[jax.ShapeDtypeStruct((t, BR_W), BF16), jax.ShapeDtypeStruct((8, BR_W), F32)],
        scratch_shapes=[pltpu.VMEM((8, BR_W), F32)],
        compiler_params=_cparams("arbitrary"),
        name="conv_prompt",
    )(z_a, z_a, z_a, z_a, w8, prev8)
    return o, last[8 - (CV_K - 1):]


def _conv_step_body(u_ref, b_ref, c_ref, g_ref, w_ref, p0_ref, p1_ref, o_ref, v_ref):
    v = c_ref[...] * u_ref[...]
    w = w_ref[...]
    y = w[0:1, :] * p0_ref[...] + w[1:2, :] * p1_ref[...] + w[2:3, :] * v
    o_ref[...] = b_ref[...] * y * _silu(g_ref[...])
    v_ref[...] = v


def conv_sample(z_a, w, prev):
    bs = z_a.shape[0]

    def col(k):
        return pl.BlockSpec((bs, BR_W), functools.partial(lambda i, k: (0, 4 + k), k=k))

    w8 = jnp.zeros((8, BR_W), F32).at[:CV_K].set(w.astype(F32))
    full = pl.BlockSpec((bs, BR_W), lambda i: (0, 0))
    o, v = pl.pallas_call(
        _conv_step_body,
        grid=(1,),
        in_specs=[col(0), col(1), col(2), col(3), pl.BlockSpec((8, BR_W), lambda i: (0, 0)), full, full],
        out_specs=[full, full],
        out_shape=[jax.ShapeDtypeStruct((bs, BR_W), F32), jax.ShapeDtypeStruct((bs, BR_W), F32)],
        compiler_params=_cparams("arbitrary"),
        name="conv_sample",
    )(z_a, z_a, z_a, z_a, w8, prev[:, 0], prev[:, 1])
    return o, jnp.stack([prev[:, 1], v], axis=1)


def _rope_tables(pos):
    half = ROT_DIM // 2
    inv = ROPE_THETA ** (-2.0 * jnp.arange(half, dtype=F32) / ROT_DIM)
    ang = pos.astype(F32)[:, None] * inv[None, :]
    cos, sin = jnp.cos(ang), jnp.sin(ang)
    m = pos.shape[0]
    ones = jnp.ones((m, NS_DH - ROT_DIM), F32)
    zeros = jnp.zeros((m, NS_DH - ROT_DIM), F32)
    zh = jnp.zeros((m, half), F32)
    cos_t = jnp.concatenate([cos, cos, ones], axis=1)
    sin_a = jnp.concatenate([zh, sin, zeros], axis=1)
    sin_b = jnp.concatenate([-sin, zh, zeros], axis=1)
    return cos_t, sin_a, sin_b


def _rope(x, cos_t, sin_a, sin_b):
    n = x.shape[1] // NS_DH
    half = ROT_DIM // 2

    def tile(a):
        return a if n == 1 else jnp.concatenate([a] * n, axis=1)

    return (x * tile(cos_t) + pltpu.roll(x, half, 1) * tile(sin_a)
            + pltpu.roll(x, x.shape[1] - half, 1) * tile(sin_b))


def _nsa_prep_body(q_ref, cv_ref, ks_ref, vs_ref, kw_ref, vw_ref, cos_ref, sa_ref, sb_ref,
                   qc_ref, qr_ref, rows_ref, win_ref):
    cos_t, sin_a, sin_b = cos_ref[...], sa_ref[...], sb_ref[...]
    scale = NS_DH ** -0.5 * LOG2_E
    q = q_ref[...]
    qc_ref[...] = (q * scale).astype(qc_ref.dtype)
    qr_ref[...] = (_rope(q, cos_t, sin_a, sin_b) * scale).astype(qr_ref.dtype)
    kvw = NS_KV * NS_DH
    rows_ref[:, 0:2 * kvw] = cv_ref[...]
    rows_ref[:, 2 * kvw:3 * kvw] = _rope(ks_ref[...], cos_t, sin_a, sin_b)
    rows_ref[:, 3 * kvw:4 * kvw] = vs_ref[...]
    win_ref[:, 0:kvw] = _rope(kw_ref[...], cos_t, sin_a, sin_b)
    win_ref[:, kvw:2 * kvw] = vw_ref[...]


def nsa_prep(z_a, pos, q_dtype):
    m = z_a.shape[0]
    tm = _pick(m, (256, 128, 64, 32, 16, 8))
    kvw = NS_KV * NS_DH
    c0 = 8 * BR_W // kvw
    tabs = _rope_tables(pos)

    def col(k, width=kvw):
        return pl.BlockSpec((tm, width), lambda i: (i, k))

    tab = pl.BlockSpec((tm, NS_DH), lambda i: (i, 0))
    return pl.pallas_call(
        _nsa_prep_body,
        grid=(m // tm,),
        in_specs=[col(8, BR_W), col((c0 + 4) // 2, 2 * kvw), col(c0 + 6), col(c0 + 7), col(c0 + 8), col(c0 + 9),
                  tab, tab, tab],
        out_specs=[pl.BlockSpec((tm, BR_W), lambda i: (i, 0)), pl.BlockSpec((tm, BR_W), lambda i: (i, 0)),
                   pl.BlockSpec((tm, 4 * kvw), lambda i: (i, 0)), pl.BlockSpec((tm, 2 * kvw), lambda i: (i, 0))],
        out_shape=[jax.ShapeDtypeStruct((m, BR_W), q_dtype), jax.ShapeDtypeStruct((m, BR_W), q_dtype),
                   jax.ShapeDtypeStruct((m, 4 * kvw), F32), jax.ShapeDtypeStruct((m, 2 * kvw), F32)],
        compiler_params=_cparams("parallel"),
        name="nsa_prep",
    )(z_a, z_a, z_a, z_a, z_a, z_a, *tabs)


def _cmp_finish_body(n_valid, p_ref, pe_ref, w1_ref, w2_ref, o_ref):
    p = p_ref[...]
    n = p.shape[0]
    pe_term = _dot(pe_ref[...], w1_ref[...])[0:1, :]
    nxt = pltpu.roll(p[:, NS_DH:], n - 1, 0)
    hid = _silu(p[:, :NS_DH] + nxt + pe_term)
    out = _dot(hid.astype(BF16), w2_ref[...])
    row = lax.broadcasted_iota(jnp.int32, out.shape, 0)
    o_ref[...] = jnp.where(row < n_valid, out, 0.0)


def _cmp_weights(w1):
    kdim = CMP_STRIDE * NS_DH
    w1b = w1.astype(BF16)
    return jnp.concatenate([w1b[:, :kdim], w1b[:, kdim:]], axis=2)


def compress_partial_rows(x_k, x_v, w1):
    w1cat = _cmp_weights(w1)
    return jnp.stack([matmul(x_k, w1cat[0], name="cmp_partial_k"), matmul(x_v, w1cat[1], name="cmp_partial_v")])


def compress_finish(p, pe, w1, w2, nch):
    w1b = w1.astype(BF16)
    groups = p.shape[1] // nch
    pe8 = jnp.broadcast_to(pe.reshape(2, 1, CMP_LEN * NS_DH), (2, 8, CMP_LEN * NS_DH)).astype(BF16)
    out = pl.pallas_call(
        functools.partial(_cmp_finish_body, nch - 1),
        grid=(2, groups),
        in_specs=[pl.BlockSpec((None, nch, 2 * NS_DH), lambda a, g: (a, g, 0)),
                  pl.BlockSpec((None, 8, CMP_LEN * NS_DH), lambda a, g: (a, 0, 0)),
                  pl.BlockSpec((None, CMP_LEN * NS_DH, NS_DH), lambda a, g: (a, 0, 0)),
                  pl.BlockSpec((None, NS_DH, NS_DH), lambda a, g: (a, 0, 0))],
        out_specs=pl.BlockSpec((None, nch, NS_DH), lambda a, g: (a, g, 0)),
        out_shape=jax.ShapeDtypeStruct((2, groups * nch, NS_DH), F32),
        compiler_params=_cparams("parallel", "parallel"),
        name="cmp_finish",
    )(p, pe8, w1b, w2.astype(BF16))
    return out.reshape(2, groups, nch, NS_DH)


def _cmp_to_sel(n_cmp_pad, n_sel_pad, n_cmp, n_sel):
    cs = np.arange(n_cmp_pad)[:, None] * CMP_STRIDE
    ss = np.arange(n_sel_pad)[None, :] * SEL_BLOCK
    ov = np.minimum(cs + CMP_LEN, ss + SEL_BLOCK) - np.maximum(cs, ss)
    m = np.clip(ov, 0, None).astype(np.float32) / CMP_LEN
    m[n_cmp:, :] = 0.0
    m[:, n_sel:] = 0.0
    return m


def _stack_heads(q):
    return jnp.concatenate([q[:, r * NS_DH:(r + 1) * NS_DH] for r in range(NS_G)], axis=0)


def _tile_heads(a):
    return jnp.concatenate([a] * NS_G, axis=1)


def _store_heads_t(o_ref, o_t):
    tq = o_t.shape[1] // NS_G
    for r in range(NS_G):
        o_ref[:, r * NS_DH:(r + 1) * NS_DH] = o_t[:, r * tq:(r + 1) * tq].T


def _cmp_sel_body(n_cmp, q_ref, kc_ref, vct_ref, msel_ref, o_ref, sel_ref):
    tq = q_ref.shape[0]
    t0 = pl.program_id(0) * tq
    qs = _stack_heads(q_ref[...])
    s_t = _dot_nt(kc_ref[...], qs)
    ncp = s_t.shape[0]
    n_i = lax.broadcasted_iota(jnp.int32, (ncp, tq), 0)
    qpos = t0 + lax.broadcasted_iota(jnp.int32, (ncp, tq), 1)
    valid = _tile_heads((n_i * CMP_STRIDE + CMP_LEN - 1 <= qpos) & (n_i < n_cmp))
    sm = jnp.where(valid, s_t, NEG_INF)
    mx = jnp.max(sm, axis=0, keepdims=True)
    e = jnp.where(valid, jnp.exp2(sm - mx), 0.0)
    den = jnp.sum(e, axis=0, keepdims=True)
    p = e * (1.0 / jnp.where(den > 0.0, den, 1.0))
    _store_heads_t(o_ref, _dot(vct_ref[...], p.astype(BF16)))

    psum = p[:, 0:tq]
    for r in range(1, NS_G):
        psum += p[:, r * tq:(r + 1) * tq]
    hi, mid, lo = _split3(psum)
    i3 = _dot(msel_ref[...], jnp.concatenate([hi, mid, lo], axis=1))
    imp = i3[:, :tq] + i3[:, tq:2 * tq] + i3[:, 2 * tq:]
    nb = imp.shape[0]
    j_i = lax.broadcasted_iota(jnp.int32, (nb, tq), 0)
    cur = (t0 + lax.broadcasted_iota(jnp.int32, (nb, tq), 1)) // SEL_BLOCK
    forced = (j_i == 0) | (j_i == cur) | (j_i == cur - 1)
    imp = jnp.where(forced, imp + FORCE_BONUS, imp)
    imp = jnp.where(j_i <= cur, imp, NEG_INF)
    j_f = j_i.astype(F32)
    sel = jnp.zeros((nb, tq), F32)
    for _ in range(min(SEL_TOPN, nb)):
        mx = jnp.max(imp, axis=0, keepdims=True)
        first = jnp.min(jnp.where(imp == mx, j_f, float(nb)), axis=0, keepdims=True)
        hit = j_f == first
        sel = jnp.where(hit, 1.0, sel)
        imp = jnp.where(hit, REMOVED, imp)
    sel_ref[...] = jnp.where(j_i <= cur, sel, 0.0)


def nsa_cmp_select(q_c, kcmp, vcmp_t, n_cmp):
    t = q_c.shape[0]
    ncp = kcmp.shape[1]
    n_sel = t // SEL_BLOCK
    msel_t = jnp.asarray(_cmp_to_sel(ncp, n_sel, n_cmp, n_sel).T, BF16)
    gw = NS_G * NS_DH
    return pl.pallas_call(
        functools.partial(_cmp_sel_body, n_cmp),
        grid=(t // Q_TILE, NS_KV),
        in_specs=[pl.BlockSpec((Q_TILE, gw), lambda i, g: (i, g)),
                  pl.BlockSpec((None, ncp, NS_DH), lambda i, g: (g, 0, 0)),
                  pl.BlockSpec((None, NS_DH, ncp), lambda i, g: (g, 0, 0)),
                  pl.BlockSpec((n_sel, ncp), lambda i, g: (0, 0))],
        out_specs=[pl.BlockSpec((Q_TILE, gw), lambda i, g: (i, g)),
                   pl.BlockSpec((None, n_sel, Q_TILE), lambda i, g: (g, 0, i))],
        out_shape=[jax.ShapeDtypeStruct((t, BR_W), F32), jax.ShapeDtypeStruct((NS_KV, n_sel, t), F32)],
        compiler_params=_cparams("parallel", "parallel"),
        name="nsa_cmp_select",
    )(q_c, kcmp, vcmp_t, msel_t)


def _sel_attn_body(q_ref, k_ref, vt_ref, sel_ref, o_ref, m_scr, l_scr, acc_scr):
    tq = q_ref.shape[0]
    t0 = pl.program_id(0) * tq
    qs = _stack_heads(q_ref[...])
    m_scr[...] = jnp.full_like(m_scr, NEG_INF)
    l_scr[...] = jnp.zeros_like(l_scr)
    acc_scr[...] = jnp.zeros_like(acc_scr)
    bpt = K_TILE // SEL_BLOCK
    n_tiles = (t0 + tq + K_TILE - 1) // K_TILE

    def step(kt, carry):
        k0 = pl.multiple_of(kt * K_TILE, K_TILE)
        s_t = _dot_nt(k_ref[pl.ds(k0, K_TILE), :], qs)
        sel = sel_ref[pl.ds(pl.multiple_of(kt * bpt, bpt), bpt), :]
        sel = jnp.broadcast_to(sel[:, None, :], (bpt, SEL_BLOCK, tq)).reshape(K_TILE, tq)
        key = k0 + lax.broadcasted_iota(jnp.int32, (K_TILE, tq), 0)
        qpos = t0 + lax.broadcasted_iota(jnp.int32, (K_TILE, tq), 1)
        s_m = s_t + _tile_heads(jnp.where((sel > 0.5) & (key <= qpos), 0.0, NEG_INF))
        m_old = m_scr[...]
        m_new = jnp.maximum(m_old, jnp.max(s_m, axis=0, keepdims=True))
        alpha = jnp.exp2(m_old - m_new)
        p = jnp.exp2(s_m - m_new)
        l_scr[...] = alpha * l_scr[...] + jnp.sum(p, axis=0, keepdims=True)
        acc_scr[...] = alpha * acc_scr[...] + _dot(vt_ref[:, pl.ds(k0, K_TILE)], p.astype(BF16))
        m_scr[...] = m_new
        return carry

    lax.fori_loop(0, n_tiles, step, 0)
    den = l_scr[...]
    _store_heads_t(o_ref, acc_scr[...] * (1.0 / jnp.where(den > 0.0, den, 1.0)))


def nsa_sel_attn(q_r, ks, vs_t, sel):
    t = q_r.shape[0]
    n_sel = sel.shape[1]
    gw = NS_G * NS_DH
    return pl.pallas_call(
        _sel_attn_body,
        grid=(t // Q_TILE, NS_KV),
        in_specs=[pl.BlockSpec((Q_TILE, gw), lambda i, g: (i, g)),
                  pl.BlockSpec((None, t, NS_DH), lambda i, g: (g, 0, 0)),
                  pl.BlockSpec((None, NS_DH, t), lambda i, g: (g, 0, 0)),
                  pl.BlockSpec((None, n_sel, Q_TILE), lambda i, g: (g, 0, i))],
        out_specs=pl.BlockSpec((Q_TILE, gw), lambda i, g: (i, g)),
        out_shape=jax.ShapeDtypeStruct((t, BR_W), F32),
        scratch_shapes=[pltpu.VMEM((1, NS_G * Q_TILE), F32), pltpu.VMEM((1, NS_G * Q_TILE), F32),
                        pltpu.VMEM((NS_DH, NS_G * Q_TILE), F32)],
        compiler_params=_cparams("parallel", "parallel"),
        name="nsa_sel_attn",
    )(q_r, ks, vs_t, sel)


def _win_attn_body(span, q_ref, k_ref, vt_ref, o_ref):
    tq = q_ref.shape[0]
    t0 = pl.program_id(0) * tq
    qs = _stack_heads(q_ref[...])
    start = pl.multiple_of(jnp.maximum(t0 - WINDOW, 0), Q_TILE)
    s_t = _dot_nt(k_ref[pl.ds(start, span), :], qs)
    key = start + lax.broadcasted_iota(jnp.int32, (span, tq), 0)
    qpos = t0 + lax.broadcasted_iota(jnp.int32, (span, tq), 1)
    sm = s_t + _tile_heads(jnp.where((key <= qpos) & (key > qpos - WINDOW), 0.0, NEG_INF))
    mx = jnp.max(sm, axis=0, keepdims=True)
    e = jnp.exp2(sm - mx)
    den = jnp.sum(e, axis=0, keepdims=True)
    p = e * (1.0 / den)
    _store_heads_t(o_ref, _dot(vt_ref[:, pl.ds(start, span)], p.astype(BF16)))


def nsa_win_attn(q_r, kw, vw_t):
    t = q_r.shape[0]
    span = min(WINDOW + Q_TILE, t)
    gw = NS_G * NS_DH
    return pl.pallas_call(
        functools.partial(_win_attn_body, span),
        grid=(t // Q_TILE, NS_KV),
        in_specs=[pl.BlockSpec((Q_TILE, gw), lambda i, g: (i, g)),
                  pl.BlockSpec((None, t, NS_DH), lambda i, g: (g, 0, 0)),
                  pl.BlockSpec((None, NS_DH, t), lambda i, g: (g, 0, 0))],
        out_specs=pl.BlockSpec((Q_TILE, gw), lambda i, g: (i, g)),
        out_shape=jax.ShapeDtypeStruct((t, BR_W), F32),
        compiler_params=_cparams("parallel", "parallel"),
        name="nsa_win_attn",
    )(q_r, kw, vw_t)


def _gate_expand():
    e = np.zeros((3, LANE, BR_W), np.float32)
    for h in range(NS_H):
        for c in range(3):
            e[c, h * 3 + c, h * NS_DH:(h + 1) * NS_DH] = 1.0
    return jnp.asarray(e, BF16)


def _combine_body(oc_ref, os_ref, ow_ref, gate_ref, g_ref, e_ref, o_ref):
    gs = jax.nn.sigmoid(gate_ref[...])
    hi = gs.astype(BF16)
    lo = (gs - hi.astype(F32)).astype(BF16)

    def expand(c):
        return _dot(hi, e_ref[c]) + _dot(lo, e_ref[c])

    o = expand(0) * oc_ref[...] + expand(1) * os_ref[...] + expand(2) * ow_ref[...]
    o_ref[...] = (o * _silu(g_ref[...])).astype(o_ref.dtype)


def nsa_combine(o_cmp, o_sel, o_win, z_g, z_b):
    m = o_cmp.shape[0]
    tm = _pick(m, (256, 128, 64, 32, 16))
    row = pl.BlockSpec((tm, BR_W), lambda i: (i, 0))
    return pl.pallas_call(
        _combine_body,
        grid=(m // tm,),
        in_specs=[row, row, row, pl.BlockSpec((tm, LANE), lambda i: (i, 0)), row,
                  pl.BlockSpec((3, LANE, BR_W), lambda i: (0, 0, 0))],
        out_specs=row,
        out_shape=jax.ShapeDtypeStruct((m, BR_W), BF16),
        compiler_params=_cparams("parallel"),
        name="nsa_combine",
    )(o_cmp, o_sel, o_win, z_g, z_b, _gate_expand())


def _mem_attn_body(q_ref, g_ref, kv_ref, o_ref):
    scale = MEM_DH ** -0.5
    q = q_ref[...]
    for h in range(MEM_H):
        k = kv_ref[:, h * MEM_DH:(h + 1) * MEM_DH].astype(BF16)
        v = kv_ref[:, (MEM_H + h) * MEM_DH:(MEM_H + h + 1) * MEM_DH].astype(BF16)
        s = _dot_nt(q[:, h * MEM_DH:(h + 1) * MEM_DH].astype(BF16), k) * scale
        e = jnp.exp(s - jnp.max(s, axis=-1, keepdims=True))
        p = e * (1.0 / jnp.sum(e, axis=-1, keepdims=True))
        o = _dot(p.astype(BF16), v)
        sl = slice(h * MEM_DH, (h + 1) * MEM_DH)
        o_ref[:, sl] = (o * _silu(g_ref[:, sl])).astype(o_ref.dtype)


def mem_attn_prompt(z_b, mkv):
    t = z_b.shape[0]
    tq = _pick(t, (512, 256, 128, 64, 32, 16))
    return pl.pallas_call(
        _mem_attn_body,
        grid=(t // tq,),
        in_specs=[pl.BlockSpec((tq, BR_W), lambda i: (i, 1)), pl.BlockSpec((tq, BR_W), lambda i: (i, 2)),
                  pl.BlockSpec(mkv.shape, lambda i: (0, 0))],
        out_specs=pl.BlockSpec((tq, BR_W), lambda i: (i, 0)),
        out_shape=jax.ShapeDtypeStruct((t, BR_W), BF16),
        compiler_params=_cparams("parallel"),
        name="mem_attn_prompt",
    )(z_b, z_b, mkv)


def _mem_attn_step_body(q_ref, g_ref, kv_ref, o_ref):
    scale = MEM_DH ** -0.5
    q = q_ref[...]
    rows_per = MEM_DH // LANE
    out_rows = []
    for h in range(MEM_H):
        qh = jnp.concatenate([q[h * rows_per + i:h * rows_per + i + 1, :] for i in range(rows_per)], axis=1)
        qh = jnp.broadcast_to(qh, (8, MEM_DH)).astype(BF16)
        k = kv_ref[:, h * MEM_DH:(h + 1) * MEM_DH].astype(BF16)
        v = kv_ref[:, (MEM_H + h) * MEM_DH:(MEM_H + h + 1) * MEM_DH].astype(BF16)
        s = _dot_nt(qh, k) * scale
        e = jnp.exp(s - jnp.max(s, axis=-1, keepdims=True))
        p = e * (1.0 / jnp.sum(e, axis=-1, keepdims=True))
        o = _dot(p.astype(BF16), v)[0:1, :]
        out_rows += [o[:, i * LANE:(i + 1) * LANE] for i in range(rows_per)]
    o_ref[...] = jnp.concatenate(out_rows, axis=0) * _silu(g_ref[...])


def mem_attn_sample(z_b3, kv_cache, layer):
    bs = z_b3.shape[0]
    mem_len, width = kv_cache.shape[2], kv_cache.shape[3]
    return pl.pallas_call(
        _mem_attn_step_body,
        grid=(bs,),
        in_specs=[pl.BlockSpec((None, 8, LANE), lambda b: (b, 1, 0)), pl.BlockSpec((None, 8, LANE), lambda b: (b, 2, 0)),
                  pl.BlockSpec((None, None, mem_len, width), lambda b: (layer, b, 0, 0))],
        out_specs=pl.BlockSpec((None, 8, LANE), lambda b: (b, 0, 0)),
        out_shape=jax.ShapeDtypeStruct((bs, 8, LANE), F32),
        compiler_params=_cparams("parallel"),
        name="mem_attn_sample",
    )(z_b3, z_b3, kv_cache)


def _paged_partial_body(npg, pt_ref, *refs):
    del pt_ref
    page_refs, w_ref, o_ref = refs[:npg], refs[npg], refs[npg + 1]
    chunks = PAGE_SIZE // CMP_STRIDE
    o_ref[...] = jnp.zeros_like(o_ref)

    def step(s, carry):
        for c in range(2 * NS_KV):
            xs = jnp.concatenate(
                [r[pl.ds(s * ROWS_PER_TOKEN + c, chunks, stride=CMP_STRIDE * ROWS_PER_TOKEN), :] for r in page_refs],
                axis=0)
            o_ref[c] += _dot(xs.astype(BF16), w_ref[c // NS_KV, s])
        return carry

    lax.fori_loop(0, CMP_STRIDE, step, 0)


def compress_partial_paged(cache2, page_table, w1, layer, n_pool):
    bs, n_pages = page_table.shape
    npg = min(PAGES_PER_STEP, n_pages)
    assert n_pages % npg == 0
    chunks = PAGE_SIZE // CMP_STRIDE
    w = _cmp_weights(w1).reshape(2, CMP_STRIDE, NS_DH, 2 * NS_DH)

    def page_spec(i):
        return pl.BlockSpec((PAGE_SIZE * ROWS_PER_TOKEN, NS_DH),
                            lambda b, j, pt: (layer * n_pool + pt[b, j * npg + i], 0))

    grid_spec = pltpu.PrefetchScalarGridSpec(
        num_scalar_prefetch=1,
        grid=(bs, n_pages // npg),
        in_specs=[page_spec(i) for i in range(npg)] + [pl.BlockSpec(w.shape, lambda b, j, pt: (0, 0, 0, 0))],
        out_specs=pl.BlockSpec((2 * NS_KV, None, npg * chunks, 2 * NS_DH), lambda b, j, pt: (0, b, j, 0)),
    )
    out = pl.pallas_call(
        functools.partial(_paged_partial_body, npg),
        grid_spec=grid_spec,
        out_shape=jax.ShapeDtypeStruct((2 * NS_KV, bs, n_pages * chunks, 2 * NS_DH), F32),
        compiler_params=_cparams("parallel", "parallel"),
        name="cmp_partial_paged",
    )(page_table, *([cache2] * npg), w)
    return out.reshape(2, NS_KV * bs * n_pages * chunks, 2 * NS_DH)


def _row_group(shape):
    return lax.broadcasted_iota(jnp.int32, shape, 0) // NS_G


def _cmp_step_body(n_cmp, qpos, q_ref, kc_ref, vc_ref, o_ref, ps_ref):
    q = q_ref[...].astype(BF16)
    ncp = kc_ref.shape[1]
    s = [_dot_nt(q, kc_ref[g].astype(BF16)) for g in range(NS_KV)]
    grp = _row_group((NS_H, ncp))
    s = jnp.where(grp == 0, s[0], s[1])
    n_i = lax.broadcasted_iota(jnp.int32, (NS_H, ncp), 1)
    valid = (n_i * CMP_STRIDE + CMP_LEN - 1 <= qpos) & (n_i < n_cmp)
    sm = jnp.where(valid, s, NEG_INF)
    mx = jnp.max(sm, axis=-1, keepdims=True)
    e = jnp.where(valid, jnp.exp2(sm - mx), 0.0)
    den = jnp.sum(e, axis=-1, keepdims=True)
    p = e * (1.0 / jnp.where(den > 0.0, den, 1.0))
    pb = p.astype(BF16)
    o = [_dot(pb, vc_ref[g].astype(BF16)) for g in range(NS_KV)]
    o_ref[...] = jnp.where(_row_group((NS_H, NS_DH)) == 0, o[0], o[1])
    ps_ref[...] = jnp.concatenate(
        [jnp.sum(jnp.where(grp == g, p, 0.0), axis=0, keepdims=True) for g in range(NS_KV)], axis=0)


def nsa_cmp_sample(q_c3, kcmp, vcmp, n_cmp, qpos):
    bs = q_c3.shape[0]
    ncp = kcmp.shape[2]
    kv = pl.BlockSpec((NS_KV, None, ncp, NS_DH), lambda b: (0, b, 0, 0))
    return pl.pallas_call(
        functools.partial(_cmp_step_body, n_cmp, qpos),
        grid=(bs,),
        in_specs=[pl.BlockSpec((None, NS_H, NS_DH), lambda b: (b, 0, 0)), kv, kv],
        out_specs=[pl.BlockSpec((None, NS_H, NS_DH), lambda b: (b, 0, 0)),
                   pl.BlockSpec((None, NS_KV, ncp), lambda b: (b, 0, 0))],
        out_shape=[jax.ShapeDtypeStruct((bs, NS_H, NS_DH), F32), jax.ShapeDtypeStruct((bs, NS_KV, ncp), F32)],
        compiler_params=_cparams("parallel"),
        name="nsa_cmp_sample",
    )(q_c3, kcmp, vcmp)


def _select_body(n_sel, qpos, ps_ref, msel_ref, idx_ref):
    hi, mid, lo = _split3(ps_ref[...])
    imp = _dot(hi, msel_ref[...]) + _dot(mid, msel_ref[...]) + _dot(lo, msel_ref[...])
    j_i = lax.broadcasted_iota(jnp.int32, imp.shape, 1)
    cur = qpos // SEL_BLOCK
    forced = (j_i == 0) | (j_i == cur) | (j_i == cur - 1)
    imp = jnp.where(forced, imp + FORCE_BONUS, imp)
    imp = jnp.where(j_i <= cur, imp, NEG_INF)
    imp = jnp.where(j_i < n_sel, imp, REMOVED)
    j_f = j_i.astype(F32)
    col = lax.broadcasted_iota(jnp.int32, idx_ref.shape, 1)
    out = jnp.zeros(idx_ref.shape, F32)
    for it in range(min(SEL_TOPN, n_sel)):
        mx = jnp.max(imp, axis=-1, keepdims=True)
        first = jnp.min(jnp.where(imp == mx, j_f, float(imp.shape[1])), axis=-1, keepdims=True)
        out = jnp.where(col == it, first, out)
        imp = jnp.where(j_f == first, REMOVED, imp)
    idx_ref[...] = out.astype(jnp.int32)


def nsa_select_sample(psum2, n_cmp, n_sel, qpos):
    rows, ncp = psum2.shape
    nsp = -(-n_sel // LANE) * LANE
    msel = jnp.asarray(_cmp_to_sel(ncp, nsp, n_cmp, n_sel), BF16)
    return pl.pallas_call(
        functools.partial(_select_body, n_sel, qpos),
        grid=(1,),
        in_specs=[pl.BlockSpec((rows, ncp), lambda i: (0, 0)), pl.BlockSpec((ncp, nsp), lambda i: (0, 0))],
        out_specs=pl.BlockSpec((rows, LANE), lambda i: (0, 0)),
        out_shape=jax.ShapeDtypeStruct((rows, LANE), jnp.int32),
        compiler_params=_cparams("arbitrary"),
        name="nsa_select_sample",
    )(psum2, msel)


def _sel_step_body(n_past_blocks, qpos, pt_ref, top_ref, q_ref, c0_ref, c1_ref, new_ref,
                   o_ref, m_scr, l_scr, acc_scr):
    del pt_ref
    b = pl.program_id(0)
    j = pl.program_id(1)

    @pl.when(j == 0)
    def _():
        m_scr[...] = jnp.full_like(m_scr, NEG_INF)
        l_scr[...] = jnp.zeros_like(l_scr)
        acc_scr[...] = jnp.zeros_like(acc_scr)

    def slab(ref, slot, g):
        return ref[pl.ds(slot * NS_KV + g, SEL_BLOCK, stride=ROWS_PER_TOKEN), :].astype(BF16)

    q = q_ref[...]
    qb = q.astype(BF16)
    grp_k = _row_group((NS_H, SEL_BLOCK))
    grp_d = _row_group((NS_H, NS_DH))
    idx = [top_ref[b, g * SEL_TOPN + j] for g in range(NS_KV)]
    idx_k = jnp.where(grp_k == 0, idx[0], idx[1])
    idx_d = jnp.where(grp_d == 0, idx[0], idx[1])
    s = jnp.where(grp_k == 0, _dot_nt(qb, slab(c0_ref, 2, 0)), _dot_nt(qb, slab(c1_ref, 2, 1)))
    tok = idx_k * SEL_BLOCK + lax.broadcasted_iota(jnp.int32, (NS_H, SEL_BLOCK), 1)
    valid = (idx_k < n_past_blocks) & (tok <= qpos)
    ks_new = jnp.where(grp_d == 0, new_ref[4:5, :], new_ref[5:6, :])
    vs_new = jnp.where(grp_d == 0, new_ref[6:7, :], new_ref[7:8, :])
    is_new = idx_d == n_past_blocks
    s_new = jnp.where(is_new, jnp.sum(q * ks_new, axis=-1, keepdims=True), NEG_INF)
    m_old = m_scr[...]
    m_new = jnp.maximum(jnp.maximum(m_old, jnp.max(jnp.where(valid, s, NEG_INF), axis=-1, keepdims=True)), s_new)
    alpha = jnp.exp2(m_old - m_new)
    p = jnp.where(valid, jnp.exp2(s - m_new[:, 0:SEL_BLOCK]), 0.0)
    p_new = jnp.where(is_new, jnp.exp2(s_new - m_new), 0.0)
    pb = p.astype(BF16)
    pv = jnp.where(grp_d == 0, _dot(pb, slab(c0_ref, 3, 0)), _dot(pb, slab(c1_ref, 3, 1)))
    l_scr[...] = alpha * l_scr[...] + jnp.sum(p, axis=-1, keepdims=True) + p_new
    acc_scr[...] = alpha * acc_scr[...] + pv + p_new * vs_new
    m_scr[...] = m_new

    @pl.when(j == pl.num_programs(1) - 1)
    def _():
        o_ref[...] = acc_scr[...] / l_scr[...]


def nsa_sel_sample(q_r3, rows3, cache2, page_table, top_idx, layer, n_pool, qpos):
    bs, n_pages = page_table.shape
    halves = PAGE_SIZE // SEL_BLOCK
    n_past_blocks = n_pages * halves

    def blk(g):
        def imap(b, j, pt, top):
            i = top[b, g * SEL_TOPN + j]
            page = pt[b, jnp.minimum(i // halves, n_pages - 1)]
            return ((layer * n_pool + page) * halves + i % halves, 0)
        return pl.BlockSpec((SEL_BLOCK * ROWS_PER_TOKEN, NS_DH), imap)

    vec = pl.BlockSpec((None, NS_H, NS_DH), lambda b, j, pt, top: (b, 0, 0))
    grid_spec = pltpu.PrefetchScalarGridSpec(
        num_scalar_prefetch=2,
        grid=(bs, SEL_TOPN),
        in_specs=[vec, blk(0), blk(1), vec],
        out_specs=vec,
        scratch_shapes=[pltpu.VMEM((NS_H, NS_DH), F32)] * 3,
    )
    return pl.pallas_call(
        functools.partial(_sel_step_body, n_past_blocks, qpos),
        grid_spec=grid_spec,
        out_shape=jax.ShapeDtypeStruct((bs, NS_H, NS_DH), F32),
        compiler_params=_cparams("parallel", "arbitrary"),
        name="nsa_sel_sample",
    )(page_table, top_idx, q_r3, cache2, cache2, rows3)


def _win_step_body(pos0, qpos, q_ref, buf_ref, new_ref, o_ref):
    q = q_ref[...]
    qb = q.astype(BF16)
    rows_per = 2 * NS_KV
    wb = buf_ref.shape[0] // rows_per

    def slab(c):
        return buf_ref[pl.ds(c, wb, stride=rows_per), :].astype(BF16)

    grp_k = _row_group((NS_H, wb))
    grp_d = _row_group((NS_H, NS_DH))
    s = jnp.where(grp_k == 0, _dot_nt(qb, slab(0)), _dot_nt(qb, slab(1)))
    kpos = pos0 + lax.broadcasted_iota(jnp.int32, (NS_H, wb), 1)
    valid = (kpos <= qpos) & (kpos > qpos - WINDOW) & (kpos >= 0)
    kw_new = jnp.where(grp_d == 0, new_ref[0:1, :], new_ref[1:2, :])
    vw_new = jnp.where(grp_d == 0, new_ref[2:3, :], new_ref[3:4, :])
    s_new = jnp.sum(q * kw_new, axis=-1, keepdims=True)
    mx = jnp.maximum(jnp.max(jnp.where(valid, s, NEG_INF), axis=-1, keepdims=True), s_new)
    p = jnp.where(valid, jnp.exp2(s - mx), 0.0)
    p_new = jnp.exp2(s_new - mx)
    den = jnp.sum(p, axis=-1, keepdims=True) + p_new
    pb = p.astype(BF16)
    pv = jnp.where(grp_d == 0, _dot(pb, slab(2)), _dot(pb, slab(3)))
    o_ref[...] = (pv + p_new * vw_new) / den


def nsa_win_sample(q_r3, win_buf2, win3, layer, wb, past_len, qpos):
    bs = q_r3.shape[0]
    return pl.pallas_call(
        functools.partial(_win_step_body, past_len - wb, qpos),
        grid=(bs,),
        in_specs=[pl.BlockSpec((None, NS_H, NS_DH), lambda b: (b, 0, 0)),
                  pl.BlockSpec((wb * 2 * NS_KV, NS_DH), lambda b: (layer * bs + b, 0)),
                  pl.BlockSpec((None, 2 * NS_KV, NS_DH), lambda b: (b, 0, 0))],
        out_specs=pl.BlockSpec((None, NS_H, NS_DH), lambda b: (b, 0, 0)),
        out_shape=jax.ShapeDtypeStruct((bs, NS_H, NS_DH), F32),
        compiler_params=_cparams("parallel"),
        name="nsa_win_sample",
    )(q_r3, win_buf2, win3)


def _project(x, norm_g, w_a, w_g, w_b):
    h = rmsnorm_rows(x, norm_g, BF16)
    return (matmul(h, w_a, name="in_proj_a"), matmul(h, w_g, name="in_proj_gate"), matmul(h, w_b, name="in_proj_b"))


def _merge_out(x, branches, z_b, w_up_b, w_out_b):
    d = x.shape[1]
    gated = merge_up(branches, z_b, w_up_b, d)
    return matmul(gated, w_out_b, res=x, name="out_proj")


def _per_group(a, width=NS_DH):
    return jnp.stack([a[:, g * width:(g + 1) * width] for g in range(NS_KV)])


def kernel(x_prompt, x_sample, mem_prompt, cache_nsa_kv, state_win_kv, state_hgrn, state_conv, cache_mem_kv,
           page_table, norm_g, final_norm_g, w_in, hg_lb_logits, hg_norm_g, cv_w, ns_pe, ns_cw1, ns_cw2,
           mem_norm_g, w_mem_kv, w_up, w_out):
    depth = w_in.shape[0]
    b_p, t, d = x_prompt.shape
    bs, ts = x_sample.shape[:2]
    assert b_p == 1 and ts == 1
    n_pages = page_table.shape[1]
    past_len = n_pages * PAGE_SIZE
    wb = state_win_kv.shape[2]
    mem_len = mem_prompt.shape[1]
    kvw = NS_KV * NS_DH

    s_lb = jax.nn.softmax(hg_lb_logits.astype(F32), axis=0)
    lower = jnp.cumsum(s_lb, axis=0) - s_lb[0]

    pos_p = jnp.arange(t, dtype=jnp.int32)
    qpos_s = past_len
    pos_s = jnp.full((bs,), qpos_s, jnp.int32)

    n_pool = cache_nsa_kv.shape[1]
    cache2 = cache_nsa_kv.reshape(depth * n_pool * PAGE_SIZE * ROWS_PER_TOKEN, NS_DH)
    win_buf2 = state_win_kv.reshape(depth * bs * wb * 2 * NS_KV, NS_DH)
    mem_cache4 = cache_mem_kv.reshape(depth, bs, mem_len, 2 * MEM_H * MEM_DH)

    nch_p = t // CMP_STRIDE
    nch_s = (past_len + 1) // CMP_STRIDE
    n_sel_s = -(-(past_len + 1) // SEL_BLOCK)

    xp = x_prompt.reshape(t, d)
    xs = x_sample.reshape(bs, d)
    rows_p, win_p, hg_p, cv_p, mkv_p = [], [], [], [], []
    rows_s, win_s, hg_s, cv_s = [], [], [], []
    for l in range(depth):
        w_l = w_in[l]
        w_a = w_l[:, :COL_A].astype(BF16)
        w_g = jnp.pad(w_l[:, COL_A:COL_B0], ((0, 0), (0, LANE - COL_GATE))).astype(BF16)
        w_b = w_l[:, COL_B0:].astype(BF16)
        w_up_b = w_up[l].astype(BF16)
        w_out_b = w_out[l].astype(BF16)

        z_a, z_g, z_b = _project(xp, norm_g[l], w_a, w_g, w_b)
        o_hg, st_hg = hgrn_prompt(z_a, lower[l], hg_norm_g[l])
        o_cv, st_cv = conv_prompt(z_a, cv_w[l], jnp.zeros((CV_K - 1, BR_W), F32))
        q_c, q_r, rows, win = nsa_prep(z_a, pos_p, BF16)
        x_k = _per_group(rows[:, 0:kvw]).reshape(NS_KV * nch_p, CMP_STRIDE * NS_DH)
        x_v = _per_group(rows[:, kvw:2 * kvw]).reshape(NS_KV * nch_p, CMP_STRIDE * NS_DH)
        cmp = compress_finish(compress_partial_rows(x_k, x_v, ns_cw1[l]), ns_pe[l], ns_cw1[l], ns_cw2[l], nch_p)
        kcmp = cmp[0].astype(BF16)
        vcmp_t = jnp.swapaxes(cmp[1], 1, 2).astype(BF16)
        o_cmp, sel = nsa_cmp_select(q_c, kcmp, vcmp_t, nch_p - 1)
        ks = _per_group(rows[:, 2 * kvw:3 * kvw]).astype(BF16)
        vs_t = jnp.swapaxes(_per_group(rows[:, 3 * kvw:4 * kvw]), 1, 2).astype(BF16)
        o_sel = nsa_sel_attn(q_r, ks, vs_t, sel)
        kw = _per_group(win[:, 0:kvw]).astype(BF16)
        vw_t = jnp.swapaxes(_per_group(win[:, kvw:2 * kvw]), 1, 2).astype(BF16)
        o_win = nsa_win_attn(q_r, kw, vw_t)
        o_ns = nsa_combine(o_cmp, o_sel, o_win, z_g, z_b)
        mem_h = rmsnorm_rows(mem_prompt.reshape(mem_len, d), mem_norm_g[l], BF16)
        mkv = matmul(mem_h, w_mem_kv[l].astype(BF16), name="mem_kv")
        o_mm = mem_attn_prompt(z_b, mkv)
        xp = _merge_out(xp, (o_hg, o_cv, o_ns, o_mm), z_b, w_up_b, w_out_b)
        rows_p.append(rows.reshape(1, t, 4, NS_KV, NS_DH))
        win_p.append(win[t - min(WINDOW, t):].reshape(1, min(WINDOW, t), 2, NS_KV, NS_DH))
        hg_p.append(st_hg.reshape(1, HG_H, HG_DK, HG_DV))
        cv_p.append(st_cv.reshape(1, CV_K - 1, BR_W))
        mkv_p.append(mkv.reshape(1, mem_len, 2, MEM_H, MEM_DH))

        z_a, z_g, z_b = _project(xs, norm_g[l], w_a, w_g, w_b)
        z_a3 = z_a.reshape(bs, COL_A // LANE, LANE)
        z_b3 = z_b.reshape(bs, z_b.shape[1] // LANE, LANE)
        o_hg, st_hg = hgrn_sample(z_a3, lower[l], hg_norm_g[l], state_hgrn[l])
        o_cv, st_cv = conv_sample(z_a, cv_w[l], state_conv[l])
        q_c, q_r, rows, win = nsa_prep(z_a, pos_s, F32)
        part = compress_partial_paged(cache2, page_table, ns_cw1[l], l, n_pool)
        cmp = compress_finish(part, ns_pe[l], ns_cw1[l], ns_cw2[l], nch_s)
        cmp = cmp.reshape(2, NS_KV, bs, nch_s, NS_DH)
        q_c3 = q_c.reshape(bs, NS_H, NS_DH)
        q_r3 = q_r.reshape(bs, NS_H, NS_DH)
        o_cmp, psum = nsa_cmp_sample(q_c3, cmp[0], cmp[1], nch_s - 1, qpos_s)
        top = nsa_select_sample(psum.reshape(bs * NS_KV, nch_s), nch_s - 1, n_sel_s, qpos_s)
        top = top[:, :SEL_TOPN].reshape(bs, NS_KV * SEL_TOPN)
        o_sel = nsa_sel_sample(q_r3, rows.reshape(bs, 4 * NS_KV, NS_DH), cache2, page_table, top, l, n_pool, qpos_s)
        o_win = nsa_win_sample(q_r3, win_buf2, win.reshape(bs, 2 * NS_KV, NS_DH), l, wb, past_len, qpos_s)
        o_ns = nsa_combine(o_cmp.reshape(bs, BR_W), o_sel.reshape(bs, BR_W), o_win.reshape(bs, BR_W), z_g, z_b)
        o_mm = mem_attn_sample(z_b3, mem_cache4, l)
        branches = (o_hg.reshape(bs, BR_W).astype(BF16), o_cv.astype(BF16), o_ns, o_mm.reshape(bs, BR_W).astype(BF16))
        xs = _merge_out(xs, branches, z_b, w_up_b, w_out_b)
        rows_s.append(rows.reshape(bs, 1, 4, NS_KV, NS_DH))
        win_s.append(jnp.concatenate([state_win_kv[l][:, 1:], win.reshape(bs, 1, 2, NS_KV, NS_DH)], axis=1))
        hg_s.append(st_hg)
        cv_s.append(st_cv)

    y_prompt = rmsnorm_rows(xp, final_norm_g, F32).reshape(1, t, d)
    y_sample = rmsnorm_rows(xs, final_norm_g, F32).reshape(bs, 1, d)
    return (y_prompt, y_sample, jnp.stack(rows_p), jnp.stack(win_p), jnp.stack(hg_p), jnp.stack(cv_p),
            jnp.stack(mkv_p), jnp.stack(rows_s), jnp.stack(win_s), jnp.stack(hg_s), jnp.stack(cv_s))
```

```python
import functools

import numpy as np
import jax
import jax.numpy as jnp
from jax import lax
from jax.experimental import pallas as pl
from jax.experimental.pallas import tpu as pltpu

F32 = jnp.float32
BF16 = jnp.bfloat16

BR_W = 1024
N_BRANCH = 4
HG_H = 8
HG_DK = 128
HG_DV = 128
F_MIN = 1e-30
CV_K = 3
NS_H = 8
NS_KV = 2
NS_G = NS_H // NS_KV
NS_DH = 128
CMP_LEN = 32
CMP_STRIDE = 16
SEL_BLOCK = 64
SEL_TOPN = 16
WINDOW = 512
FORCE_BONUS = 100.0
ROPE_THETA = 500000.0
ROT_DIM = NS_DH // 4
MEM_H = 4
MEM_DH = 256
NORM_EPS = 1e-6
NEG_INF = -1e30
REMOVED = -3e38
LOG2_E = 1.4426950408889634
PAGE_SIZE = 128
ROWS_PER_TOKEN = 4 * NS_KV

COL_A = 10752
COL_GATE = 24
COL_B0 = COL_A + COL_GATE

LANE = 128
HG_CHUNK = 128
HG_HEADS_PER_STEP = 2
Q_TILE = 128
K_TILE = 512
PAGES_PER_STEP = 16
SEL_PER_STEP = 4
VMEM_LIMIT = 56 * 1024 * 1024


def _cparams(*sem):
    return pltpu.CompilerParams(dimension_semantics=sem, vmem_limit_bytes=VMEM_LIMIT)


def _pick(n, cands):
    for c in cands:
        if n % c == 0:
            return c
    return n


def _silu(x):
    return x * jax.nn.sigmoid(x)


def _dot(a, b):
    return jnp.dot(a, b, preferred_element_type=F32)


def _dot_nt(a, b):
    return lax.dot_general(a, b, (((1,), (1,)), ((), ())), preferred_element_type=F32)


def _split3(x):
    hi = x.astype(BF16)
    r1 = x - hi.astype(F32)
    mid = r1.astype(BF16)
    lo = (r1 - mid.astype(F32)).astype(BF16)
    return hi, mid, lo


def _norm_body(x_ref, g_ref, o_ref):
    x = x_ref[...]
    ms = jnp.mean(x * x, axis=-1, keepdims=True)
    o_ref[...] = (x * lax.rsqrt(ms + NORM_EPS) * g_ref[...]).astype(o_ref.dtype)


def rmsnorm_rows(x, g, out_dtype):
    m, d = x.shape
    tm = _pick(m, (256, 128, 64, 32, 16, 8))
    return pl.pallas_call(
        _norm_body,
        grid=(m // tm,),
        in_specs=[pl.BlockSpec((tm, d), lambda i: (i, 0)), pl.BlockSpec((1, d), lambda i: (0, 0))],
        out_specs=pl.BlockSpec((tm, d), lambda i: (i, 0)),
        out_shape=jax.ShapeDtypeStruct((m, d), out_dtype),
        compiler_params=_cparams("parallel"),
        name="rmsnorm",
    )(x, g.reshape(1, d).astype(F32))


def _mm_body(a_ref, b_ref, o_ref):
    o_ref[...] = _dot(a_ref[...].astype(BF16), b_ref[...]).astype(o_ref.dtype)


def _mm_res_body(a_ref, b_ref, r_ref, o_ref):
    o_ref[...] = r_ref[...] + _dot(a_ref[...].astype(BF16), b_ref[...])


def matmul(a, b, res=None, out_dtype=F32, name="matmul"):
    m, k = a.shape
    n = b.shape[1]
    tm = _pick(m, (1024, 512, 256, 128, 64, 32, 16, 8))
    tn = _pick(n, (512, 256, 128))
    in_specs = [pl.BlockSpec((tm, k), lambda i, j: (i, 0)), pl.BlockSpec((k, tn), lambda i, j: (0, j))]
    args = [a, b]
    body = _mm_body
    if res is not None:
        in_specs.append(pl.BlockSpec((tm, tn), lambda i, j: (i, j)))
        args.append(res)
        body = _mm_res_body
    return pl.pallas_call(
        body,
        grid=(m // tm, n // tn),
        in_specs=in_specs,
        out_specs=pl.BlockSpec((tm, tn), lambda i, j: (i, j)),
        out_shape=jax.ShapeDtypeStruct((m, n), out_dtype),
        compiler_params=_cparams("parallel", "parallel"),
        name=name,
    )(*args)


def _merge_body(b0, b1, b2, b3, g0, g1, g2, g3, w0, w1, w2, w3, o_ref):
    acc = jax.nn.sigmoid(g0[...]) * _dot(b0[...], w0[...])
    acc += jax.nn.sigmoid(g1[...]) * _dot(b1[...], w1[...])
    acc += jax.nn.sigmoid(g2[...]) * _dot(b2[...], w2[...])
    acc += jax.nn.sigmoid(g3[...]) * _dot(b3[...], w3[...])
    o_ref[...] = acc.astype(o_ref.dtype)


def merge_up(branches, z_b, w_up_bf16, d_model):
    m = branches[0].shape[0]
    tm = _pick(m, (512, 256, 128, 64, 32, 16))
    tn = 512
    nj = d_model // tn
    gate_col0 = (z_b.shape[1] - N_BRANCH * d_model) // tn
    br_specs = [pl.BlockSpec((tm, BR_W), lambda i, j: (i, 0)) for _ in range(N_BRANCH)]
    g_specs = [pl.BlockSpec((tm, tn), functools.partial(lambda i, j, n: (i, gate_col0 + n * nj + j), n=n))
               for n in range(N_BRANCH)]
    w_specs = [pl.BlockSpec((None, BR_W, tn), functools.partial(lambda i, j, n: (n, 0, j), n=n))
               for n in range(N_BRANCH)]
    return pl.pallas_call(
        _merge_body,
        grid=(m // tm, nj),
        in_specs=br_specs + g_specs + w_specs,
        out_specs=pl.BlockSpec((tm, tn), lambda i, j: (i, j)),
        out_shape=jax.ShapeDtypeStruct((m, d_model), BF16),
        compiler_params=_cparams("parallel", "parallel"),
        name="merge_up",
    )(*branches, z_b, z_b, z_b, z_b, w_up_bf16, w_up_bf16, w_up_bf16, w_up_bf16)


def _hgrn_consts(c):
    nlev = int(round(np.log2(c)))
    t = np.arange(c)[:, None]
    r = np.arange(c)[None, :]
    blocks = [r <= t]
    masks = [t == r]
    for lv in range(nlev):
        h = c >> (lv + 1)
        mid = (t // (2 * h)) * (2 * h) + h
        blocks.append(np.where(t >= mid, (r >= mid) & (r <= t), (r > t) & (r <= mid - 1)))
        same = (t // (2 * h)) == (r // (2 * h))
        masks.append(same & (t % (2 * h) >= h) & (r % (2 * h) < h))
    blocks.append(r > t)
    l_all = np.concatenate(blocks, axis=0).astype(np.float32)
    return jnp.asarray(l_all, BF16), jnp.asarray(np.stack(masks).astype(np.float32)), nlev


def _hgrn_body(nlev, q_ref, z_ref, v_ref, g_ref, lb_ref, ng_ref, l_ref, mask_ref, o_ref, s_ref, st_scr):
    c_idx = pl.program_id(1)
    c = q_ref.shape[0]

    @pl.when(c_idx == 0)
    def _():
        st_scr[...] = jnp.zeros_like(st_scr)

    for hh in range(HG_HEADS_PER_STEP):
        sl = slice(hh * HG_DK, (hh + 1) * HG_DK)
        lb = lb_ref[:, sl]
        q = _silu(q_ref[:, sl])
        z = z_ref[:, sl]
        f = lb + (1.0 - lb) * jax.nn.sigmoid(z)
        logf = jnp.log(jnp.maximum(f, F_MIN))
        k = (1.0 - lb) * jax.nn.sigmoid(-z)
        v = v_ref[:, sl]

        hi, mid, lo = _split3(logf)
        e3 = _dot(l_ref[...], jnp.concatenate([hi, mid, lo], axis=1))
        dk = HG_DK
        x = jnp.exp(e3[:, :dk] + (e3[:, dk:2 * dk] + e3[:, 2 * dk:]))
        eb = x[0:c]
        est = x[(nlev + 1) * c:(nlev + 2) * c]

        st = st_scr[hh]
        inter = _dot_nt((q * eb).astype(BF16), st.astype(BF16))
        att = mask_ref[0] * _dot_nt(q.astype(BF16), k.astype(BF16))
        for lv in range(nlev):
            fac = x[(1 + lv) * c:(2 + lv) * c]
            att += mask_ref[1 + lv] * _dot_nt((q * fac).astype(BF16), (k * fac).astype(BF16))
        o = inter + _dot(att.astype(BF16), v.astype(BF16))
        o = o * lax.rsqrt(jnp.mean(o * o, axis=-1, keepdims=True) + NORM_EPS) * ng_ref[:, sl]
        o_ref[:, sl] = (o * _silu(g_ref[:, sl])).astype(o_ref.dtype)

        st_scr[hh] = st * eb[c - 1:c, :] + _dot(v.T.astype(BF16), (k * est).astype(BF16))

    @pl.when(c_idx == pl.num_programs(1) - 1)
    def _():
        for hh in range(HG_HEADS_PER_STEP):
            s_ref[hh] = st_scr[hh].T


def hgrn_prompt(z_a, lb, norm_g):
    t = z_a.shape[0]
    c = HG_CHUNK
    l_all, masks, nlev = _hgrn_consts(c)
    hp = HG_HEADS_PER_STEP
    steps = HG_H // hp
    width = hp * HG_DK

    def col(k):
        return pl.BlockSpec((c, width), functools.partial(lambda h, i, k: (i, k * steps + h), k=k))

    vec = pl.BlockSpec((None, 1, width), lambda h, i: (h, 0, 0))
    return pl.pallas_call(
        functools.partial(_hgrn_body, nlev),
        grid=(steps, t // c),
        in_specs=[col(0), col(1), col(2), col(3), vec, vec,
                  pl.BlockSpec(l_all.shape, lambda h, i: (0, 0)),
                  pl.BlockSpec(masks.shape, lambda h, i: (0, 0, 0))],
        out_specs=[pl.BlockSpec((c, width), lambda h, i: (i, h)),
                   pl.BlockSpec((hp, HG_DK, HG_DV), lambda h, i: (h, 0, 0))],
        out_shape=[jax.ShapeDtypeStruct((t, BR_W), BF16),
                   jax.ShapeDtypeStruct((HG_H, HG_DK, HG_DV), F32)],
        scratch_shapes=[pltpu.VMEM((hp, HG_DV, HG_DK), F32)],
        compiler_params=_cparams("parallel", "arbitrary"),
        name="hgrn_prompt",
    )(z_a, z_a, z_a, z_a, lb.reshape(steps, 1, width), norm_g.reshape(steps, 1, width), l_all, masks)


def _hgrn_step_body(q_ref, z_ref, v_ref, g_ref, lb_ref, ng_ref, s0_ref, o_ref, s_ref):
    lb = lb_ref[...]
    q = _silu(q_ref[...])
    z = z_ref[...]
    f = jnp.maximum(lb + (1.0 - lb) * jax.nn.sigmoid(z), F_MIN)
    k = (1.0 - lb) * jax.nn.sigmoid(-z)
    v = v_ref[...]
    rows = []
    for h in range(HG_H):
        def colb(a):
            return jnp.broadcast_to(a[h:h + 1, :], (HG_DK, HG_DK)).T
        s_new = colb(f) * s0_ref[h] + colb(k) * v[h:h + 1, :]
        s_ref[h] = s_new
        rows.append(jnp.sum(colb(q) * s_new, axis=0, keepdims=True))
    o = jnp.concatenate(rows, axis=0)
    o = o * lax.rsqrt(jnp.mean(o * o, axis=-1, keepdims=True) + NORM_EPS) * ng_ref[...]
    o_ref[...] = o * _silu(g_ref[...])


def hgrn_sample(z_a3, lb, norm_g, s0):
    bs = z_a3.shape[0]

    def grp(k):
        return pl.BlockSpec((None, HG_H, LANE), functools.partial(lambda b, k: (b, k, 0), k=k))

    vec = pl.BlockSpec((HG_H, LANE), lambda b: (0, 0))
    st = pl.BlockSpec((None, HG_H, HG_DK, HG_DV), lambda b: (b, 0, 0, 0))
    return pl.pallas_call(
        _hgrn_step_body,
        grid=(bs,),
        in_specs=[grp(0), grp(1), grp(2), grp(3), vec, vec, st],
        out_specs=[pl.BlockSpec((None, HG_H, LANE), lambda b: (b, 0, 0)), st],
        out_shape=[jax.ShapeDtypeStruct((bs, HG_H, LANE), F32),
                   jax.ShapeDtypeStruct((bs, HG_H, HG_DK, HG_DV), F32)],
        compiler_params=_cparams("parallel"),
        name="hgrn_sample",
    )(z_a3, z_a3, z_a3, z_a3, lb.reshape(HG_H, LANE), norm_g.reshape(HG_H, LANE), s0)


def _conv_body(u_ref, b_ref, c_ref, g_ref, w_ref, prev_ref, o_ref, last_ref, carry):
    @pl.when(pl.program_id(0) == 0)
    def _():
        carry[...] = prev_ref[...]

    v = c_ref[...] * u_ref[...]
    tm = v.shape[0]
    row = lax.broadcasted_iota(jnp.int32, v.shape, 0)
    p1 = carry[7:8, :]
    p2 = carry[6:7, :]
    v1 = jnp.where(row == 0, p1, pltpu.roll(v, 1, 0))
    v2 = jnp.where(row == 0, p2, jnp.where(row == 1, p1, pltpu.roll(v, 2, 0)))
    w = w_ref[...]
    y = w[0:1, :] * v2 + w[1:2, :] * v1 + w[2:3, :] * v
    o_ref[...] = (b_ref[...] * y * _silu(g_ref[...])).astype(o_ref.dtype)
    tail = v[tm - 8:tm, :]
    carry[...] = tail
    last_ref[...] = tail


def conv_prompt(z_a, w, prev):
    t = z_a.shape[0]
    tm = _pick(t, (256, 128, 64, 32, 16, 8))

    def col(k):
        return pl.BlockSpec((tm, BR_W), functools.partial(lambda i, k: (i, 4 + k), k=k))

    w8 = jnp.zeros((8, BR_W), F32).at[:CV_K].set(w.astype(F32))
    prev8 = jnp.zeros((8, BR_W), F32).at[8 - (CV_K - 1):].set(prev.astype(F32))
    full8 = pl.BlockSpec((8, BR_W), lambda i: (0, 0))
    o, last = pl.pallas_call(
        _conv_body,
        grid=(t // tm,),
        in_specs=[col(0), col(1), col(2), col(3), full8, full8],
        out_specs=[pl.BlockSpec((tm, BR_W), lambda i: (i, 0)), full8],
        out_shape=[jax.ShapeDtypeStruct((t, BR_W), BF16), jax.ShapeDtypeStruct((8, BR_W), F32)],
        scratch_shapes=[pltpu.VMEM((8, BR_W), F32)],
        compiler_params=_cparams("arbitrary"),
        name="conv_prompt",
    )(z_a, z_a, z_a, z_a, w8, prev8)
    return o, last[8 - (CV_K - 1):]


def _conv_step_body(u_ref, b_ref, c_ref, g_ref, w_ref, p0_ref, p1_ref, o_ref, v_ref):
    v = c_ref[...] * u_ref[...]
    w = w_ref[...]
    y = w[0:1, :] * p0_ref[...] + w[1:2, :] * p1_ref[...] + w[2:3, :] * v
    o_ref[...] = b_ref[...] * y * _silu(g_ref[...])
    v_ref[...] = v


def conv_sample(z_a, w, prev):
    bs = z_a.shape[0]

    def col(k):
        return pl.BlockSpec((bs, BR_W), functools.partial(lambda i, k: (0, 4 + k), k=k))

    w8 = jnp.zeros((8, BR_W), F32).at[:CV_K].set(w.astype(F32))
    full = pl.BlockSpec((bs, BR_W), lambda i: (0, 0))
    o, v = pl.pallas_call(
        _conv_step_body,
        grid=(1,),
        in_specs=[col(0), col(1), col(2), col(3), pl.BlockSpec((8, BR_W), lambda i: (0, 0)), full, full],
        out_specs=[full, full],
        out_shape=[jax.ShapeDtypeStruct((bs, BR_W), F32), jax.ShapeDtypeStruct((bs, BR_W), F32)],
        compiler_params=_cparams("arbitrary"),
        name="conv_sample",
    )(z_a, z_a, z_a, z_a, w8, prev[:, 0], prev[:, 1])
    return o, jnp.stack([prev[:, 1], v], axis=1)


def _rope_tables(pos):
    half = ROT_DIM // 2
    inv = ROPE_THETA ** (-2.0 * jnp.arange(half, dtype=F32) / ROT_DIM)
    ang = pos.astype(F32)[:, None] * inv[None, :]
    cos, sin = jnp.cos(ang), jnp.sin(ang)
    m = pos.shape[0]
    ones = jnp.ones((m, NS_DH - ROT_DIM), F32)
    zeros = jnp.zeros((m, NS_DH - ROT_DIM), F32)
    zh = jnp.zeros((m, half), F32)
    cos_t = jnp.concatenate([cos, cos, ones], axis=1)
    sin_a = jnp.concatenate([zh, sin, zeros], axis=1)
    sin_b = jnp.concatenate([-sin, zh, zeros], axis=1)
    return cos_t, sin_a, sin_b


def _rope(x, cos_t, sin_a, sin_b):
    n = x.shape[1] // NS_DH
    half = ROT_DIM // 2

    def tile(a):
        return a if n == 1 else jnp.concatenate([a] * n, axis=1)

    return (x * tile(cos_t) + pltpu.roll(x, half, 1) * tile(sin_a)
            + pltpu.roll(x, x.shape[1] - half, 1) * tile(sin_b))


def _nsa_prep_body(q_ref, cv_ref, ks_ref, vs_ref, kw_ref, vw_ref, cos_ref, sa_ref, sb_ref,
                   qc_ref, qr_ref, rows_ref, win_ref):
    cos_t, sin_a, sin_b = cos_ref[...], sa_ref[...], sb_ref[...]
    scale = NS_DH ** -0.5 * LOG2_E
    q = q_ref[...]
    qc_ref[...] = (q * scale).astype(qc_ref.dtype)
    qr_ref[...] = (_rope(q, cos_t, sin_a, sin_b) * scale).astype(qr_ref.dtype)
    kvw = NS_KV * NS_DH
    rows_ref[:, 0:2 * kvw] = cv_ref[...]
    rows_ref[:, 2 * kvw:3 * kvw] = _rope(ks_ref[...], cos_t, sin_a, sin_b)
    rows_ref[:, 3 * kvw:4 * kvw] = vs_ref[...]
    win_ref[:, 0:kvw] = _rope(kw_ref[...], cos_t, sin_a, sin_b)
    win_ref[:, kvw:2 * kvw] = vw_ref[...]


def nsa_prep(z_a, pos, q_dtype):
    m = z_a.shape[0]
    tm = _pick(m, (256, 128, 64, 32, 16, 8))
    kvw = NS_KV * NS_DH
    c0 = 8 * BR_W // kvw
    tabs = _rope_tables(pos)

    def col(k, width=kvw):
        return pl.BlockSpec((tm, width), lambda i: (i, k))

    tab = pl.BlockSpec((tm, NS_DH), lambda i: (i, 0))
    return pl.pallas_call(
        _nsa_prep_body,
        grid=(m // tm,),
        in_specs=[col(8, BR_W), col((c0 + 4) // 2, 2 * kvw), col(c0 + 6), col(c0 + 7), col(c0 + 8), col(c0 + 9),
                  tab, tab, tab],
        out_specs=[pl.BlockSpec((tm, BR_W), lambda i: (i, 0)), pl.BlockSpec((tm, BR_W), lambda i: (i, 0)),
                   pl.BlockSpec((tm, 4 * kvw), lambda i: (i, 0)), pl.BlockSpec((tm, 2 * kvw), lambda i: (i, 0))],
        out_shape=[jax.ShapeDtypeStruct((m, BR_W), q_dtype), jax.ShapeDtypeStruct((m, BR_W), q_dtype),
                   jax.ShapeDtypeStruct((m, 4 * kvw), F32), jax.ShapeDtypeStruct((m, 2 * kvw), F32)],
        compiler_params=_cparams("parallel"),
        name="nsa_prep",
    )(z_a, z_a, z_a, z_a, z_a, z_a, *tabs)


def _cmp_finish_body(n_valid, p_ref, pe_ref, w1_ref, w2_ref, o_ref):
    p = p_ref[...]
    n = p.shape[0]
    pe_term = _dot(pe_ref[...], w1_ref[...])[0:1, :]
    nxt = pltpu.roll(p[:, NS_DH:], n - 1, 0)
    hid = _silu(p[:, :NS_DH] + nxt + pe_term)
    out = _dot(hid.astype(BF16), w2_ref[...])
    row = lax.broadcasted_iota(jnp.int32, out.shape, 0)
    o_ref[...] = jnp.where(row < n_valid, out, 0.0)


def _cmp_weights(w1):
    kdim = CMP_STRIDE * NS_DH
    w1b = w1.astype(BF16)
    return jnp.concatenate([w1b[:, :kdim], w1b[:, kdim:]], axis=2)


def compress_partial_rows(x_k, x_v, w1):
    w1cat = _cmp_weights(w1)
    return jnp.stack([matmul(x_k, w1cat[0], name="cmp_partial_k"), matmul(x_v, w1cat[1], name="cmp_partial_v")])


def compress_finish(p, pe, w1, w2, nch):
    w1b = w1.astype(BF16)
    groups = p.shape[1] // nch
    pe8 = jnp.broadcast_to(pe.reshape(2, 1, CMP_LEN * NS_DH), (2, 8, CMP_LEN * NS_DH)).astype(BF16)
    out = pl.pallas_call(
        functools.partial(_cmp_finish_body, nch - 1),
        grid=(2, groups),
        in_specs=[pl.BlockSpec((None, nch, 2 * NS_DH), lambda a, g: (a, g, 0)),
                  pl.BlockSpec((None, 8, CMP_LEN * NS_DH), lambda a, g: (a, 0, 0)),
                  pl.BlockSpec((None, CMP_LEN * NS_DH, NS_DH), lambda a, g: (a, 0, 0)),
                  pl.BlockSpec((None, NS_DH, NS_DH), lambda a, g: (a, 0, 0))],
        out_specs=pl.BlockSpec((None, nch, NS_DH), lambda a, g: (a, g, 0)),
        out_shape=jax.ShapeDtypeStruct((2, groups * nch, NS_DH), F32),
        compiler_params=_cparams("parallel", "parallel"),
        name="cmp_finish",
    )(p, pe8, w1b, w2.astype(BF16))
    return out.reshape(2, groups, nch, NS_DH)


def _cmp_to_sel(n_cmp_pad, n_sel_pad, n_cmp, n_sel):
    cs = np.arange(n_cmp_pad)[:, None] * CMP_STRIDE
    ss = np.arange(n_sel_pad)[None, :] * SEL_BLOCK
    ov = np.minimum(cs + CMP_LEN, ss + SEL_BLOCK) - np.maximum(cs, ss)
    m = np.clip(ov, 0, None).astype(np.float32) / CMP_LEN
    m[n_cmp:, :] = 0.0
    m[:, n_sel:] = 0.0
    return m


def _stack_heads(q):
    return jnp.concatenate([q[:, r * NS_DH:(r + 1) * NS_DH] for r in range(NS_G)], axis=0)


def _tile_heads(a):
    return jnp.concatenate([a] * NS_G, axis=1)


def _store_heads_t(o_ref, o_t):
    tq = o_t.shape[1] // NS_G
    for r in range(NS_G):
        o_ref[:, r * NS_DH:(r + 1) * NS_DH] = o_t[:, r * tq:(r + 1) * tq].T


def _cmp_sel_body(n_cmp, q_ref, kc_ref, vct_ref, msel_ref, o_ref, bias_ref):
    tq = q_ref.shape[0]
    t0 = pl.program_id(0) * tq
    qs = _stack_heads(q_ref[...])
    s_t = _dot_nt(kc_ref[...], qs)
    ncp = s_t.shape[0]
    n_i = lax.broadcasted_iota(jnp.int32, (ncp, tq), 0)
    qpos = t0 + lax.broadcasted_iota(jnp.int32, (ncp, tq), 1)
    valid = _tile_heads((n_i * CMP_STRIDE + CMP_LEN - 1 <= qpos) & (n_i < n_cmp))
    sm = jnp.where(valid, s_t, NEG_INF)
    mx = jnp.max(sm, axis=0, keepdims=True)
    e = jnp.where(valid, jnp.exp2(sm - mx), 0.0)
    den = jnp.sum(e, axis=0, keepdims=True)
    p = e * (1.0 / jnp.where(den > 0.0, den, 1.0))
    _store_heads_t(o_ref, _dot(vct_ref[...], p.astype(BF16)))

    psum = p[:, 0:tq]
    for r in range(1, NS_G):
        psum += p[:, r * tq:(r + 1) * tq]
    hi, mid, lo = _split3(psum)
    i3 = _dot(msel_ref[...], jnp.concatenate([hi, mid, lo], axis=1))
    imp = i3[:, :tq] + i3[:, tq:2 * tq] + i3[:, 2 * tq:]
    nb = imp.shape[0]
    j_i = lax.broadcasted_iota(jnp.int32, (nb, tq), 0)
    cur = (t0 + lax.broadcasted_iota(jnp.int32, (nb, tq), 1)) // SEL_BLOCK
    forced = (j_i == 0) | (j_i == cur) | (j_i == cur - 1)
    imp = jnp.where(forced, imp + FORCE_BONUS, imp)
    imp = jnp.where(j_i <= cur, imp, NEG_INF)
    j_f = j_i.astype(F32)
    sel = jnp.zeros((nb, tq), F32)
    for _ in range(min(SEL_TOPN, nb)):
        mx = jnp.max(imp, axis=0, keepdims=True)
        first = jnp.min(jnp.where(imp == mx, j_f, float(nb)), axis=0, keepdims=True)
        hit = j_f == first
        sel = jnp.where(hit, 1.0, sel)
        imp = jnp.where(hit, REMOVED, imp)
    bias_t = jnp.where((sel > 0.5) & (j_i <= cur), 0.0, NEG_INF)
    if nb < LANE:
        bias_t = jnp.concatenate([bias_t, jnp.full((LANE - nb, tq), NEG_INF, F32)], axis=0)
    bias_ref[...] = bias_t.T.astype(bias_ref.dtype)


def nsa_cmp_select(q_c, kcmp, vcmp_t, n_cmp):
    t = q_c.shape[0]
    ncp = kcmp.shape[1]
    n_sel = t // SEL_BLOCK
    assert n_sel <= LANE
    msel_t = jnp.asarray(_cmp_to_sel(ncp, n_sel, n_cmp, n_sel).T, BF16)
    gw = NS_G * NS_DH
    return pl.pallas_call(
        functools.partial(_cmp_sel_body, n_cmp),
        grid=(t // Q_TILE, NS_KV),
        in_specs=[pl.BlockSpec((Q_TILE, gw), lambda i, g: (i, g)),
                  pl.BlockSpec((None, ncp, NS_DH), lambda i, g: (g, 0, 0)),
                  pl.BlockSpec((None, NS_DH, ncp), lambda i, g: (g, 0, 0)),
                  pl.BlockSpec((n_sel, ncp), lambda i, g: (0, 0))],
        out_specs=[pl.BlockSpec((Q_TILE, gw), lambda i, g: (i, g)),
                   pl.BlockSpec((None, Q_TILE, LANE), lambda i, g: (g, i, 0))],
        out_shape=[jax.ShapeDtypeStruct((t, BR_W), F32), jax.ShapeDtypeStruct((NS_KV, t, LANE), BF16)],
        compiler_params=_cparams("parallel", "parallel"),
        name="nsa_cmp_select",
    )(q_c, kcmp, vcmp_t, msel_t)


V_EXT_ROWS = NS_DH + 16


def _sel_attn_body(q_ref, bias_ref, k_ref, vt_ref, o_ref, m_scr, acc_scr):
    tq = q_ref.shape[0]
    t0 = pl.program_id(0) * tq
    gw = NS_G * NS_DH
    qx = [jnp.concatenate([_stack_heads(q_ref[:, g * gw:(g + 1) * gw]),
                           jnp.concatenate([bias_ref[g]] * NS_G, axis=0)], axis=1) for g in range(NS_KV)]
    m_scr[...] = jnp.full_like(m_scr, NEG_INF)
    acc_scr[...] = jnp.zeros_like(acc_scr)
    n_tiles = (t0 + tq + K_TILE - 1) // K_TILE

    def tile(kt, causal):
        k0 = pl.multiple_of(kt * K_TILE, K_TILE)
        for g in range(NS_KV):
            s_m = _dot_nt(k_ref[g, pl.ds(k0, K_TILE), :], qx[g])
            if causal:
                key = k0 + lax.broadcasted_iota(jnp.int32, (K_TILE, tq), 0)
                qpos = t0 + lax.broadcasted_iota(jnp.int32, (K_TILE, tq), 1)
                s_m = s_m + _tile_heads(jnp.where(key <= qpos, 0.0, NEG_INF))
            m_old = m_scr[g]
            m_new = jnp.maximum(m_old, jnp.max(s_m, axis=0, keepdims=True))
            alpha = jnp.exp2(m_old - m_new)
            p = jnp.exp2(s_m - m_new)
            acc_scr[g] = alpha * acc_scr[g] + _dot(vt_ref[g, :, pl.ds(k0, K_TILE)], p.astype(BF16))
            m_scr[g] = m_new

    def step(kt, carry):
        tile(kt, False)
        return carry

    lax.fori_loop(0, n_tiles - 1, step, 0)
    tile(n_tiles - 1, True)
    for g in range(NS_KV):
        acc = acc_scr[g]
        _store_heads_t(o_ref.at[:, g * gw:(g + 1) * gw], acc[0:NS_DH] * (1.0 / acc[NS_DH:NS_DH + 1]))


def nsa_sel_attn(q_r, bias, ks, vs_t):
    t = q_r.shape[0]
    gw = NS_G * NS_DH
    blk = jnp.arange(t, dtype=jnp.int32) // SEL_BLOCK
    onehot = (blk[:, None] == jnp.arange(LANE, dtype=jnp.int32)[None, :]).astype(BF16)
    ks_ext = jnp.concatenate([ks, jnp.broadcast_to(onehot, (NS_KV, t, LANE))], axis=2)
    ones = jnp.concatenate([jnp.ones((NS_KV, 1, t), BF16), jnp.zeros((NS_KV, V_EXT_ROWS - NS_DH - 1, t), BF16)], axis=1)
    vs_ext = jnp.concatenate([vs_t, ones], axis=1)
    return pl.pallas_call(
        _sel_attn_body,
        grid=(t // Q_TILE,),
        in_specs=[pl.BlockSpec((Q_TILE, NS_KV * gw), lambda i: (i, 0)),
                  pl.BlockSpec((NS_KV, Q_TILE, LANE), lambda i: (0, i, 0)),
                  pl.BlockSpec((NS_KV, t, NS_DH + LANE), lambda i: (0, 0, 0)),
                  pl.BlockSpec((NS_KV, V_EXT_ROWS, t), lambda i: (0, 0, 0))],
        out_specs=pl.BlockSpec((Q_TILE, NS_KV * gw), lambda i: (i, 0)),
        out_shape=jax.ShapeDtypeStruct((t, BR_W), F32),
        scratch_shapes=[pltpu.VMEM((NS_KV, 1, NS_G * Q_TILE), F32),
                        pltpu.VMEM((NS_KV, V_EXT_ROWS, NS_G * Q_TILE), F32)],
        compiler_params=_cparams("parallel"),
        name="nsa_sel_attn",
    )(q_r, bias, ks_ext, vs_ext)


def _win_attn_body(span, q_ref, k_ref, vt_ref, o_ref):
    tq = q_ref.shape[0]
    t0 = pl.program_id(0) * tq
    qs = _stack_heads(q_ref[...])
    start = pl.multiple_of(jnp.maximum(t0 - WINDOW, 0), Q_TILE)
    s_t = _dot_nt(k_ref[pl.ds(start, span), :], qs)
    key = start + lax.broadcasted_iota(jnp.int32, (span, tq), 0)
    qpos = t0 + lax.broadcasted_iota(jnp.int32, (span, tq), 1)
    sm = s_t + _tile_heads(jnp.where((key <= qpos) & (key > qpos - WINDOW), 0.0, NEG_INF))
    mx = jnp.max(sm, axis=0, keepdims=True)
    e = jnp.exp2(sm - mx)
    den = jnp.sum(e, axis=0, keepdims=True)
    p = e * (1.0 / den)
    _store_heads_t(o_ref, _dot(vt_ref[:, pl.ds(start, span)], p.astype(BF16)))


def nsa_win_attn(q_r, kw, vw_t):
    t = q_r.shape[0]
    span = min(WINDOW + Q_TILE, t)
    gw = NS_G * NS_DH
    return pl.pallas_call(
        functools.partial(_win_attn_body, span),
        grid=(t // Q_TILE, NS_KV),
        in_specs=[pl.BlockSpec((Q_TILE, gw), lambda i, g: (i, g)),
                  pl.BlockSpec((None, t, NS_DH), lambda i, g: (g, 0, 0)),
                  pl.BlockSpec((None, NS_DH, t), lambda i, g: (g, 0, 0))],
        out_specs=pl.BlockSpec((Q_TILE, gw), lambda i, g: (i, g)),
        out_shape=jax.ShapeDtypeStruct((t, BR_W), F32),
        compiler_params=_cparams("parallel", "parallel"),
        name="nsa_win_attn",
    )(q_r, kw, vw_t)


def _gate_expand():
    e = np.zeros((3, LANE, BR_W), np.float32)
    for h in range(NS_H):
        for c in range(3):
            e[c, h * 3 + c, h * NS_DH:(h + 1) * NS_DH] = 1.0
    return jnp.asarray(e, BF16)


def _combine_body(oc_ref, os_ref, ow_ref, gate_ref, g_ref, e_ref, o_ref):
    gs = jax.nn.sigmoid(gate_ref[...])
    hi = gs.astype(BF16)
    lo = (gs - hi.astype(F32)).astype(BF16)

    def expand(c):
        return _dot(hi, e_ref[c]) + _dot(lo, e_ref[c])

    o = expand(0) * oc_ref[...] + expand(1) * os_ref[...] + expand(2) * ow_ref[...]
    o_ref[...] = (o * _silu(g_ref[...])).astype(o_ref.dtype)


def nsa_combine(o_cmp, o_sel, o_win, z_g, z_b):
    m = o_cmp.shape[0]
    tm = _pick(m, (256, 128, 64, 32, 16))
    row = pl.BlockSpec((tm, BR_W), lambda i: (i, 0))
    return pl.pallas_call(
        _combine_body,
        grid=(m // tm,),
        in_specs=[row, row, row, pl.BlockSpec((tm, LANE), lambda i: (i, 0)), row,
                  pl.BlockSpec((3, LANE, BR_W), lambda i: (0, 0, 0))],
        out_specs=row,
        out_shape=jax.ShapeDtypeStruct((m, BR_W), BF16),
        compiler_params=_cparams("parallel"),
        name="nsa_combine",
    )(o_cmp, o_sel, o_win, z_g, z_b, _gate_expand())


def _mem_attn_body(q_ref, g_ref, kv_ref, o_ref):
    scale = MEM_DH ** -0.5
    q = q_ref[...]
    for h in range(MEM_H):
        k = kv_ref[:, h * MEM_DH:(h + 1) * MEM_DH].astype(BF16)
        v = kv_ref[:, (MEM_H + h) * MEM_DH:(MEM_H + h + 1) * MEM_DH].astype(BF16)
        s = _dot_nt(q[:, h * MEM_DH:(h + 1) * MEM_DH].astype(BF16), k) * scale
        e = jnp.exp(s - jnp.max(s, axis=-1, keepdims=True))
        p = e * (1.0 / jnp.sum(e, axis=-1, keepdims=True))
        o = _dot(p.astype(BF16), v)
        sl = slice(h * MEM_DH, (h + 1) * MEM_DH)
        o_ref[:, sl] = (o * _silu(g_ref[:, sl])).astype(o_ref.dtype)


def mem_attn_prompt(z_b, mkv):
    t = z_b.shape[0]
    tq = _pick(t, (512, 256, 128, 64, 32, 16))
    return pl.pallas_call(
        _mem_attn_body,
        grid=(t // tq,),
        in_specs=[pl.BlockSpec((tq, BR_W), lambda i: (i, 1)), pl.BlockSpec((tq, BR_W), lambda i: (i, 2)),
                  pl.BlockSpec(mkv.shape, lambda i: (0, 0))],
        out_specs=pl.BlockSpec((tq, BR_W), lambda i: (i, 0)),
        out_shape=jax.ShapeDtypeStruct((t, BR_W), BF16),
        compiler_params=_cparams("parallel"),
        name="mem_attn_prompt",
    )(z_b, z_b, mkv)


def _mem_attn_step_body(q_ref, g_ref, kv_ref, o_ref):
    scale = MEM_DH ** -0.5
    q = q_ref[...]
    rows_per = MEM_DH // LANE
    out_rows = []
    for h in range(MEM_H):
        qh = jnp.concatenate([q[h * rows_per + i:h * rows_per + i + 1, :] for i in range(rows_per)], axis=1)
        qh = jnp.broadcast_to(qh, (8, MEM_DH)).astype(BF16)
        k = kv_ref[:, h * MEM_DH:(h + 1) * MEM_DH].astype(BF16)
        v = kv_ref[:, (MEM_H + h) * MEM_DH:(MEM_H + h + 1) * MEM_DH].astype(BF16)
        s = _dot_nt(qh, k) * scale
        e = jnp.exp(s - jnp.max(s, axis=-1, keepdims=True))
        p = e * (1.0 / jnp.sum(e, axis=-1, keepdims=True))
        o = _dot(p.astype(BF16), v)[0:1, :]
        out_rows += [o[:, i * LANE:(i + 1) * LANE] for i in range(rows_per)]
    o_ref[...] = jnp.concatenate(out_rows, axis=0) * _silu(g_ref[...])


def mem_attn_sample(z_b3, kv_cache, layer):
    bs = z_b3.shape[0]
    mem_len, width = kv_cache.shape[2], kv_cache.shape[3]
    return pl.pallas_call(
        _mem_attn_step_body,
        grid=(bs,),
        in_specs=[pl.BlockSpec((None, 8, LANE), lambda b: (b, 1, 0)), pl.BlockSpec((None, 8, LANE), lambda b: (b, 2, 0)),
                  pl.BlockSpec((None, None, mem_len, width), lambda b: (layer, b, 0, 0))],
        out_specs=pl.BlockSpec((None, 8, LANE), lambda b: (b, 0, 0)),
        out_shape=jax.ShapeDtypeStruct((bs, 8, LANE), F32),
        compiler_params=_cparams("parallel"),
        name="mem_attn_sample",
    )(z_b3, z_b3, kv_cache)


def _paged_partial_body(npg, pt_ref, *refs):
    del pt_ref
    page_refs, w_ref, o_ref = refs[:npg], refs[npg], refs[npg + 1]
    chunks = PAGE_SIZE // CMP_STRIDE
    o_ref[...] = jnp.zeros_like(o_ref)

    unroll = 4

    def step(i, carry):
        for c in range(2 * NS_KV):
            acc = o_ref[c]
            for u in range(unroll):
                s = i * unroll + u
                xs = jnp.concatenate(
                    [r[pl.ds(s * ROWS_PER_TOKEN + c, chunks, stride=CMP_STRIDE * ROWS_PER_TOKEN), :]
                     for r in page_refs], axis=0)
                acc += _dot(xs.astype(BF16), w_ref[c // NS_KV, s])
            o_ref[c] = acc
        return carry

    lax.fori_loop(0, CMP_STRIDE // unroll, step, 0)


def compress_partial_paged(cache2, page_table, w1, layer, n_pool):
    bs, n_pages = page_table.shape
    npg = min(PAGES_PER_STEP, n_pages)
    assert n_pages % npg == 0
    chunks = PAGE_SIZE // CMP_STRIDE
    w = _cmp_weights(w1).reshape(2, CMP_STRIDE, NS_DH, 2 * NS_DH)

    def page_spec(i):
        return pl.BlockSpec((PAGE_SIZE * ROWS_PER_TOKEN, NS_DH),
                            lambda b, j, pt: (layer * n_pool + pt[b, j * npg + i], 0))

    grid_spec = pltpu.PrefetchScalarGridSpec(
        num_scalar_prefetch=1,
        grid=(bs, n_pages // npg),
        in_specs=[page_spec(i) for i in range(npg)] + [pl.BlockSpec(w.shape, lambda b, j, pt: (0, 0, 0, 0))],
        out_specs=pl.BlockSpec((2 * NS_KV, None, npg * chunks, 2 * NS_DH), lambda b, j, pt: (0, b, j, 0)),
    )
    out = pl.pallas_call(
        functools.partial(_paged_partial_body, npg),
        grid_spec=grid_spec,
        out_shape=jax.ShapeDtypeStruct((2 * NS_KV, bs, n_pages * chunks, 2 * NS_DH), F32),
        compiler_params=_cparams("parallel", "parallel"),
        name="cmp_partial_paged",
    )(page_table, *([cache2] * npg), w)
    return out.reshape(2, NS_KV * bs * n_pages * chunks, 2 * NS_DH)


def _row_group(shape):
    return lax.broadcasted_iota(jnp.int32, shape, 0) // NS_G


def _cmp_step_body(n_cmp, qpos, q_ref, kc_ref, vc_ref, o_ref, ps_ref):
    q = q_ref[...].astype(BF16)
    ncp = kc_ref.shape[1]
    s = [_dot_nt(q, kc_ref[g].astype(BF16)) for g in range(NS_KV)]
    grp = _row_group((NS_H, ncp))
    s = jnp.where(grp == 0, s[0], s[1])
    n_i = lax.broadcasted_iota(jnp.int32, (NS_H, ncp), 1)
    valid = (n_i * CMP_STRIDE + CMP_LEN - 1 <= qpos) & (n_i < n_cmp)
    sm = jnp.where(valid, s, NEG_INF)
    mx = jnp.max(sm, axis=-1, keepdims=True)
    e = jnp.where(valid, jnp.exp2(sm - mx), 0.0)
    den = jnp.sum(e, axis=-1, keepdims=True)
    p = e * (1.0 / jnp.where(den > 0.0, den, 1.0))
    pb = p.astype(BF16)
    o = [_dot(pb, vc_ref[g].astype(BF16)) for g in range(NS_KV)]
    o_ref[...] = jnp.where(_row_group((NS_H, NS_DH)) == 0, o[0], o[1])
    ps_ref[...] = jnp.concatenate(
        [jnp.sum(jnp.where(grp == g, p, 0.0), axis=0, keepdims=True) for g in range(NS_KV)], axis=0)


def nsa_cmp_sample(q_c3, kcmp, vcmp, n_cmp, qpos):
    bs = q_c3.shape[0]
    ncp = kcmp.shape[2]
    kv = pl.BlockSpec((NS_KV, None, ncp, NS_DH), lambda b: (0, b, 0, 0))
    return pl.pallas_call(
        functools.partial(_cmp_step_body, n_cmp, qpos),
        grid=(bs,),
        in_specs=[pl.BlockSpec((None, NS_H, NS_DH), lambda b: (b, 0, 0)), kv, kv],
        out_specs=[pl.BlockSpec((None, NS_H, NS_DH), lambda b: (b, 0, 0)),
                   pl.BlockSpec((None, NS_KV, ncp), lambda b: (b, 0, 0))],
        out_shape=[jax.ShapeDtypeStruct((bs, NS_H, NS_DH), F32), jax.ShapeDtypeStruct((bs, NS_KV, ncp), F32)],
        compiler_params=_cparams("parallel"),
        name="nsa_cmp_sample",
    )(q_c3, kcmp, vcmp)


def _select_body(n_sel, qpos, ps_ref, msel_ref, idx_ref):
    hi, mid, lo = _split3(ps_ref[...])
    imp = _dot(hi, msel_ref[...]) + _dot(mid, msel_ref[...]) + _dot(lo, msel_ref[...])
    j_i = lax.broadcasted_iota(jnp.int32, imp.shape, 1)
    cur = qpos // SEL_BLOCK
    forced = (j_i == 0) | (j_i == cur) | (j_i == cur - 1)
    imp = jnp.where(forced, imp + FORCE_BONUS, imp)
    imp = jnp.where(j_i <= cur, imp, NEG_INF)
    imp = jnp.where(j_i < n_sel, imp, REMOVED)
    j_f = j_i.astype(F32)
    col = lax.broadcasted_iota(jnp.int32, idx_ref.shape, 1)
    out = jnp.zeros(idx_ref.shape, F32)
    for it in range(min(SEL_TOPN, n_sel)):
        mx = jnp.max(imp, axis=-1, keepdims=True)
        first = jnp.min(jnp.where(imp == mx, j_f, float(imp.shape[1])), axis=-1, keepdims=True)
        out = jnp.where(col == it, first, out)
        imp = jnp.where(j_f == first, REMOVED, imp)
    idx_ref[...] = out.astype(jnp.int32)


def nsa_select_sample(psum2, n_cmp, n_sel, qpos):
    rows, ncp = psum2.shape
    nsp = -(-n_sel // LANE) * LANE
    msel = jnp.asarray(_cmp_to_sel(ncp, nsp, n_cmp, n_sel), BF16)
    return pl.pallas_call(
        functools.partial(_select_body, n_sel, qpos),
        grid=(1,),
        in_specs=[pl.BlockSpec((rows, ncp), lambda i: (0, 0)), pl.BlockSpec((ncp, nsp), lambda i: (0, 0))],
        out_specs=pl.BlockSpec((rows, LANE), lambda i: (0, 0)),
        out_shape=jax.ShapeDtypeStruct((rows, LANE), jnp.int32),
        compiler_params=_cparams("arbitrary"),
        name="nsa_select_sample",
    )(psum2, msel)


def _sel_step_body(n_past_blocks, qpos, pt_ref, top_ref, q_ref, *refs):
    del pt_ref
    nblk = NS_KV * SEL_PER_STEP
    blk_refs, new_ref, o_ref, m_scr, l_scr, acc_scr = refs[:nblk], *refs[nblk:nblk + 5]
    b = pl.program_id(0)
    j = pl.program_id(1)

    @pl.when(j == 0)
    def _():
        m_scr[...] = jnp.full_like(m_scr, NEG_INF)
        l_scr[...] = jnp.zeros_like(l_scr)
        acc_scr[...] = jnp.zeros_like(acc_scr)

    def slab(ref, slot, g):
        return ref[pl.ds(slot * NS_KV + g, SEL_BLOCK, stride=ROWS_PER_TOKEN), :].astype(BF16)

    q = q_ref[...]
    qb = q.astype(BF16)
    nk = SEL_PER_STEP * SEL_BLOCK
    grp_k = _row_group((NS_H, nk))
    grp_d = _row_group((NS_H, NS_DH))
    lane_blk = lax.broadcasted_iota(jnp.int32, (NS_H, nk), 1) // SEL_BLOCK
    s_g, idx_g, new_g = [], [], []
    for g in range(NS_KV):
        k_all = jnp.concatenate([slab(blk_refs[g * SEL_PER_STEP + k], 2, g) for k in range(SEL_PER_STEP)], axis=0)
        s_g.append(_dot_nt(qb, k_all))
        ids = [top_ref[b, g * SEL_TOPN + j * SEL_PER_STEP + k] for k in range(SEL_PER_STEP)]
        idx = jnp.zeros((NS_H, nk), jnp.int32)
        n_new = jnp.int32(0)
        for k, i in enumerate(ids):
            idx = jnp.where(lane_blk == k, i, idx)
            n_new = n_new + (i == n_past_blocks).astype(jnp.int32)
        idx_g.append(idx)
        new_g.append(n_new)
    s = jnp.where(grp_k == 0, s_g[0], s_g[1])
    idx = jnp.where(grp_k == 0, idx_g[0], idx_g[1])
    tok = idx * SEL_BLOCK + lax.broadcasted_iota(jnp.int32, (NS_H, nk), 1) % SEL_BLOCK
    valid = (idx < n_past_blocks) & (tok <= qpos)
    ks_new = jnp.where(grp_d == 0, new_ref[4:5, :], new_ref[5:6, :])
    vs_new = jnp.where(grp_d == 0, new_ref[6:7, :], new_ref[7:8, :])
    is_new = jnp.where(grp_d == 0, new_g[0], new_g[1]) > 0
    s_new = jnp.where(is_new, jnp.sum(q * ks_new, axis=-1, keepdims=True), NEG_INF)
    m_old = m_scr[...]
    m_new = jnp.maximum(jnp.maximum(m_old, jnp.max(jnp.where(valid, s, NEG_INF), axis=-1, keepdims=True)), s_new)
    alpha = jnp.exp2(m_old - m_new)
    p = jnp.where(valid, jnp.exp2(s - m_new[:, 0:1]), 0.0)
    p_new = jnp.where(is_new, jnp.exp2(s_new - m_new), 0.0)
    pb = p.astype(BF16)
    pv_g = []
    for g in range(NS_KV):
        v_all = jnp.concatenate([slab(blk_refs[g * SEL_PER_STEP + k], 3, g) for k in range(SEL_PER_STEP)], axis=0)
        pv_g.append(_dot(pb, v_all))
    pv = jnp.where(grp_d == 0, pv_g[0], pv_g[1])
    l_scr[...] = alpha * l_scr[...] + jnp.sum(p, axis=-1, keepdims=True) + p_new
    acc_scr[...] = alpha * acc_scr[...] + pv + p_new * vs_new
    m_scr[...] = m_new

    @pl.when(j == pl.num_programs(1) - 1)
    def _():
        o_ref[...] = acc_scr[...] / l_scr[...]


def nsa_sel_sample(q_r3, rows3, cache2, page_table, top_idx, layer, n_pool, qpos):
    bs, n_pages = page_table.shape
    halves = PAGE_SIZE // SEL_BLOCK
    n_past_blocks = n_pages * halves

    def blk(g, k):
        def imap(b, j, pt, top):
            i = top[b, g * SEL_TOPN + j * SEL_PER_STEP + k]
            page = pt[b, jnp.minimum(i // halves, n_pages - 1)]
            return ((layer * n_pool + page) * halves + i % halves, 0)
        return pl.BlockSpec((SEL_BLOCK * ROWS_PER_TOKEN, NS_DH), imap)

    vec = pl.BlockSpec((None, NS_H, NS_DH), lambda b, j, pt, top: (b, 0, 0))
    grid_spec = pltpu.PrefetchScalarGridSpec(
        num_scalar_prefetch=2,
        grid=(bs, SEL_TOPN // SEL_PER_STEP),
        in_specs=[vec] + [blk(g, k) for g in range(NS_KV) for k in range(SEL_PER_STEP)] + [vec],
        out_specs=vec,
        scratch_shapes=[pltpu.VMEM((NS_H, NS_DH), F32)] * 3,
    )
    return pl.pallas_call(
        functools.partial(_sel_step_body, n_past_blocks, qpos),
        grid_spec=grid_spec,
        out_shape=jax.ShapeDtypeStruct((bs, NS_H, NS_DH), F32),
        compiler_params=_cparams("parallel", "arbitrary"),
        name="nsa_sel_sample",
    )(page_table, top_idx, q_r3, *([cache2] * (NS_KV * SEL_PER_STEP)), rows3)


def _win_step_body(pos0, qpos, q_ref, buf_ref, new_ref, o_ref):
    q = q_ref[...]
    qb = q.astype(BF16)
    rows_per = 2 * NS_KV
    wb = buf_ref.shape[0] // rows_per

    def slab(c):
        return buf_ref[pl.ds(c, wb, stride=rows_per), :].astype(BF16)

    grp_k = _row_group((NS_H, wb))
    grp_d = _row_group((NS_H, NS_DH))
    s = jnp.where(grp_k == 0, _dot_nt(qb, slab(0)), _dot_nt(qb, slab(1)))
    kpos = pos0 + lax.broadcasted_iota(jnp.int32, (NS_H, wb), 1)
    valid = (kpos <= qpos) & (kpos > qpos - WINDOW) & (kpos >= 0)
    kw_new = jnp.where(grp_d == 0, new_ref[0:1, :], new_ref[1:2, :])
    vw_new = jnp.where(grp_d == 0, new_ref[2:3, :], new_ref[3:4, :])
    s_new = jnp.sum(q * kw_new, axis=-1, keepdims=True)
    mx = jnp.maximum(jnp.max(jnp.where(valid, s, NEG_INF), axis=-1, keepdims=True), s_new)
    p = jnp.where(valid, jnp.exp2(s - mx), 0.0)
    p_new = jnp.exp2(s_new - mx)
    den = jnp.sum(p, axis=-1, keepdims=True) + p_new
    pb = p.astype(BF16)
    pv = jnp.where(grp_d == 0, _dot(pb, slab(2)), _dot(pb, slab(3)))
    o_ref[...] = (pv + p_new * vw_new) / den


def nsa_win_sample(q_r3, win_buf2, win3, layer, wb, past_len, qpos):
    bs = q_r3.shape[0]
    return pl.pallas_call(
        functools.partial(_win_step_body, past_len - wb, qpos),
        grid=(bs,),
        in_specs=[pl.BlockSpec((None, NS_H, NS_DH), lambda b: (b, 0, 0)),
                  pl.BlockSpec((wb * 2 * NS_KV, NS_DH), lambda b: (layer * bs + b, 0)),
                  pl.BlockSpec((None, 2 * NS_KV, NS_DH), lambda b: (b, 0, 0))],
        out_specs=pl.BlockSpec((None, NS_H, NS_DH), lambda b: (b, 0, 0)),
        out_shape=jax.ShapeDtypeStruct((bs, NS_H, NS_DH), F32),
        compiler_params=_cparams("parallel"),
        name="nsa_win_sample",
    )(q_r3, win_buf2, win3)


def _project(x, norm_g, w_a, w_g, w_b):
    h = rmsnorm_rows(x, norm_g, BF16)
    return (matmul(h, w_a, name="in_proj_a"), matmul(h, w_g, name="in_proj_gate"), matmul(h, w_b, name="in_proj_b"))


def _merge_out(x, branches, z_b, w_up_b, w_out_b):
    d = x.shape[1]
    gated = merge_up(branches, z_b, w_up_b, d)
    return matmul(gated, w_out_b, res=x, name="out_proj")


def _per_group(a, width=NS_DH):
    return jnp.stack([a[:, g * width:(g + 1) * width] for g in range(NS_KV)])


def kernel(x_prompt, x_sample, mem_prompt, cache_nsa_kv, state_win_kv, state_hgrn, state_conv, cache_mem_kv,
           page_table, norm_g, final_norm_g, w_in, hg_lb_logits, hg_norm_g, cv_w, ns_pe, ns_cw1, ns_cw2,
           mem_norm_g, w_mem_kv, w_up, w_out):
    depth = w_in.shape[0]
    b_p, t, d = x_prompt.shape
    bs, ts = x_sample.shape[:2]
    assert b_p == 1 and ts == 1
    n_pages = page_table.shape[1]
    past_len = n_pages * PAGE_SIZE
    wb = state_win_kv.shape[2]
    mem_len = mem_prompt.shape[1]
    kvw = NS_KV * NS_DH

    s_lb = jax.nn.softmax(hg_lb_logits.astype(F32), axis=0)
    lower = jnp.cumsum(s_lb, axis=0) - s_lb[0]

    pos_p = jnp.arange(t, dtype=jnp.int32)
    qpos_s = past_len
    pos_s = jnp.full((bs,), qpos_s, jnp.int32)

    n_pool = cache_nsa_kv.shape[1]
    cache2 = cache_nsa_kv.reshape(depth * n_pool * PAGE_SIZE * ROWS_PER_TOKEN, NS_DH)
    win_buf2 = state_win_kv.reshape(depth * bs * wb * 2 * NS_KV, NS_DH)
    mem_cache4 = cache_mem_kv.reshape(depth, bs, mem_len, 2 * MEM_H * MEM_DH)

    nch_p = t // CMP_STRIDE
    nch_s = (past_len + 1) // CMP_STRIDE
    n_sel_s = -(-(past_len + 1) // SEL_BLOCK)

    xp = x_prompt.reshape(t, d)
    xs = x_sample.reshape(bs, d)
    rows_p, win_p, hg_p, cv_p, mkv_p = [], [], [], [], []
    rows_s, win_s, hg_s, cv_s = [], [], [], []
    for l in range(depth):
        w_l = w_in[l]
        w_a = w_l[:, :COL_A].astype(BF16)
        w_g = jnp.pad(w_l[:, COL_A:COL_B0], ((0, 0), (0, LANE - COL_GATE))).astype(BF16)
        w_b = w_l[:, COL_B0:].astype(BF16)
        w_up_b = w_up[l].astype(BF16)
        w_out_b = w_out[l].astype(BF16)

        z_a, z_g, z_b = _project(xp, norm_g[l], w_a, w_g, w_b)
        o_hg, st_hg = hgrn_prompt(z_a, lower[l], hg_norm_g[l])
        o_cv, st_cv = conv_prompt(z_a, cv_w[l], jnp.zeros((CV_K - 1, BR_W), F32))
        q_c, q_r, rows, win = nsa_prep(z_a, pos_p, BF16)
        x_k = _per_group(rows[:, 0:kvw]).reshape(NS_KV * nch_p, CMP_STRIDE * NS_DH)
        x_v = _per_group(rows[:, kvw:2 * kvw]).reshape(NS_KV * nch_p, CMP_STRIDE * NS_DH)
        cmp = compress_finish(compress_partial_rows(x_k, x_v, ns_cw1[l]), ns_pe[l], ns_cw1[l], ns_cw2[l], nch_p)
        kcmp = cmp[0].astype(BF16)
        vcmp_t = jnp.swapaxes(cmp[1], 1, 2).astype(BF16)
        o_cmp, sel_bias = nsa_cmp_select(q_c, kcmp, vcmp_t, nch_p - 1)
        ks = _per_group(rows[:, 2 * kvw:3 * kvw]).astype(BF16)
        vs_t = jnp.swapaxes(_per_group(rows[:, 3 * kvw:4 * kvw]), 1, 2).astype(BF16)
        o_sel = nsa_sel_attn(q_r, sel_bias, ks, vs_t)
        kw = _per_group(win[:, 0:kvw]).astype(BF16)
        vw_t = jnp.swapaxes(_per_group(win[:, kvw:2 * kvw]), 1, 2).astype(BF16)
        o_win = nsa_win_attn(q_r, kw, vw_t)
        o_ns = nsa_combine(o_cmp, o_sel, o_win, z_g, z_b)
        mem_h = rmsnorm_rows(mem_prompt.reshape(mem_len, d), mem_norm_g[l], BF16)
        mkv = matmul(mem_h, w_mem_kv[l].astype(BF16), name="mem_kv")
        o_mm = mem_attn_prompt(z_b, mkv)
        xp = _merge_out(xp, (o_hg, o_cv, o_ns, o_mm), z_b, w_up_b, w_out_b)
        rows_p.append(rows.reshape(1, t, 4, NS_KV, NS_DH))
        win_p.append(win[t - min(WINDOW, t):].reshape(1, min(WINDOW, t), 2, NS_KV, NS_DH))
        hg_p.append(st_hg.reshape(1, HG_H, HG_DK, HG_DV))
        cv_p.append(st_cv.reshape(1, CV_K - 1, BR_W))
        mkv_p.append(mkv.reshape(1, mem_len, 2, MEM_H, MEM_DH))

        z_a, z_g, z_b = _project(xs, norm_g[l], w_a, w_g, w_b)
        z_a3 = z_a.reshape(bs, COL_A // LANE, LANE)
        z_b3 = z_b.reshape(bs, z_b.shape[1] // LANE, LANE)
        o_hg, st_hg = hgrn_sample(z_a3, lower[l], hg_norm_g[l], state_hgrn[l])
        o_cv, st_cv = conv_sample(z_a, cv_w[l], state_conv[l])
        q_c, q_r, rows, win = nsa_prep(z_a, pos_s, F32)
        part = compress_partial_paged(cache2, page_table, ns_cw1[l], l, n_pool)
        cmp = compress_finish(part, ns_pe[l], ns_cw1[l], ns_cw2[l], nch_s)
        cmp = cmp.reshape(2, NS_KV, bs, nch_s, NS_DH)
        q_c3 = q_c.reshape(bs, NS_H, NS_DH)
        q_r3 = q_r.reshape(bs, NS_H, NS_DH)
        o_cmp, psum = nsa_cmp_sample(q_c3, cmp[0], cmp[1], nch_s - 1, qpos_s)
        top = nsa_select_sample(psum.reshape(bs * NS_KV, nch_s), nch_s - 1, n_sel_s, qpos_s)
        top = top[:, :SEL_TOPN].reshape(bs, NS_KV * SEL_TOPN)
        o_sel = nsa_sel_sample(q_r3, rows.reshape(bs, 4 * NS_KV, NS_DH), cache2, page_table, top, l, n_pool, qpos_s)
        o_win = nsa_win_sample(q_r3, win_buf2, win.reshape(bs, 2 * NS_KV, NS_DH), l, wb, past_len, qpos_s)
        o_ns = nsa_combine(o_cmp.reshape(bs, BR_W), o_sel.reshape(bs, BR_W), o_win.reshape(bs, BR_W), z_g, z_b)
        o_mm = mem_attn_sample(z_b3, mem_cache4, l)
        branches = (o_hg.reshape(bs, BR_W).astype(BF16), o_cv.astype(BF16), o_ns, o_mm.reshape(bs, BR_W).astype(BF16))
        xs = _merge_out(xs, branches, z_b, w_up_b, w_out_b)
        rows_s.append(rows.reshape(bs, 1, 4, NS_KV, NS_DH))
        win_s.append(jnp.concatenate([state_win_kv[l][:, 1:], win.reshape(bs, 1, 2, NS_KV, NS_DH)], axis=1))
        hg_s.append(st_hg)
        cv_s.append(st_cv)

    y_prompt = rmsnorm_rows(xp, final_norm_g, F32).reshape(1, t, d)
    y_sample = rmsnorm_rows(xs, final_norm_g, F32).reshape(bs, 1, d)
    return (y_prompt, y_sample, jnp.stack(rows_p), jnp.stack(win_p), jnp.stack(hg_p), jnp.stack(cv_p),
            jnp.stack(mkv_p), jnp.stack(rows_s), jnp.stack(win_s), jnp.stack(hg_s), jnp.stack(cv_s))
```

```python
import functools

import numpy as np
import jax
import jax.numpy as jnp
from jax import lax
from jax.experimental import pallas as pl
from jax.experimental.pallas import tpu as pltpu

F32 = jnp.float32
BF16 = jnp.bfloat16

BR_W = 1024
N_BRANCH = 4
HG_H = 8
HG_DK = 128
HG_DV = 128
F_MIN = 1e-30
CV_K = 3
NS_H = 8
NS_KV = 2
NS_G = NS_H // NS_KV
NS_DH = 128
CMP_LEN = 32
CMP_STRIDE = 16
SEL_BLOCK = 64
SEL_TOPN = 16
WINDOW = 512
FORCE_BONUS = 100.0
ROPE_THETA = 500000.0
ROT_DIM = NS_DH // 4
MEM_H = 4
MEM_DH = 256
NORM_EPS = 1e-6
NEG_INF = -1e30
REMOVED = -3e38
LOG2_E = 1.4426950408889634
PAGE_SIZE = 128
ROWS_PER_TOKEN = 4 * NS_KV

COL_A = 10752
COL_GATE = 24
COL_B0 = COL_A + COL_GATE

LANE = 128
HG_CHUNK = 128
HG_HEADS_PER_STEP = 2
Q_TILE = 128
K_TILE = 512
PAGES_PER_STEP = 16
SEL_PER_STEP = 4
VMEM_LIMIT = 56 * 1024 * 1024


def _cparams(*sem):
    return pltpu.CompilerParams(dimension_semantics=sem, vmem_limit_bytes=VMEM_LIMIT)


def _pick(n, cands):
    for c in cands:
        if n % c == 0:
            return c
    return n


def _silu(x):
    return x * jax.nn.sigmoid(x)


def _dot(a, b):
    return jnp.dot(a, b, preferred_element_type=F32)


def _dot_nt(a, b):
    return lax.dot_general(a, b, (((1,), (1,)), ((), ())), preferred_element_type=F32)


def _split3(x):
    hi = x.astype(BF16)
    r1 = x - hi.astype(F32)
    mid = r1.astype(BF16)
    lo = (r1 - mid.astype(F32)).astype(BF16)
    return hi, mid, lo


def _norm_body(x_ref, g_ref, o_ref):
    x = x_ref[...]
    ms = jnp.mean(x * x, axis=-1, keepdims=True)
    o_ref[...] = (x * lax.rsqrt(ms + NORM_EPS) * g_ref[...]).astype(o_ref.dtype)


def rmsnorm_rows(x, g, out_dtype):
    m, d = x.shape
    tm = _pick(m, (256, 128, 64, 32, 16, 8))
    return pl.pallas_call(
        _norm_body,
        grid=(m // tm,),
        in_specs=[pl.BlockSpec((tm, d), lambda i: (i, 0)), pl.BlockSpec((1, d), lambda i: (0, 0))],
        out_specs=pl.BlockSpec((tm, d), lambda i: (i, 0)),
        out_shape=jax.ShapeDtypeStruct((m, d), out_dtype),
        compiler_params=_cparams("parallel"),
        name="rmsnorm",
    )(x, g.reshape(1, d).astype(F32))


def _mm_body(a_ref, b_ref, o_ref):
    o_ref[...] = _dot(a_ref[...].astype(BF16), b_ref[...]).astype(o_ref.dtype)


def _mm_res_body(a_ref, b_ref, r_ref, o_ref):
    o_ref[...] = r_ref[...] + _dot(a_ref[...].astype(BF16), b_ref[...])


def matmul(a, b, res=None, out_dtype=F32, name="matmul"):
    m, k = a.shape
    n = b.shape[1]
    tm = _pick(m, (1024, 512, 256, 128, 64, 32, 16, 8))
    tn = _pick(n, (512, 256, 128))
    in_specs = [pl.BlockSpec((tm, k), lambda i, j: (i, 0)), pl.BlockSpec((k, tn), lambda i, j: (0, j))]
    args = [a, b]
    body = _mm_body
    if res is not None:
        in_specs.append(pl.BlockSpec((tm, tn), lambda i, j: (i, j)))
        args.append(res)
        body = _mm_res_body
    return pl.pallas_call(
        body,
        grid=(m // tm, n // tn),
        in_specs=in_specs,
        out_specs=pl.BlockSpec((tm, tn), lambda i, j: (i, j)),
        out_shape=jax.ShapeDtypeStruct((m, n), out_dtype),
        compiler_params=_cparams("parallel", "parallel"),
        name=name,
    )(*args)


def _merge_body(b0, b1, b2, b3, g0, g1, g2, g3, w0, w1, w2, w3, o_ref):
    acc = jax.nn.sigmoid(g0[...]) * _dot(b0[...], w0[...])
    acc += jax.nn.sigmoid(g1[...]) * _dot(b1[...], w1[...])
    acc += jax.nn.sigmoid(g2[...]) * _dot(b2[...], w2[...])
    acc += jax.nn.sigmoid(g3[...]) * _dot(b3[...], w3[...])
    o_ref[...] = acc.astype(o_ref.dtype)


def merge_up(branches, z_b, w_up_bf16, d_model):
    m = branches[0].shape[0]
    tm = _pick(m, (512, 256, 128, 64, 32, 16))
    tn = 512
    nj = d_model // tn
    gate_col0 = (z_b.shape[1] - N_BRANCH * d_model) // tn
    br_specs = [pl.BlockSpec((tm, BR_W), lambda i, j: (i, 0)) for _ in range(N_BRANCH)]
    g_specs = [pl.BlockSpec((tm, tn), functools.partial(lambda i, j, n: (i, gate_col0 + n * nj + j), n=n))
               for n in range(N_BRANCH)]
    w_specs = [pl.BlockSpec((None, BR_W, tn), functools.partial(lambda i, j, n: (n, 0, j), n=n))
               for n in range(N_BRANCH)]
    return pl.pallas_call(
        _merge_body,
        grid=(m // tm, nj),
        in_specs=br_specs + g_specs + w_specs,
        out_specs=pl.BlockSpec((tm, tn), lambda i, j: (i, j)),
        out_shape=jax.ShapeDtypeStruct((m, d_model), BF16),
        compiler_params=_cparams("parallel", "parallel"),
        name="merge_up",
    )(*branches, z_b, z_b, z_b, z_b, w_up_bf16, w_up_bf16, w_up_bf16, w_up_bf16)


def _hgrn_consts(c):
    nlev = int(round(np.log2(c)))
    t = np.arange(c)[:, None]
    r = np.arange(c)[None, :]
    blocks = [r <= t]
    masks = [t == r]
    for lv in range(nlev):
        h = c >> (lv + 1)
        mid = (t // (2 * h)) * (2 * h) + h
        blocks.append(np.where(t >= mid, (r >= mid) & (r <= t), (r > t) & (r <= mid - 1)))
        same = (t // (2 * h)) == (r // (2 * h))
        masks.append(same & (t % (2 * h) >= h) & (r % (2 * h) < h))
    blocks.append(r > t)
    l_all = np.concatenate(blocks, axis=0).astype(np.float32)
    return jnp.asarray(l_all, BF16), jnp.asarray(np.stack(masks).astype(np.float32)), nlev


def _hgrn_body(nlev, q_ref, z_ref, v_ref, g_ref, lb_ref, ng_ref, l_ref, mask_ref, o_ref, s_ref, st_scr):
    c_idx = pl.program_id(1)
    c = q_ref.shape[0]

    @pl.when(c_idx == 0)
    def _():
        st_scr[...] = jnp.zeros_like(st_scr)

    for hh in range(HG_HEADS_PER_STEP):
        sl = slice(hh * HG_DK, (hh + 1) * HG_DK)
        lb = lb_ref[:, sl]
        q = _silu(q_ref[:, sl])
        z = z_ref[:, sl]
        f = lb + (1.0 - lb) * jax.nn.sigmoid(z)
        logf = jnp.log(jnp.maximum(f, F_MIN))
        k = (1.0 - lb) * jax.nn.sigmoid(-z)
        v = v_ref[:, sl]

        hi, mid, lo = _split3(logf)
        e3 = _dot(l_ref[...], jnp.concatenate([hi, mid, lo], axis=1))
        dk = HG_DK
        x = jnp.exp(e3[:, :dk] + (e3[:, dk:2 * dk] + e3[:, 2 * dk:]))
        eb = x[0:c]
        est = x[(nlev + 1) * c:(nlev + 2) * c]

        st = st_scr[hh]
        inter = _dot_nt((q * eb).astype(BF16), st.astype(BF16))
        att = mask_ref[0] * _dot_nt(q.astype(BF16), k.astype(BF16))
        for lv in range(nlev):
            fac = x[(1 + lv) * c:(2 + lv) * c]
            att += mask_ref[1 + lv] * _dot_nt((q * fac).astype(BF16), (k * fac).astype(BF16))
        o = inter + _dot(att.astype(BF16), v.astype(BF16))
        o = o * lax.rsqrt(jnp.mean(o * o, axis=-1, keepdims=True) + NORM_EPS) * ng_ref[:, sl]
        o_ref[:, sl] = (o * _silu(g_ref[:, sl])).astype(o_ref.dtype)

        st_scr[hh] = st * eb[c - 1:c, :] + _dot(v.T.astype(BF16), (k * est).astype(BF16))

    @pl.when(c_idx == pl.num_programs(1) - 1)
    def _():
        for hh in range(HG_HEADS_PER_STEP):
            s_ref[hh] = st_scr[hh].T


def hgrn_prompt(z_a, lb, norm_g):
    t = z_a.shape[0]
    c = HG_CHUNK
    l_all, masks, nlev = _hgrn_consts(c)
    hp = HG_HEADS_PER_STEP
    steps = HG_H // hp
    width = hp * HG_DK

    def col(k):
        return pl.BlockSpec((c, width), functools.partial(lambda h, i, k: (i, k * steps + h), k=k))

    vec = pl.BlockSpec((None, 1, width), lambda h, i: (h, 0, 0))
    return pl.pallas_call(
        functools.partial(_hgrn_body, nlev),
        grid=(steps, t // c),
        in_specs=[col(0), col(1), col(2), col(3), vec, vec,
                  pl.BlockSpec(l_all.shape, lambda h, i: (0, 0)),
                  pl.BlockSpec(masks.shape, lambda h, i: (0, 0, 0))],
        out_specs=[pl.BlockSpec((c, width), lambda h, i: (i, h)),
                   pl.BlockSpec((hp, HG_DK, HG_DV), lambda h, i: (h, 0, 0))],
        out_shape=[jax.ShapeDtypeStruct((t, BR_W), BF16),
                   jax.ShapeDtypeStruct((HG_H, HG_DK, HG_DV), F32)],
        scratch_shapes=[pltpu.VMEM((hp, HG_DV, HG_DK), F32)],
        compiler_params=_cparams("parallel", "arbitrary"),
        name="hgrn_prompt",
    )(z_a, z_a, z_a, z_a, lb.reshape(steps, 1, width), norm_g.reshape(steps, 1, width), l_all, masks)


def _hgrn_step_body(q_ref, z_ref, v_ref, g_ref, lb_ref, ng_ref, s0_ref, o_ref, s_ref):
    lb = lb_ref[...]
    q = _silu(q_ref[...])
    z = z_ref[...]
    f = jnp.maximum(lb + (1.0 - lb) * jax.nn.sigmoid(z), F_MIN)
    k = (1.0 - lb) * jax.nn.sigmoid(-z)
    v = v_ref[...]
    rows = []
    for h in range(HG_H):
        def colb(a):
            return jnp.broadcast_to(a[h:h + 1, :], (HG_DK, HG_DK)).T
        s_new = colb(f) * s0_ref[h] + colb(k) * v[h:h + 1, :]
        s_ref[h] = s_new
        rows.append(jnp.sum(colb(q) * s_new, axis=0, keepdims=True))
    o = jnp.concatenate(rows, axis=0)
    o = o * lax.rsqrt(jnp.mean(o * o, axis=-1, keepdims=True) + NORM_EPS) * ng_ref[...]
    o_ref[...] = o * _silu(g_ref[...])


def hgrn_sample(z_a3, lb, norm_g, s0):
    bs = z_a3.shape[0]

    def grp(k):
        return pl.BlockSpec((None, HG_H, LANE), functools.partial(lambda b, k: (b, k, 0), k=k))

    vec = pl.BlockSpec((HG_H, LANE), lambda b: (0, 0))
    st = pl.BlockSpec((None, HG_H, HG_DK, HG_DV), lambda b: (b, 0, 0, 0))
    return pl.pallas_call(
        _hgrn_step_body,
        grid=(bs,),
        in_specs=[grp(0), grp(1), grp(2), grp(3), vec, vec, st],
        out_specs=[pl.BlockSpec((None, HG_H, LANE), lambda b: (b, 0, 0)), st],
        out_shape=[jax.ShapeDtypeStruct((bs, HG_H, LANE), F32),
                   jax.ShapeDtypeStruct((bs, HG_H, HG_DK, HG_DV), F32)],
        compiler_params=_cparams("parallel"),
        name="hgrn_sample",
    )(z_a3, z_a3, z_a3, z_a3, lb.reshape(HG_H, LANE), norm_g.reshape(HG_H, LANE), s0)


def _conv_body(u_ref, b_ref, c_ref, g_ref, w_ref, prev_ref, o_ref, last_ref, carry):
    @pl.when(pl.program_id(0) == 0)
    def _():
        carry[...] = prev_ref[...]

    v = c_ref[...] * u_ref[...]
    tm = v.shape[0]
    row = lax.broadcasted_iota(jnp.int32, v.shape, 0)
    p1 = carry[7:8, :]
    p2 = carry[6:7, :]
    v1 = jnp.where(row == 0, p1, pltpu.roll(v, 1, 0))
    v2 = jnp.where(row == 0, p2, jnp.where(row == 1, p1, pltpu.roll(v, 2, 0)))
    w = w_ref[...]
    y = w[0:1, :] * v2 + w[1:2, :] * v1 + w[2:3, :] * v
    o_ref[...] = (b_ref[...] * y * _silu(g_ref[...])).astype(o_ref.dtype)
    tail = v[tm - 8:tm, :]
    carry[...] = tail
    last_ref[...] = tail


def conv_prompt(z_a, w, prev):
    t = z_a.shape[0]
    tm = _pick(t, (256, 128, 64, 32, 16, 8))

    def col(k):
        return pl.BlockSpec((tm, BR_W), functools.partial(lambda i, k: (i, 4 + k), k=k))

    w8 = jnp.zeros((8, BR_W), F32).at[:CV_K].set(w.astype(F32))
    prev8 = jnp.zeros((8, BR_W), F32).at[8 - (CV_K - 1):].set(prev.astype(F32))
    full8 = pl.BlockSpec((8, BR_W), lambda i: (0, 0))
    o, last = pl.pallas_call(
        _conv_body,
        grid=(t // tm,),
        in_specs=[col(0), col(1), col(2), col(3), full8, full8],
        out_specs=[pl.BlockSpec((tm, BR_W), lambda i: (i, 0)), full8],
        out_shape=[jax.ShapeDtypeStruct((t, BR_W), BF16), jax.ShapeDtypeStruct((8, BR_W), F32)],
        scratch_shapes=[pltpu.VMEM((8, BR_W), F32)],
        compiler_params=_cparams("arbitrary"),
        name="conv_prompt",
    )(z_a, z_a, z_a, z_a, w8, prev8)
    return o, last[8 - (CV_K - 1):]


def _conv_step_body(u_ref, b_ref, c_ref, g_ref, w_ref, p0_ref, p1_ref, o_ref, v_ref):
    v = c_ref[...] * u_ref[...]
    w = w_ref[...]
    y = w[0:1, :] * p0_ref[...] + w[1:2, :] * p1_ref[...] + w[2:3, :] * v
    o_ref[...] = b_ref[...] * y * _silu(g_ref[...])
    v_ref[...] = v


def conv_sample(z_a, w, prev):
    bs = z_a.shape[0]

    def col(k):
        return pl.BlockSpec((bs, BR_W), functools.partial(lambda i, k: (0, 4 + k), k=k))

    w8 = jnp.zeros((8, BR_W), F32).at[:CV_K].set(w.astype(F32))
    full = pl.BlockSpec((bs, BR_W), lambda i: (0, 0))
    o, v = pl.pallas_call(
        _conv_step_body,
        grid=(1,),
        in_specs=[col(0), col(1), col(2), col(3), pl.BlockSpec((8, BR_W), lambda i: (0, 0)), full, full],
        out_specs=[full, full],
        out_shape=[jax.ShapeDtypeStruct((bs, BR_W), F32), jax.ShapeDtypeStruct((bs, BR_W), F32)],
        compiler_params=_cparams("arbitrary"),
        name="conv_sample",
    )(z_a, z_a, z_a, z_a, w8, prev[:, 0], prev[:, 1])
    return o, jnp.stack([prev[:, 1], v], axis=1)


def _rope_tables(pos):
    half = ROT_DIM // 2
    inv = ROPE_THETA ** (-2.0 * jnp.arange(half, dtype=F32) / ROT_DIM)
    ang = pos.astype(F32)[:, None] * inv[None, :]
    cos, sin = jnp.cos(ang), jnp.sin(ang)
    m = pos.shape[0]
    ones = jnp.ones((m, NS_DH - ROT_DIM), F32)
    zeros = jnp.zeros((m, NS_DH - ROT_DIM), F32)
    zh = jnp.zeros((m, half), F32)
    cos_t = jnp.concatenate([cos, cos, ones], axis=1)
    sin_a = jnp.concatenate([zh, sin, zeros], axis=1)
    sin_b = jnp.concatenate([-sin, zh, zeros], axis=1)
    return cos_t, sin_a, sin_b


def _rope(x, cos_t, sin_a, sin_b):
    n = x.shape[1] // NS_DH
    half = ROT_DIM // 2

    def tile(a):
        return a if n == 1 else jnp.concatenate([a] * n, axis=1)

    return (x * tile(cos_t) + pltpu.roll(x, half, 1) * tile(sin_a)
            + pltpu.roll(x, x.shape[1] - half, 1) * tile(sin_b))


def _nsa_prep_body(q_ref, cv_ref, ks_ref, vs_ref, kw_ref, vw_ref, cos_ref, sa_ref, sb_ref,
                   qc_ref, qr_ref, rows_ref, win_ref):
    cos_t, sin_a, sin_b = cos_ref[...], sa_ref[...], sb_ref[...]
    scale = NS_DH ** -0.5 * LOG2_E
    q = q_ref[...]
    qc_ref[...] = (q * scale).astype(qc_ref.dtype)
    qr_ref[...] = (_rope(q, cos_t, sin_a, sin_b) * scale).astype(qr_ref.dtype)
    kvw = NS_KV * NS_DH
    rows_ref[:, 0:2 * kvw] = cv_ref[...]
    rows_ref[:, 2 * kvw:3 * kvw] = _rope(ks_ref[...], cos_t, sin_a, sin_b)
    rows_ref[:, 3 * kvw:4 * kvw] = vs_ref[...]
    win_ref[:, 0:kvw] = _rope(kw_ref[...], cos_t, sin_a, sin_b)
    win_ref[:, kvw:2 * kvw] = vw_ref[...]


def nsa_prep(z_a, pos, q_dtype):
    m = z_a.shape[0]
    tm = _pick(m, (256, 128, 64, 32, 16, 8))
    kvw = NS_KV * NS_DH
    c0 = 8 * BR_W // kvw
    tabs = _rope_tables(pos)

    def col(k, width=kvw):
        return pl.BlockSpec((tm, width), lambda i: (i, k))

    tab = pl.BlockSpec((tm, NS_DH), lambda i: (i, 0))
    return pl.pallas_call(
        _nsa_prep_body,
        grid=(m // tm,),
        in_specs=[col(8, BR_W), col((c0 + 4) // 2, 2 * kvw), col(c0 + 6), col(c0 + 7), col(c0 + 8), col(c0 + 9),
                  tab, tab, tab],
        out_specs=[pl.BlockSpec((tm, BR_W), lambda i: (i, 0)), pl.BlockSpec((tm, BR_W), lambda i: (i, 0)),
                   pl.BlockSpec((tm, 4 * kvw), lambda i: (i, 0)), pl.BlockSpec((tm, 2 * kvw), lambda i: (i, 0))],
        out_shape=[jax.ShapeDtypeStruct((m, BR_W), q_dtype), jax.ShapeDtypeStruct((m, BR_W), q_dtype),
                   jax.ShapeDtypeStruct((m, 4 * kvw), F32), jax.ShapeDtypeStruct((m, 2 * kvw), F32)],
        compiler_params=_cparams("parallel"),
        name="nsa_prep",
    )(z_a, z_a, z_a, z_a, z_a, z_a, *tabs)


def _cmp_finish_body(n_valid, p_ref, pe_ref, w1_ref, w2_ref, o_ref):
    p = p_ref[...]
    n = p.shape[0]
    pe_term = _dot(pe_ref[...], w1_ref[...])[0:1, :]
    nxt = pltpu.roll(p[:, NS_DH:], n - 1, 0)
    hid = _silu(p[:, :NS_DH] + nxt + pe_term)
    out = _dot(hid.astype(BF16), w2_ref[...])
    row = lax.broadcasted_iota(jnp.int32, out.shape, 0)
    o_ref[...] = jnp.where(row < n_valid, out, 0.0)


def _cmp_weights(w1):
    kdim = CMP_STRIDE * NS_DH
    w1b = w1.astype(BF16)
    return jnp.concatenate([w1b[:, :kdim], w1b[:, kdim:]], axis=2)


def compress_partial_rows(x_k, x_v, w1):
    w1cat = _cmp_weights(w1)
    return jnp.stack([matmul(x_k, w1cat[0], name="cmp_partial_k"), matmul(x_v, w1cat[1], name="cmp_partial_v")])


def compress_finish(p, pe, w1, w2, nch):
    w1b = w1.astype(BF16)
    groups = p.shape[1] // nch
    pe8 = jnp.broadcast_to(pe.reshape(2, 1, CMP_LEN * NS_DH), (2, 8, CMP_LEN * NS_DH)).astype(BF16)
    out = pl.pallas_call(
        functools.partial(_cmp_finish_body, nch - 1),
        grid=(2, groups),
        in_specs=[pl.BlockSpec((None, nch, 2 * NS_DH), lambda a, g: (a, g, 0)),
                  pl.BlockSpec((None, 8, CMP_LEN * NS_DH), lambda a, g: (a, 0, 0)),
                  pl.BlockSpec((None, CMP_LEN * NS_DH, NS_DH), lambda a, g: (a, 0, 0)),
                  pl.BlockSpec((None, NS_DH, NS_DH), lambda a, g: (a, 0, 0))],
        out_specs=pl.BlockSpec((None, nch, NS_DH), lambda a, g: (a, g, 0)),
        out_shape=jax.ShapeDtypeStruct((2, groups * nch, NS_DH), F32),
        compiler_params=_cparams("parallel", "parallel"),
        name="cmp_finish",
    )(p, pe8, w1b, w2.astype(BF16))
    return out.reshape(2, groups, nch, NS_DH)


def _cmp_to_sel(n_cmp_pad, n_sel_pad, n_cmp, n_sel):
    cs = np.arange(n_cmp_pad)[:, None] * CMP_STRIDE
    ss = np.arange(n_sel_pad)[None, :] * SEL_BLOCK
    ov = np.minimum(cs + CMP_LEN, ss + SEL_BLOCK) - np.maximum(cs, ss)
    m = np.clip(ov, 0, None).astype(np.float32) / CMP_LEN
    m[n_cmp:, :] = 0.0
    m[:, n_sel:] = 0.0
    return m


def _stack_heads(q):
    return jnp.concatenate([q[:, r * NS_DH:(r + 1) * NS_DH] for r in range(NS_G)], axis=0)


def _tile_heads(a):
    return jnp.concatenate([a] * NS_G, axis=1)


def _store_heads_t(o_ref, o_t):
    tq = o_t.shape[1] // NS_G
    for r in range(NS_G):
        o_ref[:, r * NS_DH:(r + 1) * NS_DH] = o_t[:, r * tq:(r + 1) * tq].T


def _cmp_sel_body(n_cmp, q_ref, kc_ref, vct_ref, msel_ref, o_ref, bias_ref):
    gw = NS_G * NS_DH
    for g in range(NS_KV):
        _cmp_sel_group(n_cmp, q_ref.at[:, g * gw:(g + 1) * gw], kc_ref.at[g], vct_ref.at[g], msel_ref,
                       o_ref.at[:, g * gw:(g + 1) * gw], bias_ref.at[g])


def _cmp_sel_group(n_cmp, q_ref, kc_ref, vct_ref, msel_ref, o_ref, bias_ref):
    tq = q_ref.shape[0]
    t0 = pl.program_id(0) * tq
    qs = _stack_heads(q_ref[...])
    s_t = _dot_nt(kc_ref[...], qs)
    ncp = s_t.shape[0]
    n_i = lax.broadcasted_iota(jnp.int32, (ncp, tq), 0)
    qpos = t0 + lax.broadcasted_iota(jnp.int32, (ncp, tq), 1)
    valid = _tile_heads((n_i * CMP_STRIDE + CMP_LEN - 1 <= qpos) & (n_i < n_cmp))
    sm = jnp.where(valid, s_t, NEG_INF)
    mx = jnp.max(sm, axis=0, keepdims=True)
    e = jnp.where(valid, jnp.exp2(sm - mx), 0.0)
    den = jnp.sum(e, axis=0, keepdims=True)
    p = e * (1.0 / jnp.where(den > 0.0, den, 1.0))
    _store_heads_t(o_ref, _dot(vct_ref[...], p.astype(BF16)))

    psum = p[:, 0:tq]
    for r in range(1, NS_G):
        psum += p[:, r * tq:(r + 1) * tq]
    hi, mid, lo = _split3(psum)
    i3 = _dot(msel_ref[...], jnp.concatenate([hi, mid, lo], axis=1))
    imp = i3[:, :tq] + i3[:, tq:2 * tq] + i3[:, 2 * tq:]
    nb = imp.shape[0]
    j_i = lax.broadcasted_iota(jnp.int32, (nb, tq), 0)
    cur = (t0 + lax.broadcasted_iota(jnp.int32, (nb, tq), 1)) // SEL_BLOCK
    forced = (j_i == 0) | (j_i == cur) | (j_i == cur - 1)
    imp = jnp.where(forced, imp + FORCE_BONUS, imp)
    imp = jnp.where(j_i <= cur, imp, NEG_INF)
    j_f = j_i.astype(F32)
    sel = jnp.zeros((nb, tq), F32)
    for _ in range(min(SEL_TOPN, nb)):
        mx = jnp.max(imp, axis=0, keepdims=True)
        first = jnp.min(jnp.where(imp == mx, j_f, float(nb)), axis=0, keepdims=True)
        hit = j_f == first
        sel = jnp.where(hit, 1.0, sel)
        imp = jnp.where(hit, REMOVED, imp)
    bias_t = jnp.where((sel > 0.5) & (j_i <= cur), 0.0, NEG_INF)
    if nb < LANE:
        bias_t = jnp.concatenate([bias_t, jnp.full((LANE - nb, tq), NEG_INF, F32)], axis=0)
    bias_ref[...] = bias_t.T.astype(bias_ref.dtype)


def nsa_cmp_select(q_c, kcmp, vcmp_t, n_cmp):
    t = q_c.shape[0]
    ncp = kcmp.shape[1]
    n_sel = t // SEL_BLOCK
    assert n_sel <= LANE
    msel_t = jnp.asarray(_cmp_to_sel(ncp, n_sel, n_cmp, n_sel).T, BF16)
    return pl.pallas_call(
        functools.partial(_cmp_sel_body, n_cmp),
        grid=(t // Q_TILE,),
        in_specs=[pl.BlockSpec((Q_TILE, BR_W), lambda i: (i, 0)),
                  pl.BlockSpec((NS_KV, ncp, NS_DH), lambda i: (0, 0, 0)),
                  pl.BlockSpec((NS_KV, NS_DH, ncp), lambda i: (0, 0, 0)),
                  pl.BlockSpec((n_sel, ncp), lambda i: (0, 0))],
        out_specs=[pl.BlockSpec((Q_TILE, BR_W), lambda i: (i, 0)),
                   pl.BlockSpec((NS_KV, Q_TILE, LANE), lambda i: (0, i, 0))],
        out_shape=[jax.ShapeDtypeStruct((t, BR_W), F32), jax.ShapeDtypeStruct((NS_KV, t, LANE), BF16)],
        compiler_params=_cparams("parallel"),
        name="nsa_cmp_select",
    )(q_c, kcmp, vcmp_t, msel_t)


V_EXT_ROWS = NS_DH + 16


def _sel_attn_body(q_ref, bias_ref, k_ref, vt_ref, o_ref, m_scr, acc_scr, s_scr):
    tq = q_ref.shape[0]
    t0 = pl.program_id(0) * tq
    gw = NS_G * NS_DH
    qx = [jnp.concatenate([_stack_heads(q_ref[:, g * gw:(g + 1) * gw]),
                           jnp.concatenate([bias_ref[g]] * NS_G, axis=0)], axis=1) for g in range(NS_KV)]
    m_scr[...] = jnp.full_like(m_scr, NEG_INF)
    acc_scr[...] = jnp.zeros_like(acc_scr)
    n_tiles = (t0 + tq + K_TILE - 1) // K_TILE

    def scores(kt, slot):
        k0 = pl.multiple_of(kt * K_TILE, K_TILE)
        for g in range(NS_KV):
            s_scr[slot, g] = _dot_nt(k_ref[g, pl.ds(k0, K_TILE), :], qx[g])

    def consume(kt, slot, causal):
        k0 = pl.multiple_of(kt * K_TILE, K_TILE)
        for g in range(NS_KV):
            s_m = s_scr[slot, g]
            if causal:
                key = k0 + lax.broadcasted_iota(jnp.int32, (K_TILE, tq), 0)
                qpos = t0 + lax.broadcasted_iota(jnp.int32, (K_TILE, tq), 1)
                s_m = s_m + _tile_heads(jnp.where(key <= qpos, 0.0, NEG_INF))
            m_old = m_scr[g]
            m_new = jnp.maximum(m_old, jnp.max(s_m, axis=0, keepdims=True))
            alpha = jnp.exp2(m_old - m_new)
            p = jnp.exp2(s_m - m_new)
            acc_scr[g] = alpha * acc_scr[g] + _dot(vt_ref[g, :, pl.ds(k0, K_TILE)], p.astype(BF16))
            m_scr[g] = m_new

    scores(0, 0)
    n_before = n_tiles - 1

    def pair(i, carry):
        kt = 2 * i
        scores(kt + 1, 1)
        consume(kt, 0, False)
        scores(kt + 2, 0)
        consume(kt + 1, 1, False)
        return carry

    lax.fori_loop(0, n_before // 2, pair, 0)
    odd = n_before % 2

    @pl.when(odd == 1)
    def _():
        scores(n_tiles - 1, 1)
        consume(n_tiles - 2, 0, False)

    consume(n_tiles - 1, odd, True)
    for g in range(NS_KV):
        acc = acc_scr[g]
        _store_heads_t(o_ref.at[:, g * gw:(g + 1) * gw], acc[0:NS_DH] * (1.0 / acc[NS_DH:NS_DH + 1]))


def nsa_sel_attn(q_r, bias, ks, vs_t):
    t = q_r.shape[0]
    gw = NS_G * NS_DH
    blk = jnp.arange(t, dtype=jnp.int32) // SEL_BLOCK
    onehot = (blk[:, None] == jnp.arange(LANE, dtype=jnp.int32)[None, :]).astype(BF16)
    ks_ext = jnp.concatenate([ks, jnp.broadcast_to(onehot, (NS_KV, t, LANE))], axis=2)
    ones = jnp.concatenate([jnp.ones((NS_KV, 1, t), BF16), jnp.zeros((NS_KV, V_EXT_ROWS - NS_DH - 1, t), BF16)], axis=1)
    vs_ext = jnp.concatenate([vs_t, ones], axis=1)
    return pl.pallas_call(
        _sel_attn_body,
        grid=(t // Q_TILE,),
        in_specs=[pl.BlockSpec((Q_TILE, NS_KV * gw), lambda i: (i, 0)),
                  pl.BlockSpec((NS_KV, Q_TILE, LANE), lambda i: (0, i, 0)),
                  pl.BlockSpec((NS_KV, t, NS_DH + LANE), lambda i: (0, 0, 0)),
                  pl.BlockSpec((NS_KV, V_EXT_ROWS, t), lambda i: (0, 0, 0))],
        out_specs=pl.BlockSpec((Q_TILE, NS_KV * gw), lambda i: (i, 0)),
        out_shape=jax.ShapeDtypeStruct((t, BR_W), F32),
        scratch_shapes=[pltpu.VMEM((NS_KV, 1, NS_G * Q_TILE), F32),
                        pltpu.VMEM((NS_KV, V_EXT_ROWS, NS_G * Q_TILE), F32),
                        pltpu.VMEM((2, NS_KV, K_TILE, NS_G * Q_TILE), F32)],
        compiler_params=_cparams("parallel"),
        name="nsa_sel_attn",
    )(q_r, bias, ks_ext, vs_ext)


def _win_attn_body(span, q_ref, k_ref, vt_ref, o_ref):
    tq = q_ref.shape[0]
    t0 = pl.program_id(0) * tq
    gw = NS_G * NS_DH
    start = pl.multiple_of(jnp.maximum(t0 - WINDOW, 0), Q_TILE)
    key = start + lax.broadcasted_iota(jnp.int32, (span, tq), 0)
    qpos = t0 + lax.broadcasted_iota(jnp.int32, (span, tq), 1)
    bias = _tile_heads(jnp.where((key <= qpos) & (key > qpos - WINDOW), 0.0, NEG_INF))
    for g in range(NS_KV):
        qs = _stack_heads(q_ref[:, g * gw:(g + 1) * gw])
        sm = _dot_nt(k_ref[g, pl.ds(start, span), :], qs) + bias
        mx = jnp.max(sm, axis=0, keepdims=True)
        e = jnp.exp2(sm - mx)
        den = jnp.sum(e, axis=0, keepdims=True)
        p = e * (1.0 / den)
        _store_heads_t(o_ref.at[:, g * gw:(g + 1) * gw], _dot(vt_ref[g, :, pl.ds(start, span)], p.astype(BF16)))


def nsa_win_attn(q_r, kw, vw_t):
    t = q_r.shape[0]
    span = min(WINDOW + Q_TILE, t)
    return pl.pallas_call(
        functools.partial(_win_attn_body, span),
        grid=(t // Q_TILE,),
        in_specs=[pl.BlockSpec((Q_TILE, BR_W), lambda i: (i, 0)),
                  pl.BlockSpec((NS_KV, t, NS_DH), lambda i: (0, 0, 0)),
                  pl.BlockSpec((NS_KV, NS_DH, t), lambda i: (0, 0, 0))],
        out_specs=pl.BlockSpec((Q_TILE, BR_W), lambda i: (i, 0)),
        out_shape=jax.ShapeDtypeStruct((t, BR_W), F32),
        compiler_params=_cparams("parallel"),
        name="nsa_win_attn",
    )(q_r, kw, vw_t)


def _gate_expand():
    e = np.zeros((3, LANE, BR_W), np.float32)
    for h in range(NS_H):
        for c in range(3):
            e[c, h * 3 + c, h * NS_DH:(h + 1) * NS_DH] = 1.0
    return jnp.asarray(e, BF16)


def _combine_body(oc_ref, os_ref, ow_ref, gate_ref, g_ref, e_ref, o_ref):
    gs = jax.nn.sigmoid(gate_ref[...])
    hi = gs.astype(BF16)
    lo = (gs - hi.astype(F32)).astype(BF16)

    def expand(c):
        return _dot(hi, e_ref[c]) + _dot(lo, e_ref[c])

    o = expand(0) * oc_ref[...] + expand(1) * os_ref[...] + expand(2) * ow_ref[...]
    o_ref[...] = (o * _silu(g_ref[...])).astype(o_ref.dtype)


def nsa_combine(o_cmp, o_sel, o_win, z_g, z_b):
    m = o_cmp.shape[0]
    tm = _pick(m, (256, 128, 64, 32, 16))
    row = pl.BlockSpec((tm, BR_W), lambda i: (i, 0))
    return pl.pallas_call(
        _combine_body,
        grid=(m // tm,),
        in_specs=[row, row, row, pl.BlockSpec((tm, LANE), lambda i: (i, 0)), row,
                  pl.BlockSpec((3, LANE, BR_W), lambda i: (0, 0, 0))],
        out_specs=row,
        out_shape=jax.ShapeDtypeStruct((m, BR_W), BF16),
        compiler_params=_cparams("parallel"),
        name="nsa_combine",
    )(o_cmp, o_sel, o_win, z_g, z_b, _gate_expand())


def _mem_attn_body(q_ref, g_ref, kv_ref, o_ref):
    scale = MEM_DH ** -0.5
    q = q_ref[...]
    for h in range(MEM_H):
        k = kv_ref[:, h * MEM_DH:(h + 1) * MEM_DH].astype(BF16)
        v = kv_ref[:, (MEM_H + h) * MEM_DH:(MEM_H + h + 1) * MEM_DH].astype(BF16)
        s = _dot_nt(q[:, h * MEM_DH:(h + 1) * MEM_DH].astype(BF16), k) * scale
        e = jnp.exp(s - jnp.max(s, axis=-1, keepdims=True))
        p = e * (1.0 / jnp.sum(e, axis=-1, keepdims=True))
        o = _dot(p.astype(BF16), v)
        sl = slice(h * MEM_DH, (h + 1) * MEM_DH)
        o_ref[:, sl] = (o * _silu(g_ref[:, sl])).astype(o_ref.dtype)


def mem_attn_prompt(z_b, mkv):
    t = z_b.shape[0]
    tq = _pick(t, (512, 256, 128, 64, 32, 16))
    return pl.pallas_call(
        _mem_attn_body,
        grid=(t // tq,),
        in_specs=[pl.BlockSpec((tq, BR_W), lambda i: (i, 1)), pl.BlockSpec((tq, BR_W), lambda i: (i, 2)),
                  pl.BlockSpec(mkv.shape, lambda i: (0, 0))],
        out_specs=pl.BlockSpec((tq, BR_W), lambda i: (i, 0)),
        out_shape=jax.ShapeDtypeStruct((t, BR_W), BF16),
        compiler_params=_cparams("parallel"),
        name="mem_attn_prompt",
    )(z_b, z_b, mkv)


def _mem_attn_step_body(q_ref, g_ref, kv_ref, o_ref):
    scale = MEM_DH ** -0.5
    q = q_ref[...]
    rows_per = MEM_DH // LANE
    out_rows = []
    for h in range(MEM_H):
        qh = jnp.concatenate([q[h * rows_per + i:h * rows_per + i + 1, :] for i in range(rows_per)], axis=1)
        qh = jnp.broadcast_to(qh, (8, MEM_DH)).astype(BF16)
        k = kv_ref[:, h * MEM_DH:(h + 1) * MEM_DH].astype(BF16)
        v = kv_ref[:, (MEM_H + h) * MEM_DH:(MEM_H + h + 1) * MEM_DH].astype(BF16)
        s = _dot_nt(qh, k) * scale
        e = jnp.exp(s - jnp.max(s, axis=-1, keepdims=True))
        p = e * (1.0 / jnp.sum(e, axis=-1, keepdims=True))
        o = _dot(p.astype(BF16), v)[0:1, :]
        out_rows += [o[:, i * LANE:(i + 1) * LANE] for i in range(rows_per)]
    o_ref[...] = jnp.concatenate(out_rows, axis=0) * _silu(g_ref[...])


def mem_attn_sample(z_b3, kv_cache, layer):
    bs = z_b3.shape[0]
    mem_len, width = kv_cache.shape[2], kv_cache.shape[3]
    return pl.pallas_call(
        _mem_attn_step_body,
        grid=(bs,),
        in_specs=[pl.BlockSpec((None, 8, LANE), lambda b: (b, 1, 0)), pl.BlockSpec((None, 8, LANE), lambda b: (b, 2, 0)),
                  pl.BlockSpec((None, None, mem_len, width), lambda b: (layer, b, 0, 0))],
        out_specs=pl.BlockSpec((None, 8, LANE), lambda b: (b, 0, 0)),
        out_shape=jax.ShapeDtypeStruct((bs, 8, LANE), F32),
        compiler_params=_cparams("parallel"),
        name="mem_attn_sample",
    )(z_b3, z_b3, kv_cache)


def _paged_partial_body(npg, pt_ref, *refs):
    del pt_ref
    page_refs, w_ref, o_ref = refs[:npg], refs[npg], refs[npg + 1]
    chunks = PAGE_SIZE // CMP_STRIDE
    o_ref[...] = jnp.zeros_like(o_ref)

    unroll = 4

    def step(i, carry):
        for c in range(2 * NS_KV):
            acc = o_ref[c]
            for u in range(unroll):
                s = i * unroll + u
                xs = jnp.concatenate(
                    [r[pl.ds(s * ROWS_PER_TOKEN + c, chunks, stride=CMP_STRIDE * ROWS_PER_TOKEN), :]
                     for r in page_refs], axis=0)
                acc += _dot(xs.astype(BF16), w_ref[c // NS_KV, s])
            o_ref[c] = acc
        return carry

    lax.fori_loop(0, CMP_STRIDE // unroll, step, 0)


def compress_partial_paged(cache2, page_table, w1, layer, n_pool):
    bs, n_pages = page_table.shape
    npg = min(PAGES_PER_STEP, n_pages)
    assert n_pages % npg == 0
    chunks = PAGE_SIZE // CMP_STRIDE
    w = _cmp_weights(w1).reshape(2, CMP_STRIDE, NS_DH, 2 * NS_DH)

    def page_spec(i):
        return pl.BlockSpec((PAGE_SIZE * ROWS_PER_TOKEN, NS_DH),
                            lambda b, j, pt: (layer * n_pool + pt[b, j * npg + i], 0))

    grid_spec = pltpu.PrefetchScalarGridSpec(
        num_scalar_prefetch=1,
        grid=(bs, n_pages // npg),
        in_specs=[page_spec(i) for i in range(npg)] + [pl.BlockSpec(w.shape, lambda b, j, pt: (0, 0, 0, 0))],
        out_specs=pl.BlockSpec((2 * NS_KV, None, npg * chunks, 2 * NS_DH), lambda b, j, pt: (0, b, j, 0)),
    )
    out = pl.pallas_call(
        functools.partial(_paged_partial_body, npg),
        grid_spec=grid_spec,
        out_shape=jax.ShapeDtypeStruct((2 * NS_KV, bs, n_pages * chunks, 2 * NS_DH), F32),
        compiler_params=_cparams("parallel", "parallel"),
        name="cmp_partial_paged",
    )(page_table, *([cache2] * npg), w)
    return out.reshape(2, NS_KV * bs * n_pages * chunks, 2 * NS_DH)


def _row_group(shape):
    return lax.broadcasted_iota(jnp.int32, shape, 0) // NS_G


def _cmp_step_body(n_cmp, qpos, q_ref, kc_ref, vc_ref, o_ref, ps_ref):
    q = q_ref[...].astype(BF16)
    ncp = kc_ref.shape[1]
    s = [_dot_nt(q, kc_ref[g].astype(BF16)) for g in range(NS_KV)]
    grp = _row_group((NS_H, ncp))
    s = jnp.where(grp == 0, s[0], s[1])
    n_i = lax.broadcasted_iota(jnp.int32, (NS_H, ncp), 1)
    valid = (n_i * CMP_STRIDE + CMP_LEN - 1 <= qpos) & (n_i < n_cmp)
    sm = jnp.where(valid, s, NEG_INF)
    mx = jnp.max(sm, axis=-1, keepdims=True)
    e = jnp.where(valid, jnp.exp2(sm - mx), 0.0)
    den = jnp.sum(e, axis=-1, keepdims=True)
    p = e * (1.0 / jnp.where(den > 0.0, den, 1.0))
    pb = p.astype(BF16)
    o = [_dot(pb, vc_ref[g].astype(BF16)) for g in range(NS_KV)]
    o_ref[...] = jnp.where(_row_group((NS_H, NS_DH)) == 0, o[0], o[1])
    ps_ref[...] = jnp.concatenate(
        [jnp.sum(jnp.where(grp == g, p, 0.0), axis=0, keepdims=True) for g in range(NS_KV)], axis=0)


def nsa_cmp_sample(q_c3, kcmp, vcmp, n_cmp, qpos):
    bs = q_c3.shape[0]
    ncp = kcmp.shape[2]
    kv = pl.BlockSpec((NS_KV, None, ncp, NS_DH), lambda b: (0, b, 0, 0))
    return pl.pallas_call(
        functools.partial(_cmp_step_body, n_cmp, qpos),
        grid=(bs,),
        in_specs=[pl.BlockSpec((None, NS_H, NS_DH), lambda b: (b, 0, 0)), kv, kv],
        out_specs=[pl.BlockSpec((None, NS_H, NS_DH), lambda b: (b, 0, 0)),
                   pl.BlockSpec((None, NS_KV, ncp), lambda b: (b, 0, 0))],
        out_shape=[jax.ShapeDtypeStruct((bs, NS_H, NS_DH), F32), jax.ShapeDtypeStruct((bs, NS_KV, ncp), F32)],
        compiler_params=_cparams("parallel"),
        name="nsa_cmp_sample",
    )(q_c3, kcmp, vcmp)


def _select_body(n_sel, qpos, ps_ref, msel_ref, idx_ref):
    hi, mid, lo = _split3(ps_ref[...])
    imp = _dot(hi, msel_ref[...]) + _dot(mid, msel_ref[...]) + _dot(lo, msel_ref[...])
    j_i = lax.broadcasted_iota(jnp.int32, imp.shape, 1)
    cur = qpos // SEL_BLOCK
    forced = (j_i == 0) | (j_i == cur) | (j_i == cur - 1)
    imp = jnp.where(forced, imp + FORCE_BONUS, imp)
    imp = jnp.where(j_i <= cur, imp, NEG_INF)
    imp = jnp.where(j_i < n_sel, imp, REMOVED)
    j_f = j_i.astype(F32)
    col = lax.broadcasted_iota(jnp.int32, idx_ref.shape, 1)
    out = jnp.zeros(idx_ref.shape, F32)
    for it in range(min(SEL_TOPN, n_sel)):
        mx = jnp.max(imp, axis=-1, keepdims=True)
        first = jnp.min(jnp.where(imp == mx, j_f, float(imp.shape[1])), axis=-1, keepdims=True)
        out = jnp.where(col == it, first, out)
        imp = jnp.where(j_f == first, REMOVED, imp)
    idx_ref[...] = out.astype(jnp.int32)


def nsa_select_sample(psum2, n_cmp, n_sel, qpos):
    rows, ncp = psum2.shape
    nsp = -(-n_sel // LANE) * LANE
    msel = jnp.asarray(_cmp_to_sel(ncp, nsp, n_cmp, n_sel), BF16)
    return pl.pallas_call(
        functools.partial(_select_body, n_sel, qpos),
        grid=(1,),
        in_specs=[pl.BlockSpec((rows, ncp), lambda i: (0, 0)), pl.BlockSpec((ncp, nsp), lambda i: (0, 0))],
        out_specs=pl.BlockSpec((rows, LANE), lambda i: (0, 0)),
        out_shape=jax.ShapeDtypeStruct((rows, LANE), jnp.int32),
        compiler_params=_cparams("arbitrary"),
        name="nsa_select_sample",
    )(psum2, msel)


def _sel_step_body(n_past_blocks, qpos, pt_ref, top_ref, q_ref, *refs):
    del pt_ref
    nblk = NS_KV * SEL_PER_STEP
    blk_refs, new_ref, o_ref, m_scr, l_scr, acc_scr = refs[:nblk], *refs[nblk:nblk + 5]
    b = pl.program_id(0)
    j = pl.program_id(1)

    @pl.when(j == 0)
    def _():
        m_scr[...] = jnp.full_like(m_scr, NEG_INF)
        l_scr[...] = jnp.zeros_like(l_scr)
        acc_scr[...] = jnp.zeros_like(acc_scr)

    def slab(ref, slot, g):
        return ref[pl.ds(slot * NS_KV + g, SEL_BLOCK, stride=ROWS_PER_TOKEN), :].astype(BF16)

    q = q_ref[...]
    qb = q.astype(BF16)
    nk = SEL_PER_STEP * SEL_BLOCK
    grp_k = _row_group((NS_H, nk))
    grp_d = _row_group((NS_H, NS_DH))
    lane_blk = lax.broadcasted_iota(jnp.int32, (NS_H, nk), 1) // SEL_BLOCK
    s_g, idx_g, new_g = [], [], []
    for g in range(NS_KV):
        k_all = jnp.concatenate([slab(blk_refs[g * SEL_PER_STEP + k], 2, g) for k in range(SEL_PER_STEP)], axis=0)
        s_g.append(_dot_nt(qb, k_all))
        ids = [top_ref[b, g * SEL_TOPN + j * SEL_PER_STEP + k] for k in range(SEL_PER_STEP)]
        idx = jnp.zeros((NS_H, nk), jnp.int32)
        n_new = jnp.int32(0)
        for k, i in enumerate(ids):
            idx = jnp.where(lane_blk == k, i, idx)
            n_new = n_new + (i == n_past_blocks).astype(jnp.int32)
        idx_g.append(idx)
        new_g.append(n_new)
    s = jnp.where(grp_k == 0, s_g[0], s_g[1])
    idx = jnp.where(grp_k == 0, idx_g[0], idx_g[1])
    tok = idx * SEL_BLOCK + lax.broadcasted_iota(jnp.int32, (NS_H, nk), 1) % SEL_BLOCK
    valid = (idx < n_past_blocks) & (tok <= qpos)
    ks_new = jnp.where(grp_d == 0, new_ref[4:5, :], new_ref[5:6, :])
    vs_new = jnp.where(grp_d == 0, new_ref[6:7, :], new_ref[7:8, :])
    is_new = jnp.where(grp_d == 0, new_g[0], new_g[1]) > 0
    s_new = jnp.where(is_new, jnp.sum(q * ks_new, axis=-1, keepdims=True), NEG_INF)
    m_old = m_scr[...]
    m_new = jnp.maximum(jnp.maximum(m_old, jnp.max(jnp.where(valid, s, NEG_INF), axis=-1, keepdims=True)), s_new)
    alpha = jnp.exp2(m_old - m_new)
    p = jnp.where(valid, jnp.exp2(s - m_new[:, 0:1]), 0.0)
    p_new = jnp.where(is_new, jnp.exp2(s_new - m_new), 0.0)
    pb = p.astype(BF16)
    pv_g = []
    for g in range(NS_KV):
        v_all = jnp.concatenate([slab(blk_refs[g * SEL_PER_STEP + k], 3, g) for k in range(SEL_PER_STEP)], axis=0)
        pv_g.append(_dot(pb, v_all))
    pv = jnp.where(grp_d == 0, pv_g[0], pv_g[1])
    l_scr[...] = alpha * l_scr[...] + jnp.sum(p, axis=-1, keepdims=True) + p_new
    acc_scr[...] = alpha * acc_scr[...] + pv + p_new * vs_new
    m_scr[...] = m_new

    @pl.when(j == pl.num_programs(1) - 1)
    def _():
        o_ref[...] = acc_scr[...] / l_scr[...]


def nsa_sel_sample(q_r3, rows3, cache2, page_table, top_idx, layer, n_pool, qpos):
    bs, n_pages = page_table.shape
    halves = PAGE_SIZE // SEL_BLOCK
    n_past_blocks = n_pages * halves

    def blk(g, k):
        def imap(b, j, pt, top):
            i = top[b, g * SEL_TOPN + j * SEL_PER_STEP + k]
            page = pt[b, jnp.minimum(i // halves, n_pages - 1)]
            return ((layer * n_pool + page) * halves + i % halves, 0)
        return pl.BlockSpec((SEL_BLOCK * ROWS_PER_TOKEN, NS_DH), imap)

    vec = pl.BlockSpec((None, NS_H, NS_DH), lambda b, j, pt, top: (b, 0, 0))
    grid_spec = pltpu.PrefetchScalarGridSpec(
        num_scalar_prefetch=2,
        grid=(bs, SEL_TOPN // SEL_PER_STEP),
        in_specs=[vec] + [blk(g, k) for g in range(NS_KV) for k in range(SEL_PER_STEP)] + [vec],
        out_specs=vec,
        scratch_shapes=[pltpu.VMEM((NS_H, NS_DH), F32)] * 3,
    )
    return pl.pallas_call(
        functools.partial(_sel_step_body, n_past_blocks, qpos),
        grid_spec=grid_spec,
        out_shape=jax.ShapeDtypeStruct((bs, NS_H, NS_DH), F32),
        compiler_params=_cparams("parallel", "arbitrary"),
        name="nsa_sel_sample",
    )(page_table, top_idx, q_r3, *([cache2] * (NS_KV * SEL_PER_STEP)), rows3)


def _win_step_body(pos0, qpos, q_ref, buf_ref, new_ref, o_ref):
    q = q_ref[...]
    qb = q.astype(BF16)
    rows_per = 2 * NS_KV
    wb = buf_ref.shape[0] // rows_per

    def slab(c):
        return buf_ref[pl.ds(c, wb, stride=rows_per), :].astype(BF16)

    grp_k = _row_group((NS_H, wb))
    grp_d = _row_group((NS_H, NS_DH))
    s = jnp.where(grp_k == 0, _dot_nt(qb, slab(0)), _dot_nt(qb, slab(1)))
    kpos = pos0 + lax.broadcasted_iota(jnp.int32, (NS_H, wb), 1)
    valid = (kpos <= qpos) & (kpos > qpos - WINDOW) & (kpos >= 0)
    kw_new = jnp.where(grp_d == 0, new_ref[0:1, :], new_ref[1:2, :])
    vw_new = jnp.where(grp_d == 0, new_ref[2:3, :], new_ref[3:4, :])
    s_new = jnp.sum(q * kw_new, axis=-1, keepdims=True)
    mx = jnp.maximum(jnp.max(jnp.where(valid, s, NEG_INF), axis=-1, keepdims=True), s_new)
    p = jnp.where(valid, jnp.exp2(s - mx), 0.0)
    p_new = jnp.exp2(s_new - mx)
    den = jnp.sum(p, axis=-1, keepdims=True) + p_new
    pb = p.astype(BF16)
    pv = jnp.where(grp_d == 0, _dot(pb, slab(2)), _dot(pb, slab(3)))
    o_ref[...] = (pv + p_new * vw_new) / den


def nsa_win_sample(q_r3, win_buf2, win3, layer, wb, past_len, qpos):
    bs = q_r3.shape[0]
    return pl.pallas_call(
        functools.partial(_win_step_body, past_len - wb, qpos),
        grid=(bs,),
        in_specs=[pl.BlockSpec((None, NS_H, NS_DH), lambda b: (b, 0, 0)),
                  pl.BlockSpec((wb * 2 * NS_KV, NS_DH), lambda b: (layer * bs + b, 0)),
                  pl.BlockSpec((None, 2 * NS_KV, NS_DH), lambda b: (b, 0, 0))],
        out_specs=pl.BlockSpec((None, NS_H, NS_DH), lambda b: (b, 0, 0)),
        out_shape=jax.ShapeDtypeStruct((bs, NS_H, NS_DH), F32),
        compiler_params=_cparams("parallel"),
        name="nsa_win_sample",
    )(q_r3, win_buf2, win3)


def _project(x, norm_g, w_a, w_g, w_b):
    h = rmsnorm_rows(x, norm_g, BF16)
    return (matmul(h, w_a, name="in_proj_a"), matmul(h, w_g, name="in_proj_gate"), matmul(h, w_b, name="in_proj_b"))


def _merge_out(x, branches, z_b, w_up_b, w_out_b):
    d = x.shape[1]
    gated = merge_up(branches, z_b, w_up_b, d)
    return matmul(gated, w_out_b, res=x, name="out_proj")


def _per_group(a, width=NS_DH):
    return jnp.stack([a[:, g * width:(g + 1) * width] for g in range(NS_KV)])


def kernel(x_prompt, x_sample, mem_prompt, cache_nsa_kv, state_win_kv, state_hgrn, state_conv, cache_mem_kv,
           page_table, norm_g, final_norm_g, w_in, hg_lb_logits, hg_norm_g, cv_w, ns_pe, ns_cw1, ns_cw2,
           mem_norm_g, w_mem_kv, w_up, w_out):
    depth = w_in.shape[0]
    b_p, t, d = x_prompt.shape
    bs, ts = x_sample.shape[:2]
    assert b_p == 1 and ts == 1
    n_pages = page_table.shape[1]
    past_len = n_pages * PAGE_SIZE
    wb = state_win_kv.shape[2]
    mem_len = mem_prompt.shape[1]
    kvw = NS_KV * NS_DH

    s_lb = jax.nn.softmax(hg_lb_logits.astype(F32), axis=0)
    lower = jnp.cumsum(s_lb, axis=0) - s_lb[0]

    pos_p = jnp.arange(t, dtype=jnp.int32)
    qpos_s = past_len
    pos_s = jnp.full((bs,), qpos_s, jnp.int32)

    n_pool = cache_nsa_kv.shape[1]
    cache2 = cache_nsa_kv.reshape(depth * n_pool * PAGE_SIZE * ROWS_PER_TOKEN, NS_DH)
    win_buf2 = state_win_kv.reshape(depth * bs * wb * 2 * NS_KV, NS_DH)
    mem_cache4 = cache_mem_kv.reshape(depth, bs, mem_len, 2 * MEM_H * MEM_DH)

    nch_p = t // CMP_STRIDE
    nch_s = (past_len + 1) // CMP_STRIDE
    n_sel_s = -(-(past_len + 1) // SEL_BLOCK)

    xp = x_prompt.reshape(t, d)
    xs = x_sample.reshape(bs, d)
    rows_p, win_p, hg_p, cv_p, mkv_p = [], [], [], [], []
    rows_s, win_s, hg_s, cv_s = [], [], [], []
    for l in range(depth):
        w_l = w_in[l]
        w_a = w_l[:, :COL_A].astype(BF16)
        w_g = jnp.pad(w_l[:, COL_A:COL_B0], ((0, 0), (0, LANE - COL_GATE))).astype(BF16)
        w_b = w_l[:, COL_B0:].astype(BF16)
        w_up_b = w_up[l].astype(BF16)
        w_out_b = w_out[l].astype(BF16)

        z_a, z_g, z_b = _project(xp, norm_g[l], w_a, w_g, w_b)
        o_hg, st_hg = hgrn_prompt(z_a, lower[l], hg_norm_g[l])
        o_cv, st_cv = conv_prompt(z_a, cv_w[l], jnp.zeros((CV_K - 1, BR_W), F32))
        q_c, q_r, rows, win = nsa_prep(z_a, pos_p, BF16)
        x_k = _per_group(rows[:, 0:kvw]).reshape(NS_KV * nch_p, CMP_STRIDE * NS_DH)
        x_v = _per_group(rows[:, kvw:2 * kvw]).reshape(NS_KV * nch_p, CMP_STRIDE * NS_DH)
        cmp = compress_finish(compress_partial_rows(x_k, x_v, ns_cw1[l]), ns_pe[l], ns_cw1[l], ns_cw2[l], nch_p)
        kcmp = cmp[0].astype(BF16)
        vcmp_t = jnp.swapaxes(cmp[1], 1, 2).astype(BF16)
        o_cmp, sel_bias = nsa_cmp_select(q_c, kcmp, vcmp_t, nch_p - 1)
        ks = _per_group(rows[:, 2 * kvw:3 * kvw]).astype(BF16)
        vs_t = jnp.swapaxes(_per_group(rows[:, 3 * kvw:4 * kvw]), 1, 2).astype(BF16)
        o_sel = nsa_sel_attn(q_r, sel_bias, ks, vs_t)
        kw = _per_group(win[:, 0:kvw]).astype(BF16)
        vw_t = jnp.swapaxes(_per_group(win[:, kvw:2 * kvw]), 1, 2).astype(BF16)
        o_win = nsa_win_attn(q_r, kw, vw_t)
        o_ns = nsa_combine(o_cmp, o_sel, o_win, z_g, z_b)
        mem_h = rmsnorm_rows(mem_prompt.reshape(mem_len, d), mem_norm_g[l], BF16)
        mkv = matmul(mem_h, w_mem_kv[l].astype(BF16), name="mem_kv")
        o_mm = mem_attn_prompt(z_b, mkv)
        xp = _merge_out(xp, (o_hg, o_cv, o_ns, o_mm), z_b, w_up_b, w_out_b)
        rows_p.append(rows.reshape(1, t, 4, NS_KV, NS_DH))
        win_p.append(win[t - min(WINDOW, t):].reshape(1, min(WINDOW, t), 2, NS_KV, NS_DH))
        hg_p.append(st_hg.reshape(1, HG_H, HG_DK, HG_DV))
        cv_p.append(st_cv.reshape(1, CV_K - 1, BR_W))
        mkv_p.append(mkv.reshape(1, mem_len, 2, MEM_H, MEM_DH))

        z_a, z_g, z_b = _project(xs, norm_g[l], w_a, w_g, w_b)
        z_a3 = z_a.reshape(bs, COL_A // LANE, LANE)
        z_b3 = z_b.reshape(bs, z_b.shape[1] // LANE, LANE)
        o_hg, st_hg = hgrn_sample(z_a3, lower[l], hg_norm_g[l], state_hgrn[l])
        o_cv, st_cv = conv_sample(z_a, cv_w[l], state_conv[l])
        q_c, q_r, rows, win = nsa_prep(z_a, pos_s, F32)
        part = compress_partial_paged(cache2, page_table, ns_cw1[l], l, n_pool)
        cmp = compress_finish(part, ns_pe[l], ns_cw1[l], ns_cw2[l], nch_s)
        cmp = cmp.reshape(2, NS_KV, bs, nch_s, NS_DH)
        q_c3 = q_c.reshape(bs, NS_H, NS_DH)
        q_r3 = q_r.reshape(bs, NS_H, NS_DH)
        o_cmp, psum = nsa_cmp_sample(q_c3, cmp[0], cmp[1], nch_s - 1, qpos_s)
        top = nsa_select_sample(psum.reshape(bs * NS_KV, nch_s), nch_s - 1, n_sel_s, qpos_s)
        top = top[:, :SEL_TOPN].reshape(bs, NS_KV * SEL_TOPN)
        o_sel = nsa_sel_sample(q_r3, rows.reshape(bs, 4 * NS_KV, NS_DH), cache2, page_table, top, l, n_pool, qpos_s)
        o_win = nsa_win_sample(q_r3, win_buf2, win.reshape(bs, 2 * NS_KV, NS_DH), l, wb, past_len, qpos_s)
        o_ns = nsa_combine(o_cmp.reshape(bs, BR_W), o_sel.reshape(bs, BR_W), o_win.reshape(bs, BR_W), z_g, z_b)
        o_mm = mem_attn_sample(z_b3, mem_cache4, l)
        branches = (o_hg.reshape(bs, BR_W).astype(BF16), o_cv.astype(BF16), o_ns, o_mm.reshape(bs, BR_W).astype(BF16))
        xs = _merge_out(xs, branches, z_b, w_up_b, w_out_b)
        rows_s.append(rows.reshape(bs, 1, 4, NS_KV, NS_DH))
        win_s.append(jnp.concatenate([state_win_kv[l][:, 1:], win.reshape(bs, 1, 2, NS_KV, NS_DH)], axis=1))
        hg_s.append(st_hg)
        cv_s.append(st_cv)

    y_prompt = rmsnorm_rows(xp, final_norm_g, F32).reshape(1, t, d)
    y_sample = rmsnorm_rows(xs, final_norm_g, F32).reshape(bs, 1, d)
    return (y_prompt, y_sample, jnp.stack(rows_p), jnp.stack(win_p), jnp.stack(hg_p), jnp.stack(cv_p),
            jnp.stack(mkv_p), jnp.stack(rows_s), jnp.stack(win_s), jnp.stack(hg_s), jnp.stack(cv_s))
```

```python
import functools

import numpy as np
import jax
import jax.numpy as jnp
from jax import lax
from jax.experimental import pallas as pl
from jax.experimental.pallas import tpu as pltpu

F32 = jnp.float32
BF16 = jnp.bfloat16

BR_W = 1024
N_BRANCH = 4
HG_H = 8
HG_DK = 128
HG_DV = 128
F_MIN = 1e-30
CV_K = 3
NS_H = 8
NS_KV = 2
NS_G = NS_H // NS_KV
NS_DH = 128
CMP_LEN = 32
CMP_STRIDE = 16
SEL_BLOCK = 64
SEL_TOPN = 16
WINDOW = 512
FORCE_BONUS = 100.0
ROPE_THETA = 500000.0
ROT_DIM = NS_DH // 4
MEM_H = 4
MEM_DH = 256
NORM_EPS = 1e-6
NEG_INF = -1e30
REMOVED = -3e38
LOG2_E = 1.4426950408889634
PAGE_SIZE = 128
ROWS_PER_TOKEN = 4 * NS_KV

COL_A = 10752
COL_GATE = 24
COL_B0 = COL_A + COL_GATE

LANE = 128
HG_CHUNK = 128
HG_HEADS_PER_STEP = 2
Q_TILE = 128
K_TILE = 512
PAGES_PER_STEP = 16
SEL_PER_STEP = 4
VMEM_LIMIT = 56 * 1024 * 1024


def _cparams(*sem):
    return pltpu.CompilerParams(dimension_semantics=sem, vmem_limit_bytes=VMEM_LIMIT)


def _pick(n, cands):
    for c in cands:
        if n % c == 0:
            return c
    return n


def _silu(x):
    return x * jax.nn.sigmoid(x)


def _dot(a, b):
    return jnp.dot(a, b, preferred_element_type=F32)


def _dot_nt(a, b):
    return lax.dot_general(a, b, (((1,), (1,)), ((), ())), preferred_element_type=F32)


def _split3(x):
    hi = x.astype(BF16)
    r1 = x - hi.astype(F32)
    mid = r1.astype(BF16)
    lo = (r1 - mid.astype(F32)).astype(BF16)
    return hi, mid, lo


def _norm_body(x_ref, g_ref, o_ref):
    x = x_ref[...]
    ms = jnp.mean(x * x, axis=-1, keepdims=True)
    o_ref[...] = (x * lax.rsqrt(ms + NORM_EPS) * g_ref[...]).astype(o_ref.dtype)


def rmsnorm_rows(x, g, out_dtype):
    m, d = x.shape
    tm = _pick(m, (256, 128, 64, 32, 16, 8))
    return pl.pallas_call(
        _norm_body,
        grid=(m // tm,),
        in_specs=[pl.BlockSpec((tm, d), lambda i: (i, 0)), pl.BlockSpec((1, d), lambda i: (0, 0))],
        out_specs=pl.BlockSpec((tm, d), lambda i: (i, 0)),
        out_shape=jax.ShapeDtypeStruct((m, d), out_dtype),
        compiler_params=_cparams("parallel"),
        name="rmsnorm",
    )(x, g.reshape(1, d).astype(F32))


def _mm_body(a_ref, b_ref, o_ref):
    o_ref[...] = _dot(a_ref[...].astype(BF16), b_ref[...]).astype(o_ref.dtype)


def _mm_res_body(a_ref, b_ref, r_ref, o_ref):
    o_ref[...] = r_ref[...] + _dot(a_ref[...].astype(BF16), b_ref[...])


def matmul(a, b, res=None, out_dtype=F32, name="matmul"):
    m, k = a.shape
    n = b.shape[1]
    tm = _pick(m, (1024, 512, 256, 128, 64, 32, 16, 8))
    tn = _pick(n, (512, 256, 128))
    in_specs = [pl.BlockSpec((tm, k), lambda i, j: (i, 0)), pl.BlockSpec((k, tn), lambda i, j: (0, j))]
    args = [a, b]
    body = _mm_body
    if res is not None:
        in_specs.append(pl.BlockSpec((tm, tn), lambda i, j: (i, j)))
        args.append(res)
        body = _mm_res_body
    return pl.pallas_call(
        body,
        grid=(m // tm, n // tn),
        in_specs=in_specs,
        out_specs=pl.BlockSpec((tm, tn), lambda i, j: (i, j)),
        out_shape=jax.ShapeDtypeStruct((m, n), out_dtype),
        compiler_params=_cparams("parallel", "parallel"),
        name=name,
    )(*args)


def _merge_body(b0, b1, b2, b3, g0, g1, g2, g3, w0, w1, w2, w3, o_ref):
    acc = jax.nn.sigmoid(g0[...]) * _dot(b0[...], w0[...])
    acc += jax.nn.sigmoid(g1[...]) * _dot(b1[...], w1[...])
    acc += jax.nn.sigmoid(g2[...]) * _dot(b2[...], w2[...])
    acc += jax.nn.sigmoid(g3[...]) * _dot(b3[...], w3[...])
    o_ref[...] = acc.astype(o_ref.dtype)


def merge_up(branches, z_b, w_up_bf16, d_model):
    m = branches[0].shape[0]
    tm = _pick(m, (512, 256, 128, 64, 32, 16))
    tn = 512
    nj = d_model // tn
    gate_col0 = (z_b.shape[1] - N_BRANCH * d_model) // tn
    br_specs = [pl.BlockSpec((tm, BR_W), lambda i, j: (i, 0)) for _ in range(N_BRANCH)]
    g_specs = [pl.BlockSpec((tm, tn), functools.partial(lambda i, j, n: (i, gate_col0 + n * nj + j), n=n))
               for n in range(N_BRANCH)]
    w_specs = [pl.BlockSpec((None, BR_W, tn), functools.partial(lambda i, j, n: (n, 0, j), n=n))
               for n in range(N_BRANCH)]
    return pl.pallas_call(
        _merge_body,
        grid=(m // tm, nj),
        in_specs=br_specs + g_specs + w_specs,
        out_specs=pl.BlockSpec((tm, tn), lambda i, j: (i, j)),
        out_shape=jax.ShapeDtypeStruct((m, d_model), BF16),
        compiler_params=_cparams("parallel", "parallel"),
        name="merge_up",
    )(*branches, z_b, z_b, z_b, z_b, w_up_bf16, w_up_bf16, w_up_bf16, w_up_bf16)


def _hgrn_consts(c):
    nlev = int(round(np.log2(c)))
    t = np.arange(c)[:, None]
    r = np.arange(c)[None, :]
    blocks = [r <= t]
    masks = [t == r]
    for lv in range(nlev):
        h = c >> (lv + 1)
        mid = (t // (2 * h)) * (2 * h) + h
        blocks.append(np.where(t >= mid, (r >= mid) & (r <= t), (r > t) & (r <= mid - 1)))
        same = (t // (2 * h)) == (r // (2 * h))
        masks.append(same & (t % (2 * h) >= h) & (r % (2 * h) < h))
    blocks.append(r > t)
    l_all = np.concatenate(blocks, axis=0).astype(np.float32)
    return jnp.asarray(l_all, BF16), jnp.asarray(np.stack(masks).astype(np.float32)), nlev


def _hgrn_body(nlev, q_ref, z_ref, v_ref, g_ref, lb_ref, ng_ref, l_ref, mask_ref, o_ref, s_ref, st_scr):
    c_idx = pl.program_id(1)
    c = q_ref.shape[0]

    @pl.when(c_idx == 0)
    def _():
        st_scr[...] = jnp.zeros_like(st_scr)

    for hh in range(HG_HEADS_PER_STEP):
        sl = slice(hh * HG_DK, (hh + 1) * HG_DK)
        lb = lb_ref[:, sl]
        q = _silu(q_ref[:, sl])
        z = z_ref[:, sl]
        f = lb + (1.0 - lb) * jax.nn.sigmoid(z)
        logf = jnp.log(jnp.maximum(f, F_MIN))
        k = (1.0 - lb) * jax.nn.sigmoid(-z)
        v = v_ref[:, sl]

        hi, mid, lo = _split3(logf)
        e3 = _dot(l_ref[...], jnp.concatenate([hi, mid, lo], axis=1))
        dk = HG_DK
        x = jnp.exp(e3[:, :dk] + (e3[:, dk:2 * dk] + e3[:, 2 * dk:]))
        eb = x[0:c]
        est = x[(nlev + 1) * c:(nlev + 2) * c]

        st = st_scr[hh]
        inter = _dot_nt((q * eb).astype(BF16), st.astype(BF16))
        att = mask_ref[0] * _dot_nt(q.astype(BF16), k.astype(BF16))
        for lv in range(nlev):
            fac = x[(1 + lv) * c:(2 + lv) * c]
            att += mask_ref[1 + lv] * _dot_nt((q * fac).astype(BF16), (k * fac).astype(BF16))
        o = inter + _dot(att.astype(BF16), v.astype(BF16))
        o = o * lax.rsqrt(jnp.mean(o * o, axis=-1, keepdims=True) + NORM_EPS) * ng_ref[:, sl]
        o_ref[:, sl] = (o * _silu(g_ref[:, sl])).astype(o_ref.dtype)

        st_scr[hh] = st * eb[c - 1:c, :] + _dot(v.T.astype(BF16), (k * est).astype(BF16))

    @pl.when(c_idx == pl.num_programs(1) - 1)
    def _():
        for hh in range(HG_HEADS_PER_STEP):
            s_ref[hh] = st_scr[hh].T


def hgrn_prompt(z_a, lb, norm_g):
    t = z_a.shape[0]
    c = HG_CHUNK
    l_all, masks, nlev = _hgrn_consts(c)
    hp = HG_HEADS_PER_STEP
    steps = HG_H // hp
    width = hp * HG_DK

    def col(k):
        return pl.BlockSpec((c, width), functools.partial(lambda h, i, k: (i, k * steps + h), k=k))

    vec = pl.BlockSpec((None, 1, width), lambda h, i: (h, 0, 0))
    return pl.pallas_call(
        functools.partial(_hgrn_body, nlev),
        grid=(steps, t // c),
        in_specs=[col(0), col(1), col(2), col(3), vec, vec,
                  pl.BlockSpec(l_all.shape, lambda h, i: (0, 0)),
                  pl.BlockSpec(masks.shape, lambda h, i: (0, 0, 0))],
        out_specs=[pl.BlockSpec((c, width), lambda h, i: (i, h)),
                   pl.BlockSpec((hp, HG_DK, HG_DV), lambda h, i: (h, 0, 0))],
        out_shape=[jax.ShapeDtypeStruct((t, BR_W), BF16),
                   jax.ShapeDtypeStruct((HG_H, HG_DK, HG_DV), F32)],
        scratch_shapes=[pltpu.VMEM((hp, HG_DV, HG_DK), F32)],
        compiler_params=_cparams("parallel", "arbitrary"),
        name="hgrn_prompt",
    )(z_a, z_a, z_a, z_a, lb.reshape(steps, 1, width), norm_g.reshape(steps, 1, width), l_all, masks)


def _hgrn_step_body(q_ref, z_ref, v_ref, g_ref, lb_ref, ng_ref, s0_ref, o_ref, s_ref):
    lb = lb_ref[...]
    q = _silu(q_ref[...])
    z = z_ref[...]
    f = jnp.maximum(lb + (1.0 - lb) * jax.nn.sigmoid(z), F_MIN)
    k = (1.0 - lb) * jax.nn.sigmoid(-z)
    v = v_ref[...]
    rows = []
    for h in range(HG_H):
        def colb(a):
            return jnp.broadcast_to(a[h:h + 1, :], (HG_DK, HG_DK)).T
        s_new = colb(f) * s0_ref[h] + colb(k) * v[h:h + 1, :]
        s_ref[h] = s_new
        rows.append(jnp.sum(colb(q) * s_new, axis=0, keepdims=True))
    o = jnp.concatenate(rows, axis=0)
    o = o * lax.rsqrt(jnp.mean(o * o, axis=-1, keepdims=True) + NORM_EPS) * ng_ref[...]
    o_ref[...] = o * _silu(g_ref[...])


def hgrn_sample(z_a3, lb, norm_g, s0):
    bs = z_a3.shape[0]

    def grp(k):
        return pl.BlockSpec((None, HG_H, LANE), functools.partial(lambda b, k: (b, k, 0), k=k))

    vec = pl.BlockSpec((HG_H, LANE), lambda b: (0, 0))
    st = pl.BlockSpec((None, HG_H, HG_DK, HG_DV), lambda b: (b, 0, 0, 0))
    return pl.pallas_call(
        _hgrn_step_body,
        grid=(bs,),
        in_specs=[grp(0), grp(1), grp(2), grp(3), vec, vec, st],
        out_specs=[pl.BlockSpec((None, HG_H, LANE), lambda b: (b, 0, 0)), st],
        out_shape=[jax.ShapeDtypeStruct((bs, HG_H, LANE), F32),
                   jax.ShapeDtypeStruct((bs, HG_H, HG_DK, HG_DV), F32)],
        compiler_params=_cparams("parallel"),
        name="hgrn_sample",
    )(z_a3, z_a3, z_a3, z_a3, lb.reshape(HG_H, LANE), norm_g.reshape(HG_H, LANE), s0)


def _conv_body(u_ref, b_ref, c_ref, g_ref, w_ref, prev_ref, o_ref, last_ref, carry):
    @pl.when(pl.program_id(0) == 0)
    def _():
        carry[...] = prev_ref[...]

    v = c_ref[...] * u_ref[...]
    tm = v.shape[0]
    row = lax.broadcasted_iota(jnp.int32, v.shape, 0)
    p1 = carry[7:8, :]
    p2 = carry[6:7, :]
    v1 = jnp.where(row == 0, p1, pltpu.roll(v, 1, 0))
    v2 = jnp.where(row == 0, p2, jnp.where(row == 1, p1, pltpu.roll(v, 2, 0)))
    w = w_ref[...]
    y = w[0:1, :] * v2 + w[1:2, :] * v1 + w[2:3, :] * v
    o_ref[...] = (b_ref[...] * y * _silu(g_ref[...])).astype(o_ref.dtype)
    tail = v[tm - 8:tm, :]
    carry[...] = tail
    last_ref[...] = tail


def conv_prompt(z_a, w, prev):
    t = z_a.shape[0]
    tm = _pick(t, (256, 128, 64, 32, 16, 8))

    def col(k):
        return pl.BlockSpec((tm, BR_W), functools.partial(lambda i, k: (i, 4 + k), k=k))

    w8 = jnp.zeros((8, BR_W), F32).at[:CV_K].set(w.astype(F32))
    prev8 = jnp.zeros((8, BR_W), F32).at[8 - (CV_K - 1):].set(prev.astype(F32))
    full8 = pl.BlockSpec((8, BR_W), lambda i: (0, 0))
    o, last = pl.pallas_call(
        _conv_body,
        grid=(t // tm,),
        in_specs=[col(0), col(1), col(2), col(3), full8, full8],
        out_specs=[pl.BlockSpec((tm, BR_W), lambda i: (i, 0)), full8],
        out_shape=[jax.ShapeDtypeStruct((t, BR_W), BF16), jax.ShapeDtypeStruct((8, BR_W), F32)],
        scratch_shapes=[pltpu.VMEM((8, BR_W), F32)],
        compiler_params=_cparams("arbitrary"),
        name="conv_prompt",
    )(z_a, z_a, z_a, z_a, w8, prev8)
    return o, last[8 - (CV_K - 1):]


def _conv_step_body(u_ref, b_ref, c_ref, g_ref, w_ref, p0_ref, p1_ref, o_ref, v_ref):
    v = c_ref[...] * u_ref[...]
    w = w_ref[...]
    y = w[0:1, :] * p0_ref[...] + w[1:2, :] * p1_ref[...] + w[2:3, :] * v
    o_ref[...] = b_ref[...] * y * _silu(g_ref[...])
    v_ref[...] = v


def conv_sample(z_a, w, prev):
    bs = z_a.shape[0]

    def col(k):
        return pl.BlockSpec((bs, BR_W), functools.partial(lambda i, k: (0, 4 + k), k=k))

    w8 = jnp.zeros((8, BR_W), F32).at[:CV_K].set(w.astype(F32))
    full = pl.BlockSpec((bs, BR_W), lambda i: (0, 0))
    o, v = pl.pallas_call(
        _conv_step_body,
        grid=(1,),
        in_specs=[col(0), col(1), col(2), col(3), pl.BlockSpec((8, BR_W), lambda i: (0, 0)), full, full],
        out_specs=[full, full],
        out_shape=[jax.ShapeDtypeStruct((bs, BR_W), F32), jax.ShapeDtypeStruct((bs, BR_W), F32)],
        compiler_params=_cparams("arbitrary"),
        name="conv_sample",
    )(z_a, z_a, z_a, z_a, w8, prev[:, 0], prev[:, 1])
    return o, jnp.stack([prev[:, 1], v], axis=1)


def _rope_tables(pos):
    half = ROT_DIM // 2
    inv = ROPE_THETA ** (-2.0 * jnp.arange(half, dtype=F32) / ROT_DIM)
    ang = pos.astype(F32)[:, None] * inv[None, :]
    cos, sin = jnp.cos(ang), jnp.sin(ang)
    m = pos.shape[0]
    ones = jnp.ones((m, NS_DH - ROT_DIM), F32)
    zeros = jnp.zeros((m, NS_DH - ROT_DIM), F32)
    zh = jnp.zeros((m, half), F32)
    cos_t = jnp.concatenate([cos, cos, ones], axis=1)
    sin_a = jnp.concatenate([zh, sin, zeros], axis=1)
    sin_b = jnp.concatenate([-sin, zh, zeros], axis=1)
    return cos_t, sin_a, sin_b


def _rope(x, cos_t, sin_a, sin_b):
    n = x.shape[1] // NS_DH
    half = ROT_DIM // 2

    def tile(a):
        return a if n == 1 else jnp.concatenate([a] * n, axis=1)

    return (x * tile(cos_t) + pltpu.roll(x, half, 1) * tile(sin_a)
            + pltpu.roll(x, x.shape[1] - half, 1) * tile(sin_b))


def _nsa_prep_body(q_ref, cv_ref, ks_ref, vs_ref, kw_ref, vw_ref, cos_ref, sa_ref, sb_ref,
                   qc_ref, qr_ref, rows_ref, win_ref, *attn_refs):
    cos_t, sin_a, sin_b = cos_ref[...], sa_ref[...], sb_ref[...]
    scale = NS_DH ** -0.5 * LOG2_E
    q = q_ref[...]
    qc_ref[...] = (q * scale).astype(qc_ref.dtype)
    qr_ref[...] = (_rope(q, cos_t, sin_a, sin_b) * scale).astype(qr_ref.dtype)
    kvw = NS_KV * NS_DH
    ks = _rope(ks_ref[...], cos_t, sin_a, sin_b)
    kw = _rope(kw_ref[...], cos_t, sin_a, sin_b)
    vs = vs_ref[...]
    vw = vw_ref[...]
    rows_ref[:, 0:2 * kvw] = cv_ref[...]
    rows_ref[:, 2 * kvw:3 * kvw] = ks
    rows_ref[:, 3 * kvw:4 * kvw] = vs
    win_ref[:, 0:kvw] = kw
    win_ref[:, kvw:2 * kvw] = vw
    if attn_refs:
        ksx_ref, kwb_ref, vst_ref, vwt_ref = attn_refs
        tm = q.shape[0]
        tok = pl.program_id(0) * tm + lax.broadcasted_iota(jnp.int32, (tm, LANE), 0)
        onehot = jnp.where(tok // SEL_BLOCK == lax.broadcasted_iota(jnp.int32, (tm, LANE), 1),
                           1.0, 0.0).astype(ksx_ref.dtype)
        ones_rows = jnp.where(lax.broadcasted_iota(jnp.int32, (V_EXT_ROWS - NS_DH, tm), 0) == 0,
                              1.0, 0.0).astype(vst_ref.dtype)
        for g in range(NS_KV):
            sl = slice(g * NS_DH, (g + 1) * NS_DH)
            ksx_ref[g, :, 0:NS_DH] = ks[:, sl].astype(ksx_ref.dtype)
            ksx_ref[g, :, NS_DH:NS_DH + LANE] = onehot
            kwb_ref[g] = kw[:, sl].astype(kwb_ref.dtype)
            vst_ref[g, 0:NS_DH, :] = vs[:, sl].T.astype(vst_ref.dtype)
            vst_ref[g, NS_DH:V_EXT_ROWS, :] = ones_rows
            vwt_ref[g] = vw[:, sl].T.astype(vwt_ref.dtype)


def nsa_prep(z_a, pos, q_dtype, attn_layouts=False):
    m = z_a.shape[0]
    tm = _pick(m, (256, 128, 64, 32, 16, 8))
    kvw = NS_KV * NS_DH
    c0 = 8 * BR_W // kvw
    tabs = _rope_tables(pos)

    def col(k, width=kvw):
        return pl.BlockSpec((tm, width), lambda i: (i, k))

    tab = pl.BlockSpec((tm, NS_DH), lambda i: (i, 0))
    out_specs = [pl.BlockSpec((tm, BR_W), lambda i: (i, 0)), pl.BlockSpec((tm, BR_W), lambda i: (i, 0)),
                 pl.BlockSpec((tm, 4 * kvw), lambda i: (i, 0)), pl.BlockSpec((tm, 2 * kvw), lambda i: (i, 0))]
    out_shape = [jax.ShapeDtypeStruct((m, BR_W), q_dtype), jax.ShapeDtypeStruct((m, BR_W), q_dtype),
                 jax.ShapeDtypeStruct((m, 4 * kvw), F32), jax.ShapeDtypeStruct((m, 2 * kvw), F32)]
    if attn_layouts:
        out_specs += [pl.BlockSpec((NS_KV, tm, NS_DH + LANE), lambda i: (0, i, 0)),
                      pl.BlockSpec((NS_KV, tm, NS_DH), lambda i: (0, i, 0)),
                      pl.BlockSpec((NS_KV, V_EXT_ROWS, tm), lambda i: (0, 0, i)),
                      pl.BlockSpec((NS_KV, NS_DH, tm), lambda i: (0, 0, i))]
        out_shape += [jax.ShapeDtypeStruct((NS_KV, m, NS_DH + LANE), BF16),
                      jax.ShapeDtypeStruct((NS_KV, m, NS_DH), BF16),
                      jax.ShapeDtypeStruct((NS_KV, V_EXT_ROWS, m), BF16),
                      jax.ShapeDtypeStruct((NS_KV, NS_DH, m), BF16)]
    return pl.pallas_call(
        _nsa_prep_body,
        grid=(m // tm,),
        in_specs=[col(8, BR_W), col((c0 + 4) // 2, 2 * kvw), col(c0 + 6), col(c0 + 7), col(c0 + 8), col(c0 + 9),
                  tab, tab, tab],
        out_specs=out_specs,
        out_shape=out_shape,
        compiler_params=_cparams("parallel"),
        name="nsa_prep",
    )(z_a, z_a, z_a, z_a, z_a, z_a, *tabs)


def _cmp_finish_body(n_valid, p_ref, pe_ref, w2_ref, o_ref):
    p = p_ref[...]
    n = p.shape[0]
    nxt = pltpu.roll(p[:, NS_DH:], n - 1, 0)
    hid = _silu(p[:, :NS_DH] + nxt + pe_ref[0:1, :])
    out = _dot(hid.astype(BF16), w2_ref[...])
    row = lax.broadcasted_iota(jnp.int32, out.shape, 0)
    o_ref[...] = jnp.where(row < n_valid, out, 0.0)


def _cmp_weights(w1):
    kdim = CMP_STRIDE * NS_DH
    w1b = w1.astype(BF16)
    return jnp.concatenate([w1b[:, :kdim], w1b[:, kdim:]], axis=2)


def _partial_rows_body(x_ref, w_ref, o_ref):
    kv = pl.program_id(0) // NS_KV
    nch = o_ref.shape[0]
    acc = jnp.zeros(o_ref.shape, F32)
    for s in range(CMP_STRIDE):
        acc += _dot(x_ref[pl.ds(s, nch, stride=CMP_STRIDE), :].astype(BF16), w_ref[kv, s])
    o_ref[...] = acc


def compress_partial_rows(z_a, w1):
    t = z_a.shape[0]
    nch = t // CMP_STRIDE
    col0 = (8 * BR_W + BR_W) // NS_DH
    w = _cmp_weights(w1).reshape(2, CMP_STRIDE, NS_DH, 2 * NS_DH)
    out = pl.pallas_call(
        _partial_rows_body,
        grid=(2 * NS_KV,),
        in_specs=[pl.BlockSpec((t, NS_DH), lambda c: (0, col0 + c)), pl.BlockSpec(w.shape, lambda c: (0, 0, 0, 0))],
        out_specs=pl.BlockSpec((None, nch, 2 * NS_DH), lambda c: (c, 0, 0)),
        out_shape=jax.ShapeDtypeStruct((2 * NS_KV, nch, 2 * NS_DH), F32),
        compiler_params=_cparams("parallel"),
        name="cmp_partial_rows",
    )(z_a, w)
    return out.reshape(2, NS_KV * nch, 2 * NS_DH)


def compress_pe_term(pe, w1):
    w1b = w1.astype(BF16)
    pe8 = jnp.broadcast_to(pe.reshape(2, 1, CMP_LEN * NS_DH), (2, 8, CMP_LEN * NS_DH)).astype(BF16)
    return jnp.stack([matmul(pe8[a], w1b[a], name="cmp_pe_term") for a in range(2)])


def compress_finish(p, pe_term, w2, nch):
    groups = p.shape[1] // nch
    out = pl.pallas_call(
        functools.partial(_cmp_finish_body, nch - 1),
        grid=(2, groups),
        in_specs=[pl.BlockSpec((None, nch, 2 * NS_DH), lambda a, g: (a, g, 0)),
                  pl.BlockSpec((None, 8, NS_DH), lambda a, g: (a, 0, 0)),
                  pl.BlockSpec((None, NS_DH, NS_DH), lambda a, g: (a, 0, 0))],
        out_specs=pl.BlockSpec((None, nch, NS_DH), lambda a, g: (a, g, 0)),
        out_shape=jax.ShapeDtypeStruct((2, groups * nch, NS_DH), F32),
        compiler_params=_cparams("parallel", "parallel"),
        name="cmp_finish",
    )(p, pe_term, w2.astype(BF16))
    return out.reshape(2, groups, nch, NS_DH)


def _cmp_to_sel(n_cmp_pad, n_sel_pad, n_cmp, n_sel):
    cs = np.arange(n_cmp_pad)[:, None] * CMP_STRIDE
    ss = np.arange(n_sel_pad)[None, :] * SEL_BLOCK
    ov = np.minimum(cs + CMP_LEN, ss + SEL_BLOCK) - np.maximum(cs, ss)
    m = np.clip(ov, 0, None).astype(np.float32) / CMP_LEN
    m[n_cmp:, :] = 0.0
    m[:, n_sel:] = 0.0
    return m


def _stack_heads(q):
    return jnp.concatenate([q[:, r * NS_DH:(r + 1) * NS_DH] for r in range(NS_G)], axis=0)


def _tile_heads(a):
    return jnp.concatenate([a] * NS_G, axis=1)


def _store_heads_t(o_ref, o_t):
    tq = o_t.shape[1] // NS_G
    for r in range(NS_G):
        o_ref[:, r * NS_DH:(r + 1) * NS_DH] = o_t[:, r * tq:(r + 1) * tq].T


def _cmp_sel_body(n_cmp, q_ref, kc_ref, vct_ref, msel_ref, o_ref, bias_ref):
    gw = NS_G * NS_DH
    for g in range(NS_KV):
        _cmp_sel_group(n_cmp, q_ref.at[:, g * gw:(g + 1) * gw], kc_ref.at[g], vct_ref.at[g], msel_ref,
                       o_ref.at[:, g * gw:(g + 1) * gw], bias_ref.at[g])


def _cmp_sel_group(n_cmp, q_ref, kc_ref, vct_ref, msel_ref, o_ref, bias_ref):
    tq = q_ref.shape[0]
    t0 = pl.program_id(0) * tq
    qs = _stack_heads(q_ref[...])
    s_t = _dot_nt(kc_ref[...], qs)
    ncp = s_t.shape[0]
    n_i = lax.broadcasted_iota(jnp.int32, (ncp, tq), 0)
    qpos = t0 + lax.broadcasted_iota(jnp.int32, (ncp, tq), 1)
    valid = _tile_heads((n_i * CMP_STRIDE + CMP_LEN - 1 <= qpos) & (n_i < n_cmp))
    sm = jnp.where(valid, s_t, NEG_INF)
    mx = jnp.max(sm, axis=0, keepdims=True)
    e = jnp.where(valid, jnp.exp2(sm - mx), 0.0)
    den = jnp.sum(e, axis=0, keepdims=True)
    p = e * (1.0 / jnp.where(den > 0.0, den, 1.0))
    _store_heads_t(o_ref, _dot(vct_ref[...], p.astype(BF16)))

    psum = p[:, 0:tq]
    for r in range(1, NS_G):
        psum += p[:, r * tq:(r + 1) * tq]
    hi, mid, lo = _split3(psum)
    i3 = _dot(msel_ref[...], jnp.concatenate([hi, mid, lo], axis=1))
    imp = i3[:, :tq] + i3[:, tq:2 * tq] + i3[:, 2 * tq:]
    nb = imp.shape[0]
    j_i = lax.broadcasted_iota(jnp.int32, (nb, tq), 0)
    cur = (t0 + lax.broadcasted_iota(jnp.int32, (nb, tq), 1)) // SEL_BLOCK
    forced = (j_i == 0) | (j_i == cur) | (j_i == cur - 1)
    imp = jnp.where(forced, imp + FORCE_BONUS, imp)
    imp = jnp.where(j_i <= cur, imp, NEG_INF)
    j_f = j_i.astype(F32)
    sel = jnp.zeros((nb, tq), F32)
    for _ in range(min(SEL_TOPN, nb)):
        mx = jnp.max(imp, axis=0, keepdims=True)
        first = jnp.min(jnp.where(imp == mx, j_f, float(nb)), axis=0, keepdims=True)
        hit = j_f == first
        sel = jnp.where(hit, 1.0, sel)
        imp = jnp.where(hit, REMOVED, imp)
    bias_t = jnp.where((sel > 0.5) & (j_i <= cur), 0.0, NEG_INF)
    if nb < LANE:
        bias_t = jnp.concatenate([bias_t, jnp.full((LANE - nb, tq), NEG_INF, F32)], axis=0)
    bias_ref[...] = bias_t.T.astype(bias_ref.dtype)


def nsa_cmp_select(q_c, kcmp, vcmp_t, n_cmp):
    t = q_c.shape[0]
    ncp = kcmp.shape[1]
    n_sel = t // SEL_BLOCK
    assert n_sel <= LANE
    msel_t = jnp.asarray(_cmp_to_sel(ncp, n_sel, n_cmp, n_sel).T, BF16)
    return pl.pallas_call(
        functools.partial(_cmp_sel_body, n_cmp),
        grid=(t // Q_TILE,),
        in_specs=[pl.BlockSpec((Q_TILE, BR_W), lambda i: (i, 0)),
                  pl.BlockSpec((NS_KV, ncp, NS_DH), lambda i: (0, 0, 0)),
                  pl.BlockSpec((NS_KV, NS_DH, ncp), lambda i: (0, 0, 0)),
                  pl.BlockSpec((n_sel, ncp), lambda i: (0, 0))],
        out_specs=[pl.BlockSpec((Q_TILE, BR_W), lambda i: (i, 0)),
                   pl.BlockSpec((NS_KV, Q_TILE, LANE), lambda i: (0, i, 0))],
        out_shape=[jax.ShapeDtypeStruct((t, BR_W), F32), jax.ShapeDtypeStruct((NS_KV, t, LANE), BF16)],
        compiler_params=_cparams("parallel"),
        name="nsa_cmp_select",
    )(q_c, kcmp, vcmp_t, msel_t)


V_EXT_ROWS = NS_DH + 16


def _sel_attn_body(q_ref, bias_ref, k_ref, vt_ref, o_ref, m_scr, acc_scr, s_scr):
    tq = q_ref.shape[0]
    t0 = pl.program_id(0) * tq
    gw = NS_G * NS_DH
    qx = [jnp.concatenate([_stack_heads(q_ref[:, g * gw:(g + 1) * gw]),
                           jnp.concatenate([bias_ref[g]] * NS_G, axis=0)], axis=1) for g in range(NS_KV)]
    m_scr[...] = jnp.full_like(m_scr, NEG_INF)
    acc_scr[...] = jnp.zeros_like(acc_scr)
    n_tiles = (t0 + tq + K_TILE - 1) // K_TILE

    def scores(kt, slot):
        k0 = pl.multiple_of(kt * K_TILE, K_TILE)
        for g in range(NS_KV):
            s_scr[slot, g] = _dot_nt(k_ref[g, pl.ds(k0, K_TILE), :], qx[g])

    def consume(kt, slot, causal):
        k0 = pl.multiple_of(kt * K_TILE, K_TILE)
        for g in range(NS_KV):
            s_m = s_scr[slot, g]
            if causal:
                key = k0 + lax.broadcasted_iota(jnp.int32, (K_TILE, tq), 0)
                qpos = t0 + lax.broadcasted_iota(jnp.int32, (K_TILE, tq), 1)
                s_m = s_m + _tile_heads(jnp.where(key <= qpos, 0.0, NEG_INF))
            m_old = m_scr[g]
            m_new = jnp.maximum(m_old, jnp.max(s_m, axis=0, keepdims=True))
            alpha = jnp.exp2(m_old - m_new)
            p = jnp.exp2(s_m - m_new)
            acc_scr[g] = alpha * acc_scr[g] + _dot(vt_ref[g, :, pl.ds(k0, K_TILE)], p.astype(BF16))
            m_scr[g] = m_new

    scores(0, 0)
    n_before = n_tiles - 1

    def pair(i, carry):
        kt = 2 * i
        scores(kt + 1, 1)
        consume(kt, 0, False)
        scores(kt + 2, 0)
        consume(kt + 1, 1, False)
        return carry

    lax.fori_loop(0, n_before // 2, pair, 0)
    odd = n_before % 2

    @pl.when(odd == 1)
    def _():
        scores(n_tiles - 1, 1)
        consume(n_tiles - 2, 0, False)

    consume(n_tiles - 1, odd, True)
    for g in range(NS_KV):
        acc = acc_scr[g]
        _store_heads_t(o_ref.at[:, g * gw:(g + 1) * gw], acc[0:NS_DH] * (1.0 / acc[NS_DH:NS_DH + 1]))


def nsa_sel_attn(q_r, bias, ks_ext, vs_ext):
    t = q_r.shape[0]
    gw = NS_G * NS_DH
    return pl.pallas_call(
        _sel_attn_body,
        grid=(t // Q_TILE,),
        in_specs=[pl.BlockSpec((Q_TILE, NS_KV * gw), lambda i: (i, 0)),
                  pl.BlockSpec((NS_KV, Q_TILE, LANE), lambda i: (0, i, 0)),
                  pl.BlockSpec((NS_KV, t, NS_DH + LANE), lambda i: (0, 0, 0)),
                  pl.BlockSpec((NS_KV, V_EXT_ROWS, t), lambda i: (0, 0, 0))],
        out_specs=pl.BlockSpec((Q_TILE, NS_KV * gw), lambda i: (i, 0)),
        out_shape=jax.ShapeDtypeStruct((t, BR_W), F32),
        scratch_shapes=[pltpu.VMEM((NS_KV, 1, NS_G * Q_TILE), F32),
                        pltpu.VMEM((NS_KV, V_EXT_ROWS, NS_G * Q_TILE), F32),
                        pltpu.VMEM((2, NS_KV, K_TILE, NS_G * Q_TILE), F32)],
        compiler_params=_cparams("parallel"),
        name="nsa_sel_attn",
    )(q_r, bias, ks_ext, vs_ext)


def _win_attn_body(span, q_ref, k_ref, vt_ref, o_ref):
    tq = q_ref.shape[0]
    t0 = pl.program_id(0) * tq
    gw = NS_G * NS_DH
    start = pl.multiple_of(jnp.maximum(t0 - WINDOW, 0), Q_TILE)
    key = start + lax.broadcasted_iota(jnp.int32, (span, tq), 0)
    qpos = t0 + lax.broadcasted_iota(jnp.int32, (span, tq), 1)
    bias = _tile_heads(jnp.where((key <= qpos) & (key > qpos - WINDOW), 0.0, NEG_INF))
    for g in range(NS_KV):
        qs = _stack_heads(q_ref[:, g * gw:(g + 1) * gw])
        sm = _dot_nt(k_ref[g, pl.ds(start, span), :], qs) + bias
        mx = jnp.max(sm, axis=0, keepdims=True)
        e = jnp.exp2(sm - mx)
        den = jnp.sum(e, axis=0, keepdims=True)
        p = e * (1.0 / den)
        _store_heads_t(o_ref.at[:, g * gw:(g + 1) * gw], _dot(vt_ref[g, :, pl.ds(start, span)], p.astype(BF16)))


def nsa_win_attn(q_r, kw, vw_t):
    t = q_r.shape[0]
    span = min(WINDOW + Q_TILE, t)
    return pl.pallas_call(
        functools.partial(_win_attn_body, span),
        grid=(t // Q_TILE,),
        in_specs=[pl.BlockSpec((Q_TILE, BR_W), lambda i: (i, 0)),
                  pl.BlockSpec((NS_KV, t, NS_DH), lambda i: (0, 0, 0)),
                  pl.BlockSpec((NS_KV, NS_DH, t), lambda i: (0, 0, 0))],
        out_specs=pl.BlockSpec((Q_TILE, BR_W), lambda i: (i, 0)),
        out_shape=jax.ShapeDtypeStruct((t, BR_W), F32),
        compiler_params=_cparams("parallel"),
        name="nsa_win_attn",
    )(q_r, kw, vw_t)


def _gate_expand():
    e = np.zeros((3, LANE, BR_W), np.float32)
    for h in range(NS_H):
        for c in range(3):
            e[c, h * 3 + c, h * NS_DH:(h + 1) * NS_DH] = 1.0
    return jnp.asarray(e, BF16)


def _combine_body(oc_ref, os_ref, ow_ref, gate_ref, g_ref, e_ref, o_ref):
    gs = jax.nn.sigmoid(gate_ref[...])
    hi = gs.astype(BF16)
    lo = (gs - hi.astype(F32)).astype(BF16)

    def expand(c):
        return _dot(hi, e_ref[c]) + _dot(lo, e_ref[c])

    o = expand(0) * oc_ref[...] + expand(1) * os_ref[...] + expand(2) * ow_ref[...]
    o_ref[...] = (o * _silu(g_ref[...])).astype(o_ref.dtype)


def nsa_combine(o_cmp, o_sel, o_win, z_g, z_b):
    m = o_cmp.shape[0]
    tm = _pick(m, (256, 128, 64, 32, 16))
    row = pl.BlockSpec((tm, BR_W), lambda i: (i, 0))
    return pl.pallas_call(
        _combine_body,
        grid=(m // tm,),
        in_specs=[row, row, row, pl.BlockSpec((tm, LANE), lambda i: (i, 0)), row,
                  pl.BlockSpec((3, LANE, BR_W), lambda i: (0, 0, 0))],
        out_specs=row,
        out_shape=jax.ShapeDtypeStruct((m, BR_W), BF16),
        compiler_params=_cparams("parallel"),
        name="nsa_combine",
    )(o_cmp, o_sel, o_win, z_g, z_b, _gate_expand())


def _mem_attn_body(q_ref, g_ref, kv_ref, o_ref):
    scale = MEM_DH ** -0.5
    q = q_ref[...]
    for h in range(MEM_H):
        k = kv_ref[:, h * MEM_DH:(h + 1) * MEM_DH].astype(BF16)
        v = kv_ref[:, (MEM_H + h) * MEM_DH:(MEM_H + h + 1) * MEM_DH].astype(BF16)
        s = _dot_nt(q[:, h * MEM_DH:(h + 1) * MEM_DH].astype(BF16), k) * scale
        e = jnp.exp(s - jnp.max(s, axis=-1, keepdims=True))
        p = e * (1.0 / jnp.sum(e, axis=-1, keepdims=True))
        o = _dot(p.astype(BF16), v)
        sl = slice(h * MEM_DH, (h + 1) * MEM_DH)
        o_ref[:, sl] = (o * _silu(g_ref[:, sl])).astype(o_ref.dtype)


def mem_attn_prompt(z_b, mkv):
    t = z_b.shape[0]
    tq = _pick(t, (512, 256, 128, 64, 32, 16))
    return pl.pallas_call(
        _mem_attn_body,
        grid=(t // tq,),
        in_specs=[pl.BlockSpec((tq, BR_W), lambda i: (i, 1)), pl.BlockSpec((tq, BR_W), lambda i: (i, 2)),
                  pl.BlockSpec(mkv.shape, lambda i: (0, 0))],
        out_specs=pl.BlockSpec((tq, BR_W), lambda i: (i, 0)),
        out_shape=jax.ShapeDtypeStruct((t, BR_W), BF16),
        compiler_params=_cparams("parallel"),
        name="mem_attn_prompt",
    )(z_b, z_b, mkv)


def _mem_attn_step_body(q_ref, g_ref, kv_ref, o_ref):
    scale = MEM_DH ** -0.5
    q = q_ref[...]
    rows_per = MEM_DH // LANE
    out_rows = []
    for h in range(MEM_H):
        qh = jnp.concatenate([q[h * rows_per + i:h * rows_per + i + 1, :] for i in range(rows_per)], axis=1)
        qh = jnp.broadcast_to(qh, (8, MEM_DH)).astype(BF16)
        k = kv_ref[:, h * MEM_DH:(h + 1) * MEM_DH].astype(BF16)
        v = kv_ref[:, (MEM_H + h) * MEM_DH:(MEM_H + h + 1) * MEM_DH].astype(BF16)
        s = _dot_nt(qh, k) * scale
        e = jnp.exp(s - jnp.max(s, axis=-1, keepdims=True))
        p = e * (1.0 / jnp.sum(e, axis=-1, keepdims=True))
        o = _dot(p.astype(BF16), v)[0:1, :]
        out_rows += [o[:, i * LANE:(i + 1) * LANE] for i in range(rows_per)]
    o_ref[...] = jnp.concatenate(out_rows, axis=0) * _silu(g_ref[...])


def mem_attn_sample(z_b3, kv_cache, layer):
    bs = z_b3.shape[0]
    mem_len, width = kv_cache.shape[2], kv_cache.shape[3]
    return pl.pallas_call(
        _mem_attn_step_body,
        grid=(bs,),
        in_specs=[pl.BlockSpec((None, 8, LANE), lambda b: (b, 1, 0)), pl.BlockSpec((None, 8, LANE), lambda b: (b, 2, 0)),
                  pl.BlockSpec((None, None, mem_len, width), lambda b: (layer, b, 0, 0))],
        out_specs=pl.BlockSpec((None, 8, LANE), lambda b: (b, 0, 0)),
        out_shape=jax.ShapeDtypeStruct((bs, 8, LANE), F32),
        compiler_params=_cparams("parallel"),
        name="mem_attn_sample",
    )(z_b3, z_b3, kv_cache)


def _paged_partial_body(npg, pt_ref, *refs):
    del pt_ref
    page_refs, w_ref, o_ref = refs[:npg], refs[npg], refs[npg + 1]
    chunks = PAGE_SIZE // CMP_STRIDE
    o_ref[...] = jnp.zeros_like(o_ref)

    unroll = 4

    def step(i, carry):
        for c in range(2 * NS_KV):
            acc = o_ref[c]
            for u in range(unroll):
                s = i * unroll + u
                xs = jnp.concatenate(
                    [r[pl.ds(s * ROWS_PER_TOKEN + c, chunks, stride=CMP_STRIDE * ROWS_PER_TOKEN), :]
                     for r in page_refs], axis=0)
                acc += _dot(xs.astype(BF16), w_ref[c // NS_KV, s])
            o_ref[c] = acc
        return carry

    lax.fori_loop(0, CMP_STRIDE // unroll, step, 0)


def compress_partial_paged(cache2, page_table, w1, layer, n_pool):
    bs, n_pages = page_table.shape
    npg = min(PAGES_PER_STEP, n_pages)
    assert n_pages % npg == 0
    chunks = PAGE_SIZE // CMP_STRIDE
    w = _cmp_weights(w1).reshape(2, CMP_STRIDE, NS_DH, 2 * NS_DH)

    def page_spec(i):
        return pl.BlockSpec((PAGE_SIZE * ROWS_PER_TOKEN, NS_DH),
                            lambda b, j, pt: (layer * n_pool + pt[b, j * npg + i], 0))

    grid_spec = pltpu.PrefetchScalarGridSpec(
        num_scalar_prefetch=1,
        grid=(bs, n_pages // npg),
        in_specs=[page_spec(i) for i in range(npg)] + [pl.BlockSpec(w.shape, lambda b, j, pt: (0, 0, 0, 0))],
        out_specs=pl.BlockSpec((2 * NS_KV, None, npg * chunks, 2 * NS_DH), lambda b, j, pt: (0, b, j, 0)),
    )
    out = pl.pallas_call(
        functools.partial(_paged_partial_body, npg),
        grid_spec=grid_spec,
        out_shape=jax.ShapeDtypeStruct((2 * NS_KV, bs, n_pages * chunks, 2 * NS_DH), F32),
        compiler_params=_cparams("parallel", "parallel"),
        name="cmp_partial_paged",
    )(page_table, *([cache2] * npg), w)
    return out.reshape(2, NS_KV * bs * n_pages * chunks, 2 * NS_DH)


def _row_group(shape):
    return lax.broadcasted_iota(jnp.int32, shape, 0) // NS_G


def _cmp_step_body(n_cmp, qpos, q_ref, kc_ref, vc_ref, o_ref, ps_ref):
    q = q_ref[...].astype(BF16)
    ncp = kc_ref.shape[1]
    s = [_dot_nt(q, kc_ref[g].astype(BF16)) for g in range(NS_KV)]
    grp = _row_group((NS_H, ncp))
    s = jnp.where(grp == 0, s[0], s[1])
    n_i = lax.broadcasted_iota(jnp.int32, (NS_H, ncp), 1)
    valid = (n_i * CMP_STRIDE + CMP_LEN - 1 <= qpos) & (n_i < n_cmp)
    sm = jnp.where(valid, s, NEG_INF)
    mx = jnp.max(sm, axis=-1, keepdims=True)
    e = jnp.where(valid, jnp.exp2(sm - mx), 0.0)
    den = jnp.sum(e, axis=-1, keepdims=True)
    p = e * (1.0 / jnp.where(den > 0.0, den, 1.0))
    pb = p.astype(BF16)
    o = [_dot(pb, vc_ref[g].astype(BF16)) for g in range(NS_KV)]
    o_ref[...] = jnp.where(_row_group((NS_H, NS_DH)) == 0, o[0], o[1])
    ps_ref[...] = jnp.concatenate(
        [jnp.sum(jnp.where(grp == g, p, 0.0), axis=0, keepdims=True) for g in range(NS_KV)], axis=0)


def nsa_cmp_sample(q_c3, kcmp, vcmp, n_cmp, qpos):
    bs = q_c3.shape[0]
    ncp = kcmp.shape[2]
    kv = pl.BlockSpec((NS_KV, None, ncp, NS_DH), lambda b: (0, b, 0, 0))
    return pl.pallas_call(
        functools.partial(_cmp_step_body, n_cmp, qpos),
        grid=(bs,),
        in_specs=[pl.BlockSpec((None, NS_H, NS_DH), lambda b: (b, 0, 0)), kv, kv],
        out_specs=[pl.BlockSpec((None, NS_H, NS_DH), lambda b: (b, 0, 0)),
                   pl.BlockSpec((None, NS_KV, ncp), lambda b: (b, 0, 0))],
        out_shape=[jax.ShapeDtypeStruct((bs, NS_H, NS_DH), F32), jax.ShapeDtypeStruct((bs, NS_KV, ncp), F32)],
        compiler_params=_cparams("parallel"),
        name="nsa_cmp_sample",
    )(q_c3, kcmp, vcmp)


def _select_body(n_sel, qpos, ps_ref, msel_ref, idx_ref):
    hi, mid, lo = _split3(ps_ref[...])
    imp = _dot(hi, msel_ref[...]) + _dot(mid, msel_ref[...]) + _dot(lo, msel_ref[...])
    j_i = lax.broadcasted_iota(jnp.int32, imp.shape, 1)
    cur = qpos // SEL_BLOCK
    forced = (j_i == 0) | (j_i == cur) | (j_i == cur - 1)
    imp = jnp.where(forced, imp + FORCE_BONUS, imp)
    imp = jnp.where(j_i <= cur, imp, NEG_INF)
    imp = jnp.where(j_i < n_sel, imp, REMOVED)
    j_f = j_i.astype(F32)
    col = lax.broadcasted_iota(jnp.int32, idx_ref.shape, 1)
    out = jnp.zeros(idx_ref.shape, F32)
    for it in range(min(SEL_TOPN, n_sel)):
        mx = jnp.max(imp, axis=-1, keepdims=True)
        first = jnp.min(jnp.where(imp == mx, j_f, float(imp.shape[1])), axis=-1, keepdims=True)
        out = jnp.where(col == it, first, out)
        imp = jnp.where(j_f == first, REMOVED, imp)
    idx_ref[...] = out.astype(jnp.int32)


def nsa_select_sample(psum2, n_cmp, n_sel, qpos):
    rows, ncp = psum2.shape
    nsp = -(-n_sel // LANE) * LANE
    msel = jnp.asarray(_cmp_to_sel(ncp, nsp, n_cmp, n_sel), BF16)
    return pl.pallas_call(
        functools.partial(_select_body, n_sel, qpos),
        grid=(1,),
        in_specs=[pl.BlockSpec((rows, ncp), lambda i: (0, 0)), pl.BlockSpec((ncp, nsp), lambda i: (0, 0))],
        out_specs=pl.BlockSpec((rows, LANE), lambda i: (0, 0)),
        out_shape=jax.ShapeDtypeStruct((rows, LANE), jnp.int32),
        compiler_params=_cparams("arbitrary"),
        name="nsa_select_sample",
    )(psum2, msel)


def _sel_step_body(n_past_blocks, qpos, pt_ref, top_ref, q_ref, *refs):
    del pt_ref
    nblk = NS_KV * SEL_PER_STEP
    blk_refs, new_ref, o_ref, m_scr, l_scr, acc_scr = refs[:nblk], *refs[nblk:nblk + 5]
    b = pl.program_id(0)
    j = pl.program_id(1)

    @pl.when(j == 0)
    def _():
        m_scr[...] = jnp.full_like(m_scr, NEG_INF)
        l_scr[...] = jnp.zeros_like(l_scr)
        acc_scr[...] = jnp.zeros_like(acc_scr)

    def slab(ref, slot, g):
        return ref[pl.ds(slot * NS_KV + g, SEL_BLOCK, stride=ROWS_PER_TOKEN), :].astype(BF16)

    q = q_ref[...]
    qb = q.astype(BF16)
    nk = SEL_PER_STEP * SEL_BLOCK
    grp_k = _row_group((NS_H, nk))
    grp_d = _row_group((NS_H, NS_DH))
    lane_blk = lax.broadcasted_iota(jnp.int32, (NS_H, nk), 1) // SEL_BLOCK
    s_g, idx_g, new_g = [], [], []
    for g in range(NS_KV):
        k_all = jnp.concatenate([slab(blk_refs[g * SEL_PER_STEP + k], 2, g) for k in range(SEL_PER_STEP)], axis=0)
        s_g.append(_dot_nt(qb, k_all))
        ids = [top_ref[b, g * SEL_TOPN + j * SEL_PER_STEP + k] for k in range(SEL_PER_STEP)]
        idx = jnp.zeros((NS_H, nk), jnp.int32)
        n_new = jnp.int32(0)
        for k, i in enumerate(ids):
            idx = jnp.where(lane_blk == k, i, idx)
            n_new = n_new + (i == n_past_blocks).astype(jnp.int32)
        idx_g.append(idx)
        new_g.append(n_new)
    s = jnp.where(grp_k == 0, s_g[0], s_g[1])
    idx = jnp.where(grp_k == 0, idx_g[0], idx_g[1])
    tok = idx * SEL_BLOCK + lax.broadcasted_iota(jnp.int32, (NS_H, nk), 1) % SEL_BLOCK
    valid = (idx < n_past_blocks) & (tok <= qpos)
    ks_new = jnp.where(grp_d == 0, new_ref[4:5, :], new_ref[5:6, :])
    vs_new = jnp.where(grp_d == 0, new_ref[6:7, :], new_ref[7:8, :])
    is_new = jnp.where(grp_d == 0, new_g[0], new_g[1]) > 0
    s_new = jnp.where(is_new, jnp.sum(q * ks_new, axis=-1, keepdims=True), NEG_INF)
    m_old = m_scr[...]
    m_new = jnp.maximum(jnp.maximum(m_old, jnp.max(jnp.where(valid, s, NEG_INF), axis=-1, keepdims=True)), s_new)
    alpha = jnp.exp2(m_old - m_new)
    p = jnp.where(valid, jnp.exp2(s - m_new[:, 0:1]), 0.0)
    p_new = jnp.where(is_new, jnp.exp2(s_new - m_new), 0.0)
    pb = p.astype(BF16)
    pv_g = []
    for g in range(NS_KV):
        v_all = jnp.concatenate([slab(blk_refs[g * SEL_PER_STEP + k], 3, g) for k in range(SEL_PER_STEP)], axis=0)
        pv_g.append(_dot(pb, v_all))
    pv = jnp.where(grp_d == 0, pv_g[0], pv_g[1])
    l_scr[...] = alpha * l_scr[...] + jnp.sum(p, axis=-1, keepdims=True) + p_new
    acc_scr[...] = alpha * acc_scr[...] + pv + p_new * vs_new
    m_scr[...] = m_new

    @pl.when(j == pl.num_programs(1) - 1)
    def _():
        o_ref[...] = acc_scr[...] / l_scr[...]


def nsa_sel_sample(q_r3, rows3, cache2, page_table, top_idx, layer, n_pool, qpos):
    bs, n_pages = page_table.shape
    halves = PAGE_SIZE // SEL_BLOCK
    n_past_blocks = n_pages * halves

    def blk(g, k):
        def imap(b, j, pt, top):
            i = top[b, g * SEL_TOPN + j * SEL_PER_STEP + k]
            page = pt[b, jnp.minimum(i // halves, n_pages - 1)]
            return ((layer * n_pool + page) * halves + i % halves, 0)
        return pl.BlockSpec((SEL_BLOCK * ROWS_PER_TOKEN, NS_DH), imap)

    vec = pl.BlockSpec((None, NS_H, NS_DH), lambda b, j, pt, top: (b, 0, 0))
    grid_spec = pltpu.PrefetchScalarGridSpec(
        num_scalar_prefetch=2,
        grid=(bs, SEL_TOPN // SEL_PER_STEP),
        in_specs=[vec] + [blk(g, k) for g in range(NS_KV) for k in range(SEL_PER_STEP)] + [vec],
        out_specs=vec,
        scratch_shapes=[pltpu.VMEM((NS_H, NS_DH), F32)] * 3,
    )
    return pl.pallas_call(
        functools.partial(_sel_step_body, n_past_blocks, qpos),
        grid_spec=grid_spec,
        out_shape=jax.ShapeDtypeStruct((bs, NS_H, NS_DH), F32),
        compiler_params=_cparams("parallel", "arbitrary"),
        name="nsa_sel_sample",
    )(page_table, top_idx, q_r3, *([cache2] * (NS_KV * SEL_PER_STEP)), rows3)


def _win_step_body(pos0, qpos, q_ref, buf_ref, new_ref, o_ref):
    q = q_ref[...]
    qb = q.astype(BF16)
    rows_per = 2 * NS_KV
    wb = buf_ref.shape[0] // rows_per

    def slab(c):
        return buf_ref[pl.ds(c, wb, stride=rows_per), :].astype(BF16)

    grp_k = _row_group((NS_H, wb))
    grp_d = _row_group((NS_H, NS_DH))
    s = jnp.where(grp_k == 0, _dot_nt(qb, slab(0)), _dot_nt(qb, slab(1)))
    kpos = pos0 + lax.broadcasted_iota(jnp.int32, (NS_H, wb), 1)
    valid = (kpos <= qpos) & (kpos > qpos - WINDOW) & (kpos >= 0)
    kw_new = jnp.where(grp_d == 0, new_ref[0:1, :], new_ref[1:2, :])
    vw_new = jnp.where(grp_d == 0, new_ref[2:3, :], new_ref[3:4, :])
    s_new = jnp.sum(q * kw_new, axis=-1, keepdims=True)
    mx = jnp.maximum(jnp.max(jnp.where(valid, s, NEG_INF), axis=-1, keepdims=True), s_new)
    p = jnp.where(valid, jnp.exp2(s - mx), 0.0)
    p_new = jnp.exp2(s_new - mx)
    den = jnp.sum(p, axis=-1, keepdims=True) + p_new
    pb = p.astype(BF16)
    pv = jnp.where(grp_d == 0, _dot(pb, slab(2)), _dot(pb, slab(3)))
    o_ref[...] = (pv + p_new * vw_new) / den


def nsa_win_sample(q_r3, win_buf2, win3, layer, wb, past_len, qpos):
    bs = q_r3.shape[0]
    return pl.pallas_call(
        functools.partial(_win_step_body, past_len - wb, qpos),
        grid=(bs,),
        in_specs=[pl.BlockSpec((None, NS_H, NS_DH), lambda b: (b, 0, 0)),
                  pl.BlockSpec((wb * 2 * NS_KV, NS_DH), lambda b: (layer * bs + b, 0)),
                  pl.BlockSpec((None, 2 * NS_KV, NS_DH), lambda b: (b, 0, 0))],
        out_specs=pl.BlockSpec((None, NS_H, NS_DH), lambda b: (b, 0, 0)),
        out_shape=jax.ShapeDtypeStruct((bs, NS_H, NS_DH), F32),
        compiler_params=_cparams("parallel"),
        name="nsa_win_sample",
    )(q_r3, win_buf2, win3)


def _project(x, norm_g, w_a, w_g, w_b):
    h = rmsnorm_rows(x, norm_g, BF16)
    return (matmul(h, w_a, name="in_proj_a"), matmul(h, w_g, name="in_proj_gate"), matmul(h, w_b, name="in_proj_b"))


def _merge_out(x, branches, z_b, w_up_b, w_out_b):
    d = x.shape[1]
    gated = merge_up(branches, z_b, w_up_b, d)
    return matmul(gated, w_out_b, res=x, name="out_proj")


def kernel(x_prompt, x_sample, mem_prompt, cache_nsa_kv, state_win_kv, state_hgrn, state_conv, cache_mem_kv,
           page_table, norm_g, final_norm_g, w_in, hg_lb_logits, hg_norm_g, cv_w, ns_pe, ns_cw1, ns_cw2,
           mem_norm_g, w_mem_kv, w_up, w_out):
    depth = w_in.shape[0]
    b_p, t, d = x_prompt.shape
    bs, ts = x_sample.shape[:2]
    assert b_p == 1 and ts == 1
    n_pages = page_table.shape[1]
    past_len = n_pages * PAGE_SIZE
    wb = state_win_kv.shape[2]
    mem_len = mem_prompt.shape[1]
    kvw = NS_KV * NS_DH

    s_lb = jax.nn.softmax(hg_lb_logits.astype(F32), axis=0)
    lower = jnp.cumsum(s_lb, axis=0) - s_lb[0]

    pos_p = jnp.arange(t, dtype=jnp.int32)
    qpos_s = past_len
    pos_s = jnp.full((bs,), qpos_s, jnp.int32)

    n_pool = cache_nsa_kv.shape[1]
    cache2 = cache_nsa_kv.reshape(depth * n_pool * PAGE_SIZE * ROWS_PER_TOKEN, NS_DH)
    win_buf2 = state_win_kv.reshape(depth * bs * wb * 2 * NS_KV, NS_DH)
    mem_cache4 = cache_mem_kv.reshape(depth, bs, mem_len, 2 * MEM_H * MEM_DH)

    nch_p = t // CMP_STRIDE
    nch_s = (past_len + 1) // CMP_STRIDE
    n_sel_s = -(-(past_len + 1) // SEL_BLOCK)

    xp = x_prompt.reshape(t, d)
    xs = x_sample.reshape(bs, d)
    rows_p, win_p, hg_p, cv_p, mkv_p = [], [], [], [], []
    rows_s, win_s, hg_s, cv_s = [], [], [], []
    for l in range(depth):
        w_l = w_in[l]
        w_a = w_l[:, :COL_A].astype(BF16)
        w_g = jnp.pad(w_l[:, COL_A:COL_B0], ((0, 0), (0, LANE - COL_GATE))).astype(BF16)
        w_b = w_l[:, COL_B0:].astype(BF16)
        w_up_b = w_up[l].astype(BF16)
        w_out_b = w_out[l].astype(BF16)

        z_a, z_g, z_b = _project(xp, norm_g[l], w_a, w_g, w_b)
        o_hg, st_hg = hgrn_prompt(z_a, lower[l], hg_norm_g[l])
        o_cv, st_cv = conv_prompt(z_a, cv_w[l], jnp.zeros((CV_K - 1, BR_W), F32))
        q_c, q_r, rows, win, ks_ext, kw, vs_ext, vw_t = nsa_prep(z_a, pos_p, BF16, attn_layouts=True)
        pe_term = compress_pe_term(ns_pe[l], ns_cw1[l])
        cmp = compress_finish(compress_partial_rows(z_a, ns_cw1[l]), pe_term, ns_cw2[l], nch_p)
        kcmp = cmp[0].astype(BF16)
        vcmp_t = jnp.swapaxes(cmp[1], 1, 2).astype(BF16)
        o_cmp, sel_bias = nsa_cmp_select(q_c, kcmp, vcmp_t, nch_p - 1)
        o_sel = nsa_sel_attn(q_r, sel_bias, ks_ext, vs_ext)
        o_win = nsa_win_attn(q_r, kw, vw_t)
        o_ns = nsa_combine(o_cmp, o_sel, o_win, z_g, z_b)
        mem_h = rmsnorm_rows(mem_prompt.reshape(mem_len, d), mem_norm_g[l], BF16)
        mkv = matmul(mem_h, w_mem_kv[l].astype(BF16), name="mem_kv")
        o_mm = mem_attn_prompt(z_b, mkv)
        xp = _merge_out(xp, (o_hg, o_cv, o_ns, o_mm), z_b, w_up_b, w_out_b)
        rows_p.append(rows.reshape(1, t, 4, NS_KV, NS_DH))
        win_p.append(win[t - min(WINDOW, t):].reshape(1, min(WINDOW, t), 2, NS_KV, NS_DH))
        hg_p.append(st_hg.reshape(1, HG_H, HG_DK, HG_DV))
        cv_p.append(st_cv.reshape(1, CV_K - 1, BR_W))
        mkv_p.append(mkv.reshape(1, mem_len, 2, MEM_H, MEM_DH))

        z_a, z_g, z_b = _project(xs, norm_g[l], w_a, w_g, w_b)
        z_a3 = z_a.reshape(bs, COL_A // LANE, LANE)
        z_b3 = z_b.reshape(bs, z_b.shape[1] // LANE, LANE)
        o_hg, st_hg = hgrn_sample(z_a3, lower[l], hg_norm_g[l], state_hgrn[l])
        o_cv, st_cv = conv_sample(z_a, cv_w[l], state_conv[l])
        q_c, q_r, rows, win = nsa_prep(z_a, pos_s, F32)
        part = compress_partial_paged(cache2, page_table, ns_cw1[l], l, n_pool)
        cmp = compress_finish(part, pe_term, ns_cw2[l], nch_s)
        cmp = cmp.reshape(2, NS_KV, bs, nch_s, NS_DH)
        q_c3 = q_c.reshape(bs, NS_H, NS_DH)
        q_r3 = q_r.reshape(bs, NS_H, NS_DH)
        o_cmp, psum = nsa_cmp_sample(q_c3, cmp[0], cmp[1], nch_s - 1, qpos_s)
        top = nsa_select_sample(psum.reshape(bs * NS_KV, nch_s), nch_s - 1, n_sel_s, qpos_s)
        top = top[:, :SEL_TOPN].reshape(bs, NS_KV * SEL_TOPN)
        o_sel = nsa_sel_sample(q_r3, rows.reshape(bs, 4 * NS_KV, NS_DH), cache2, page_table, top, l, n_pool, qpos_s)
        o_win = nsa_win_sample(q_r3, win_buf2, win.reshape(bs, 2 * NS_KV, NS_DH), l, wb, past_len, qpos_s)
        o_ns = nsa_combine(o_cmp.reshape(bs, BR_W), o_sel.reshape(bs, BR_W), o_win.reshape(bs, BR_W), z_g, z_b)
        o_mm = mem_attn_sample(z_b3, mem_cache4, l)
        branches = (o_hg.reshape(bs, BR_W).astype(BF16), o_cv.astype(BF16), o_ns, o_mm.reshape(bs, BR_W).astype(BF16))
        xs = _merge_out(xs, branches, z_b, w_up_b, w_out_b)
        rows_s.append(rows.reshape(bs, 1, 4, NS_KV, NS_DH))
        win_s.append(jnp.concatenate([state_win_kv[l][:, 1:], win.reshape(bs, 1, 2, NS_KV, NS_DH)], axis=1))
        hg_s.append(st_hg)
        cv_s.append(st_cv)

    y_prompt = rmsnorm_rows(xp, final_norm_g, F32).reshape(1, t, d)
    y_sample = rmsnorm_rows(xs, final_norm_g, F32).reshape(bs, 1, d)
    return (y_prompt, y_sample, jnp.stack(rows_p), jnp.stack(win_p), jnp.stack(hg_p), jnp.stack(cv_p),
            jnp.stack(mkv_p), jnp.stack(rows_s), jnp.stack(win_s), jnp.stack(hg_s), jnp.stack(cv_s))
```

```python
import functools

import numpy as np
import jax
import jax.numpy as jnp
from jax import lax
from jax.experimental import pallas as pl
from jax.experimental.pallas import tpu as pltpu

F32 = jnp.float32
BF16 = jnp.bfloat16

BR_W = 1024
N_BRANCH = 4
HG_H = 8
HG_DK = 128
HG_DV = 128
F_MIN = 1e-30
CV_K = 3
NS_H = 8
NS_KV = 2
NS_G = NS_H // NS_KV
NS_DH = 128
CMP_LEN = 32
CMP_STRIDE = 16
SEL_BLOCK = 64
SEL_TOPN = 16
WINDOW = 512
FORCE_BONUS = 100.0
ROPE_THETA = 500000.0
ROT_DIM = NS_DH // 4
MEM_H = 4
MEM_DH = 256
NORM_EPS = 1e-6
NEG_INF = -1e30
REMOVED = -3e38
LOG2_E = 1.4426950408889634
PAGE_SIZE = 128
ROWS_PER_TOKEN = 4 * NS_KV

COL_A = 10752
COL_GATE = 24
COL_B0 = COL_A + COL_GATE

LANE = 128
HG_CHUNK = 128
HG_HEADS_PER_STEP = 4
Q_TILE = 128
K_TILE = 512
PAGES_PER_STEP = 16
SEL_PER_STEP = 4
VMEM_LIMIT = 56 * 1024 * 1024


def _cparams(*sem):
    return pltpu.CompilerParams(dimension_semantics=sem, vmem_limit_bytes=VMEM_LIMIT)


def _pick(n, cands):
    for c in cands:
        if n % c == 0:
            return c
    return n


def _silu(x):
    return x * jax.nn.sigmoid(x)


def _dot(a, b):
    return jnp.dot(a, b, preferred_element_type=F32)


def _dot_nt(a, b):
    return lax.dot_general(a, b, (((1,), (1,)), ((), ())), preferred_element_type=F32)


def _split3(x):
    hi = x.astype(BF16)
    r1 = x - hi.astype(F32)
    mid = r1.astype(BF16)
    lo = (r1 - mid.astype(F32)).astype(BF16)
    return hi, mid, lo


def _norm_body(x_ref, g_ref, o_ref):
    x = x_ref[...]
    ms = jnp.mean(x * x, axis=-1, keepdims=True)
    o_ref[...] = (x * lax.rsqrt(ms + NORM_EPS) * g_ref[...]).astype(o_ref.dtype)


def rmsnorm_rows(x, g, out_dtype):
    m, d = x.shape
    tm = _pick(m, (256, 128, 64, 32, 16, 8))
    return pl.pallas_call(
        _norm_body,
        grid=(m // tm,),
        in_specs=[pl.BlockSpec((tm, d), lambda i: (i, 0)), pl.BlockSpec((1, d), lambda i: (0, 0))],
        out_specs=pl.BlockSpec((tm, d), lambda i: (i, 0)),
        out_shape=jax.ShapeDtypeStruct((m, d), out_dtype),
        compiler_params=_cparams("parallel"),
        name="rmsnorm",
    )(x, g.reshape(1, d).astype(F32))


def _mm_body(a_ref, b_ref, o_ref):
    o_ref[...] = _dot(a_ref[...].astype(BF16), b_ref[...]).astype(o_ref.dtype)


def _mm_res_body(a_ref, b_ref, r_ref, o_ref):
    o_ref[...] = r_ref[...] + _dot(a_ref[...].astype(BF16), b_ref[...])


def matmul(a, b, res=None, out_dtype=F32, name="matmul"):
    m, k = a.shape
    n = b.shape[1]
    tm = _pick(m, (1024, 512, 256, 128, 64, 32, 16, 8))
    tn = _pick(n, (1536, 1024, 512, 256, 128))
    in_specs = [pl.BlockSpec((tm, k), lambda i, j: (i, 0)), pl.BlockSpec((k, tn), lambda i, j: (0, j))]
    args = [a, b]
    body = _mm_body
    if res is not None:
        in_specs.append(pl.BlockSpec((tm, tn), lambda i, j: (i, j)))
        args.append(res)
        body = _mm_res_body
    return pl.pallas_call(
        body,
        grid=(m // tm, n // tn),
        in_specs=in_specs,
        out_specs=pl.BlockSpec((tm, tn), lambda i, j: (i, j)),
        out_shape=jax.ShapeDtypeStruct((m, n), out_dtype),
        compiler_params=_cparams("parallel", "parallel"),
        name=name,
    )(*args)


def _merge_body(b0, b1, b2, b3, g0, g1, g2, g3, w0, w1, w2, w3, o_ref):
    acc = jax.nn.sigmoid(g0[...]) * _dot(b0[...], w0[...])
    acc += jax.nn.sigmoid(g1[...]) * _dot(b1[...], w1[...])
    acc += jax.nn.sigmoid(g2[...]) * _dot(b2[...], w2[...])
    acc += jax.nn.sigmoid(g3[...]) * _dot(b3[...], w3[...])
    o_ref[...] = acc.astype(o_ref.dtype)


def merge_up(branches, z_b, w_up_bf16, d_model):
    m = branches[0].shape[0]
    tm = _pick(m, (512, 256, 128, 64, 32, 16))
    tn = 512
    nj = d_model // tn
    gate_col0 = (z_b.shape[1] - N_BRANCH * d_model) // tn
    br_specs = [pl.BlockSpec((tm, BR_W), lambda i, j: (i, 0)) for _ in range(N_BRANCH)]
    g_specs = [pl.BlockSpec((tm, tn), functools.partial(lambda i, j, n: (i, gate_col0 + n * nj + j), n=n))
               for n in range(N_BRANCH)]
    w_specs = [pl.BlockSpec((None, BR_W, tn), functools.partial(lambda i, j, n: (n, 0, j), n=n))
               for n in range(N_BRANCH)]
    return pl.pallas_call(
        _merge_body,
        grid=(m // tm, nj),
        in_specs=br_specs + g_specs + w_specs,
        out_specs=pl.BlockSpec((tm, tn), lambda i, j: (i, j)),
        out_shape=jax.ShapeDtypeStruct((m, d_model), BF16),
        compiler_params=_cparams("parallel", "parallel"),
        name="merge_up",
    )(*branches, z_b, z_b, z_b, z_b, w_up_bf16, w_up_bf16, w_up_bf16, w_up_bf16)


def _hgrn_consts(c):
    nlev = int(round(np.log2(c)))
    t = np.arange(c)[:, None]
    r = np.arange(c)[None, :]
    blocks = [r <= t]
    masks = [t == r]
    for lv in range(nlev):
        h = c >> (lv + 1)
        mid = (t // (2 * h)) * (2 * h) + h
        blocks.append(np.where(t >= mid, (r >= mid) & (r <= t), (r > t) & (r <= mid - 1)))
        same = (t // (2 * h)) == (r // (2 * h))
        masks.append(same & (t % (2 * h) >= h) & (r % (2 * h) < h))
    blocks.append(r > t)
    l_all = np.concatenate(blocks, axis=0).astype(np.float32)
    return jnp.asarray(l_all, BF16), jnp.asarray(np.stack(masks).astype(np.float32)), nlev


def _hgrn_body(nlev, q_ref, z_ref, v_ref, g_ref, lb_ref, ng_ref, l_ref, mask_ref, o_ref, s_ref, st_scr):
    c_idx = pl.program_id(1)
    c = q_ref.shape[0]

    @pl.when(c_idx == 0)
    def _():
        st_scr[...] = jnp.zeros_like(st_scr)

    for hh in range(HG_HEADS_PER_STEP):
        sl = slice(hh * HG_DK, (hh + 1) * HG_DK)
        lb = lb_ref[:, sl]
        q = _silu(q_ref[:, sl])
        z = z_ref[:, sl]
        f = lb + (1.0 - lb) * jax.nn.sigmoid(z)
        logf = jnp.log(jnp.maximum(f, F_MIN))
        k = (1.0 - lb) * jax.nn.sigmoid(-z)
        v = v_ref[:, sl]

        hi, mid, lo = _split3(logf)
        e3 = _dot(l_ref[...], jnp.concatenate([hi, mid, lo], axis=1))
        dk = HG_DK
        x = jnp.exp(e3[:, :dk] + (e3[:, dk:2 * dk] + e3[:, 2 * dk:]))
        eb = x[0:c]
        est = x[(nlev + 1) * c:(nlev + 2) * c]

        st = st_scr[hh]
        inter = _dot_nt((q * eb).astype(BF16), st.astype(BF16))
        att = mask_ref[0] * _dot_nt(q.astype(BF16), k.astype(BF16))
        for lv in range(nlev):
            fac = x[(1 + lv) * c:(2 + lv) * c]
            att += mask_ref[1 + lv] * _dot_nt((q * fac).astype(BF16), (k * fac).astype(BF16))
        o = inter + _dot(att.astype(BF16), v.astype(BF16))
        o = o * lax.rsqrt(jnp.mean(o * o, axis=-1, keepdims=True) + NORM_EPS) * ng_ref[:, sl]
        o_ref[:, sl] = (o * _silu(g_ref[:, sl])).astype(o_ref.dtype)

        st_scr[hh] = st * eb[c - 1:c, :] + _dot(v.T.astype(BF16), (k * est).astype(BF16))

    @pl.when(c_idx == pl.num_programs(1) - 1)
    def _():
        for hh in range(HG_HEADS_PER_STEP):
            s_ref[hh] = st_scr[hh].T


def hgrn_prompt(z_a, lb, norm_g):
    t = z_a.shape[0]
    c = HG_CHUNK
    l_all, masks, nlev = _hgrn_consts(c)
    hp = HG_HEADS_PER_STEP
    steps = HG_H // hp
    width = hp * HG_DK

    def col(k):
        return pl.BlockSpec((c, width), functools.partial(lambda h, i, k: (i, k * steps + h), k=k))

    vec = pl.BlockSpec((None, 1, width), lambda h, i: (h, 0, 0))
    return pl.pallas_call(
        functools.partial(_hgrn_body, nlev),
        grid=(steps, t // c),
        in_specs=[col(0), col(1), col(2), col(3), vec, vec,
                  pl.BlockSpec(l_all.shape, lambda h, i: (0, 0)),
                  pl.BlockSpec(masks.shape, lambda h, i: (0, 0, 0))],
        out_specs=[pl.BlockSpec((c, width), lambda h, i: (i, h)),
                   pl.BlockSpec((hp, HG_DK, HG_DV), lambda h, i: (h, 0, 0))],
        out_shape=[jax.ShapeDtypeStruct((t, BR_W), BF16),
                   jax.ShapeDtypeStruct((HG_H, HG_DK, HG_DV), F32)],
        scratch_shapes=[pltpu.VMEM((hp, HG_DV, HG_DK), F32)],
        compiler_params=_cparams("parallel", "arbitrary"),
        name="hgrn_prompt",
    )(z_a, z_a, z_a, z_a, lb.reshape(steps, 1, width), norm_g.reshape(steps, 1, width), l_all, masks)


def _hgrn_step_body(q_ref, z_ref, v_ref, g_ref, lb_ref, ng_ref, s0_ref, o_ref, s_ref):
    lb = lb_ref[...]
    q = _silu(q_ref[...])
    z = z_ref[...]
    f = jnp.maximum(lb + (1.0 - lb) * jax.nn.sigmoid(z), F_MIN)
    k = (1.0 - lb) * jax.nn.sigmoid(-z)
    v = v_ref[...]
    rows = []
    for h in range(HG_H):
        def colb(a):
            return jnp.broadcast_to(a[h:h + 1, :], (HG_DK, HG_DK)).T
        s_new = colb(f) * s0_ref[h] + colb(k) * v[h:h + 1, :]
        s_ref[h] = s_new
        rows.append(jnp.sum(colb(q) * s_new, axis=0, keepdims=True))
    o = jnp.concatenate(rows, axis=0)
    o = o * lax.rsqrt(jnp.mean(o * o, axis=-1, keepdims=True) + NORM_EPS) * ng_ref[...]
    o_ref[...] = o * _silu(g_ref[...])


def hgrn_sample(z_a3, lb, norm_g, s0):
    bs = z_a3.shape[0]

    def grp(k):
        return pl.BlockSpec((None, HG_H, LANE), functools.partial(lambda b, k: (b, k, 0), k=k))

    vec = pl.BlockSpec((HG_H, LANE), lambda b: (0, 0))
    st = pl.BlockSpec((None, HG_H, HG_DK, HG_DV), lambda b: (b, 0, 0, 0))
    return pl.pallas_call(
        _hgrn_step_body,
        grid=(bs,),
        in_specs=[grp(0), grp(1), grp(2), grp(3), vec, vec, st],
        out_specs=[pl.BlockSpec((None, HG_H, LANE), lambda b: (b, 0, 0)), st],
        out_shape=[jax.ShapeDtypeStruct((bs, HG_H, LANE), F32),
                   jax.ShapeDtypeStruct((bs, HG_H, HG_DK, HG_DV), F32)],
        compiler_params=_cparams("parallel"),
        name="hgrn_sample",
    )(z_a3, z_a3, z_a3, z_a3, lb.reshape(HG_H, LANE), norm_g.reshape(HG_H, LANE), s0)


def _conv_body(u_ref, b_ref, c_ref, g_ref, w_ref, prev_ref, o_ref, last_ref, carry):
    @pl.when(pl.program_id(0) == 0)
    def _():
        carry[...] = prev_ref[...]

    v = c_ref[...] * u_ref[...]
    tm = v.shape[0]
    row = lax.broadcasted_iota(jnp.int32, v.shape, 0)
    p1 = carry[7:8, :]
    p2 = carry[6:7, :]
    v1 = jnp.where(row == 0, p1, pltpu.roll(v, 1, 0))
    v2 = jnp.where(row == 0, p2, jnp.where(row == 1, p1, pltpu.roll(v, 2, 0)))
    w = w_ref[...]
    y = w[0:1, :] * v2 + w[1:2, :] * v1 + w[2:3, :] * v
    o_ref[...] = (b_ref[...] * y * _silu(g_ref[...])).astype(o_ref.dtype)
    tail = v[tm - 8:tm, :]
    carry[...] = tail
    last_ref[...] = tail


def conv_prompt(z_a, w, prev):
    t = z_a.shape[0]
    tm = _pick(t, (256, 128, 64, 32, 16, 8))

    def col(k):
        return pl.BlockSpec((tm, BR_W), functools.partial(lambda i, k: (i, 4 + k), k=k))

    w8 = jnp.zeros((8, BR_W), F32).at[:CV_K].set(w.astype(F32))
    prev8 = jnp.zeros((8, BR_W), F32).at[8 - (CV_K - 1):].set(prev.astype(F32))
    full8 = pl.BlockSpec((8, BR_W), lambda i: (0, 0))
    o, last = pl.pallas_call(
        _conv_body,
        grid=(t // tm,),
        in_specs=[col(0), col(1), col(2), col(3), full8, full8],
        out_specs=[pl.BlockSpec((tm, BR_W), lambda i: (i, 0)), full8],
        out_shape=[jax.ShapeDtypeStruct((t, BR_W), BF16), jax.ShapeDtypeStruct((8, BR_W), F32)],
        scratch_shapes=[pltpu.VMEM((8, BR_W), F32)],
        compiler_params=_cparams("arbitrary"),
        name="conv_prompt",
    )(z_a, z_a, z_a, z_a, w8, prev8)
    return o, last[8 - (CV_K - 1):]


def _conv_step_body(u_ref, b_ref, c_ref, g_ref, w_ref, p0_ref, p1_ref, o_ref, v_ref):
    v = c_ref[...] * u_ref[...]
    w = w_ref[...]
    y = w[0:1, :] * p0_ref[...] + w[1:2, :] * p1_ref[...] + w[2:3, :] * v
    o_ref[...] = b_ref[...] * y * _silu(g_ref[...])
    v_ref[...] = v


def conv_sample(z_a, w, prev):
    bs = z_a.shape[0]

    def col(k):
        return pl.BlockSpec((bs, BR_W), functools.partial(lambda i, k: (0, 4 + k), k=k))

    w8 = jnp.zeros((8, BR_W), F32).at[:CV_K].set(w.astype(F32))
    full = pl.BlockSpec((bs, BR_W), lambda i: (0, 0))
    o, v = pl.pallas_call(
        _conv_step_body,
        grid=(1,),
        in_specs=[col(0), col(1), col(2), col(3), pl.BlockSpec((8, BR_W), lambda i: (0, 0)), full, full],
        out_specs=[full, full],
        out_shape=[jax.ShapeDtypeStruct((bs, BR_W), F32), jax.ShapeDtypeStruct((bs, BR_W), F32)],
        compiler_params=_cparams("arbitrary"),
        name="conv_sample",
    )(z_a, z_a, z_a, z_a, w8, prev[:, 0], prev[:, 1])
    return o, jnp.stack([prev[:, 1], v], axis=1)


def _rope_tables(pos):
    half = ROT_DIM // 2
    inv = ROPE_THETA ** (-2.0 * jnp.arange(half, dtype=F32) / ROT_DIM)
    ang = pos.astype(F32)[:, None] * inv[None, :]
    cos, sin = jnp.cos(ang), jnp.sin(ang)
    m = pos.shape[0]
    ones = jnp.ones((m, NS_DH - ROT_DIM), F32)
    zeros = jnp.zeros((m, NS_DH - ROT_DIM), F32)
    zh = jnp.zeros((m, half), F32)
    cos_t = jnp.concatenate([cos, cos, ones], axis=1)
    sin_a = jnp.concatenate([zh, sin, zeros], axis=1)
    sin_b = jnp.concatenate([-sin, zh, zeros], axis=1)
    return cos_t, sin_a, sin_b


def _rope(x, cos_t, sin_a, sin_b):
    n = x.shape[1] // NS_DH
    half = ROT_DIM // 2

    def tile(a):
        return a if n == 1 else jnp.concatenate([a] * n, axis=1)

    return (x * tile(cos_t) + pltpu.roll(x, half, 1) * tile(sin_a)
            + pltpu.roll(x, x.shape[1] - half, 1) * tile(sin_b))


def _nsa_prep_body(q_ref, cv_ref, ks_ref, vs_ref, kw_ref, vw_ref, cos_ref, sa_ref, sb_ref,
                   qc_ref, qr_ref, rows_ref, win_ref, *attn_refs):
    cos_t, sin_a, sin_b = cos_ref[...], sa_ref[...], sb_ref[...]
    scale = NS_DH ** -0.5 * LOG2_E
    q = q_ref[...]
    qc_ref[...] = (q * scale).astype(qc_ref.dtype)
    qr_ref[...] = (_rope(q, cos_t, sin_a, sin_b) * scale).astype(qr_ref.dtype)
    kvw = NS_KV * NS_DH
    ks = _rope(ks_ref[...], cos_t, sin_a, sin_b)
    kw = _rope(kw_ref[...], cos_t, sin_a, sin_b)
    vs = vs_ref[...]
    vw = vw_ref[...]
    rows_ref[:, 0:2 * kvw] = cv_ref[...]
    rows_ref[:, 2 * kvw:3 * kvw] = ks
    rows_ref[:, 3 * kvw:4 * kvw] = vs
    win_ref[:, 0:kvw] = kw
    win_ref[:, kvw:2 * kvw] = vw
    if attn_refs:
        ksx_ref, kwb_ref, vst_ref, vwt_ref = attn_refs
        tm = q.shape[0]
        tok = pl.program_id(0) * tm + lax.broadcasted_iota(jnp.int32, (tm, LANE), 0)
        onehot = jnp.where(tok // SEL_BLOCK == lax.broadcasted_iota(jnp.int32, (tm, LANE), 1),
                           1.0, 0.0).astype(ksx_ref.dtype)
        ones_rows = jnp.where(lax.broadcasted_iota(jnp.int32, (V_EXT_ROWS - NS_DH, tm), 0) == 0,
                              1.0, 0.0).astype(vst_ref.dtype)
        for g in range(NS_KV):
            sl = slice(g * NS_DH, (g + 1) * NS_DH)
            ksx_ref[g, :, 0:NS_DH] = ks[:, sl].astype(ksx_ref.dtype)
            ksx_ref[g, :, NS_DH:NS_DH + LANE] = onehot
            kwb_ref[g] = kw[:, sl].astype(kwb_ref.dtype)
            vst_ref[g, 0:NS_DH, :] = vs[:, sl].T.astype(vst_ref.dtype)
            vst_ref[g, NS_DH:V_EXT_ROWS, :] = ones_rows
            vwt_ref[g] = vw[:, sl].T.astype(vwt_ref.dtype)


def nsa_prep(z_a, pos, q_dtype, attn_layouts=False):
    m = z_a.shape[0]
    tm = _pick(m, (256, 128, 64, 32, 16, 8))
    kvw = NS_KV * NS_DH
    c0 = 8 * BR_W // kvw
    tabs = _rope_tables(pos)

    def col(k, width=kvw):
        return pl.BlockSpec((tm, width), lambda i: (i, k))

    tab = pl.BlockSpec((tm, NS_DH), lambda i: (i, 0))
    out_specs = [pl.BlockSpec((tm, BR_W), lambda i: (i, 0)), pl.BlockSpec((tm, BR_W), lambda i: (i, 0)),
                 pl.BlockSpec((tm, 4 * kvw), lambda i: (i, 0)), pl.BlockSpec((tm, 2 * kvw), lambda i: (i, 0))]
    out_shape = [jax.ShapeDtypeStruct((m, BR_W), q_dtype), jax.ShapeDtypeStruct((m, BR_W), q_dtype),
                 jax.ShapeDtypeStruct((m, 4 * kvw), F32), jax.ShapeDtypeStruct((m, 2 * kvw), F32)]
    if attn_layouts:
        out_specs += [pl.BlockSpec((NS_KV, tm, NS_DH + LANE), lambda i: (0, i, 0)),
                      pl.BlockSpec((NS_KV, tm, NS_DH), lambda i: (0, i, 0)),
                      pl.BlockSpec((NS_KV, V_EXT_ROWS, tm), lambda i: (0, 0, i)),
                      pl.BlockSpec((NS_KV, NS_DH, tm), lambda i: (0, 0, i))]
        out_shape += [jax.ShapeDtypeStruct((NS_KV, m, NS_DH + LANE), BF16),
                      jax.ShapeDtypeStruct((NS_KV, m, NS_DH), BF16),
                      jax.ShapeDtypeStruct((NS_KV, V_EXT_ROWS, m), BF16),
                      jax.ShapeDtypeStruct((NS_KV, NS_DH, m), BF16)]
    return pl.pallas_call(
        _nsa_prep_body,
        grid=(m // tm,),
        in_specs=[col(8, BR_W), col((c0 + 4) // 2, 2 * kvw), col(c0 + 6), col(c0 + 7), col(c0 + 8), col(c0 + 9),
                  tab, tab, tab],
        out_specs=out_specs,
        out_shape=out_shape,
        compiler_params=_cparams("parallel"),
        name="nsa_prep",
    )(z_a, z_a, z_a, z_a, z_a, z_a, *tabs)


def _cmp_finish_body(nch, p_ref, pe_ref, w2_ref, o_ref):
    p = p_ref[...]
    n = p.shape[0]
    nxt = pltpu.roll(p[:, NS_DH:], n - 1, 0)
    hid = _silu(p[:, :NS_DH] + nxt + pe_ref[0:1, :])
    out = _dot(hid.astype(BF16), w2_ref[...])
    row = lax.broadcasted_iota(jnp.int32, out.shape, 0)
    o_ref[...] = jnp.where(row % nch < nch - 1, out, 0.0)


def _cmp_weights(w1):
    kdim = CMP_STRIDE * NS_DH
    w1b = w1.astype(BF16)
    return jnp.concatenate([w1b[:, :kdim], w1b[:, kdim:]], axis=2)


def _partial_rows_body(x_ref, w_ref, o_ref):
    kv = pl.program_id(0) // NS_KV
    nch = o_ref.shape[0]
    acc = jnp.zeros(o_ref.shape, F32)
    for s in range(CMP_STRIDE):
        acc += _dot(x_ref[pl.ds(s, nch, stride=CMP_STRIDE), :].astype(BF16), w_ref[kv, s])
    o_ref[...] = acc


def compress_partial_rows(z_a, w1):
    t = z_a.shape[0]
    nch = t // CMP_STRIDE
    col0 = (8 * BR_W + BR_W) // NS_DH
    w = _cmp_weights(w1).reshape(2, CMP_STRIDE, NS_DH, 2 * NS_DH)
    out = pl.pallas_call(
        _partial_rows_body,
        grid=(2 * NS_KV,),
        in_specs=[pl.BlockSpec((t, NS_DH), lambda c: (0, col0 + c)), pl.BlockSpec(w.shape, lambda c: (0, 0, 0, 0))],
        out_specs=pl.BlockSpec((None, nch, 2 * NS_DH), lambda c: (c, 0, 0)),
        out_shape=jax.ShapeDtypeStruct((2 * NS_KV, nch, 2 * NS_DH), F32),
        compiler_params=_cparams("parallel"),
        name="cmp_partial_rows",
    )(z_a, w)
    return out.reshape(2, NS_KV * nch, 2 * NS_DH)


def compress_pe_term(pe, w1):
    w1b = w1.astype(BF16)
    pe8 = jnp.broadcast_to(pe.reshape(2, 1, CMP_LEN * NS_DH), (2, 8, CMP_LEN * NS_DH)).astype(BF16)
    return jnp.stack([matmul(pe8[a], w1b[a], name="cmp_pe_term") for a in range(2)])


def compress_finish(p, pe_term, w2, nch):
    groups = p.shape[1] // nch
    gps = _pick(groups, (8, 4, 2, 1))
    out = pl.pallas_call(
        functools.partial(_cmp_finish_body, nch),
        grid=(2, groups // gps),
        in_specs=[pl.BlockSpec((None, gps * nch, 2 * NS_DH), lambda a, g: (a, g, 0)),
                  pl.BlockSpec((None, 8, NS_DH), lambda a, g: (a, 0, 0)),
                  pl.BlockSpec((None, NS_DH, NS_DH), lambda a, g: (a, 0, 0))],
        out_specs=pl.BlockSpec((None, gps * nch, NS_DH), lambda a, g: (a, g, 0)),
        out_shape=jax.ShapeDtypeStruct((2, groups * nch, NS_DH), F32),
        compiler_params=_cparams("parallel", "parallel"),
        name="cmp_finish",
    )(p, pe_term, w2.astype(BF16))
    return out.reshape(2, groups, nch, NS_DH)


def _cmp_to_sel(n_cmp_pad, n_sel_pad, n_cmp, n_sel):
    cs = np.arange(n_cmp_pad)[:, None] * CMP_STRIDE
    ss = np.arange(n_sel_pad)[None, :] * SEL_BLOCK
    ov = np.minimum(cs + CMP_LEN, ss + SEL_BLOCK) - np.maximum(cs, ss)
    m = np.clip(ov, 0, None).astype(np.float32) / CMP_LEN
    m[n_cmp:, :] = 0.0
    m[:, n_sel:] = 0.0
    return m


def _stack_heads(q):
    return jnp.concatenate([q[:, r * NS_DH:(r + 1) * NS_DH] for r in range(NS_G)], axis=0)


def _tile_heads(a):
    return jnp.concatenate([a] * NS_G, axis=1)


def _store_heads_t(o_ref, o_t):
    tq = o_t.shape[1] // NS_G
    for r in range(NS_G):
        o_ref[:, r * NS_DH:(r + 1) * NS_DH] = o_t[:, r * tq:(r + 1) * tq].T


def _cmp_sel_body(n_cmp, q_ref, kc_ref, vct_ref, msel_ref, o_ref, bias_ref):
    gw = NS_G * NS_DH
    for g in range(NS_KV):
        _cmp_sel_group(n_cmp, q_ref.at[:, g * gw:(g + 1) * gw], kc_ref.at[g], vct_ref.at[g], msel_ref,
                       o_ref.at[:, g * gw:(g + 1) * gw], bias_ref.at[g])


def _cmp_sel_group(n_cmp, q_ref, kc_ref, vct_ref, msel_ref, o_ref, bias_ref):
    tq = q_ref.shape[0]
    t0 = pl.program_id(0) * tq
    qs = _stack_heads(q_ref[...])
    s_t = _dot_nt(kc_ref[...], qs)
    ncp = s_t.shape[0]
    n_i = lax.broadcasted_iota(jnp.int32, (ncp, tq), 0)
    qpos = t0 + lax.broadcasted_iota(jnp.int32, (ncp, tq), 1)
    valid = _tile_heads((n_i * CMP_STRIDE + CMP_LEN - 1 <= qpos) & (n_i < n_cmp))
    sm = jnp.where(valid, s_t, NEG_INF)
    mx = jnp.max(sm, axis=0, keepdims=True)
    e = jnp.where(valid, jnp.exp2(sm - mx), 0.0)
    den = jnp.sum(e, axis=0, keepdims=True)
    p = e * (1.0 / jnp.where(den > 0.0, den, 1.0))
    _store_heads_t(o_ref, _dot(vct_ref[...], p.astype(BF16)))

    psum = p[:, 0:tq]
    for r in range(1, NS_G):
        psum += p[:, r * tq:(r + 1) * tq]
    hi, mid, lo = _split3(psum)
    i3 = _dot(msel_ref[...], jnp.concatenate([hi, mid, lo], axis=1))
    imp = i3[:, :tq] + i3[:, tq:2 * tq] + i3[:, 2 * tq:]
    nb = imp.shape[0]
    j_i = lax.broadcasted_iota(jnp.int32, (nb, tq), 0)
    cur = (t0 + lax.broadcasted_iota(jnp.int32, (nb, tq), 1)) // SEL_BLOCK
    forced = (j_i == 0) | (j_i == cur) | (j_i == cur - 1)
    imp = jnp.where(forced, imp + FORCE_BONUS, imp)
    imp = jnp.where(j_i <= cur, imp, NEG_INF)
    j_f = j_i.astype(F32)
    sel = jnp.zeros((nb, tq), F32)
    for _ in range(min(SEL_TOPN, nb)):
        mx = jnp.max(imp, axis=0, keepdims=True)
        first = jnp.min(jnp.where(imp == mx, j_f, float(nb)), axis=0, keepdims=True)
        hit = j_f == first
        sel = jnp.where(hit, 1.0, sel)
        imp = jnp.where(hit, REMOVED, imp)
    bias_t = jnp.where((sel > 0.5) & (j_i <= cur), 0.0, NEG_INF)
    if nb < LANE:
        bias_t = jnp.concatenate([bias_t, jnp.full((LANE - nb, tq), NEG_INF, F32)], axis=0)
    bias_ref[...] = bias_t.T.astype(bias_ref.dtype)


def nsa_cmp_select(q_c, kcmp, vcmp_t, n_cmp):
    t = q_c.shape[0]
    ncp = kcmp.shape[1]
    n_sel = t // SEL_BLOCK
    assert n_sel <= LANE
    msel_t = jnp.asarray(_cmp_to_sel(ncp, n_sel, n_cmp, n_sel).T, BF16)
    return pl.pallas_call(
        functools.partial(_cmp_sel_body, n_cmp),
        grid=(t // Q_TILE,),
        in_specs=[pl.BlockSpec((Q_TILE, BR_W), lambda i: (i, 0)),
                  pl.BlockSpec((NS_KV, ncp, NS_DH), lambda i: (0, 0, 0)),
                  pl.BlockSpec((NS_KV, NS_DH, ncp), lambda i: (0, 0, 0)),
                  pl.BlockSpec((n_sel, ncp), lambda i: (0, 0))],
        out_specs=[pl.BlockSpec((Q_TILE, BR_W), lambda i: (i, 0)),
                   pl.BlockSpec((NS_KV, Q_TILE, LANE), lambda i: (0, i, 0))],
        out_shape=[jax.ShapeDtypeStruct((t, BR_W), F32), jax.ShapeDtypeStruct((NS_KV, t, LANE), BF16)],
        compiler_params=_cparams("parallel"),
        name="nsa_cmp_select",
    )(q_c, kcmp, vcmp_t, msel_t)


V_EXT_ROWS = NS_DH + 16


def _sel_attn_body(q_ref, bias_ref, k_ref, vt_ref, o_ref, m_scr, acc_scr, s_scr):
    tq = q_ref.shape[0]
    t0 = pl.program_id(0) * tq
    gw = NS_G * NS_DH
    qx = [jnp.concatenate([_stack_heads(q_ref[:, g * gw:(g + 1) * gw]),
                           jnp.concatenate([bias_ref[g]] * NS_G, axis=0)], axis=1) for g in range(NS_KV)]
    m_scr[...] = jnp.full_like(m_scr, NEG_INF)
    acc_scr[...] = jnp.zeros_like(acc_scr)
    n_tiles = (t0 + tq + K_TILE - 1) // K_TILE

    def scores(kt, slot):
        k0 = pl.multiple_of(kt * K_TILE, K_TILE)
        for g in range(NS_KV):
            s_scr[slot, g] = _dot_nt(k_ref[g, pl.ds(k0, K_TILE), :], qx[g])

    def consume(kt, slot, causal):
        k0 = pl.multiple_of(kt * K_TILE, K_TILE)
        for g in range(NS_KV):
            s_m = s_scr[slot, g]
            if causal:
                key = k0 + lax.broadcasted_iota(jnp.int32, (K_TILE, tq), 0)
                qpos = t0 + lax.broadcasted_iota(jnp.int32, (K_TILE, tq), 1)
                s_m = s_m + _tile_heads(jnp.where(key <= qpos, 0.0, NEG_INF))
            m_old = m_scr[g]
            m_new = jnp.maximum(m_old, jnp.max(s_m, axis=0, keepdims=True))
            alpha = jnp.exp2(m_old - m_new)
            p = jnp.exp2(s_m - m_new)
            acc_scr[g] = alpha * acc_scr[g] + _dot(vt_ref[g, :, pl.ds(k0, K_TILE)], p.astype(BF16))
            m_scr[g] = m_new

    scores(0, 0)
    n_before = n_tiles - 1

    def pair(i, carry):
        kt = 2 * i
        scores(kt + 1, 1)
        consume(kt, 0, False)
        scores(kt + 2, 0)
        consume(kt + 1, 1, False)
        return carry

    lax.fori_loop(0, n_before // 2, pair, 0)
    odd = n_before % 2

    @pl.when(odd == 1)
    def _():
        scores(n_tiles - 1, 1)
        consume(n_tiles - 2, 0, False)

    consume(n_tiles - 1, odd, True)
    for g in range(NS_KV):
        acc = acc_scr[g]
        _store_heads_t(o_ref.at[:, g * gw:(g + 1) * gw], acc[0:NS_DH] * (1.0 / acc[NS_DH:NS_DH + 1]))


def nsa_sel_attn(q_r, bias, ks_ext, vs_ext):
    t = q_r.shape[0]
    gw = NS_G * NS_DH
    return pl.pallas_call(
        _sel_attn_body,
        grid=(t // Q_TILE,),
        in_specs=[pl.BlockSpec((Q_TILE, NS_KV * gw), lambda i: (i, 0)),
                  pl.BlockSpec((NS_KV, Q_TILE, LANE), lambda i: (0, i, 0)),
                  pl.BlockSpec((NS_KV, t, NS_DH + LANE), lambda i: (0, 0, 0)),
                  pl.BlockSpec((NS_KV, V_EXT_ROWS, t), lambda i: (0, 0, 0))],
        out_specs=pl.BlockSpec((Q_TILE, NS_KV * gw), lambda i: (i, 0)),
        out_shape=jax.ShapeDtypeStruct((t, BR_W), F32),
        scratch_shapes=[pltpu.VMEM((NS_KV, 1, NS_G * Q_TILE), F32),
                        pltpu.VMEM((NS_KV, V_EXT_ROWS, NS_G * Q_TILE), F32),
                        pltpu.VMEM((2, NS_KV, K_TILE, NS_G * Q_TILE), F32)],
        compiler_params=_cparams("parallel"),
        name="nsa_sel_attn",
    )(q_r, bias, ks_ext, vs_ext)


def _win_attn_body(span, q_ref, k_ref, vt_ref, o_ref):
    tq = q_ref.shape[0]
    t0 = pl.program_id(0) * tq
    gw = NS_G * NS_DH
    start = pl.multiple_of(jnp.maximum(t0 - WINDOW, 0), Q_TILE)
    key = start + lax.broadcasted_iota(jnp.int32, (span, tq), 0)
    qpos = t0 + lax.broadcasted_iota(jnp.int32, (span, tq), 1)
    bias = _tile_heads(jnp.where((key <= qpos) & (key > qpos - WINDOW), 0.0, NEG_INF))
    for g in range(NS_KV):
        qs = _stack_heads(q_ref[:, g * gw:(g + 1) * gw])
        sm = _dot_nt(k_ref[g, pl.ds(start, span), :], qs) + bias
        mx = jnp.max(sm, axis=0, keepdims=True)
        e = jnp.exp2(sm - mx)
        den = jnp.sum(e, axis=0, keepdims=True)
        p = e * (1.0 / den)
        _store_heads_t(o_ref.at[:, g * gw:(g + 1) * gw], _dot(vt_ref[g, :, pl.ds(start, span)], p.astype(BF16)))


def nsa_win_attn(q_r, kw, vw_t):
    t = q_r.shape[0]
    span = min(WINDOW + Q_TILE, t)
    return pl.pallas_call(
        functools.partial(_win_attn_body, span),
        grid=(t // Q_TILE,),
        in_specs=[pl.BlockSpec((Q_TILE, BR_W), lambda i: (i, 0)),
                  pl.BlockSpec((NS_KV, t, NS_DH), lambda i: (0, 0, 0)),
                  pl.BlockSpec((NS_KV, NS_DH, t), lambda i: (0, 0, 0))],
        out_specs=pl.BlockSpec((Q_TILE, BR_W), lambda i: (i, 0)),
        out_shape=jax.ShapeDtypeStruct((t, BR_W), F32),
        compiler_params=_cparams("parallel"),
        name="nsa_win_attn",
    )(q_r, kw, vw_t)


def _gate_expand():
    e = np.zeros((3, LANE, BR_W), np.float32)
    for h in range(NS_H):
        for c in range(3):
            e[c, h * 3 + c, h * NS_DH:(h + 1) * NS_DH] = 1.0
    return jnp.asarray(e, BF16)


def _combine_body(oc_ref, os_ref, ow_ref, gate_ref, g_ref, e_ref, o_ref):
    gs = jax.nn.sigmoid(gate_ref[...])
    hi = gs.astype(BF16)
    lo = (gs - hi.astype(F32)).astype(BF16)

    def expand(c):
        return _dot(hi, e_ref[c]) + _dot(lo, e_ref[c])

    o = expand(0) * oc_ref[...] + expand(1) * os_ref[...] + expand(2) * ow_ref[...]
    o_ref[...] = (o * _silu(g_ref[...])).astype(o_ref.dtype)


def nsa_combine(o_cmp, o_sel, o_win, z_g, z_b):
    m = o_cmp.shape[0]
    tm = _pick(m, (256, 128, 64, 32, 16))
    row = pl.BlockSpec((tm, BR_W), lambda i: (i, 0))
    return pl.pallas_call(
        _combine_body,
        grid=(m // tm,),
        in_specs=[row, row, row, pl.BlockSpec((tm, LANE), lambda i: (i, 0)), row,
                  pl.BlockSpec((3, LANE, BR_W), lambda i: (0, 0, 0))],
        out_specs=row,
        out_shape=jax.ShapeDtypeStruct((m, BR_W), BF16),
        compiler_params=_cparams("parallel"),
        name="nsa_combine",
    )(o_cmp, o_sel, o_win, z_g, z_b, _gate_expand())


def _mem_attn_body(q_ref, g_ref, kv_ref, o_ref):
    scale = MEM_DH ** -0.5
    q = q_ref[...]
    for h in range(MEM_H):
        k = kv_ref[:, h * MEM_DH:(h + 1) * MEM_DH].astype(BF16)
        v = kv_ref[:, (MEM_H + h) * MEM_DH:(MEM_H + h + 1) * MEM_DH].astype(BF16)
        s = _dot_nt(q[:, h * MEM_DH:(h + 1) * MEM_DH].astype(BF16), k) * scale
        e = jnp.exp(s - jnp.max(s, axis=-1, keepdims=True))
        p = e * (1.0 / jnp.sum(e, axis=-1, keepdims=True))
        o = _dot(p.astype(BF16), v)
        sl = slice(h * MEM_DH, (h + 1) * MEM_DH)
        o_ref[:, sl] = (o * _silu(g_ref[:, sl])).astype(o_ref.dtype)


def mem_attn_prompt(z_b, mkv):
    t = z_b.shape[0]
    tq = _pick(t, (512, 256, 128, 64, 32, 16))
    return pl.pallas_call(
        _mem_attn_body,
        grid=(t // tq,),
        in_specs=[pl.BlockSpec((tq, BR_W), lambda i: (i, 1)), pl.BlockSpec((tq, BR_W), lambda i: (i, 2)),
                  pl.BlockSpec(mkv.shape, lambda i: (0, 0))],
        out_specs=pl.BlockSpec((tq, BR_W), lambda i: (i, 0)),
        out_shape=jax.ShapeDtypeStruct((t, BR_W), BF16),
        compiler_params=_cparams("parallel"),
        name="mem_attn_prompt",
    )(z_b, z_b, mkv)


def _mem_attn_step_body(q_ref, g_ref, kv_ref, o_ref):
    scale = MEM_DH ** -0.5
    q = q_ref[...]
    rows_per = MEM_DH // LANE
    out_rows = []
    for h in range(MEM_H):
        qh = jnp.concatenate([q[h * rows_per + i:h * rows_per + i + 1, :] for i in range(rows_per)], axis=1)
        qh = jnp.broadcast_to(qh, (8, MEM_DH)).astype(BF16)
        k = kv_ref[:, h * MEM_DH:(h + 1) * MEM_DH].astype(BF16)
        v = kv_ref[:, (MEM_H + h) * MEM_DH:(MEM_H + h + 1) * MEM_DH].astype(BF16)
        s = _dot_nt(qh, k) * scale
        e = jnp.exp(s - jnp.max(s, axis=-1, keepdims=True))
        p = e * (1.0 / jnp.sum(e, axis=-1, keepdims=True))
        o = _dot(p.astype(BF16), v)[0:1, :]
        out_rows += [o[:, i * LANE:(i + 1) * LANE] for i in range(rows_per)]
    o_ref[...] = jnp.concatenate(out_rows, axis=0) * _silu(g_ref[...])


def mem_attn_sample(z_b3, kv_cache, layer):
    bs = z_b3.shape[0]
    mem_len, width = kv_cache.shape[2], kv_cache.shape[3]
    return pl.pallas_call(
        _mem_attn_step_body,
        grid=(bs,),
        in_specs=[pl.BlockSpec((None, 8, LANE), lambda b: (b, 1, 0)), pl.BlockSpec((None, 8, LANE), lambda b: (b, 2, 0)),
                  pl.BlockSpec((None, None, mem_len, width), lambda b: (layer, b, 0, 0))],
        out_specs=pl.BlockSpec((None, 8, LANE), lambda b: (b, 0, 0)),
        out_shape=jax.ShapeDtypeStruct((bs, 8, LANE), F32),
        compiler_params=_cparams("parallel"),
        name="mem_attn_sample",
    )(z_b3, z_b3, kv_cache)


def _paged_partial_body(npg, pt_ref, *refs):
    del pt_ref
    page_refs, w_ref, o_ref = refs[:npg], refs[npg], refs[npg + 1]
    chunks = PAGE_SIZE // CMP_STRIDE
    o_ref[...] = jnp.zeros_like(o_ref)

    unroll = 4

    def step(i, carry):
        for c in range(2 * NS_KV):
            acc = o_ref[c]
            for u in range(unroll):
                s = i * unroll + u
                xs = jnp.concatenate(
                    [r[pl.ds(s * ROWS_PER_TOKEN + c, chunks, stride=CMP_STRIDE * ROWS_PER_TOKEN), :]
                     for r in page_refs], axis=0)
                acc += _dot(xs.astype(BF16), w_ref[c // NS_KV, s])
            o_ref[c] = acc
        return carry

    lax.fori_loop(0, CMP_STRIDE // unroll, step, 0)


def compress_partial_paged(cache2, page_table, w1, layer, n_pool):
    bs, n_pages = page_table.shape
    npg = min(PAGES_PER_STEP, n_pages)
    assert n_pages % npg == 0
    chunks = PAGE_SIZE // CMP_STRIDE
    w = _cmp_weights(w1).reshape(2, CMP_STRIDE, NS_DH, 2 * NS_DH)

    def page_spec(i):
        return pl.BlockSpec((PAGE_SIZE * ROWS_PER_TOKEN, NS_DH),
                            lambda b, j, pt: (layer * n_pool + pt[b, j * npg + i], 0))

    grid_spec = pltpu.PrefetchScalarGridSpec(
        num_scalar_prefetch=1,
        grid=(bs, n_pages // npg),
        in_specs=[page_spec(i) for i in range(npg)] + [pl.BlockSpec(w.shape, lambda b, j, pt: (0, 0, 0, 0))],
        out_specs=pl.BlockSpec((2 * NS_KV, None, npg * chunks, 2 * NS_DH), lambda b, j, pt: (0, b, j, 0)),
    )
    out = pl.pallas_call(
        functools.partial(_paged_partial_body, npg),
        grid_spec=grid_spec,
        out_shape=jax.ShapeDtypeStruct((2 * NS_KV, bs, n_pages * chunks, 2 * NS_DH), F32),
        compiler_params=_cparams("parallel", "parallel"),
        name="cmp_partial_paged",
    )(page_table, *([cache2] * npg), w)
    return out.reshape(2, NS_KV * bs * n_pages * chunks, 2 * NS_DH)


def _row_group(shape):
    return lax.broadcasted_iota(jnp.int32, shape, 0) // NS_G


def _cmp_step_body(n_cmp, qpos, q_ref, kc_ref, vc_ref, o_ref, ps_ref):
    q = q_ref[...].astype(BF16)
    ncp = kc_ref.shape[1]
    s = [_dot_nt(q, kc_ref[g].astype(BF16)) for g in range(NS_KV)]
    grp = _row_group((NS_H, ncp))
    s = jnp.where(grp == 0, s[0], s[1])
    n_i = lax.broadcasted_iota(jnp.int32, (NS_H, ncp), 1)
    valid = (n_i * CMP_STRIDE + CMP_LEN - 1 <= qpos) & (n_i < n_cmp)
    sm = jnp.where(valid, s, NEG_INF)
    mx = jnp.max(sm, axis=-1, keepdims=True)
    e = jnp.where(valid, jnp.exp2(sm - mx), 0.0)
    den = jnp.sum(e, axis=-1, keepdims=True)
    p = e * (1.0 / jnp.where(den > 0.0, den, 1.0))
    pb = p.astype(BF16)
    o = [_dot(pb, vc_ref[g].astype(BF16)) for g in range(NS_KV)]
    o_ref[...] = jnp.where(_row_group((NS_H, NS_DH)) == 0, o[0], o[1])
    ps_ref[...] = jnp.concatenate(
        [jnp.sum(jnp.where(grp == g, p, 0.0), axis=0, keepdims=True) for g in range(NS_KV)], axis=0)


def nsa_cmp_sample(q_c3, kcmp, vcmp, n_cmp, qpos):
    bs = q_c3.shape[0]
    ncp = kcmp.shape[2]
    kv = pl.BlockSpec((NS_KV, None, ncp, NS_DH), lambda b: (0, b, 0, 0))
    return pl.pallas_call(
        functools.partial(_cmp_step_body, n_cmp, qpos),
        grid=(bs,),
        in_specs=[pl.BlockSpec((None, NS_H, NS_DH), lambda b: (b, 0, 0)), kv, kv],
        out_specs=[pl.BlockSpec((None, NS_H, NS_DH), lambda b: (b, 0, 0)),
                   pl.BlockSpec((None, NS_KV, ncp), lambda b: (b, 0, 0))],
        out_shape=[jax.ShapeDtypeStruct((bs, NS_H, NS_DH), F32), jax.ShapeDtypeStruct((bs, NS_KV, ncp), F32)],
        compiler_params=_cparams("parallel"),
        name="nsa_cmp_sample",
    )(q_c3, kcmp, vcmp)


def _select_body(n_sel, qpos, ps_ref, msel_ref, idx_ref):
    hi, mid, lo = _split3(ps_ref[...])
    imp = _dot(hi, msel_ref[...]) + _dot(mid, msel_ref[...]) + _dot(lo, msel_ref[...])
    j_i = lax.broadcasted_iota(jnp.int32, imp.shape, 1)
    cur = qpos // SEL_BLOCK
    forced = (j_i == 0) | (j_i == cur) | (j_i == cur - 1)
    imp = jnp.where(forced, imp + FORCE_BONUS, imp)
    imp = jnp.where(j_i <= cur, imp, NEG_INF)
    imp = jnp.where(j_i < n_sel, imp, REMOVED)
    j_f = j_i.astype(F32)
    col = lax.broadcasted_iota(jnp.int32, idx_ref.shape, 1)
    out = jnp.zeros(idx_ref.shape, F32)
    for it in range(min(SEL_TOPN, n_sel)):
        mx = jnp.max(imp, axis=-1, keepdims=True)
        first = jnp.min(jnp.where(imp == mx, j_f, float(imp.shape[1])), axis=-1, keepdims=True)
        out = jnp.where(col == it, first, out)
        imp = jnp.where(j_f == first, REMOVED, imp)
    idx_ref[...] = out.astype(jnp.int32)


def nsa_select_sample(psum2, n_cmp, n_sel, qpos):
    rows, ncp = psum2.shape
    nsp = -(-n_sel // LANE) * LANE
    msel = jnp.asarray(_cmp_to_sel(ncp, nsp, n_cmp, n_sel), BF16)
    return pl.pallas_call(
        functools.partial(_select_body, n_sel, qpos),
        grid=(1,),
        in_specs=[pl.BlockSpec((rows, ncp), lambda i: (0, 0)), pl.BlockSpec((ncp, nsp), lambda i: (0, 0))],
        out_specs=pl.BlockSpec((rows, LANE), lambda i: (0, 0)),
        out_shape=jax.ShapeDtypeStruct((rows, LANE), jnp.int32),
        compiler_params=_cparams("arbitrary"),
        name="nsa_select_sample",
    )(psum2, msel)


def _sel_step_body(n_past_blocks, qpos, pt_ref, top_ref, q_ref, *refs):
    del pt_ref
    nblk = NS_KV * SEL_PER_STEP
    blk_refs, new_ref, o_ref, m_scr, l_scr, acc_scr = refs[:nblk], *refs[nblk:nblk + 5]
    b = pl.program_id(0)
    j = pl.program_id(1)

    @pl.when(j == 0)
    def _():
        m_scr[...] = jnp.full_like(m_scr, NEG_INF)
        l_scr[...] = jnp.zeros_like(l_scr)
        acc_scr[...] = jnp.zeros_like(acc_scr)

    def slab(ref, slot, g):
        return ref[pl.ds(slot * NS_KV + g, SEL_BLOCK, stride=ROWS_PER_TOKEN), :].astype(BF16)

    q = q_ref[...]
    qb = q.astype(BF16)
    nk = SEL_PER_STEP * SEL_BLOCK
    grp_k = _row_group((NS_H, nk))
    grp_d = _row_group((NS_H, NS_DH))
    lane_blk = lax.broadcasted_iota(jnp.int32, (NS_H, nk), 1) // SEL_BLOCK
    s_g, idx_g, new_g = [], [], []
    for g in range(NS_KV):
        k_all = jnp.concatenate([slab(blk_refs[g * SEL_PER_STEP + k], 2, g) for k in range(SEL_PER_STEP)], axis=0)
        s_g.append(_dot_nt(qb, k_all))
        ids = [top_ref[b, g * SEL_TOPN + j * SEL_PER_STEP + k] for k in range(SEL_PER_STEP)]
        idx = jnp.zeros((NS_H, nk), jnp.int32)
        n_new = jnp.int32(0)
        for k, i in enumerate(ids):
            idx = jnp.where(lane_blk == k, i, idx)
            n_new = n_new + (i == n_past_blocks).astype(jnp.int32)
        idx_g.append(idx)
        new_g.append(n_new)
    s = jnp.where(grp_k == 0, s_g[0], s_g[1])
    idx = jnp.where(grp_k == 0, idx_g[0], idx_g[1])
    tok = idx * SEL_BLOCK + lax.broadcasted_iota(jnp.int32, (NS_H, nk), 1) % SEL_BLOCK
    valid = (idx < n_past_blocks) & (tok <= qpos)
    ks_new = jnp.where(grp_d == 0, new_ref[4:5, :], new_ref[5:6, :])
    vs_new = jnp.where(grp_d == 0, new_ref[6:7, :], new_ref[7:8, :])
    is_new = jnp.where(grp_d == 0, new_g[0], new_g[1]) > 0
    s_new = jnp.where(is_new, jnp.sum(q * ks_new, axis=-1, keepdims=True), NEG_INF)
    m_old = m_scr[...]
    m_new = jnp.maximum(jnp.maximum(m_old, jnp.max(jnp.where(valid, s, NEG_INF), axis=-1, keepdims=True)), s_new)
    alpha = jnp.exp2(m_old - m_new)
    p = jnp.where(valid, jnp.exp2(s - m_new[:, 0:1]), 0.0)
    p_new = jnp.where(is_new, jnp.exp2(s_new - m_new), 0.0)
    pb = p.astype(BF16)
    pv_g = []
    for g in range(NS_KV):
        v_all = jnp.concatenate([slab(blk_refs[g * SEL_PER_STEP + k], 3, g) for k in range(SEL_PER_STEP)], axis=0)
        pv_g.append(_dot(pb, v_all))
    pv = jnp.where(grp_d == 0, pv_g[0], pv_g[1])
    l_scr[...] = alpha * l_scr[...] + jnp.sum(p, axis=-1, keepdims=True) + p_new
    acc_scr[...] = alpha * acc_scr[...] + pv + p_new * vs_new
    m_scr[...] = m_new

    @pl.when(j == pl.num_programs(1) - 1)
    def _():
        o_ref[...] = acc_scr[...] / l_scr[...]


def nsa_sel_sample(q_r3, rows3, cache2, page_table, top_idx, layer, n_pool, qpos):
    bs, n_pages = page_table.shape
    halves = PAGE_SIZE // SEL_BLOCK
    n_past_blocks = n_pages * halves

    def blk(g, k):
        def imap(b, j, pt, top):
            i = top[b, g * SEL_TOPN + j * SEL_PER_STEP + k]
            page = pt[b, jnp.minimum(i // halves, n_pages - 1)]
            return ((layer * n_pool + page) * halves + i % halves, 0)
        return pl.BlockSpec((SEL_BLOCK * ROWS_PER_TOKEN, NS_DH), imap)

    vec = pl.BlockSpec((None, NS_H, NS_DH), lambda b, j, pt, top: (b, 0, 0))
    grid_spec = pltpu.PrefetchScalarGridSpec(
        num_scalar_prefetch=2,
        grid=(bs, SEL_TOPN // SEL_PER_STEP),
        in_specs=[vec] + [blk(g, k) for g in range(NS_KV) for k in range(SEL_PER_STEP)] + [vec],
        out_specs=vec,
        scratch_shapes=[pltpu.VMEM((NS_H, NS_DH), F32)] * 3,
    )
    return pl.pallas_call(
        functools.partial(_sel_step_body, n_past_blocks, qpos),
        grid_spec=grid_spec,
        out_shape=jax.ShapeDtypeStruct((bs, NS_H, NS_DH), F32),
        compiler_params=_cparams("parallel", "arbitrary"),
        name="nsa_sel_sample",
    )(page_table, top_idx, q_r3, *([cache2] * (NS_KV * SEL_PER_STEP)), rows3)


def _win_step_body(pos0, qpos, q_ref, buf_ref, new_ref, o_ref):
    q = q_ref[...]
    qb = q.astype(BF16)
    rows_per = 2 * NS_KV
    wb = buf_ref.shape[0] // rows_per

    def slab(c):
        return buf_ref[pl.ds(c, wb, stride=rows_per), :].astype(BF16)

    grp_k = _row_group((NS_H, wb))
    grp_d = _row_group((NS_H, NS_DH))
    s = jnp.where(grp_k == 0, _dot_nt(qb, slab(0)), _dot_nt(qb, slab(1)))
    kpos = pos0 + lax.broadcasted_iota(jnp.int32, (NS_H, wb), 1)
    valid = (kpos <= qpos) & (kpos > qpos - WINDOW) & (kpos >= 0)
    kw_new = jnp.where(grp_d == 0, new_ref[0:1, :], new_ref[1:2, :])
    vw_new = jnp.where(grp_d == 0, new_ref[2:3, :], new_ref[3:4, :])
    s_new = jnp.sum(q * kw_new, axis=-1, keepdims=True)
    mx = jnp.maximum(jnp.max(jnp.where(valid, s, NEG_INF), axis=-1, keepdims=True), s_new)
    p = jnp.where(valid, jnp.exp2(s - mx), 0.0)
    p_new = jnp.exp2(s_new - mx)
    den = jnp.sum(p, axis=-1, keepdims=True) + p_new
    pb = p.astype(BF16)
    pv = jnp.where(grp_d == 0, _dot(pb, slab(2)), _dot(pb, slab(3)))
    o_ref[...] = (pv + p_new * vw_new) / den


def nsa_win_sample(q_r3, win_buf2, win3, layer, wb, past_len, qpos):
    bs = q_r3.shape[0]
    return pl.pallas_call(
        functools.partial(_win_step_body, past_len - wb, qpos),
        grid=(bs,),
        in_specs=[pl.BlockSpec((None, NS_H, NS_DH), lambda b: (b, 0, 0)),
                  pl.BlockSpec((wb * 2 * NS_KV, NS_DH), lambda b: (layer * bs + b, 0)),
                  pl.BlockSpec((None, 2 * NS_KV, NS_DH), lambda b: (b, 0, 0))],
        out_specs=pl.BlockSpec((None, NS_H, NS_DH), lambda b: (b, 0, 0)),
        out_shape=jax.ShapeDtypeStruct((bs, NS_H, NS_DH), F32),
        compiler_params=_cparams("parallel"),
        name="nsa_win_sample",
    )(q_r3, win_buf2, win3)


def _project(x, norm_g, w_a, w_g, w_b):
    h = rmsnorm_rows(x, norm_g, BF16)
    return (matmul(h, w_a, name="in_proj_a"), matmul(h, w_g, name="in_proj_gate"), matmul(h, w_b, name="in_proj_b"))


def _merge_out(x, branches, z_b, w_up_b, w_out_b):
    d = x.shape[1]
    gated = merge_up(branches, z_b, w_up_b, d)
    return matmul(gated, w_out_b, res=x, name="out_proj")


def kernel(x_prompt, x_sample, mem_prompt, cache_nsa_kv, state_win_kv, state_hgrn, state_conv, cache_mem_kv,
           page_table, norm_g, final_norm_g, w_in, hg_lb_logits, hg_norm_g, cv_w, ns_pe, ns_cw1, ns_cw2,
           mem_norm_g, w_mem_kv, w_up, w_out):
    depth = w_in.shape[0]
    b_p, t, d = x_prompt.shape
    bs, ts = x_sample.shape[:2]
    assert b_p == 1 and ts == 1
    n_pages = page_table.shape[1]
    past_len = n_pages * PAGE_SIZE
    wb = state_win_kv.shape[2]
    mem_len = mem_prompt.shape[1]
    kvw = NS_KV * NS_DH

    s_lb = jax.nn.softmax(hg_lb_logits.astype(F32), axis=0)
    lower = jnp.cumsum(s_lb, axis=0) - s_lb[0]

    pos_p = jnp.arange(t, dtype=jnp.int32)
    qpos_s = past_len
    pos_s = jnp.full((bs,), qpos_s, jnp.int32)

    n_pool = cache_nsa_kv.shape[1]
    cache2 = cache_nsa_kv.reshape(depth * n_pool * PAGE_SIZE * ROWS_PER_TOKEN, NS_DH)
    win_buf2 = state_win_kv.reshape(depth * bs * wb * 2 * NS_KV, NS_DH)
    mem_cache4 = cache_mem_kv.reshape(depth, bs, mem_len, 2 * MEM_H * MEM_DH)

    nch_p = t // CMP_STRIDE
    nch_s = (past_len + 1) // CMP_STRIDE
    n_sel_s = -(-(past_len + 1) // SEL_BLOCK)

    xp = x_prompt.reshape(t, d)
    xs = x_sample.reshape(bs, d)
    rows_p, win_p, hg_p, cv_p, mkv_p = [], [], [], [], []
    rows_s, win_s, hg_s, cv_s = [], [], [], []
    for l in range(depth):
        w_l = w_in[l]
        w_a = w_l[:, :COL_A].astype(BF16)
        w_g = jnp.pad(w_l[:, COL_A:COL_B0], ((0, 0), (0, LANE - COL_GATE))).astype(BF16)
        w_b = w_l[:, COL_B0:].astype(BF16)
        w_up_b = w_up[l].astype(BF16)
        w_out_b = w_out[l].astype(BF16)

        z_a, z_g, z_b = _project(xp, norm_g[l], w_a, w_g, w_b)
        o_hg, st_hg = hgrn_prompt(z_a, lower[l], hg_norm_g[l])
        o_cv, st_cv = conv_prompt(z_a, cv_w[l], jnp.zeros((CV_K - 1, BR_W), F32))
        q_c, q_r, rows, win, ks_ext, kw, vs_ext, vw_t = nsa_prep(z_a, pos_p, BF16, attn_layouts=True)
        pe_term = compress_pe_term(ns_pe[l], ns_cw1[l])
        cmp = compress_finish(compress_partial_rows(z_a, ns_cw1[l]), pe_term, ns_cw2[l], nch_p)
        kcmp = cmp[0].astype(BF16)
        vcmp_t = jnp.swapaxes(cmp[1], 1, 2).astype(BF16)
        o_cmp, sel_bias = nsa_cmp_select(q_c, kcmp, vcmp_t, nch_p - 1)
        o_sel = nsa_sel_attn(q_r, sel_bias, ks_ext, vs_ext)
        o_win = nsa_win_attn(q_r, kw, vw_t)
        o_ns = nsa_combine(o_cmp, o_sel, o_win, z_g, z_b)
        mem_h = rmsnorm_rows(mem_prompt.reshape(mem_len, d), mem_norm_g[l], BF16)
        mkv = matmul(mem_h, w_mem_kv[l].astype(BF16), name="mem_kv")
        o_mm = mem_attn_prompt(z_b, mkv)
        xp = _merge_out(xp, (o_hg, o_cv, o_ns, o_mm), z_b, w_up_b, w_out_b)
        rows_p.append(rows)
        win_p.append(win[t - min(WINDOW, t):])
        hg_p.append(st_hg.reshape(1, HG_H, HG_DK, HG_DV))
        cv_p.append(st_cv.reshape(1, CV_K - 1, BR_W))
        mkv_p.append(mkv.reshape(1, mem_len, 2, MEM_H, MEM_DH))

        z_a, z_g, z_b = _project(xs, norm_g[l], w_a, w_g, w_b)
        z_a3 = z_a.reshape(bs, COL_A // LANE, LANE)
        z_b3 = z_b.reshape(bs, z_b.shape[1] // LANE, LANE)
        o_hg, st_hg = hgrn_sample(z_a3, lower[l], hg_norm_g[l], state_hgrn[l])
        o_cv, st_cv = conv_sample(z_a, cv_w[l], state_conv[l])
        q_c, q_r, rows, win = nsa_prep(z_a, pos_s, F32)
        part = compress_partial_paged(cache2, page_table, ns_cw1[l], l, n_pool)
        cmp = compress_finish(part, pe_term, ns_cw2[l], nch_s)
        cmp = cmp.reshape(2, NS_KV, bs, nch_s, NS_DH)
        q_c3 = q_c.reshape(bs, NS_H, NS_DH)
        q_r3 = q_r.reshape(bs, NS_H, NS_DH)
        o_cmp, psum = nsa_cmp_sample(q_c3, cmp[0], cmp[1], nch_s - 1, qpos_s)
        top = nsa_select_sample(psum.reshape(bs * NS_KV, nch_s), nch_s - 1, n_sel_s, qpos_s)
        top = top[:, :SEL_TOPN].reshape(bs, NS_KV * SEL_TOPN)
        o_sel = nsa_sel_sample(q_r3, rows.reshape(bs, 4 * NS_KV, NS_DH), cache2, page_table, top, l, n_pool, qpos_s)
        o_win = nsa_win_sample(q_r3, win_buf2, win.reshape(bs, 2 * NS_KV, NS_DH), l, wb, past_len, qpos_s)
        o_ns = nsa_combine(o_cmp.reshape(bs, BR_W), o_sel.reshape(bs, BR_W), o_win.reshape(bs, BR_W), z_g, z_b)
        o_mm = mem_attn_sample(z_b3, mem_cache4, l)
        branches = (o_hg.reshape(bs, BR_W).astype(BF16), o_cv.astype(BF16), o_ns, o_mm.reshape(bs, BR_W).astype(BF16))
        xs = _merge_out(xs, branches, z_b, w_up_b, w_out_b)
        rows_s.append(rows)
        win_s.append(win)
        hg_s.append(st_hg)
        cv_s.append(st_cv)

    y_prompt = rmsnorm_rows(xp, final_norm_g, F32).reshape(1, t, d)
    y_sample = rmsnorm_rows(xs, final_norm_g, F32).reshape(bs, 1, d)
    wlen = min(WINDOW, t)
    new_rows_p = jnp.stack(rows_p).reshape(depth, 1, t, 4, NS_KV, NS_DH)
    new_win_p = jnp.stack(win_p).reshape(depth, 1, wlen, 2, NS_KV, NS_DH)
    new_rows_s = jnp.stack(rows_s).reshape(depth, bs, 1, 4, NS_KV, NS_DH)
    new_win_s = jnp.concatenate(
        [state_win_kv[:, :, 1:], jnp.stack(win_s).reshape(depth, bs, 1, 2, NS_KV, NS_DH)], axis=2)
    return (y_prompt, y_sample, new_rows_p, new_win_p, jnp.stack(hg_p), jnp.stack(cv_p),
            jnp.stack(mkv_p), new_rows_s, new_win_s, jnp.stack(hg_s), jnp.stack(cv_s))
```

```python
import functools

import numpy as np
import jax
import jax.numpy as jnp
from jax import lax
from jax.experimental import pallas as pl
from jax.experimental.pallas import tpu as pltpu

F32 = jnp.float32
BF16 = jnp.bfloat16

BR_W = 1024
N_BRANCH = 4
HG_H = 8
HG_DK = 128
HG_DV = 128
F_MIN = 1e-30
CV_K = 3
NS_H = 8
NS_KV = 2
NS_G = NS_H // NS_KV
NS_DH = 128
CMP_LEN = 32
CMP_STRIDE = 16
SEL_BLOCK = 64
SEL_TOPN = 16
WINDOW = 512
FORCE_BONUS = 100.0
ROPE_THETA = 500000.0
ROT_DIM = NS_DH // 4
MEM_H = 4
MEM_DH = 256
NORM_EPS = 1e-6
NEG_INF = -1e30
REMOVED = -3e38
LOG2_E = 1.4426950408889634
PAGE_SIZE = 128
ROWS_PER_TOKEN = 4 * NS_KV

COL_A = 10752
COL_GATE = 24
COL_B0 = COL_A + COL_GATE

LANE = 128
HG_CHUNK = 128
HG_HEADS_PER_STEP = 4
Q_TILE = 128
K_TILE = 512
PAGES_PER_STEP = 16
SEL_PER_STEP = 4
VMEM_LIMIT = 56 * 1024 * 1024


def _cparams(*sem):
    return pltpu.CompilerParams(dimension_semantics=sem, vmem_limit_bytes=VMEM_LIMIT)


def _pick(n, cands):
    for c in cands:
        if n % c == 0:
            return c
    return n


def _silu(x):
    return x * jax.nn.sigmoid(x)


def _dot(a, b):
    return jnp.dot(a, b, preferred_element_type=F32)


def _dot_nt(a, b):
    return lax.dot_general(a, b, (((1,), (1,)), ((), ())), preferred_element_type=F32)


def _split3(x):
    hi = x.astype(BF16)
    r1 = x - hi.astype(F32)
    mid = r1.astype(BF16)
    lo = (r1 - mid.astype(F32)).astype(BF16)
    return hi, mid, lo


def _norm_body(x_ref, g_ref, o_ref):
    x = x_ref[...]
    ms = jnp.mean(x * x, axis=-1, keepdims=True)
    o_ref[...] = (x * lax.rsqrt(ms + NORM_EPS) * g_ref[...]).astype(o_ref.dtype)


def rmsnorm_rows(x, g, out_dtype):
    m, d = x.shape
    tm = _pick(m, (256, 128, 64, 32, 16, 8))
    return pl.pallas_call(
        _norm_body,
        grid=(m // tm,),
        in_specs=[pl.BlockSpec((tm, d), lambda i: (i, 0)), pl.BlockSpec((1, d), lambda i: (0, 0))],
        out_specs=pl.BlockSpec((tm, d), lambda i: (i, 0)),
        out_shape=jax.ShapeDtypeStruct((m, d), out_dtype),
        compiler_params=_cparams("parallel"),
        name="rmsnorm",
    )(x, g.reshape(1, d).astype(F32))


def _mm_body(a_ref, b_ref, o_ref):
    o_ref[...] = _dot(a_ref[...].astype(BF16), b_ref[...]).astype(o_ref.dtype)


def _mm_res_body(a_ref, b_ref, r_ref, o_ref):
    o_ref[...] = r_ref[...] + _dot(a_ref[...].astype(BF16), b_ref[...])


def matmul(a, b, res=None, out_dtype=F32, name="matmul"):
    m, k = a.shape
    n = b.shape[1]
    tm = _pick(m, (1024, 512, 256, 128, 64, 32, 16, 8))
    tn = _pick(n, (1536, 1024, 512, 256, 128))
    in_specs = [pl.BlockSpec((tm, k), lambda i, j: (i, 0)), pl.BlockSpec((k, tn), lambda i, j: (0, j))]
    args = [a, b]
    body = _mm_body
    if res is not None:
        in_specs.append(pl.BlockSpec((tm, tn), lambda i, j: (i, j)))
        args.append(res)
        body = _mm_res_body
    return pl.pallas_call(
        body,
        grid=(m // tm, n // tn),
        in_specs=in_specs,
        out_specs=pl.BlockSpec((tm, tn), lambda i, j: (i, j)),
        out_shape=jax.ShapeDtypeStruct((m, n), out_dtype),
        compiler_params=_cparams("parallel", "parallel"),
        name=name,
    )(*args)


def _out_norm_body(keep_x, a_ref, b_ref, r_ref, g_ref, *out_refs):
    x = r_ref[...] + _dot(a_ref[...], b_ref[...])
    if keep_x:
        out_refs[0][...] = x
    ms = jnp.mean(x * x, axis=-1, keepdims=True)
    h_ref = out_refs[-1]
    h_ref[...] = (x * lax.rsqrt(ms + NORM_EPS) * g_ref[...]).astype(h_ref.dtype)


def out_proj_norm(a, b, res, g, h_dtype, keep_x):
    m, k = a.shape
    n = b.shape[1]
    tm = _pick(m, (512, 256, 128, 64, 32, 16, 8))
    row = pl.BlockSpec((tm, n), lambda i: (i, 0))
    out_specs = [row, row] if keep_x else [row]
    out_shape = [jax.ShapeDtypeStruct((m, n), h_dtype)]
    if keep_x:
        out_shape = [jax.ShapeDtypeStruct((m, n), F32)] + out_shape
    outs = pl.pallas_call(
        functools.partial(_out_norm_body, keep_x),
        grid=(m // tm,),
        in_specs=[pl.BlockSpec((tm, k), lambda i: (i, 0)), pl.BlockSpec((k, n), lambda i: (0, 0)), row,
                  pl.BlockSpec((1, n), lambda i: (0, 0))],
        out_specs=out_specs,
        out_shape=out_shape,
        compiler_params=_cparams("parallel"),
        name="out_proj_norm",
    )(a, b, res, g.reshape(1, n).astype(F32))
    return outs if keep_x else (None, outs[0])


def _merge_body(b0, b1, b2, b3, g0, g1, g2, g3, w0, w1, w2, w3, o_ref):
    acc = jax.nn.sigmoid(g0[...]) * _dot(b0[...], w0[...])
    acc += jax.nn.sigmoid(g1[...]) * _dot(b1[...], w1[...])
    acc += jax.nn.sigmoid(g2[...]) * _dot(b2[...], w2[...])
    acc += jax.nn.sigmoid(g3[...]) * _dot(b3[...], w3[...])
    o_ref[...] = acc.astype(o_ref.dtype)


def merge_up(branches, z_b, w_up_bf16, d_model):
    m = branches[0].shape[0]
    tm = _pick(m, (512, 256, 128, 64, 32, 16))
    tn = 512
    nj = d_model // tn
    gate_col0 = (z_b.shape[1] - N_BRANCH * d_model) // tn
    br_specs = [pl.BlockSpec((tm, BR_W), lambda i, j: (i, 0)) for _ in range(N_BRANCH)]
    g_specs = [pl.BlockSpec((tm, tn), functools.partial(lambda i, j, n: (i, gate_col0 + n * nj + j), n=n))
               for n in range(N_BRANCH)]
    w_specs = [pl.BlockSpec((None, BR_W, tn), functools.partial(lambda i, j, n: (n, 0, j), n=n))
               for n in range(N_BRANCH)]
    return pl.pallas_call(
        _merge_body,
        grid=(m // tm, nj),
        in_specs=br_specs + g_specs + w_specs,
        out_specs=pl.BlockSpec((tm, tn), lambda i, j: (i, j)),
        out_shape=jax.ShapeDtypeStruct((m, d_model), BF16),
        compiler_params=_cparams("parallel", "parallel"),
        name="merge_up",
    )(*branches, z_b, z_b, z_b, z_b, w_up_bf16, w_up_bf16, w_up_bf16, w_up_bf16)


def _hgrn_consts(c):
    nlev = int(round(np.log2(c)))
    t = np.arange(c)[:, None]
    r = np.arange(c)[None, :]
    blocks = [r <= t]
    masks = [t == r]
    for lv in range(nlev):
        h = c >> (lv + 1)
        mid = (t // (2 * h)) * (2 * h) + h
        blocks.append(np.where(t >= mid, (r >= mid) & (r <= t), (r > t) & (r <= mid - 1)))
        same = (t // (2 * h)) == (r // (2 * h))
        masks.append(same & (t % (2 * h) >= h) & (r % (2 * h) < h))
    blocks.append(r > t)
    l_all = np.concatenate(blocks, axis=0).astype(np.float32)
    return jnp.asarray(l_all, BF16), jnp.asarray(np.stack(masks).astype(np.float32)), nlev


def _hgrn_body(nlev, q_ref, z_ref, v_ref, g_ref, lb_ref, ng_ref, l_ref, mask_ref, o_ref, s_ref, st_scr):
    c_idx = pl.program_id(1)
    c = q_ref.shape[0]

    @pl.when(c_idx == 0)
    def _():
        st_scr[...] = jnp.zeros_like(st_scr)

    for hh in range(HG_HEADS_PER_STEP):
        sl = slice(hh * HG_DK, (hh + 1) * HG_DK)
        lb = lb_ref[:, sl]
        q = _silu(q_ref[:, sl])
        z = z_ref[:, sl]
        f = lb + (1.0 - lb) * jax.nn.sigmoid(z)
        logf = jnp.log(jnp.maximum(f, F_MIN))
        k = (1.0 - lb) * jax.nn.sigmoid(-z)
        v = v_ref[:, sl]

        hi, mid, _ = _split3(logf)
        e2 = _dot(l_ref[...], jnp.concatenate([hi, mid], axis=1))
        dk = HG_DK
        x = jnp.exp(e2[:, :dk] + e2[:, dk:])
        eb = x[0:c]
        est = x[(nlev + 1) * c:(nlev + 2) * c]

        st = st_scr[hh]
        inter = _dot_nt((q * eb).astype(BF16), st.astype(BF16))
        att = mask_ref[0] * _dot_nt(q.astype(BF16), k.astype(BF16))
        for lv in range(nlev):
            fac = x[(1 + lv) * c:(2 + lv) * c]
            att += mask_ref[1 + lv] * _dot_nt((q * fac).astype(BF16), (k * fac).astype(BF16))
        o = inter + _dot(att.astype(BF16), v.astype(BF16))
        o = o * lax.rsqrt(jnp.mean(o * o, axis=-1, keepdims=True) + NORM_EPS) * ng_ref[:, sl]
        o_ref[:, sl] = (o * _silu(g_ref[:, sl])).astype(o_ref.dtype)

        st_scr[hh] = st * eb[c - 1:c, :] + _dot(v.T.astype(BF16), (k * est).astype(BF16))

    @pl.when(c_idx == pl.num_programs(1) - 1)
    def _():
        for hh in range(HG_HEADS_PER_STEP):
            s_ref[hh] = st_scr[hh].T


def hgrn_prompt(z_a, lb, norm_g):
    t = z_a.shape[0]
    c = HG_CHUNK
    l_all, masks, nlev = _hgrn_consts(c)
    hp = HG_HEADS_PER_STEP
    steps = HG_H // hp
    width = hp * HG_DK

    def col(k):
        return pl.BlockSpec((c, width), functools.partial(lambda h, i, k: (i, k * steps + h), k=k))

    vec = pl.BlockSpec((None, 1, width), lambda h, i: (h, 0, 0))
    return pl.pallas_call(
        functools.partial(_hgrn_body, nlev),
        grid=(steps, t // c),
        in_specs=[col(0), col(1), col(2), col(3), vec, vec,
                  pl.BlockSpec(l_all.shape, lambda h, i: (0, 0)),
                  pl.BlockSpec(masks.shape, lambda h, i: (0, 0, 0))],
        out_specs=[pl.BlockSpec((c, width), lambda h, i: (i, h)),
                   pl.BlockSpec((hp, HG_DK, HG_DV), lambda h, i: (h, 0, 0))],
        out_shape=[jax.ShapeDtypeStruct((t, BR_W), BF16),
                   jax.ShapeDtypeStruct((HG_H, HG_DK, HG_DV), F32)],
        scratch_shapes=[pltpu.VMEM((hp, HG_DV, HG_DK), F32)],
        compiler_params=_cparams("parallel", "arbitrary"),
        name="hgrn_prompt",
    )(z_a, z_a, z_a, z_a, lb.reshape(steps, 1, width), norm_g.reshape(steps, 1, width), l_all, masks)


def _hgrn_step_body(q_ref, z_ref, v_ref, g_ref, lb_ref, ng_ref, s0_ref, o_ref, s_ref):
    lb = lb_ref[...]
    q = _silu(q_ref[...])
    z = z_ref[...]
    f = jnp.maximum(lb + (1.0 - lb) * jax.nn.sigmoid(z), F_MIN)
    k = (1.0 - lb) * jax.nn.sigmoid(-z)
    v = v_ref[...]
    rows = []
    for h in range(HG_H):
        def colb(a):
            return jnp.broadcast_to(a[h:h + 1, :], (HG_DK, HG_DK)).T
        s_new = colb(f) * s0_ref[h] + colb(k) * v[h:h + 1, :]
        s_ref[h] = s_new
        rows.append(jnp.sum(colb(q) * s_new, axis=0, keepdims=True))
    o = jnp.concatenate(rows, axis=0)
    o = o * lax.rsqrt(jnp.mean(o * o, axis=-1, keepdims=True) + NORM_EPS) * ng_ref[...]
    o_ref[...] = o * _silu(g_ref[...])


def hgrn_sample(z_a3, lb, norm_g, s0):
    bs = z_a3.shape[0]

    def grp(k):
        return pl.BlockSpec((None, HG_H, LANE), functools.partial(lambda b, k: (b, k, 0), k=k))

    vec = pl.BlockSpec((HG_H, LANE), lambda b: (0, 0))
    st = pl.BlockSpec((None, HG_H, HG_DK, HG_DV), lambda b: (b, 0, 0, 0))
    return pl.pallas_call(
        _hgrn_step_body,
        grid=(bs,),
        in_specs=[grp(0), grp(1), grp(2), grp(3), vec, vec, st],
        out_specs=[pl.BlockSpec((None, HG_H, LANE), lambda b: (b, 0, 0)), st],
        out_shape=[jax.ShapeDtypeStruct((bs, HG_H, LANE), F32),
                   jax.ShapeDtypeStruct((bs, HG_H, HG_DK, HG_DV), F32)],
        compiler_params=_cparams("parallel"),
        name="hgrn_sample",
    )(z_a3, z_a3, z_a3, z_a3, lb.reshape(HG_H, LANE), norm_g.reshape(HG_H, LANE), s0)


def _conv_body(u_ref, b_ref, c_ref, g_ref, w_ref, prev_ref, o_ref, last_ref, carry):
    @pl.when(pl.program_id(0) == 0)
    def _():
        carry[...] = prev_ref[...]

    v = c_ref[...] * u_ref[...]
    tm = v.shape[0]
    row = lax.broadcasted_iota(jnp.int32, v.shape, 0)
    p1 = carry[7:8, :]
    p2 = carry[6:7, :]
    v1 = jnp.where(row == 0, p1, pltpu.roll(v, 1, 0))
    v2 = jnp.where(row == 0, p2, jnp.where(row == 1, p1, pltpu.roll(v, 2, 0)))
    w = w_ref[...]
    y = w[0:1, :] * v2 + w[1:2, :] * v1 + w[2:3, :] * v
    o_ref[...] = (b_ref[...] * y * _silu(g_ref[...])).astype(o_ref.dtype)
    tail = v[tm - 8:tm, :]
    carry[...] = tail
    last_ref[...] = tail


def conv_prompt(z_a, w, prev):
    t = z_a.shape[0]
    tm = _pick(t, (256, 128, 64, 32, 16, 8))

    def col(k):
        return pl.BlockSpec((tm, BR_W), functools.partial(lambda i, k: (i, 4 + k), k=k))

    w8 = jnp.zeros((8, BR_W), F32).at[:CV_K].set(w.astype(F32))
    prev8 = jnp.zeros((8, BR_W), F32).at[8 - (CV_K - 1):].set(prev.astype(F32))
    full8 = pl.BlockSpec((8, BR_W), lambda i: (0, 0))
    o, last = pl.pallas_call(
        _conv_body,
        grid=(t // tm,),
        in_specs=[col(0), col(1), col(2), col(3), full8, full8],
        out_specs=[pl.BlockSpec((tm, BR_W), lambda i: (i, 0)), full8],
        out_shape=[jax.ShapeDtypeStruct((t, BR_W), BF16), jax.ShapeDtypeStruct((8, BR_W), F32)],
        scratch_shapes=[pltpu.VMEM((8, BR_W), F32)],
        compiler_params=_cparams("arbitrary"),
        name="conv_prompt",
    )(z_a, z_a, z_a, z_a, w8, prev8)
    return o, last[8 - (CV_K - 1):]


def _conv_step_body(u_ref, b_ref, c_ref, g_ref, w_ref, p0_ref, p1_ref, o_ref, v_ref):
    v = c_ref[...] * u_ref[...]
    w = w_ref[...]
    y = w[0:1, :] * p0_ref[...] + w[1:2, :] * p1_ref[...] + w[2:3, :] * v
    o_ref[...] = b_ref[...] * y * _silu(g_ref[...])
    v_ref[...] = v


def conv_sample(z_a, w, prev):
    bs = z_a.shape[0]

    def col(k):
        return pl.BlockSpec((bs, BR_W), functools.partial(lambda i, k: (0, 4 + k), k=k))

    w8 = jnp.zeros((8, BR_W), F32).at[:CV_K].set(w.astype(F32))
    full = pl.BlockSpec((bs, BR_W), lambda i: (0, 0))
    o, v = pl.pallas_call(
        _conv_step_body,
        grid=(1,),
        in_specs=[col(0), col(1), col(2), col(3), pl.BlockSpec((8, BR_W), lambda i: (0, 0)), full, full],
        out_specs=[full, full],
        out_shape=[jax.ShapeDtypeStruct((bs, BR_W), F32), jax.ShapeDtypeStruct((bs, BR_W), F32)],
        compiler_params=_cparams("arbitrary"),
        name="conv_sample",
    )(z_a, z_a, z_a, z_a, w8, prev[:, 0], prev[:, 1])
    return o, jnp.stack([prev[:, 1], v], axis=1)


def _rope_tables(pos):
    half = ROT_DIM // 2
    inv = ROPE_THETA ** (-2.0 * jnp.arange(half, dtype=F32) / ROT_DIM)
    ang = pos.astype(F32)[:, None] * inv[None, :]
    cos, sin = jnp.cos(ang), jnp.sin(ang)
    m = pos.shape[0]
    ones = jnp.ones((m, NS_DH - ROT_DIM), F32)
    zeros = jnp.zeros((m, NS_DH - ROT_DIM), F32)
    zh = jnp.zeros((m, half), F32)
    cos_t = jnp.concatenate([cos, cos, ones], axis=1)
    sin_a = jnp.concatenate([zh, sin, zeros], axis=1)
    sin_b = jnp.concatenate([-sin, zh, zeros], axis=1)
    return cos_t, sin_a, sin_b


def _rope(x, cos_t, sin_a, sin_b):
    n = x.shape[1] // NS_DH
    half = ROT_DIM // 2

    def tile(a):
        return a if n == 1 else jnp.concatenate([a] * n, axis=1)

    return (x * tile(cos_t) + pltpu.roll(x, half, 1) * tile(sin_a)
            + pltpu.roll(x, x.shape[1] - half, 1) * tile(sin_b))


def _nsa_prep_body(q_ref, cv_ref, ks_ref, vs_ref, kw_ref, vw_ref, cos_ref, sa_ref, sb_ref,
                   qc_ref, qr_ref, rows_ref, win_ref, *attn_refs):
    cos_t, sin_a, sin_b = cos_ref[...], sa_ref[...], sb_ref[...]
    scale = NS_DH ** -0.5 * LOG2_E
    q = q_ref[...]
    qc_ref[...] = (q * scale).astype(qc_ref.dtype)
    qr_ref[...] = (_rope(q, cos_t, sin_a, sin_b) * scale).astype(qr_ref.dtype)
    kvw = NS_KV * NS_DH
    ks = _rope(ks_ref[...], cos_t, sin_a, sin_b)
    kw = _rope(kw_ref[...], cos_t, sin_a, sin_b)
    vs = vs_ref[...]
    vw = vw_ref[...]
    rows_ref[:, 0:2 * kvw] = cv_ref[...]
    rows_ref[:, 2 * kvw:3 * kvw] = ks
    rows_ref[:, 3 * kvw:4 * kvw] = vs
    win_ref[:, 0:kvw] = kw
    win_ref[:, kvw:2 * kvw] = vw
    if attn_refs:
        ksx_ref, kwb_ref, vst_ref, vwt_ref = attn_refs
        tm = q.shape[0]
        tok = pl.program_id(0) * tm + lax.broadcasted_iota(jnp.int32, (tm, LANE), 0)
        onehot = jnp.where(tok // SEL_BLOCK == lax.broadcasted_iota(jnp.int32, (tm, LANE), 1),
                           1.0, 0.0).astype(ksx_ref.dtype)
        ones_rows = jnp.where(lax.broadcasted_iota(jnp.int32, (V_EXT_ROWS - NS_DH, tm), 0) == 0,
                              1.0, 0.0).astype(vst_ref.dtype)
        for g in range(NS_KV):
            sl = slice(g * NS_DH, (g + 1) * NS_DH)
            ksx_ref[g, :, 0:NS_DH] = ks[:, sl].astype(ksx_ref.dtype)
            ksx_ref[g, :, NS_DH:NS_DH + LANE] = onehot
            kwb_ref[g] = kw[:, sl].astype(kwb_ref.dtype)
            vst_ref[g, 0:NS_DH, :] = vs[:, sl].T.astype(vst_ref.dtype)
            vst_ref[g, NS_DH:V_EXT_ROWS, :] = ones_rows
            vwt_ref[g] = vw[:, sl].T.astype(vwt_ref.dtype)


def nsa_prep(z_a, pos, q_dtype, attn_layouts=False):
    m = z_a.shape[0]
    tm = _pick(m, (256, 128, 64, 32, 16, 8))
    kvw = NS_KV * NS_DH
    c0 = 8 * BR_W // kvw
    tabs = _rope_tables(pos)

    def col(k, width=kvw):
        return pl.BlockSpec((tm, width), lambda i: (i, k))

    tab = pl.BlockSpec((tm, NS_DH), lambda i: (i, 0))
    out_specs = [pl.BlockSpec((tm, BR_W), lambda i: (i, 0)), pl.BlockSpec((tm, BR_W), lambda i: (i, 0)),
                 pl.BlockSpec((tm, 4 * kvw), lambda i: (i, 0)), pl.BlockSpec((tm, 2 * kvw), lambda i: (i, 0))]
    out_shape = [jax.ShapeDtypeStruct((m, BR_W), q_dtype), jax.ShapeDtypeStruct((m, BR_W), q_dtype),
                 jax.ShapeDtypeStruct((m, 4 * kvw), F32), jax.ShapeDtypeStruct((m, 2 * kvw), F32)]
    if attn_layouts:
        out_specs += [pl.BlockSpec((NS_KV, tm, NS_DH + LANE), lambda i: (0, i, 0)),
                      pl.BlockSpec((NS_KV, tm, NS_DH), lambda i: (0, i, 0)),
                      pl.BlockSpec((NS_KV, V_EXT_ROWS, tm), lambda i: (0, 0, i)),
                      pl.BlockSpec((NS_KV, NS_DH, tm), lambda i: (0, 0, i))]
        out_shape += [jax.ShapeDtypeStruct((NS_KV, m, NS_DH + LANE), BF16),
                      jax.ShapeDtypeStruct((NS_KV, m, NS_DH), BF16),
                      jax.ShapeDtypeStruct((NS_KV, V_EXT_ROWS, m), BF16),
                      jax.ShapeDtypeStruct((NS_KV, NS_DH, m), BF16)]
    return pl.pallas_call(
        _nsa_prep_body,
        grid=(m // tm,),
        in_specs=[col(8, BR_W), col((c0 + 4) // 2, 2 * kvw), col(c0 + 6), col(c0 + 7), col(c0 + 8), col(c0 + 9),
                  tab, tab, tab],
        out_specs=out_specs,
        out_shape=out_shape,
        compiler_params=_cparams("parallel"),
        name="nsa_prep",
    )(z_a, z_a, z_a, z_a, z_a, z_a, *tabs)


def _cmp_finish_body(nch, p_ref, pe_ref, w2_ref, o_ref):
    p = p_ref[...]
    n = p.shape[0]
    nxt = pltpu.roll(p[:, NS_DH:], n - 1, 0)
    hid = _silu(p[:, :NS_DH] + nxt + pe_ref[0:1, :])
    out = _dot(hid.astype(BF16), w2_ref[...])
    row = lax.broadcasted_iota(jnp.int32, out.shape, 0)
    o_ref[...] = jnp.where(row % nch < nch - 1, out, 0.0)


def _cmp_weights(w1):
    kdim = CMP_STRIDE * NS_DH
    w1b = w1.astype(BF16)
    return jnp.concatenate([w1b[:, :kdim], w1b[:, kdim:]], axis=2)


def _partial_rows_body(x_ref, w_ref, o_ref):
    kv = pl.program_id(0) // NS_KV
    nch = o_ref.shape[0]
    acc = jnp.zeros(o_ref.shape, F32)
    for s in range(CMP_STRIDE):
        acc += _dot(x_ref[pl.ds(s, nch, stride=CMP_STRIDE), :].astype(BF16), w_ref[kv, s])
    o_ref[...] = acc


def compress_partial_rows(z_a, w1):
    t = z_a.shape[0]
    nch = t // CMP_STRIDE
    col0 = (8 * BR_W + BR_W) // NS_DH
    w = _cmp_weights(w1).reshape(2, CMP_STRIDE, NS_DH, 2 * NS_DH)
    out = pl.pallas_call(
        _partial_rows_body,
        grid=(2 * NS_KV,),
        in_specs=[pl.BlockSpec((t, NS_DH), lambda c: (0, col0 + c)), pl.BlockSpec(w.shape, lambda c: (0, 0, 0, 0))],
        out_specs=pl.BlockSpec((None, nch, 2 * NS_DH), lambda c: (c, 0, 0)),
        out_shape=jax.ShapeDtypeStruct((2 * NS_KV, nch, 2 * NS_DH), F32),
        compiler_params=_cparams("parallel"),
        name="cmp_partial_rows",
    )(z_a, w)
    return out.reshape(2, NS_KV * nch, 2 * NS_DH)


def compress_pe_term(pe, w1):
    w1b = w1.astype(BF16)
    pe8 = jnp.broadcast_to(pe.reshape(2, 1, CMP_LEN * NS_DH), (2, 8, CMP_LEN * NS_DH)).astype(BF16)
    return jnp.stack([matmul(pe8[a], w1b[a], name="cmp_pe_term") for a in range(2)])


def compress_finish(p, pe_term, w2, nch):
    groups = p.shape[1] // nch
    gps = _pick(groups, (8, 4, 2, 1))
    out = pl.pallas_call(
        functools.partial(_cmp_finish_body, nch),
        grid=(2, groups // gps),
        in_specs=[pl.BlockSpec((None, gps * nch, 2 * NS_DH), lambda a, g: (a, g, 0)),
                  pl.BlockSpec((None, 8, NS_DH), lambda a, g: (a, 0, 0)),
                  pl.BlockSpec((None, NS_DH, NS_DH), lambda a, g: (a, 0, 0))],
        out_specs=pl.BlockSpec((None, gps * nch, NS_DH), lambda a, g: (a, g, 0)),
        out_shape=jax.ShapeDtypeStruct((2, groups * nch, NS_DH), F32),
        compiler_params=_cparams("parallel", "parallel"),
        name="cmp_finish",
    )(p, pe_term, w2.astype(BF16))
    return out.reshape(2, groups, nch, NS_DH)


def _cmp_to_sel(n_cmp_pad, n_sel_pad, n_cmp, n_sel):
    cs = np.arange(n_cmp_pad)[:, None] * CMP_STRIDE
    ss = np.arange(n_sel_pad)[None, :] * SEL_BLOCK
    ov = np.minimum(cs + CMP_LEN, ss + SEL_BLOCK) - np.maximum(cs, ss)
    m = np.clip(ov, 0, None).astype(np.float32) / CMP_LEN
    m[n_cmp:, :] = 0.0
    m[:, n_sel:] = 0.0
    return m


def _stack_heads(q):
    return jnp.concatenate([q[:, r * NS_DH:(r + 1) * NS_DH] for r in range(NS_G)], axis=0)


def _tile_heads(a):
    return jnp.concatenate([a] * NS_G, axis=1)


def _store_heads_t(o_ref, o_t):
    tq = o_t.shape[1] // NS_G
    for r in range(NS_G):
        o_ref[:, r * NS_DH:(r + 1) * NS_DH] = o_t[:, r * tq:(r + 1) * tq].T


def _cmp_sel_body(n_cmp, q_ref, kc_ref, vct_ref, msel_ref, o_ref, bias_ref):
    gw = NS_G * NS_DH
    for g in range(NS_KV):
        _cmp_sel_group(n_cmp, q_ref.at[:, g * gw:(g + 1) * gw], kc_ref.at[g], vct_ref.at[g], msel_ref,
                       o_ref.at[:, g * gw:(g + 1) * gw], bias_ref.at[g])


def _cmp_sel_group(n_cmp, q_ref, kc_ref, vct_ref, msel_ref, o_ref, bias_ref):
    tq = q_ref.shape[0]
    t0 = pl.program_id(0) * tq
    qs = _stack_heads(q_ref[...])
    s_t = _dot_nt(kc_ref[...], qs)
    ncp = s_t.shape[0]
    n_i = lax.broadcasted_iota(jnp.int32, (ncp, tq), 0)
    qpos = t0 + lax.broadcasted_iota(jnp.int32, (ncp, tq), 1)
    valid = _tile_heads((n_i * CMP_STRIDE + CMP_LEN - 1 <= qpos) & (n_i < n_cmp))
    sm = jnp.where(valid, s_t, NEG_INF)
    mx = jnp.max(sm, axis=0, keepdims=True)
    e = jnp.where(valid, jnp.exp2(sm - mx), 0.0)
    den = jnp.sum(e, axis=0, keepdims=True)
    p = e * (1.0 / jnp.where(den > 0.0, den, 1.0))
    _store_heads_t(o_ref, _dot(vct_ref[...], p.astype(BF16)))

    psum = p[:, 0:tq]
    for r in range(1, NS_G):
        psum += p[:, r * tq:(r + 1) * tq]
    hi, mid, lo = _split3(psum)
    i3 = _dot(msel_ref[...], jnp.concatenate([hi, mid, lo], axis=1))
    imp = i3[:, :tq] + i3[:, tq:2 * tq] + i3[:, 2 * tq:]
    nb = imp.shape[0]
    j_i = lax.broadcasted_iota(jnp.int32, (nb, tq), 0)
    cur = (t0 + lax.broadcasted_iota(jnp.int32, (nb, tq), 1)) // SEL_BLOCK
    forced = (j_i == 0) | (j_i == cur) | (j_i == cur - 1)
    imp = jnp.where(forced, imp + FORCE_BONUS, imp)
    imp = jnp.where(j_i <= cur, imp, NEG_INF)
    j_f = j_i.astype(F32)
    sel = jnp.zeros((nb, tq), F32)
    for _ in range(min(SEL_TOPN, nb)):
        mx = jnp.max(imp, axis=0, keepdims=True)
        first = jnp.min(jnp.where(imp == mx, j_f, float(nb)), axis=0, keepdims=True)
        hit = j_f == first
        sel = jnp.where(hit, 1.0, sel)
        imp = jnp.where(hit, REMOVED, imp)
    bias_t = jnp.where((sel > 0.5) & (j_i <= cur), 0.0, NEG_INF)
    if nb < LANE:
        bias_t = jnp.concatenate([bias_t, jnp.full((LANE - nb, tq), NEG_INF, F32)], axis=0)
    bias_ref[...] = bias_t.T.astype(bias_ref.dtype)


def nsa_cmp_select(q_c, kcmp, vcmp_t, n_cmp):
    t = q_c.shape[0]
    ncp = kcmp.shape[1]
    n_sel = t // SEL_BLOCK
    assert n_sel <= LANE
    msel_t = jnp.asarray(_cmp_to_sel(ncp, n_sel, n_cmp, n_sel).T, BF16)
    return pl.pallas_call(
        functools.partial(_cmp_sel_body, n_cmp),
        grid=(t // Q_TILE,),
        in_specs=[pl.BlockSpec((Q_TILE, BR_W), lambda i: (i, 0)),
                  pl.BlockSpec((NS_KV, ncp, NS_DH), lambda i: (0, 0, 0)),
                  pl.BlockSpec((NS_KV, NS_DH, ncp), lambda i: (0, 0, 0)),
                  pl.BlockSpec((n_sel, ncp), lambda i: (0, 0))],
        out_specs=[pl.BlockSpec((Q_TILE, BR_W), lambda i: (i, 0)),
                   pl.BlockSpec((NS_KV, Q_TILE, LANE), lambda i: (0, i, 0))],
        out_shape=[jax.ShapeDtypeStruct((t, BR_W), F32), jax.ShapeDtypeStruct((NS_KV, t, LANE), BF16)],
        compiler_params=_cparams("parallel"),
        name="nsa_cmp_select",
    )(q_c, kcmp, vcmp_t, msel_t)


V_EXT_ROWS = NS_DH + 16


def _sel_attn_body(q_ref, bias_ref, k_ref, vt_ref, o_ref, m_scr, acc_scr, s_scr):
    tq = q_ref.shape[0]
    t0 = pl.program_id(0) * tq
    gw = NS_G * NS_DH
    qx = [jnp.concatenate([_stack_heads(q_ref[:, g * gw:(g + 1) * gw]),
                           jnp.concatenate([bias_ref[g]] * NS_G, axis=0)], axis=1) for g in range(NS_KV)]
    m_scr[...] = jnp.full_like(m_scr, NEG_INF)
    acc_scr[...] = jnp.zeros_like(acc_scr)
    n_tiles = (t0 + tq + K_TILE - 1) // K_TILE

    def scores(kt, slot):
        k0 = pl.multiple_of(kt * K_TILE, K_TILE)
        for g in range(NS_KV):
            s_scr[slot, g] = _dot_nt(k_ref[g, pl.ds(k0, K_TILE), :], qx[g])

    def consume(kt, slot, causal):
        k0 = pl.multiple_of(kt * K_TILE, K_TILE)
        for g in range(NS_KV):
            s_m = s_scr[slot, g]
            if causal:
                key = k0 + lax.broadcasted_iota(jnp.int32, (K_TILE, tq), 0)
                qpos = t0 + lax.broadcasted_iota(jnp.int32, (K_TILE, tq), 1)
                s_m = s_m + _tile_heads(jnp.where(key <= qpos, 0.0, NEG_INF))
            m_old = m_scr[g]
            m_new = jnp.maximum(m_old, jnp.max(s_m, axis=0, keepdims=True))
            alpha = jnp.exp2(m_old - m_new)
            p = jnp.exp2(s_m - m_new)
            acc_scr[g] = alpha * acc_scr[g] + _dot(vt_ref[g, :, pl.ds(k0, K_TILE)], p.astype(BF16))
            m_scr[g] = m_new

    scores(0, 0)
    n_before = n_tiles - 1

    def pair(i, carry):
        kt = 2 * i
        scores(kt + 1, 1)
        consume(kt, 0, False)
        scores(kt + 2, 0)
        consume(kt + 1, 1, False)
        return carry

    lax.fori_loop(0, n_before // 2, pair, 0)
    odd = n_before % 2

    @pl.when(odd == 1)
    def _():
        scores(n_tiles - 1, 1)
        consume(n_tiles - 2, 0, False)

    consume(n_tiles - 1, odd, True)
    for g in range(NS_KV):
        acc = acc_scr[g]
        _store_heads_t(o_ref.at[:, g * gw:(g + 1) * gw], acc[0:NS_DH] * (1.0 / acc[NS_DH:NS_DH + 1]))


def nsa_sel_attn(q_r, bias, ks_ext, vs_ext):
    t = q_r.shape[0]
    gw = NS_G * NS_DH
    return pl.pallas_call(
        _sel_attn_body,
        grid=(t // Q_TILE,),
        in_specs=[pl.BlockSpec((Q_TILE, NS_KV * gw), lambda i: (i, 0)),
                  pl.BlockSpec((NS_KV, Q_TILE, LANE), lambda i: (0, i, 0)),
                  pl.BlockSpec((NS_KV, t, NS_DH + LANE), lambda i: (0, 0, 0)),
                  pl.BlockSpec((NS_KV, V_EXT_ROWS, t), lambda i: (0, 0, 0))],
        out_specs=pl.BlockSpec((Q_TILE, NS_KV * gw), lambda i: (i, 0)),
        out_shape=jax.ShapeDtypeStruct((t, BR_W), F32),
        scratch_shapes=[pltpu.VMEM((NS_KV, 1, NS_G * Q_TILE), F32),
                        pltpu.VMEM((NS_KV, V_EXT_ROWS, NS_G * Q_TILE), F32),
                        pltpu.VMEM((2, NS_KV, K_TILE, NS_G * Q_TILE), F32)],
        compiler_params=_cparams("parallel"),
        name="nsa_sel_attn",
    )(q_r, bias, ks_ext, vs_ext)


def _win_attn_body(span, q_ref, k_ref, vt_ref, o_ref):
    tq = q_ref.shape[0]
    t0 = pl.program_id(0) * tq
    gw = NS_G * NS_DH
    start = pl.multiple_of(jnp.maximum(t0 - WINDOW, 0), Q_TILE)
    key = start + lax.broadcasted_iota(jnp.int32, (span, tq), 0)
    qpos = t0 + lax.broadcasted_iota(jnp.int32, (span, tq), 1)
    bias = _tile_heads(jnp.where((key <= qpos) & (key > qpos - WINDOW), 0.0, NEG_INF))
    for g in range(NS_KV):
        qs = _stack_heads(q_ref[:, g * gw:(g + 1) * gw])
        sm = _dot_nt(k_ref[g, pl.ds(start, span), :], qs) + bias
        mx = jnp.max(sm, axis=0, keepdims=True)
        e = jnp.exp2(sm - mx)
        den = jnp.sum(e, axis=0, keepdims=True)
        p = e * (1.0 / den)
        _store_heads_t(o_ref.at[:, g * gw:(g + 1) * gw], _dot(vt_ref[g, :, pl.ds(start, span)], p.astype(BF16)))


def nsa_win_attn(q_r, kw, vw_t):
    t = q_r.shape[0]
    span = min(WINDOW + Q_TILE, t)
    return pl.pallas_call(
        functools.partial(_win_attn_body, span),
        grid=(t // Q_TILE,),
        in_specs=[pl.BlockSpec((Q_TILE, BR_W), lambda i: (i, 0)),
                  pl.BlockSpec((NS_KV, t, NS_DH), lambda i: (0, 0, 0)),
                  pl.BlockSpec((NS_KV, NS_DH, t), lambda i: (0, 0, 0))],
        out_specs=pl.BlockSpec((Q_TILE, BR_W), lambda i: (i, 0)),
        out_shape=jax.ShapeDtypeStruct((t, BR_W), F32),
        compiler_params=_cparams("parallel"),
        name="nsa_win_attn",
    )(q_r, kw, vw_t)


def _gate_expand():
    e = np.zeros((3, LANE, BR_W), np.float32)
    for h in range(NS_H):
        for c in range(3):
            e[c, h * 3 + c, h * NS_DH:(h + 1) * NS_DH] = 1.0
    return jnp.asarray(e, BF16)


def _combine_body(oc_ref, os_ref, ow_ref, gate_ref, g_ref, e_ref, o_ref):
    gs = jax.nn.sigmoid(gate_ref[...])
    hi = gs.astype(BF16)
    lo = (gs - hi.astype(F32)).astype(BF16)

    def expand(c):
        return _dot(hi, e_ref[c]) + _dot(lo, e_ref[c])

    o = expand(0) * oc_ref[...] + expand(1) * os_ref[...] + expand(2) * ow_ref[...]
    o_ref[...] = (o * _silu(g_ref[...])).astype(o_ref.dtype)


def nsa_combine(o_cmp, o_sel, o_win, z_g, z_b):
    m = o_cmp.shape[0]
    tm = _pick(m, (256, 128, 64, 32, 16))
    row = pl.BlockSpec((tm, BR_W), lambda i: (i, 0))
    return pl.pallas_call(
        _combine_body,
        grid=(m // tm,),
        in_specs=[row, row, row, pl.BlockSpec((tm, LANE), lambda i: (i, 0)), row,
                  pl.BlockSpec((3, LANE, BR_W), lambda i: (0, 0, 0))],
        out_specs=row,
        out_shape=jax.ShapeDtypeStruct((m, BR_W), BF16),
        compiler_params=_cparams("parallel"),
        name="nsa_combine",
    )(o_cmp, o_sel, o_win, z_g, z_b, _gate_expand())


def _mem_attn_body(q_ref, g_ref, kv_ref, o_ref):
    scale = MEM_DH ** -0.5
    q = q_ref[...]
    for h in range(MEM_H):
        k = kv_ref[:, h * MEM_DH:(h + 1) * MEM_DH].astype(BF16)
        v = kv_ref[:, (MEM_H + h) * MEM_DH:(MEM_H + h + 1) * MEM_DH].astype(BF16)
        s = _dot_nt(q[:, h * MEM_DH:(h + 1) * MEM_DH].astype(BF16), k) * scale
        e = jnp.exp(s - jnp.max(s, axis=-1, keepdims=True))
        p = e * (1.0 / jnp.sum(e, axis=-1, keepdims=True))
        o = _dot(p.astype(BF16), v)
        sl = slice(h * MEM_DH, (h + 1) * MEM_DH)
        o_ref[:, sl] = (o * _silu(g_ref[:, sl])).astype(o_ref.dtype)


def mem_attn_prompt(z_b, mkv):
    t = z_b.shape[0]
    tq = _pick(t, (512, 256, 128, 64, 32, 16))
    return pl.pallas_call(
        _mem_attn_body,
        grid=(t // tq,),
        in_specs=[pl.BlockSpec((tq, BR_W), lambda i: (i, 1)), pl.BlockSpec((tq, BR_W), lambda i: (i, 2)),
                  pl.BlockSpec(mkv.shape, lambda i: (0, 0))],
        out_specs=pl.BlockSpec((tq, BR_W), lambda i: (i, 0)),
        out_shape=jax.ShapeDtypeStruct((t, BR_W), BF16),
        compiler_params=_cparams("parallel"),
        name="mem_attn_prompt",
    )(z_b, z_b, mkv)


def _mem_attn_step_body(q_ref, g_ref, kv_ref, o_ref):
    scale = MEM_DH ** -0.5
    q = q_ref[...]
    rows_per = MEM_DH // LANE
    out_rows = []
    for h in range(MEM_H):
        qh = jnp.concatenate([q[h * rows_per + i:h * rows_per + i + 1, :] for i in range(rows_per)], axis=1)
        qh = jnp.broadcast_to(qh, (8, MEM_DH)).astype(BF16)
        k = kv_ref[:, h * MEM_DH:(h + 1) * MEM_DH].astype(BF16)
        v = kv_ref[:, (MEM_H + h) * MEM_DH:(MEM_H + h + 1) * MEM_DH].astype(BF16)
        s = _dot_nt(qh, k) * scale
        e = jnp.exp(s - jnp.max(s, axis=-1, keepdims=True))
        p = e * (1.0 / jnp.sum(e, axis=-1, keepdims=True))
        o = _dot(p.astype(BF16), v)[0:1, :]
        out_rows += [o[:, i * LANE:(i + 1) * LANE] for i in range(rows_per)]
    o_ref[...] = jnp.concatenate(out_rows, axis=0) * _silu(g_ref[...])


def mem_attn_sample(z_b3, kv_cache, layer):
    bs = z_b3.shape[0]
    mem_len, width = kv_cache.shape[2], kv_cache.shape[3]
    return pl.pallas_call(
        _mem_attn_step_body,
        grid=(bs,),
        in_specs=[pl.BlockSpec((None, 8, LANE), lambda b: (b, 1, 0)), pl.BlockSpec((None, 8, LANE), lambda b: (b, 2, 0)),
                  pl.BlockSpec((None, None, mem_len, width), lambda b: (layer, b, 0, 0))],
        out_specs=pl.BlockSpec((None, 8, LANE), lambda b: (b, 0, 0)),
        out_shape=jax.ShapeDtypeStruct((bs, 8, LANE), F32),
        compiler_params=_cparams("parallel"),
        name="mem_attn_sample",
    )(z_b3, z_b3, kv_cache)


def _paged_partial_body(npg, pt_ref, *refs):
    del pt_ref
    page_refs, w_ref, o_ref = refs[:npg], refs[npg], refs[npg + 1]
    chunks = PAGE_SIZE // CMP_STRIDE
    o_ref[...] = jnp.zeros_like(o_ref)

    unroll = 4

    def step(i, carry):
        for c in range(2 * NS_KV):
            acc = o_ref[c]
            for u in range(unroll):
                s = i * unroll + u
                xs = jnp.concatenate(
                    [r[pl.ds(s * ROWS_PER_TOKEN + c, chunks, stride=CMP_STRIDE * ROWS_PER_TOKEN), :]
                     for r in page_refs], axis=0)
                acc += _dot(xs.astype(BF16), w_ref[c // NS_KV, s])
            o_ref[c] = acc
        return carry

    lax.fori_loop(0, CMP_STRIDE // unroll, step, 0)


def compress_partial_paged(cache2, page_table, w1, layer, n_pool):
    bs, n_pages = page_table.shape
    npg = min(PAGES_PER_STEP, n_pages)
    assert n_pages % npg == 0
    chunks = PAGE_SIZE // CMP_STRIDE
    w = _cmp_weights(w1).reshape(2, CMP_STRIDE, NS_DH, 2 * NS_DH)

    def page_spec(i):
        return pl.BlockSpec((PAGE_SIZE * ROWS_PER_TOKEN, NS_DH),
                            lambda b, j, pt: (layer * n_pool + pt[b, j * npg + i], 0))

    grid_spec = pltpu.PrefetchScalarGridSpec(
        num_scalar_prefetch=1,
        grid=(bs, n_pages // npg),
        in_specs=[page_spec(i) for i in range(npg)] + [pl.BlockSpec(w.shape, lambda b, j, pt: (0, 0, 0, 0))],
        out_specs=pl.BlockSpec((2 * NS_KV, None, npg * chunks, 2 * NS_DH), lambda b, j, pt: (0, b, j, 0)),
    )
    out = pl.pallas_call(
        functools.partial(_paged_partial_body, npg),
        grid_spec=grid_spec,
        out_shape=jax.ShapeDtypeStruct((2 * NS_KV, bs, n_pages * chunks, 2 * NS_DH), F32),
        compiler_params=_cparams("parallel", "parallel"),
        name="cmp_partial_paged",
    )(page_table, *([cache2] * npg), w)
    return out.reshape(2, NS_KV * bs * n_pages * chunks, 2 * NS_DH)


def _row_group(shape):
    return lax.broadcasted_iota(jnp.int32, shape, 0) // NS_G


def _cmp_step_body(n_cmp, qpos, q_ref, kc_ref, vc_ref, o_ref, ps_ref):
    q = q_ref[...].astype(BF16)
    ncp = kc_ref.shape[1]
    s = [_dot_nt(q, kc_ref[g].astype(BF16)) for g in range(NS_KV)]
    grp = _row_group((NS_H, ncp))
    s = jnp.where(grp == 0, s[0], s[1])
    n_i = lax.broadcasted_iota(jnp.int32, (NS_H, ncp), 1)
    valid = (n_i * CMP_STRIDE + CMP_LEN - 1 <= qpos) & (n_i < n_cmp)
    sm = jnp.where(valid, s, NEG_INF)
    mx = jnp.max(sm, axis=-1, keepdims=True)
    e = jnp.where(valid, jnp.exp2(sm - mx), 0.0)
    den = jnp.sum(e, axis=-1, keepdims=True)
    p = e * (1.0 / jnp.where(den > 0.0, den, 1.0))
    pb = p.astype(BF16)
    o = [_dot(pb, vc_ref[g].astype(BF16)) for g in range(NS_KV)]
    o_ref[...] = jnp.where(_row_group((NS_H, NS_DH)) == 0, o[0], o[1])
    ps_ref[...] = jnp.concatenate(
        [jnp.sum(jnp.where(grp == g, p, 0.0), axis=0, keepdims=True) for g in range(NS_KV)], axis=0)


def nsa_cmp_sample(q_c3, kcmp, vcmp, n_cmp, qpos):
    bs = q_c3.shape[0]
    ncp = kcmp.shape[2]
    kv = pl.BlockSpec((NS_KV, None, ncp, NS_DH), lambda b: (0, b, 0, 0))
    return pl.pallas_call(
        functools.partial(_cmp_step_body, n_cmp, qpos),
        grid=(bs,),
        in_specs=[pl.BlockSpec((None, NS_H, NS_DH), lambda b: (b, 0, 0)), kv, kv],
        out_specs=[pl.BlockSpec((None, NS_H, NS_DH), lambda b: (b, 0, 0)),
                   pl.BlockSpec((None, NS_KV, ncp), lambda b: (b, 0, 0))],
        out_shape=[jax.ShapeDtypeStruct((bs, NS_H, NS_DH), F32), jax.ShapeDtypeStruct((bs, NS_KV, ncp), F32)],
        compiler_params=_cparams("parallel"),
        name="nsa_cmp_sample",
    )(q_c3, kcmp, vcmp)


def _select_body(n_sel, qpos, ps_ref, msel_ref, idx_ref):
    hi, mid, lo = _split3(ps_ref[...])
    imp = _dot(hi, msel_ref[...]) + _dot(mid, msel_ref[...]) + _dot(lo, msel_ref[...])
    j_i = lax.broadcasted_iota(jnp.int32, imp.shape, 1)
    cur = qpos // SEL_BLOCK
    forced = (j_i == 0) | (j_i == cur) | (j_i == cur - 1)
    imp = jnp.where(forced, imp + FORCE_BONUS, imp)
    imp = jnp.where(j_i <= cur, imp, NEG_INF)
    imp = jnp.where(j_i < n_sel, imp, REMOVED)
    j_f = j_i.astype(F32)
    col = lax.broadcasted_iota(jnp.int32, idx_ref.shape, 1)
    out = jnp.zeros(idx_ref.shape, F32)
    for it in range(min(SEL_TOPN, n_sel)):
        mx = jnp.max(imp, axis=-1, keepdims=True)
        first = jnp.min(jnp.where(imp == mx, j_f, float(imp.shape[1])), axis=-1, keepdims=True)
        out = jnp.where(col == it, first, out)
        imp = jnp.where(j_f == first, REMOVED, imp)
    idx_ref[...] = out.astype(jnp.int32)


def nsa_select_sample(psum2, n_cmp, n_sel, qpos):
    rows, ncp = psum2.shape
    nsp = -(-n_sel // LANE) * LANE
    msel = jnp.asarray(_cmp_to_sel(ncp, nsp, n_cmp, n_sel), BF16)
    return pl.pallas_call(
        functools.partial(_select_body, n_sel, qpos),
        grid=(1,),
        in_specs=[pl.BlockSpec((rows, ncp), lambda i: (0, 0)), pl.BlockSpec((ncp, nsp), lambda i: (0, 0))],
        out_specs=pl.BlockSpec((rows, LANE), lambda i: (0, 0)),
        out_shape=jax.ShapeDtypeStruct((rows, LANE), jnp.int32),
        compiler_params=_cparams("arbitrary"),
        name="nsa_select_sample",
    )(psum2, msel)


def _sel_step_body(n_past_blocks, qpos, pt_ref, top_ref, q_ref, *refs):
    del pt_ref
    nblk = NS_KV * SEL_PER_STEP
    blk_refs, new_ref, o_ref, m_scr, l_scr, acc_scr = refs[:nblk], *refs[nblk:nblk + 5]
    b = pl.program_id(0)
    j = pl.program_id(1)

    @pl.when(j == 0)
    def _():
        m_scr[...] = jnp.full_like(m_scr, NEG_INF)
        l_scr[...] = jnp.zeros_like(l_scr)
        acc_scr[...] = jnp.zeros_like(acc_scr)

    def slab(ref, slot, g):
        return ref[pl.ds(slot * NS_KV + g, SEL_BLOCK, stride=ROWS_PER_TOKEN), :].astype(BF16)

    q = q_ref[...]
    qb = q.astype(BF16)
    nk = SEL_PER_STEP * SEL_BLOCK
    grp_k = _row_group((NS_H, nk))
    grp_d = _row_group((NS_H, NS_DH))
    lane_blk = lax.broadcasted_iota(jnp.int32, (NS_H, nk), 1) // SEL_BLOCK
    s_g, idx_g, new_g = [], [], []
    for g in range(NS_KV):
        k_all = jnp.concatenate([slab(blk_refs[g * SEL_PER_STEP + k], 2, g) for k in range(SEL_PER_STEP)], axis=0)
        s_g.append(_dot_nt(qb, k_all))
        ids = [top_ref[b, g * SEL_TOPN + j * SEL_PER_STEP + k] for k in range(SEL_PER_STEP)]
        idx = jnp.zeros((NS_H, nk), jnp.int32)
        n_new = jnp.int32(0)
        for k, i in enumerate(ids):
            idx = jnp.where(lane_blk == k, i, idx)
            n_new = n_new + (i == n_past_blocks).astype(jnp.int32)
        idx_g.append(idx)
        new_g.append(n_new)
    s = jnp.where(grp_k == 0, s_g[0], s_g[1])
    idx = jnp.where(grp_k == 0, idx_g[0], idx_g[1])
    tok = idx * SEL_BLOCK + lax.broadcasted_iota(jnp.int32, (NS_H, nk), 1) % SEL_BLOCK
    valid = (idx < n_past_blocks) & (tok <= qpos)
    ks_new = jnp.where(grp_d == 0, new_ref[4:5, :], new_ref[5:6, :])
    vs_new = jnp.where(grp_d == 0, new_ref[6:7, :], new_ref[7:8, :])
    is_new = jnp.where(grp_d == 0, new_g[0], new_g[1]) > 0
    s_new = jnp.where(is_new, jnp.sum(q * ks_new, axis=-1, keepdims=True), NEG_INF)
    m_old = m_scr[...]
    m_new = jnp.maximum(jnp.maximum(m_old, jnp.max(jnp.where(valid, s, NEG_INF), axis=-1, keepdims=True)), s_new)
    alpha = jnp.exp2(m_old - m_new)
    p = jnp.where(valid, jnp.exp2(s - m_new[:, 0:1]), 0.0)
    p_new = jnp.where(is_new, jnp.exp2(s_new - m_new), 0.0)
    pb = p.astype(BF16)
    pv_g = []
    for g in range(NS_KV):
        v_all = jnp.concatenate([slab(blk_refs[g * SEL_PER_STEP + k], 3, g) for k in range(SEL_PER_STEP)], axis=0)
        pv_g.append(_dot(pb, v_all))
    pv = jnp.where(grp_d == 0, pv_g[0], pv_g[1])
    l_scr[...] = alpha * l_scr[...] + jnp.sum(p, axis=-1, keepdims=True) + p_new
    acc_scr[...] = alpha * acc_scr[...] + pv + p_new * vs_new
    m_scr[...] = m_new

    @pl.when(j == pl.num_programs(1) - 1)
    def _():
        o_ref[...] = acc_scr[...] / l_scr[...]


def nsa_sel_sample(q_r3, rows3, cache2, page_table, top_idx, layer, n_pool, qpos):
    bs, n_pages = page_table.shape
    halves = PAGE_SIZE // SEL_BLOCK
    n_past_blocks = n_pages * halves

    def blk(g, k):
        def imap(b, j, pt, top):
            i = top[b, g * SEL_TOPN + j * SEL_PER_STEP + k]
            page = pt[b, jnp.minimum(i // halves, n_pages - 1)]
            return ((layer * n_pool + page) * halves + i % halves, 0)
        return pl.BlockSpec((SEL_BLOCK * ROWS_PER_TOKEN, NS_DH), imap)

    vec = pl.BlockSpec((None, NS_H, NS_DH), lambda b, j, pt, top: (b, 0, 0))
    grid_spec = pltpu.PrefetchScalarGridSpec(
        num_scalar_prefetch=2,
        grid=(bs, SEL_TOPN // SEL_PER_STEP),
        in_specs=[vec] + [blk(g, k) for g in range(NS_KV) for k in range(SEL_PER_STEP)] + [vec],
        out_specs=vec,
        scratch_shapes=[pltpu.VMEM((NS_H, NS_DH), F32)] * 3,
    )
    return pl.pallas_call(
        functools.partial(_sel_step_body, n_past_blocks, qpos),
        grid_spec=grid_spec,
        out_shape=jax.ShapeDtypeStruct((bs, NS_H, NS_DH), F32),
        compiler_params=_cparams("parallel", "arbitrary"),
        name="nsa_sel_sample",
    )(page_table, top_idx, q_r3, *([cache2] * (NS_KV * SEL_PER_STEP)), rows3)


def _win_step_body(pos0, qpos, q_ref, buf_ref, new_ref, o_ref):
    q = q_ref[...]
    qb = q.astype(BF16)
    rows_per = 2 * NS_KV
    wb = buf_ref.shape[0] // rows_per

    def slab(c):
        return buf_ref[pl.ds(c, wb, stride=rows_per), :].astype(BF16)

    grp_k = _row_group((NS_H, wb))
    grp_d = _row_group((NS_H, NS_DH))
    s = jnp.where(grp_k == 0, _dot_nt(qb, slab(0)), _dot_nt(qb, slab(1)))
    kpos = pos0 + lax.broadcasted_iota(jnp.int32, (NS_H, wb), 1)
    valid = (kpos <= qpos) & (kpos > qpos - WINDOW) & (kpos >= 0)
    kw_new = jnp.where(grp_d == 0, new_ref[0:1, :], new_ref[1:2, :])
    vw_new = jnp.where(grp_d == 0, new_ref[2:3, :], new_ref[3:4, :])
    s_new = jnp.sum(q * kw_new, axis=-1, keepdims=True)
    mx = jnp.maximum(jnp.max(jnp.where(valid, s, NEG_INF), axis=-1, keepdims=True), s_new)
    p = jnp.where(valid, jnp.exp2(s - mx), 0.0)
    p_new = jnp.exp2(s_new - mx)
    den = jnp.sum(p, axis=-1, keepdims=True) + p_new
    pb = p.astype(BF16)
    pv = jnp.where(grp_d == 0, _dot(pb, slab(2)), _dot(pb, slab(3)))
    o_ref[...] = (pv + p_new * vw_new) / den


def nsa_win_sample(q_r3, win_buf2, win3, layer, wb, past_len, qpos):
    bs = q_r3.shape[0]
    return pl.pallas_call(
        functools.partial(_win_step_body, past_len - wb, qpos),
        grid=(bs,),
        in_specs=[pl.BlockSpec((None, NS_H, NS_DH), lambda b: (b, 0, 0)),
                  pl.BlockSpec((wb * 2 * NS_KV, NS_DH), lambda b: (layer * bs + b, 0)),
                  pl.BlockSpec((None, 2 * NS_KV, NS_DH), lambda b: (b, 0, 0))],
        out_specs=pl.BlockSpec((None, NS_H, NS_DH), lambda b: (b, 0, 0)),
        out_shape=jax.ShapeDtypeStruct((bs, NS_H, NS_DH), F32),
        compiler_params=_cparams("parallel"),
        name="nsa_win_sample",
    )(q_r3, win_buf2, win3)


def _project(h, w_a, w_g, w_b):
    return (matmul(h, w_a, name="in_proj_a"), matmul(h, w_g, name="in_proj_gate"), matmul(h, w_b, name="in_proj_b"))


def _merge_out(x, branches, z_b, w_up_b, w_out_b, next_g, last):
    d = x.shape[1]
    gated = merge_up(branches, z_b, w_up_b, d)
    return out_proj_norm(gated, w_out_b, x, next_g, F32 if last else BF16, keep_x=not last)


def kernel(x_prompt, x_sample, mem_prompt, cache_nsa_kv, state_win_kv, state_hgrn, state_conv, cache_mem_kv,
           page_table, norm_g, final_norm_g, w_in, hg_lb_logits, hg_norm_g, cv_w, ns_pe, ns_cw1, ns_cw2,
           mem_norm_g, w_mem_kv, w_up, w_out):
    depth = w_in.shape[0]
    b_p, t, d = x_prompt.shape
    bs, ts = x_sample.shape[:2]
    assert b_p == 1 and ts == 1
    n_pages = page_table.shape[1]
    past_len = n_pages * PAGE_SIZE
    wb = state_win_kv.shape[2]
    mem_len = mem_prompt.shape[1]
    kvw = NS_KV * NS_DH

    s_lb = jax.nn.softmax(hg_lb_logits.astype(F32), axis=0)
    lower = jnp.cumsum(s_lb, axis=0) - s_lb[0]

    pos_p = jnp.arange(t, dtype=jnp.int32)
    qpos_s = past_len
    pos_s = jnp.full((bs,), qpos_s, jnp.int32)

    n_pool = cache_nsa_kv.shape[1]
    cache2 = cache_nsa_kv.reshape(depth * n_pool * PAGE_SIZE * ROWS_PER_TOKEN, NS_DH)
    win_buf2 = state_win_kv.reshape(depth * bs * wb * 2 * NS_KV, NS_DH)
    mem_cache4 = cache_mem_kv.reshape(depth, bs, mem_len, 2 * MEM_H * MEM_DH)

    nch_p = t // CMP_STRIDE
    nch_s = (past_len + 1) // CMP_STRIDE
    n_sel_s = -(-(past_len + 1) // SEL_BLOCK)

    xp = x_prompt.reshape(t, d)
    xs = x_sample.reshape(bs, d)
    rows_p, win_p, hg_p, cv_p, mkv_p = [], [], [], [], []
    rows_s, win_s, hg_s, cv_s = [], [], [], []
    hp = rmsnorm_rows(xp, norm_g[0], BF16)
    hs = rmsnorm_rows(xs, norm_g[0], BF16)
    for l in range(depth):
        last = l == depth - 1
        next_g = final_norm_g if last else norm_g[l + 1]
        w_l = w_in[l]
        w_a = w_l[:, :COL_A].astype(BF16)
        w_g = jnp.pad(w_l[:, COL_A:COL_B0], ((0, 0), (0, LANE - COL_GATE))).astype(BF16)
        w_b = w_l[:, COL_B0:].astype(BF16)
        w_up_b = w_up[l].astype(BF16)
        w_out_b = w_out[l].astype(BF16)

        z_a, z_g, z_b = _project(hp, w_a, w_g, w_b)
        o_hg, st_hg = hgrn_prompt(z_a, lower[l], hg_norm_g[l])
        o_cv, st_cv = conv_prompt(z_a, cv_w[l], jnp.zeros((CV_K - 1, BR_W), F32))
        q_c, q_r, rows, win, ks_ext, kw, vs_ext, vw_t = nsa_prep(z_a, pos_p, BF16, attn_layouts=True)
        pe_term = compress_pe_term(ns_pe[l], ns_cw1[l])
        cmp = compress_finish(compress_partial_rows(z_a, ns_cw1[l]), pe_term, ns_cw2[l], nch_p)
        kcmp = cmp[0].astype(BF16)
        vcmp_t = jnp.swapaxes(cmp[1], 1, 2).astype(BF16)
        o_cmp, sel_bias = nsa_cmp_select(q_c, kcmp, vcmp_t, nch_p - 1)
        o_sel = nsa_sel_attn(q_r, sel_bias, ks_ext, vs_ext)
        o_win = nsa_win_attn(q_r, kw, vw_t)
        o_ns = nsa_combine(o_cmp, o_sel, o_win, z_g, z_b)
        mem_h = rmsnorm_rows(mem_prompt.reshape(mem_len, d), mem_norm_g[l], BF16)
        mkv = matmul(mem_h, w_mem_kv[l].astype(BF16), name="mem_kv")
        o_mm = mem_attn_prompt(z_b, mkv)
        xp, hp = _merge_out(xp, (o_hg, o_cv, o_ns, o_mm), z_b, w_up_b, w_out_b, next_g, last)
        rows_p.append(rows)
        win_p.append(win[t - min(WINDOW, t):])
        hg_p.append(st_hg.reshape(1, HG_H, HG_DK, HG_DV))
        cv_p.append(st_cv.reshape(1, CV_K - 1, BR_W))
        mkv_p.append(mkv.reshape(1, mem_len, 2, MEM_H, MEM_DH))

        z_a, z_g, z_b = _project(hs, w_a, w_g, w_b)
        z_a3 = z_a.reshape(bs, COL_A // LANE, LANE)
        z_b3 = z_b.reshape(bs, z_b.shape[1] // LANE, LANE)
        o_hg, st_hg = hgrn_sample(z_a3, lower[l], hg_norm_g[l], state_hgrn[l])
        o_cv, st_cv = conv_sample(z_a, cv_w[l], state_conv[l])
        q_c, q_r, rows, win = nsa_prep(z_a, pos_s, F32)
        part = compress_partial_paged(cache2, page_table, ns_cw1[l], l, n_pool)
        cmp = compress_finish(part, pe_term, ns_cw2[l], nch_s)
        cmp = cmp.reshape(2, NS_KV, bs, nch_s, NS_DH)
        q_c3 = q_c.reshape(bs, NS_H, NS_DH)
        q_r3 = q_r.reshape(bs, NS_H, NS_DH)
        o_cmp, psum = nsa_cmp_sample(q_c3, cmp[0], cmp[1], nch_s - 1, qpos_s)
        top = nsa_select_sample(psum.reshape(bs * NS_KV, nch_s), nch_s - 1, n_sel_s, qpos_s)
        top = top[:, :SEL_TOPN].reshape(bs, NS_KV * SEL_TOPN)
        o_sel = nsa_sel_sample(q_r3, rows.reshape(bs, 4 * NS_KV, NS_DH), cache2, page_table, top, l, n_pool, qpos_s)
        o_win = nsa_win_sample(q_r3, win_buf2, win.reshape(bs, 2 * NS_KV, NS_DH), l, wb, past_len, qpos_s)
        o_ns = nsa_combine(o_cmp.reshape(bs, BR_W), o_sel.reshape(bs, BR_W), o_win.reshape(bs, BR_W), z_g, z_b)
        o_mm = mem_attn_sample(z_b3, mem_cache4, l)
        branches = (o_hg.reshape(bs, BR_W).astype(BF16), o_cv.astype(BF16), o_ns, o_mm.reshape(bs, BR_W).astype(BF16))
        xs, hs = _merge_out(xs, branches, z_b, w_up_b, w_out_b, next_g, last)
        rows_s.append(rows)
        win_s.append(win)
        hg_s.append(st_hg)
        cv_s.append(st_cv)

    y_prompt = hp.reshape(1, t, d)
    y_sample = hs.reshape(bs, 1, d)
    wlen = min(WINDOW, t)
    new_rows_p = jnp.stack(rows_p).reshape(depth, 1, t, 4, NS_KV, NS_DH)
    new_win_p = jnp.stack(win_p).reshape(depth, 1, wlen, 2, NS_KV, NS_DH)
    new_rows_s = jnp.stack(rows_s).reshape(depth, bs, 1, 4, NS_KV, NS_DH)
    new_win_s = jnp.concatenate(
        [state_win_kv[:, :, 1:], jnp.stack(win_s).reshape(depth, bs, 1, 2, NS_KV, NS_DH)], axis=2)
    return (y_prompt, y_sample, new_rows_p, new_win_p, jnp.stack(hg_p), jnp.stack(cv_p),
            jnp.stack(mkv_p), new_rows_s, new_win_s, jnp.stack(hg_s), jnp.stack(cv_s))
```

```python
import functools

import numpy as np
import jax
import jax.numpy as jnp
from jax import lax
from jax.experimental import pallas as pl
from jax.experimental.pallas import tpu as pltpu

F32 = jnp.float32
BF16 = jnp.bfloat16

BR_W = 1024
N_BRANCH = 4
HG_H = 8
HG_DK = 128
HG_DV = 128
F_MIN = 1e-30
CV_K = 3
NS_H = 8
NS_KV = 2
NS_G = NS_H // NS_KV
NS_DH = 128
CMP_LEN = 32
CMP_STRIDE = 16
SEL_BLOCK = 64
SEL_TOPN = 16
WINDOW = 512
FORCE_BONUS = 100.0
ROPE_THETA = 500000.0
ROT_DIM = NS_DH // 4
MEM_H = 4
MEM_DH = 256
NORM_EPS = 1e-6
NEG_INF = -1e30
REMOVED = -3e38
LOG2_E = 1.4426950408889634
PAGE_SIZE = 128
ROWS_PER_TOKEN = 4 * NS_KV

COL_A = 10752
COL_GATE = 24
COL_B0 = COL_A + COL_GATE

LANE = 128
HG_CHUNK = 128
HG_HEADS_PER_STEP = 4
Q_TILE = 128
K_TILE = 512
PAGES_PER_STEP = 16
SEL_PER_STEP = 4
VMEM_LIMIT = 56 * 1024 * 1024


def _cparams(*sem):
    return pltpu.CompilerParams(dimension_semantics=sem, vmem_limit_bytes=VMEM_LIMIT)


def _pick(n, cands):
    for c in cands:
        if n % c == 0:
            return c
    return n


def _silu(x):
    return x * jax.nn.sigmoid(x)


def _dot(a, b):
    return jnp.dot(a, b, preferred_element_type=F32)


def _dot_nt(a, b):
    return lax.dot_general(a, b, (((1,), (1,)), ((), ())), preferred_element_type=F32)


def _split3(x):
    hi = x.astype(BF16)
    r1 = x - hi.astype(F32)
    mid = r1.astype(BF16)
    lo = (r1 - mid.astype(F32)).astype(BF16)
    return hi, mid, lo


def _norm_body(x_ref, g_ref, o_ref):
    x = x_ref[...]
    ms = jnp.mean(x * x, axis=-1, keepdims=True)
    o_ref[...] = (x * lax.rsqrt(ms + NORM_EPS) * g_ref[...]).astype(o_ref.dtype)


def rmsnorm_rows(x, g, out_dtype):
    m, d = x.shape
    tm = _pick(m, (256, 128, 64, 32, 16, 8))
    return pl.pallas_call(
        _norm_body,
        grid=(m // tm,),
        in_specs=[pl.BlockSpec((tm, d), lambda i: (i, 0)), pl.BlockSpec((1, d), lambda i: (0, 0))],
        out_specs=pl.BlockSpec((tm, d), lambda i: (i, 0)),
        out_shape=jax.ShapeDtypeStruct((m, d), out_dtype),
        compiler_params=_cparams("parallel"),
        name="rmsnorm",
    )(x, g.reshape(1, d).astype(F32))


def _mm_body(a_ref, b_ref, o_ref):
    o_ref[...] = _dot(a_ref[...].astype(BF16), b_ref[...]).astype(o_ref.dtype)


def _mm_res_body(a_ref, b_ref, r_ref, o_ref):
    o_ref[...] = r_ref[...] + _dot(a_ref[...].astype(BF16), b_ref[...])


def matmul(a, b, res=None, out_dtype=F32, name="matmul"):
    m, k = a.shape
    n = b.shape[1]
    tm = _pick(m, (1024, 512, 256, 128, 64, 32, 16, 8))
    tn = _pick(n, (1536, 1024, 512, 256, 128))
    in_specs = [pl.BlockSpec((tm, k), lambda i, j: (i, 0)), pl.BlockSpec((k, tn), lambda i, j: (0, j))]
    args = [a, b]
    body = _mm_body
    if res is not None:
        in_specs.append(pl.BlockSpec((tm, tn), lambda i, j: (i, j)))
        args.append(res)
        body = _mm_res_body
    return pl.pallas_call(
        body,
        grid=(m // tm, n // tn),
        in_specs=in_specs,
        out_specs=pl.BlockSpec((tm, tn), lambda i, j: (i, j)),
        out_shape=jax.ShapeDtypeStruct((m, n), out_dtype),
        compiler_params=_cparams("parallel", "parallel"),
        name=name,
    )(*args)


def _out_norm_body(keep_x, a_ref, b_ref, r_ref, g_ref, *out_refs):
    x = r_ref[...] + _dot(a_ref[...], b_ref[...])
    if keep_x:
        out_refs[0][...] = x
    ms = jnp.mean(x * x, axis=-1, keepdims=True)
    h_ref = out_refs[-1]
    h_ref[...] = (x * lax.rsqrt(ms + NORM_EPS) * g_ref[...]).astype(h_ref.dtype)


def out_proj_norm(a, b, res, g, h_dtype, keep_x):
    m, k = a.shape
    n = b.shape[1]
    tm = _pick(m, (512, 256, 128, 64, 32, 16, 8))
    row = pl.BlockSpec((tm, n), lambda i: (i, 0))
    out_specs = [row, row] if keep_x else [row]
    out_shape = [jax.ShapeDtypeStruct((m, n), h_dtype)]
    if keep_x:
        out_shape = [jax.ShapeDtypeStruct((m, n), F32)] + out_shape
    outs = pl.pallas_call(
        functools.partial(_out_norm_body, keep_x),
        grid=(m // tm,),
        in_specs=[pl.BlockSpec((tm, k), lambda i: (i, 0)), pl.BlockSpec((k, n), lambda i: (0, 0)), row,
                  pl.BlockSpec((1, n), lambda i: (0, 0))],
        out_specs=out_specs,
        out_shape=out_shape,
        compiler_params=_cparams("parallel"),
        name="out_proj_norm",
    )(a, b, res, g.reshape(1, n).astype(F32))
    return outs if keep_x else (None, outs[0])


def _merge_body(b0, b1, b2, b3, g0, g1, g2, g3, w0, w1, w2, w3, o_ref):
    acc = jax.nn.sigmoid(g0[...]) * _dot(b0[...], w0[...])
    acc += jax.nn.sigmoid(g1[...]) * _dot(b1[...], w1[...])
    acc += jax.nn.sigmoid(g2[...]) * _dot(b2[...], w2[...])
    acc += jax.nn.sigmoid(g3[...]) * _dot(b3[...], w3[...])
    o_ref[...] = acc.astype(o_ref.dtype)


def merge_up(branches, z_b, w_up_bf16, d_model):
    m = branches[0].shape[0]
    tm = _pick(m, (512, 256, 128, 64, 32, 16))
    tn = 512
    nj = d_model // tn
    gate_col0 = (z_b.shape[1] - N_BRANCH * d_model) // tn
    br_specs = [pl.BlockSpec((tm, BR_W), lambda i, j: (i, 0)) for _ in range(N_BRANCH)]
    g_specs = [pl.BlockSpec((tm, tn), functools.partial(lambda i, j, n: (i, gate_col0 + n * nj + j), n=n))
               for n in range(N_BRANCH)]
    w_specs = [pl.BlockSpec((None, BR_W, tn), functools.partial(lambda i, j, n: (n, 0, j), n=n))
               for n in range(N_BRANCH)]
    return pl.pallas_call(
        _merge_body,
        grid=(m // tm, nj),
        in_specs=br_specs + g_specs + w_specs,
        out_specs=pl.BlockSpec((tm, tn), lambda i, j: (i, j)),
        out_shape=jax.ShapeDtypeStruct((m, d_model), BF16),
        compiler_params=_cparams("parallel", "parallel"),
        name="merge_up",
    )(*branches, z_b, z_b, z_b, z_b, w_up_bf16, w_up_bf16, w_up_bf16, w_up_bf16)


def _hgrn_consts(c):
    nlev = int(round(np.log2(c)))
    t = np.arange(c)[:, None]
    r = np.arange(c)[None, :]
    blocks = [r <= t]
    masks = [t == r]
    for lv in range(nlev):
        h = c >> (lv + 1)
        mid = (t // (2 * h)) * (2 * h) + h
        blocks.append(np.where(t >= mid, (r >= mid) & (r <= t), (r > t) & (r <= mid - 1)))
        same = (t // (2 * h)) == (r // (2 * h))
        masks.append(same & (t % (2 * h) >= h) & (r % (2 * h) < h))
    blocks.append(r > t)
    l_all = np.concatenate(blocks, axis=0).astype(np.float32)
    return jnp.asarray(l_all, BF16), jnp.asarray(np.stack(masks).astype(np.float32)), nlev


def _hgrn_body(nlev, q_ref, z_ref, v_ref, g_ref, lb_ref, ng_ref, l_ref, mask_ref, o_ref, s_ref, st_scr):
    c_idx = pl.program_id(1)
    c = q_ref.shape[0]

    @pl.when(c_idx == 0)
    def _():
        st_scr[...] = jnp.zeros_like(st_scr)

    for hh in range(HG_HEADS_PER_STEP):
        sl = slice(hh * HG_DK, (hh + 1) * HG_DK)
        lb = lb_ref[:, sl]
        q = _silu(q_ref[:, sl])
        z = z_ref[:, sl]
        f = lb + (1.0 - lb) * jax.nn.sigmoid(z)
        logf = jnp.log(jnp.maximum(f, F_MIN))
        k = (1.0 - lb) * jax.nn.sigmoid(-z)
        v = v_ref[:, sl]

        hi, mid, _ = _split3(logf)
        e2 = _dot(l_ref[...], jnp.concatenate([hi, mid], axis=1))
        dk = HG_DK
        x = jnp.exp(e2[:, :dk] + e2[:, dk:])
        eb = x[0:c]
        est = x[(nlev + 1) * c:(nlev + 2) * c]

        st = st_scr[hh]
        inter = _dot_nt((q * eb).astype(BF16), st.astype(BF16))
        att = mask_ref[0] * _dot_nt(q.astype(BF16), k.astype(BF16))
        for lv in range(nlev):
            fac = x[(1 + lv) * c:(2 + lv) * c]
            att += mask_ref[1 + lv] * _dot_nt((q * fac).astype(BF16), (k * fac).astype(BF16))
        o = inter + _dot(att.astype(BF16), v.astype(BF16))
        o = o * lax.rsqrt(jnp.mean(o * o, axis=-1, keepdims=True) + NORM_EPS) * ng_ref[:, sl]
        o_ref[:, sl] = (o * _silu(g_ref[:, sl])).astype(o_ref.dtype)

        st_scr[hh] = st * eb[c - 1:c, :] + _dot(v.T.astype(BF16), (k * est).astype(BF16))

    @pl.when(c_idx == pl.num_programs(1) - 1)
    def _():
        for hh in range(HG_HEADS_PER_STEP):
            s_ref[hh] = st_scr[hh].T


def hgrn_prompt(z_a, lb, norm_g):
    t = z_a.shape[0]
    c = HG_CHUNK
    l_all, masks, nlev = _hgrn_consts(c)
    hp = HG_HEADS_PER_STEP
    steps = HG_H // hp
    width = hp * HG_DK

    def col(k):
        return pl.BlockSpec((c, width), functools.partial(lambda h, i, k: (i, k * steps + h), k=k))

    vec = pl.BlockSpec((None, 1, width), lambda h, i: (h, 0, 0))
    return pl.pallas_call(
        functools.partial(_hgrn_body, nlev),
        grid=(steps, t // c),
        in_specs=[col(0), col(1), col(2), col(3), vec, vec,
                  pl.BlockSpec(l_all.shape, lambda h, i: (0, 0)),
                  pl.BlockSpec(masks.shape, lambda h, i: (0, 0, 0))],
        out_specs=[pl.BlockSpec((c, width), lambda h, i: (i, h)),
                   pl.BlockSpec((hp, HG_DK, HG_DV), lambda h, i: (h, 0, 0))],
        out_shape=[jax.ShapeDtypeStruct((t, BR_W), BF16),
                   jax.ShapeDtypeStruct((HG_H, HG_DK, HG_DV), F32)],
        scratch_shapes=[pltpu.VMEM((hp, HG_DV, HG_DK), F32)],
        compiler_params=_cparams("parallel", "arbitrary"),
        name="hgrn_prompt",
    )(z_a, z_a, z_a, z_a, lb.reshape(steps, 1, width), norm_g.reshape(steps, 1, width), l_all, masks)


def _hgrn_step_body(q_ref, z_ref, v_ref, g_ref, lb_ref, ng_ref, s0_ref, o_ref, s_ref):
    lb = lb_ref[...]
    q = _silu(q_ref[...])
    z = z_ref[...]
    f = jnp.maximum(lb + (1.0 - lb) * jax.nn.sigmoid(z), F_MIN)
    k = (1.0 - lb) * jax.nn.sigmoid(-z)
    v = v_ref[...]
    rows = []
    for h in range(HG_H):
        def colb(a):
            return jnp.broadcast_to(a[h:h + 1, :], (HG_DK, HG_DK)).T
        s_new = colb(f) * s0_ref[h] + colb(k) * v[h:h + 1, :]
        s_ref[h] = s_new
        rows.append(jnp.sum(colb(q) * s_new, axis=0, keepdims=True))
    o = jnp.concatenate(rows, axis=0)
    o = o * lax.rsqrt(jnp.mean(o * o, axis=-1, keepdims=True) + NORM_EPS) * ng_ref[...]
    o_ref[...] = o * _silu(g_ref[...])


def hgrn_sample(z_a3, lb, norm_g, s0):
    bs = z_a3.shape[0]

    def grp(k):
        return pl.BlockSpec((None, HG_H, LANE), functools.partial(lambda b, k: (b, k, 0), k=k))

    vec = pl.BlockSpec((HG_H, LANE), lambda b: (0, 0))
    st = pl.BlockSpec((None, HG_H, HG_DK, HG_DV), lambda b: (b, 0, 0, 0))
    return pl.pallas_call(
        _hgrn_step_body,
        grid=(bs,),
        in_specs=[grp(0), grp(1), grp(2), grp(3), vec, vec, st],
        out_specs=[pl.BlockSpec((None, HG_H, LANE), lambda b: (b, 0, 0)), st],
        out_shape=[jax.ShapeDtypeStruct((bs, HG_H, LANE), F32),
                   jax.ShapeDtypeStruct((bs, HG_H, HG_DK, HG_DV), F32)],
        compiler_params=_cparams("parallel"),
        name="hgrn_sample",
    )(z_a3, z_a3, z_a3, z_a3, lb.reshape(HG_H, LANE), norm_g.reshape(HG_H, LANE), s0)


def _conv_body(u_ref, b_ref, c_ref, g_ref, w_ref, prev_ref, o_ref, last_ref, carry):
    @pl.when(pl.program_id(0) == 0)
    def _():
        carry[...] = prev_ref[...]

    v = c_ref[...] * u_ref[...]
    tm = v.shape[0]
    row = lax.broadcasted_iota(jnp.int32, v.shape, 0)
    p1 = carry[7:8, :]
    p2 = carry[6:7, :]
    v1 = jnp.where(row == 0, p1, pltpu.roll(v, 1, 0))
    v2 = jnp.where(row == 0, p2, jnp.where(row == 1, p1, pltpu.roll(v, 2, 0)))
    w = w_ref[...]
    y = w[0:1, :] * v2 + w[1:2, :] * v1 + w[2:3, :] * v
    o_ref[...] = (b_ref[...] * y * _silu(g_ref[...])).astype(o_ref.dtype)
    tail = v[tm - 8:tm, :]
    carry[...] = tail
    last_ref[...] = tail


def conv_prompt(z_a, w, prev):
    t = z_a.shape[0]
    tm = _pick(t, (256, 128, 64, 32, 16, 8))

    def col(k):
        return pl.BlockSpec((tm, BR_W), functools.partial(lambda i, k: (i, 4 + k), k=k))

    w8 = jnp.zeros((8, BR_W), F32).at[:CV_K].set(w.astype(F32))
    prev8 = jnp.zeros((8, BR_W), F32).at[8 - (CV_K - 1):].set(prev.astype(F32))
    full8 = pl.BlockSpec((8, BR_W), lambda i: (0, 0))
    o, last = pl.pallas_call(
        _conv_body,
        grid=(t // tm,),
        in_specs=[col(0), col(1), col(2), col(3), full8, full8],
        out_specs=[pl.BlockSpec((tm, BR_W), lambda i: (i, 0)), full8],
        out_shape=[jax.ShapeDtypeStruct((t, BR_W), BF16), jax.ShapeDtypeStruct((8, BR_W), F32)],
        scratch_shapes=[pltpu.VMEM((8, BR_W), F32)],
        compiler_params=_cparams("arbitrary"),
        name="conv_prompt",
    )(z_a, z_a, z_a, z_a, w8, prev8)
    return o, last[8 - (CV_K - 1):]


def _conv_step_body(u_ref, b_ref, c_ref, g_ref, w_ref, p0_ref, p1_ref, o_ref, v_ref):
    v = c_ref[...] * u_ref[...]
    w = w_ref[...]
    y = w[0:1, :] * p0_ref[...] + w[1:2, :] * p1_ref[...] + w[2:3, :] * v
    o_ref[...] = b_ref[...] * y * _silu(g_ref[...])
    v_ref[...] = v


def conv_sample(z_a, w, prev):
    bs = z_a.shape[0]

    def col(k):
        return pl.BlockSpec((bs, BR_W), functools.partial(lambda i, k: (0, 4 + k), k=k))

    w8 = jnp.zeros((8, BR_W), F32).at[:CV_K].set(w.astype(F32))
    full = pl.BlockSpec((bs, BR_W), lambda i: (0, 0))
    o, v = pl.pallas_call(
        _conv_step_body,
        grid=(1,),
        in_specs=[col(0), col(1), col(2), col(3), pl.BlockSpec((8, BR_W), lambda i: (0, 0)), full, full],
        out_specs=[full, full],
        out_shape=[jax.ShapeDtypeStruct((bs, BR_W), F32), jax.ShapeDtypeStruct((bs, BR_W), F32)],
        compiler_params=_cparams("arbitrary"),
        name="conv_sample",
    )(z_a, z_a, z_a, z_a, w8, prev[:, 0], prev[:, 1])
    return o, jnp.stack([prev[:, 1], v], axis=1)


def _rope_tables(pos):
    half = ROT_DIM // 2
    inv = ROPE_THETA ** (-2.0 * jnp.arange(half, dtype=F32) / ROT_DIM)
    ang = pos.astype(F32)[:, None] * inv[None, :]
    cos, sin = jnp.cos(ang), jnp.sin(ang)
    m = pos.shape[0]
    ones = jnp.ones((m, NS_DH - ROT_DIM), F32)
    zeros = jnp.zeros((m, NS_DH - ROT_DIM), F32)
    zh = jnp.zeros((m, half), F32)
    cos_t = jnp.concatenate([cos, cos, ones], axis=1)
    sin_a = jnp.concatenate([zh, sin, zeros], axis=1)
    sin_b = jnp.concatenate([-sin, zh, zeros], axis=1)
    return cos_t, sin_a, sin_b


def _rope(x, cos_t, sin_a, sin_b):
    n = x.shape[1] // NS_DH
    half = ROT_DIM // 2

    def tile(a):
        return a if n == 1 else jnp.concatenate([a] * n, axis=1)

    return (x * tile(cos_t) + pltpu.roll(x, half, 1) * tile(sin_a)
            + pltpu.roll(x, x.shape[1] - half, 1) * tile(sin_b))


def _nsa_prep_body(q_ref, cv_ref, ks_ref, vs_ref, kw_ref, vw_ref, cos_ref, sa_ref, sb_ref,
                   qc_ref, qr_ref, rows_ref, win_ref, *attn_refs):
    cos_t, sin_a, sin_b = cos_ref[...], sa_ref[...], sb_ref[...]
    scale = NS_DH ** -0.5 * LOG2_E
    q = q_ref[...]
    qc_ref[...] = (q * scale).astype(qc_ref.dtype)
    qr_ref[...] = (_rope(q, cos_t, sin_a, sin_b) * scale).astype(qr_ref.dtype)
    kvw = NS_KV * NS_DH
    ks = _rope(ks_ref[...], cos_t, sin_a, sin_b)
    kw = _rope(kw_ref[...], cos_t, sin_a, sin_b)
    vs = vs_ref[...]
    vw = vw_ref[...]
    rows_ref[:, 0:2 * kvw] = cv_ref[...]
    rows_ref[:, 2 * kvw:3 * kvw] = ks
    rows_ref[:, 3 * kvw:4 * kvw] = vs
    win_ref[:, 0:kvw] = kw
    win_ref[:, kvw:2 * kvw] = vw
    if attn_refs:
        ksx_ref, kwb_ref, vst_ref, vwt_ref = attn_refs
        tm = q.shape[0]
        tok = pl.program_id(0) * tm + lax.broadcasted_iota(jnp.int32, (tm, LANE), 0)
        onehot = jnp.where(tok // SEL_BLOCK == lax.broadcasted_iota(jnp.int32, (tm, LANE), 1),
                           1.0, 0.0).astype(ksx_ref.dtype)
        ones_rows = jnp.where(lax.broadcasted_iota(jnp.int32, (V_EXT_ROWS - NS_DH, tm), 0) == 0,
                              1.0, 0.0).astype(vst_ref.dtype)
        for g in range(NS_KV):
            sl = slice(g * NS_DH, (g + 1) * NS_DH)
            ksx_ref[g, :, 0:NS_DH] = ks[:, sl].astype(ksx_ref.dtype)
            ksx_ref[g, :, NS_DH:NS_DH + LANE] = onehot
            kwb_ref[g] = kw[:, sl].astype(kwb_ref.dtype)
            vst_ref[g, 0:NS_DH, :] = vs[:, sl].T.astype(vst_ref.dtype)
            vst_ref[g, NS_DH:V_EXT_ROWS, :] = ones_rows
            vwt_ref[g] = vw[:, sl].T.astype(vwt_ref.dtype)


def nsa_prep(z_a, pos, q_dtype, attn_layouts=False):
    m = z_a.shape[0]
    tm = _pick(m, (256, 128, 64, 32, 16, 8))
    kvw = NS_KV * NS_DH
    c0 = 8 * BR_W // kvw
    tabs = _rope_tables(pos)

    def col(k, width=kvw):
        return pl.BlockSpec((tm, width), lambda i: (i, k))

    tab = pl.BlockSpec((tm, NS_DH), lambda i: (i, 0))
    out_specs = [pl.BlockSpec((tm, BR_W), lambda i: (i, 0)), pl.BlockSpec((tm, BR_W), lambda i: (i, 0)),
                 pl.BlockSpec((tm, 4 * kvw), lambda i: (i, 0)), pl.BlockSpec((tm, 2 * kvw), lambda i: (i, 0))]
    out_shape = [jax.ShapeDtypeStruct((m, BR_W), q_dtype), jax.ShapeDtypeStruct((m, BR_W), q_dtype),
                 jax.ShapeDtypeStruct((m, 4 * kvw), F32), jax.ShapeDtypeStruct((m, 2 * kvw), F32)]
    if attn_layouts:
        out_specs += [pl.BlockSpec((NS_KV, tm, NS_DH + LANE), lambda i: (0, i, 0)),
                      pl.BlockSpec((NS_KV, tm, NS_DH), lambda i: (0, i, 0)),
                      pl.BlockSpec((NS_KV, V_EXT_ROWS, tm), lambda i: (0, 0, i)),
                      pl.BlockSpec((NS_KV, NS_DH, tm), lambda i: (0, 0, i))]
        out_shape += [jax.ShapeDtypeStruct((NS_KV, m, NS_DH + LANE), BF16),
                      jax.ShapeDtypeStruct((NS_KV, m, NS_DH), BF16),
                      jax.ShapeDtypeStruct((NS_KV, V_EXT_ROWS, m), BF16),
                      jax.ShapeDtypeStruct((NS_KV, NS_DH, m), BF16)]
    return pl.pallas_call(
        _nsa_prep_body,
        grid=(m // tm,),
        in_specs=[col(8, BR_W), col((c0 + 4) // 2, 2 * kvw), col(c0 + 6), col(c0 + 7), col(c0 + 8), col(c0 + 9),
                  tab, tab, tab],
        out_specs=out_specs,
        out_shape=out_shape,
        compiler_params=_cparams("parallel"),
        name="nsa_prep",
    )(z_a, z_a, z_a, z_a, z_a, z_a, *tabs)


def _cmp_finish_body(nch, p_ref, pe_ref, w2_ref, o_ref):
    p = p_ref[...]
    n = p.shape[0]
    nxt = pltpu.roll(p[:, NS_DH:], n - 1, 0)
    hid = _silu(p[:, :NS_DH] + nxt + pe_ref[0:1, :])
    out = _dot(hid.astype(BF16), w2_ref[...])
    row = lax.broadcasted_iota(jnp.int32, out.shape, 0)
    o_ref[...] = jnp.where(row % nch < nch - 1, out, 0.0)


def _cmp_weights(w1):
    kdim = CMP_STRIDE * NS_DH
    w1b = w1.astype(BF16)
    return jnp.concatenate([w1b[:, :kdim], w1b[:, kdim:]], axis=2)


def _partial_rows_body(x_ref, w_ref, o_ref):
    kv = pl.program_id(0) // NS_KV
    nch = o_ref.shape[0]
    acc = jnp.zeros(o_ref.shape, F32)
    for s in range(CMP_STRIDE):
        acc += _dot(x_ref[pl.ds(s, nch, stride=CMP_STRIDE), :].astype(BF16), w_ref[kv, s])
    o_ref[...] = acc


def compress_partial_rows(z_a, w1):
    t = z_a.shape[0]
    nch = t // CMP_STRIDE
    col0 = (8 * BR_W + BR_W) // NS_DH
    w = _cmp_weights(w1).reshape(2, CMP_STRIDE, NS_DH, 2 * NS_DH)
    out = pl.pallas_call(
        _partial_rows_body,
        grid=(2 * NS_KV,),
        in_specs=[pl.BlockSpec((t, NS_DH), lambda c: (0, col0 + c)), pl.BlockSpec(w.shape, lambda c: (0, 0, 0, 0))],
        out_specs=pl.BlockSpec((None, nch, 2 * NS_DH), lambda c: (c, 0, 0)),
        out_shape=jax.ShapeDtypeStruct((2 * NS_KV, nch, 2 * NS_DH), F32),
        compiler_params=_cparams("parallel"),
        name="cmp_partial_rows",
    )(z_a, w)
    return out.reshape(2, NS_KV * nch, 2 * NS_DH)


def compress_pe_term(pe, w1):
    w1b = w1.astype(BF16)
    pe8 = jnp.broadcast_to(pe.reshape(2, 1, CMP_LEN * NS_DH), (2, 8, CMP_LEN * NS_DH)).astype(BF16)
    return jnp.stack([matmul(pe8[a], w1b[a], name="cmp_pe_term") for a in range(2)])


def compress_finish(p, pe_term, w2, nch):
    groups = p.shape[1] // nch
    gps = _pick(groups, (8, 4, 2, 1))
    out = pl.pallas_call(
        functools.partial(_cmp_finish_body, nch),
        grid=(2, groups // gps),
        in_specs=[pl.BlockSpec((None, gps * nch, 2 * NS_DH), lambda a, g: (a, g, 0)),
                  pl.BlockSpec((None, 8, NS_DH), lambda a, g: (a, 0, 0)),
                  pl.BlockSpec((None, NS_DH, NS_DH), lambda a, g: (a, 0, 0))],
        out_specs=pl.BlockSpec((None, gps * nch, NS_DH), lambda a, g: (a, g, 0)),
        out_shape=jax.ShapeDtypeStruct((2, groups * nch, NS_DH), F32),
        compiler_params=_cparams("parallel", "parallel"),
        name="cmp_finish",
    )(p, pe_term, w2.astype(BF16))
    return out.reshape(2, groups, nch, NS_DH)


def _cmp_to_sel(n_cmp_pad, n_sel_pad, n_cmp, n_sel):
    cs = np.arange(n_cmp_pad)[:, None] * CMP_STRIDE
    ss = np.arange(n_sel_pad)[None, :] * SEL_BLOCK
    ov = np.minimum(cs + CMP_LEN, ss + SEL_BLOCK) - np.maximum(cs, ss)
    m = np.clip(ov, 0, None).astype(np.float32) / CMP_LEN
    m[n_cmp:, :] = 0.0
    m[:, n_sel:] = 0.0
    return m


def _stack_heads(q):
    return jnp.concatenate([q[:, r * NS_DH:(r + 1) * NS_DH] for r in range(NS_G)], axis=0)


def _tile_heads(a):
    return jnp.concatenate([a] * NS_G, axis=1)


def _store_heads_t(o_ref, o_t):
    tq = o_t.shape[1] // NS_G
    for r in range(NS_G):
        o_ref[:, r * NS_DH:(r + 1) * NS_DH] = o_t[:, r * tq:(r + 1) * tq].T


def _cmp_sel_body(n_cmp, q_ref, kc_ref, vct_ref, msel_ref, o_ref, bias_ref):
    gw = NS_G * NS_DH
    for g in range(NS_KV):
        _cmp_sel_group(n_cmp, q_ref.at[:, g * gw:(g + 1) * gw], kc_ref.at[g], vct_ref.at[g], msel_ref,
                       o_ref.at[:, g * gw:(g + 1) * gw], bias_ref.at[g])


def _cmp_sel_group(n_cmp, q_ref, kc_ref, vct_ref, msel_ref, o_ref, bias_ref):
    tq = q_ref.shape[0]
    t0 = pl.program_id(0) * tq
    qs = _stack_heads(q_ref[...])
    s_t = _dot_nt(kc_ref[...], qs)
    ncp = s_t.shape[0]
    n_i = lax.broadcasted_iota(jnp.int32, (ncp, tq), 0)
    qpos = t0 + lax.broadcasted_iota(jnp.int32, (ncp, tq), 1)
    valid = _tile_heads((n_i * CMP_STRIDE + CMP_LEN - 1 <= qpos) & (n_i < n_cmp))
    sm = jnp.where(valid, s_t, NEG_INF)
    mx = jnp.max(sm, axis=0, keepdims=True)
    e = jnp.exp2(sm - mx)
    den = jnp.sum(e, axis=0, keepdims=True)
    p = e * jnp.where(mx > 0.5 * NEG_INF, 1.0 / den, 0.0)
    _store_heads_t(o_ref, _dot(vct_ref[...], p.astype(BF16)))

    psum = p[:, 0:tq]
    for r in range(1, NS_G):
        psum += p[:, r * tq:(r + 1) * tq]
    hi, mid, lo = _split3(psum)
    i3 = _dot(msel_ref[...], jnp.concatenate([hi, mid, lo], axis=1))
    imp = i3[:, :tq] + i3[:, tq:2 * tq] + i3[:, 2 * tq:]
    nb = imp.shape[0]
    j_i = lax.broadcasted_iota(jnp.int32, (nb, tq), 0)
    cur = (t0 + lax.broadcasted_iota(jnp.int32, (nb, tq), 1)) // SEL_BLOCK
    forced = (j_i == 0) | (j_i == cur) | (j_i == cur - 1)
    imp = jnp.where(forced, imp + FORCE_BONUS, imp)
    imp = jnp.where(j_i <= cur, imp, NEG_INF)
    j_f = j_i.astype(F32)
    sel = jnp.zeros((nb, tq), F32)
    for _ in range(min(SEL_TOPN, nb)):
        mx = jnp.max(imp, axis=0, keepdims=True)
        first = jnp.min(jnp.where(imp == mx, j_f, float(nb)), axis=0, keepdims=True)
        hit = j_f == first
        sel = jnp.where(hit, 1.0, sel)
        imp = jnp.where(hit, REMOVED, imp)
    bias_t = jnp.where((sel > 0.5) & (j_i <= cur), 0.0, NEG_INF)
    if nb < LANE:
        bias_t = jnp.concatenate([bias_t, jnp.full((LANE - nb, tq), NEG_INF, F32)], axis=0)
    bias_ref[...] = bias_t.T.astype(bias_ref.dtype)


def nsa_cmp_select(q_c, kcmp, vcmp_t, n_cmp):
    t = q_c.shape[0]
    ncp = kcmp.shape[1]
    n_sel = t // SEL_BLOCK
    assert n_sel <= LANE
    msel_t = jnp.asarray(_cmp_to_sel(ncp, n_sel, n_cmp, n_sel).T, BF16)
    return pl.pallas_call(
        functools.partial(_cmp_sel_body, n_cmp),
        grid=(t // Q_TILE,),
        in_specs=[pl.BlockSpec((Q_TILE, BR_W), lambda i: (i, 0)),
                  pl.BlockSpec((NS_KV, ncp, NS_DH), lambda i: (0, 0, 0)),
                  pl.BlockSpec((NS_KV, NS_DH, ncp), lambda i: (0, 0, 0)),
                  pl.BlockSpec((n_sel, ncp), lambda i: (0, 0))],
        out_specs=[pl.BlockSpec((Q_TILE, BR_W), lambda i: (i, 0)),
                   pl.BlockSpec((NS_KV, Q_TILE, LANE), lambda i: (0, i, 0))],
        out_shape=[jax.ShapeDtypeStruct((t, BR_W), F32), jax.ShapeDtypeStruct((NS_KV, t, LANE), BF16)],
        compiler_params=_cparams("parallel"),
        name="nsa_cmp_select",
    )(q_c, kcmp, vcmp_t, msel_t)


V_EXT_ROWS = NS_DH + 16


def _sel_attn_body(q_ref, bias_ref, k_ref, vt_ref, o_ref, m_scr, acc_scr, s_scr):
    tq = q_ref.shape[0]
    t0 = pl.program_id(0) * tq
    gw = NS_G * NS_DH
    qx = [jnp.concatenate([_stack_heads(q_ref[:, g * gw:(g + 1) * gw]),
                           jnp.concatenate([bias_ref[g]] * NS_G, axis=0)], axis=1) for g in range(NS_KV)]
    m_scr[...] = jnp.full_like(m_scr, NEG_INF)
    acc_scr[...] = jnp.zeros_like(acc_scr)
    n_tiles = (t0 + tq + K_TILE - 1) // K_TILE

    def scores(kt, slot):
        k0 = pl.multiple_of(kt * K_TILE, K_TILE)
        for g in range(NS_KV):
            s_scr[slot, g] = _dot_nt(k_ref[g, pl.ds(k0, K_TILE), :], qx[g])

    def consume(kt, slot, causal):
        k0 = pl.multiple_of(kt * K_TILE, K_TILE)
        for g in range(NS_KV):
            s_m = s_scr[slot, g]
            if causal:
                key = k0 + lax.broadcasted_iota(jnp.int32, (K_TILE, tq), 0)
                qpos = t0 + lax.broadcasted_iota(jnp.int32, (K_TILE, tq), 1)
                s_m = s_m + _tile_heads(jnp.where(key <= qpos, 0.0, NEG_INF))
            m_old = m_scr[g]
            m_new = jnp.maximum(m_old, jnp.max(s_m, axis=0, keepdims=True))
            alpha = jnp.exp2(m_old - m_new)
            p = jnp.exp2(s_m - m_new)
            acc_scr[g] = alpha * acc_scr[g] + _dot(vt_ref[g, :, pl.ds(k0, K_TILE)], p.astype(BF16))
            m_scr[g] = m_new

    scores(0, 0)
    n_before = n_tiles - 1

    def pair(i, carry):
        kt = 2 * i
        scores(kt + 1, 1)
        consume(kt, 0, False)
        scores(kt + 2, 0)
        consume(kt + 1, 1, False)
        return carry

    lax.fori_loop(0, n_before // 2, pair, 0)
    odd = n_before % 2

    @pl.when(odd == 1)
    def _():
        scores(n_tiles - 1, 1)
        consume(n_tiles - 2, 0, False)

    consume(n_tiles - 1, odd, True)
    for g in range(NS_KV):
        acc = acc_scr[g]
        _store_heads_t(o_ref.at[:, g * gw:(g + 1) * gw], acc[0:NS_DH] * (1.0 / acc[NS_DH:NS_DH + 1]))


def nsa_sel_attn(q_r, bias, ks_ext, vs_ext):
    t = q_r.shape[0]
    gw = NS_G * NS_DH
    return pl.pallas_call(
        _sel_attn_body,
        grid=(t // Q_TILE,),
        in_specs=[pl.BlockSpec((Q_TILE, NS_KV * gw), lambda i: (i, 0)),
                  pl.BlockSpec((NS_KV, Q_TILE, LANE), lambda i: (0, i, 0)),
                  pl.BlockSpec((NS_KV, t, NS_DH + LANE), lambda i: (0, 0, 0)),
                  pl.BlockSpec((NS_KV, V_EXT_ROWS, t), lambda i: (0, 0, 0))],
        out_specs=pl.BlockSpec((Q_TILE, NS_KV * gw), lambda i: (i, 0)),
        out_shape=jax.ShapeDtypeStruct((t, BR_W), F32),
        scratch_shapes=[pltpu.VMEM((NS_KV, 1, NS_G * Q_TILE), F32),
                        pltpu.VMEM((NS_KV, V_EXT_ROWS, NS_G * Q_TILE), F32),
                        pltpu.VMEM((2, NS_KV, K_TILE, NS_G * Q_TILE), F32)],
        compiler_params=_cparams("parallel"),
        name="nsa_sel_attn",
    )(q_r, bias, ks_ext, vs_ext)


def _win_attn_body(span, q_ref, k_ref, vt_ref, o_ref):
    tq = q_ref.shape[0]
    t0 = pl.program_id(0) * tq
    gw = NS_G * NS_DH
    start = pl.multiple_of(jnp.maximum(t0 - WINDOW, 0), Q_TILE)
    key = start + lax.broadcasted_iota(jnp.int32, (span, tq), 0)
    qpos = t0 + lax.broadcasted_iota(jnp.int32, (span, tq), 1)
    bias = _tile_heads(jnp.where((key <= qpos) & (key > qpos - WINDOW), 0.0, NEG_INF))
    for g in range(NS_KV):
        qs = _stack_heads(q_ref[:, g * gw:(g + 1) * gw])
        sm = _dot_nt(k_ref[g, pl.ds(start, span), :], qs) + bias
        mx = jnp.max(sm, axis=0, keepdims=True)
        e = jnp.exp2(sm - mx)
        den = jnp.sum(e, axis=0, keepdims=True)
        p = e * (1.0 / den)
        _store_heads_t(o_ref.at[:, g * gw:(g + 1) * gw], _dot(vt_ref[g, :, pl.ds(start, span)], p.astype(BF16)))


def nsa_win_attn(q_r, kw, vw_t):
    t = q_r.shape[0]
    span = min(WINDOW + Q_TILE, t)
    return pl.pallas_call(
        functools.partial(_win_attn_body, span),
        grid=(t // Q_TILE,),
        in_specs=[pl.BlockSpec((Q_TILE, BR_W), lambda i: (i, 0)),
                  pl.BlockSpec((NS_KV, t, NS_DH), lambda i: (0, 0, 0)),
                  pl.BlockSpec((NS_KV, NS_DH, t), lambda i: (0, 0, 0))],
        out_specs=pl.BlockSpec((Q_TILE, BR_W), lambda i: (i, 0)),
        out_shape=jax.ShapeDtypeStruct((t, BR_W), F32),
        compiler_params=_cparams("parallel"),
        name="nsa_win_attn",
    )(q_r, kw, vw_t)


def _gate_expand():
    e = np.zeros((3, LANE, BR_W), np.float32)
    for h in range(NS_H):
        for c in range(3):
            e[c, h * 3 + c, h * NS_DH:(h + 1) * NS_DH] = 1.0
    return jnp.asarray(e, BF16)


def _combine_body(oc_ref, os_ref, ow_ref, gate_ref, g_ref, e_ref, o_ref):
    gs = jax.nn.sigmoid(gate_ref[...])
    hi = gs.astype(BF16)
    lo = (gs - hi.astype(F32)).astype(BF16)

    def expand(c):
        return _dot(hi, e_ref[c]) + _dot(lo, e_ref[c])

    o = expand(0) * oc_ref[...] + expand(1) * os_ref[...] + expand(2) * ow_ref[...]
    o_ref[...] = (o * _silu(g_ref[...])).astype(o_ref.dtype)


def nsa_combine(o_cmp, o_sel, o_win, z_g, z_b):
    m = o_cmp.shape[0]
    tm = _pick(m, (256, 128, 64, 32, 16))
    row = pl.BlockSpec((tm, BR_W), lambda i: (i, 0))
    return pl.pallas_call(
        _combine_body,
        grid=(m // tm,),
        in_specs=[row, row, row, pl.BlockSpec((tm, LANE), lambda i: (i, 0)), row,
                  pl.BlockSpec((3, LANE, BR_W), lambda i: (0, 0, 0))],
        out_specs=row,
        out_shape=jax.ShapeDtypeStruct((m, BR_W), BF16),
        compiler_params=_cparams("parallel"),
        name="nsa_combine",
    )(o_cmp, o_sel, o_win, z_g, z_b, _gate_expand())


def _mem_attn_body(q_ref, g_ref, kv_ref, o_ref):
    scale = MEM_DH ** -0.5
    q = q_ref[...]
    for h in range(MEM_H):
        k = kv_ref[:, h * MEM_DH:(h + 1) * MEM_DH].astype(BF16)
        v = kv_ref[:, (MEM_H + h) * MEM_DH:(MEM_H + h + 1) * MEM_DH].astype(BF16)
        s = _dot_nt(q[:, h * MEM_DH:(h + 1) * MEM_DH].astype(BF16), k) * scale
        e = jnp.exp(s - jnp.max(s, axis=-1, keepdims=True))
        p = e * (1.0 / jnp.sum(e, axis=-1, keepdims=True))
        o = _dot(p.astype(BF16), v)
        sl = slice(h * MEM_DH, (h + 1) * MEM_DH)
        o_ref[:, sl] = (o * _silu(g_ref[:, sl])).astype(o_ref.dtype)


def mem_attn_prompt(z_b, mkv):
    t = z_b.shape[0]
    tq = _pick(t, (512, 256, 128, 64, 32, 16))
    return pl.pallas_call(
        _mem_attn_body,
        grid=(t // tq,),
        in_specs=[pl.BlockSpec((tq, BR_W), lambda i: (i, 1)), pl.BlockSpec((tq, BR_W), lambda i: (i, 2)),
                  pl.BlockSpec(mkv.shape, lambda i: (0, 0))],
        out_specs=pl.BlockSpec((tq, BR_W), lambda i: (i, 0)),
        out_shape=jax.ShapeDtypeStruct((t, BR_W), BF16),
        compiler_params=_cparams("parallel"),
        name="mem_attn_prompt",
    )(z_b, z_b, mkv)


def _mem_attn_step_body(q_ref, g_ref, kv_ref, o_ref):
    scale = MEM_DH ** -0.5
    q = q_ref[...]
    rows_per = MEM_DH // LANE
    out_rows = []
    for h in range(MEM_H):
        qh = jnp.concatenate([q[h * rows_per + i:h * rows_per + i + 1, :] for i in range(rows_per)], axis=1)
        qh = jnp.broadcast_to(qh, (8, MEM_DH)).astype(BF16)
        k = kv_ref[:, h * MEM_DH:(h + 1) * MEM_DH].astype(BF16)
        v = kv_ref[:, (MEM_H + h) * MEM_DH:(MEM_H + h + 1) * MEM_DH].astype(BF16)
        s = _dot_nt(qh, k) * scale
        e = jnp.exp(s - jnp.max(s, axis=-1, keepdims=True))
        p = e * (1.0 / jnp.sum(e, axis=-1, keepdims=True))
        o = _dot(p.astype(BF16), v)[0:1, :]
        out_rows += [o[:, i * LANE:(i + 1) * LANE] for i in range(rows_per)]
    o_ref[...] = jnp.concatenate(out_rows, axis=0) * _silu(g_ref[...])


def mem_attn_sample(z_b3, kv_cache, layer):
    bs = z_b3.shape[0]
    mem_len, width = kv_cache.shape[2], kv_cache.shape[3]
    return pl.pallas_call(
        _mem_attn_step_body,
        grid=(bs,),
        in_specs=[pl.BlockSpec((None, 8, LANE), lambda b: (b, 1, 0)), pl.BlockSpec((None, 8, LANE), lambda b: (b, 2, 0)),
                  pl.BlockSpec((None, None, mem_len, width), lambda b: (layer, b, 0, 0))],
        out_specs=pl.BlockSpec((None, 8, LANE), lambda b: (b, 0, 0)),
        out_shape=jax.ShapeDtypeStruct((bs, 8, LANE), F32),
        compiler_params=_cparams("parallel"),
        name="mem_attn_sample",
    )(z_b3, z_b3, kv_cache)


def _paged_partial_body(npg, pt_ref, *refs):
    del pt_ref
    page_refs, w_ref, o_ref = refs[:npg], refs[npg], refs[npg + 1]
    chunks = PAGE_SIZE // CMP_STRIDE
    o_ref[...] = jnp.zeros_like(o_ref)

    unroll = 8

    def step(i, carry):
        for c in range(2 * NS_KV):
            acc = o_ref[c]
            for u in range(unroll):
                s = i * unroll + u
                xs = jnp.concatenate(
                    [r[pl.ds(s * ROWS_PER_TOKEN + c, chunks, stride=CMP_STRIDE * ROWS_PER_TOKEN), :]
                     for r in page_refs], axis=0)
                acc += _dot(xs.astype(BF16), w_ref[c // NS_KV, s])
            o_ref[c] = acc
        return carry

    lax.fori_loop(0, CMP_STRIDE // unroll, step, 0)


def compress_partial_paged(cache2, page_table, w1, layer, n_pool):
    bs, n_pages = page_table.shape
    npg = min(PAGES_PER_STEP, n_pages)
    assert n_pages % npg == 0
    chunks = PAGE_SIZE // CMP_STRIDE
    w = _cmp_weights(w1).reshape(2, CMP_STRIDE, NS_DH, 2 * NS_DH)

    def page_spec(i):
        return pl.BlockSpec((PAGE_SIZE * ROWS_PER_TOKEN, NS_DH),
                            lambda b, j, pt: (layer * n_pool + pt[b, j * npg + i], 0))

    grid_spec = pltpu.PrefetchScalarGridSpec(
        num_scalar_prefetch=1,
        grid=(bs, n_pages // npg),
        in_specs=[page_spec(i) for i in range(npg)] + [pl.BlockSpec(w.shape, lambda b, j, pt: (0, 0, 0, 0))],
        out_specs=pl.BlockSpec((2 * NS_KV, None, npg * chunks, 2 * NS_DH), lambda b, j, pt: (0, b, j, 0)),
    )
    out = pl.pallas_call(
        functools.partial(_paged_partial_body, npg),
        grid_spec=grid_spec,
        out_shape=jax.ShapeDtypeStruct((2 * NS_KV, bs, n_pages * chunks, 2 * NS_DH), F32),
        compiler_params=_cparams("parallel", "parallel"),
        name="cmp_partial_paged",
    )(page_table, *([cache2] * npg), w)
    return out.reshape(2, NS_KV * bs * n_pages * chunks, 2 * NS_DH)


def _row_group(shape):
    return lax.broadcasted_iota(jnp.int32, shape, 0) // NS_G


def _cmp_step_body(n_cmp, qpos, q_ref, kc_ref, vc_ref, o_ref, ps_ref):
    q = q_ref[...].astype(BF16)
    ncp = kc_ref.shape[1]
    s = [_dot_nt(q, kc_ref[g].astype(BF16)) for g in range(NS_KV)]
    grp = _row_group((NS_H, ncp))
    s = jnp.where(grp == 0, s[0], s[1])
    n_i = lax.broadcasted_iota(jnp.int32, (NS_H, ncp), 1)
    valid = (n_i * CMP_STRIDE + CMP_LEN - 1 <= qpos) & (n_i < n_cmp)
    sm = jnp.where(valid, s, NEG_INF)
    mx = jnp.max(sm, axis=-1, keepdims=True)
    e = jnp.where(valid, jnp.exp2(sm - mx), 0.0)
    den = jnp.sum(e, axis=-1, keepdims=True)
    p = e * (1.0 / jnp.where(den > 0.0, den, 1.0))
    pb = p.astype(BF16)
    o = [_dot(pb, vc_ref[g].astype(BF16)) for g in range(NS_KV)]
    o_ref[...] = jnp.where(_row_group((NS_H, NS_DH)) == 0, o[0], o[1])
    ps_ref[...] = jnp.concatenate(
        [jnp.sum(jnp.where(grp == g, p, 0.0), axis=0, keepdims=True) for g in range(NS_KV)], axis=0)


def nsa_cmp_sample(q_c3, kcmp, vcmp, n_cmp, qpos):
    bs = q_c3.shape[0]
    ncp = kcmp.shape[2]
    kv = pl.BlockSpec((NS_KV, None, ncp, NS_DH), lambda b: (0, b, 0, 0))
    return pl.pallas_call(
        functools.partial(_cmp_step_body, n_cmp, qpos),
        grid=(bs,),
        in_specs=[pl.BlockSpec((None, NS_H, NS_DH), lambda b: (b, 0, 0)), kv, kv],
        out_specs=[pl.BlockSpec((None, NS_H, NS_DH), lambda b: (b, 0, 0)),
                   pl.BlockSpec((None, NS_KV, ncp), lambda b: (b, 0, 0))],
        out_shape=[jax.ShapeDtypeStruct((bs, NS_H, NS_DH), F32), jax.ShapeDtypeStruct((bs, NS_KV, ncp), F32)],
        compiler_params=_cparams("parallel"),
        name="nsa_cmp_sample",
    )(q_c3, kcmp, vcmp)


def _select_body(n_sel, qpos, ps_ref, msel_ref, idx_ref):
    hi, mid, lo = _split3(ps_ref[...])
    imp = _dot(hi, msel_ref[...]) + _dot(mid, msel_ref[...]) + _dot(lo, msel_ref[...])
    j_i = lax.broadcasted_iota(jnp.int32, imp.shape, 1)
    cur = qpos // SEL_BLOCK
    forced = (j_i == 0) | (j_i == cur) | (j_i == cur - 1)
    imp = jnp.where(forced, imp + FORCE_BONUS, imp)
    imp = jnp.where(j_i <= cur, imp, NEG_INF)
    imp = jnp.where(j_i < n_sel, imp, REMOVED)
    j_f = j_i.astype(F32)
    col = lax.broadcasted_iota(jnp.int32, idx_ref.shape, 1)
    out = jnp.zeros(idx_ref.shape, F32)
    for it in range(min(SEL_TOPN, n_sel)):
        mx = jnp.max(imp, axis=-1, keepdims=True)
        first = jnp.min(jnp.where(imp == mx, j_f, float(imp.shape[1])), axis=-1, keepdims=True)
        out = jnp.where(col == it, first, out)
        imp = jnp.where(j_f == first, REMOVED, imp)
    idx_ref[...] = out.astype(jnp.int32)


def nsa_select_sample(psum2, n_cmp, n_sel, qpos):
    rows, ncp = psum2.shape
    nsp = -(-n_sel // LANE) * LANE
    msel = jnp.asarray(_cmp_to_sel(ncp, nsp, n_cmp, n_sel), BF16)
    return pl.pallas_call(
        functools.partial(_select_body, n_sel, qpos),
        grid=(1,),
        in_specs=[pl.BlockSpec((rows, ncp), lambda i: (0, 0)), pl.BlockSpec((ncp, nsp), lambda i: (0, 0))],
        out_specs=pl.BlockSpec((rows, LANE), lambda i: (0, 0)),
        out_shape=jax.ShapeDtypeStruct((rows, LANE), jnp.int32),
        compiler_params=_cparams("arbitrary"),
        name="nsa_select_sample",
    )(psum2, msel)


def _sel_step_body(n_past_blocks, qpos, pt_ref, top_ref, q_ref, *refs):
    del pt_ref
    nblk = NS_KV * SEL_PER_STEP
    blk_refs, new_ref, o_ref, m_scr, l_scr, acc_scr = refs[:nblk], *refs[nblk:nblk + 5]
    b = pl.program_id(0)
    j = pl.program_id(1)

    @pl.when(j == 0)
    def _():
        m_scr[...] = jnp.full_like(m_scr, NEG_INF)
        l_scr[...] = jnp.zeros_like(l_scr)
        acc_scr[...] = jnp.zeros_like(acc_scr)

    def slab(ref, slot, g):
        return ref[pl.ds(slot * NS_KV + g, SEL_BLOCK, stride=ROWS_PER_TOKEN), :].astype(BF16)

    q = q_ref[...]
    qb = q.astype(BF16)
    nk = SEL_PER_STEP * SEL_BLOCK
    grp_k = _row_group((NS_H, nk))
    grp_d = _row_group((NS_H, NS_DH))
    lane_blk = lax.broadcasted_iota(jnp.int32, (NS_H, nk), 1) // SEL_BLOCK
    s_g, idx_g, new_g = [], [], []
    for g in range(NS_KV):
        k_all = jnp.concatenate([slab(blk_refs[g * SEL_PER_STEP + k], 2, g) for k in range(SEL_PER_STEP)], axis=0)
        s_g.append(_dot_nt(qb, k_all))
        ids = [top_ref[b, g * SEL_TOPN + j * SEL_PER_STEP + k] for k in range(SEL_PER_STEP)]
        idx = jnp.zeros((NS_H, nk), jnp.int32)
        n_new = jnp.int32(0)
        for k, i in enumerate(ids):
            idx = jnp.where(lane_blk == k, i, idx)
            n_new = n_new + (i == n_past_blocks).astype(jnp.int32)
        idx_g.append(idx)
        new_g.append(n_new)
    s = jnp.where(grp_k == 0, s_g[0], s_g[1])
    idx = jnp.where(grp_k == 0, idx_g[0], idx_g[1])
    tok = idx * SEL_BLOCK + lax.broadcasted_iota(jnp.int32, (NS_H, nk), 1) % SEL_BLOCK
    valid = (idx < n_past_blocks) & (tok <= qpos)
    ks_new = jnp.where(grp_d == 0, new_ref[4:5, :], new_ref[5:6, :])
    vs_new = jnp.where(grp_d == 0, new_ref[6:7, :], new_ref[7:8, :])
    is_new = jnp.where(grp_d == 0, new_g[0], new_g[1]) > 0
    s_new = jnp.where(is_new, jnp.sum(q * ks_new, axis=-1, keepdims=True), NEG_INF)
    m_old = m_scr[...]
    m_new = jnp.maximum(jnp.maximum(m_old, jnp.max(jnp.where(valid, s, NEG_INF), axis=-1, keepdims=True)), s_new)
    alpha = jnp.exp2(m_old - m_new)
    p = jnp.where(valid, jnp.exp2(s - m_new[:, 0:1]), 0.0)
    p_new = jnp.where(is_new, jnp.exp2(s_new - m_new), 0.0)
    pb = p.astype(BF16)
    pv_g = []
    for g in range(NS_KV):
        v_all = jnp.concatenate([slab(blk_refs[g * SEL_PER_STEP + k], 3, g) for k in range(SEL_PER_STEP)], axis=0)
        pv_g.append(_dot(pb, v_all))
    pv = jnp.where(grp_d == 0, pv_g[0], pv_g[1])
    l_scr[...] = alpha * l_scr[...] + jnp.sum(p, axis=-1, keepdims=True) + p_new
    acc_scr[...] = alpha * acc_scr[...] + pv + p_new * vs_new
    m_scr[...] = m_new

    @pl.when(j == pl.num_programs(1) - 1)
    def _():
        o_ref[...] = acc_scr[...] / l_scr[...]


def nsa_sel_sample(q_r3, rows3, cache2, page_table, top_idx, layer, n_pool, qpos):
    bs, n_pages = page_table.shape
    halves = PAGE_SIZE // SEL_BLOCK
    n_past_blocks = n_pages * halves

    def blk(g, k):
        def imap(b, j, pt, top):
            i = top[b, g * SEL_TOPN + j * SEL_PER_STEP + k]
            page = pt[b, jnp.minimum(i // halves, n_pages - 1)]
            return ((layer * n_pool + page) * halves + i % halves, 0)
        return pl.BlockSpec((SEL_BLOCK * ROWS_PER_TOKEN, NS_DH), imap)

    vec = pl.BlockSpec((None, NS_H, NS_DH), lambda b, j, pt, top: (b, 0, 0))
    grid_spec = pltpu.PrefetchScalarGridSpec(
        num_scalar_prefetch=2,
        grid=(bs, SEL_TOPN // SEL_PER_STEP),
        in_specs=[vec] + [blk(g, k) for g in range(NS_KV) for k in range(SEL_PER_STEP)] + [vec],
        out_specs=vec,
        scratch_shapes=[pltpu.VMEM((NS_H, NS_DH), F32)] * 3,
    )
    return pl.pallas_call(
        functools.partial(_sel_step_body, n_past_blocks, qpos),
        grid_spec=grid_spec,
        out_shape=jax.ShapeDtypeStruct((bs, NS_H, NS_DH), F32),
        compiler_params=_cparams("parallel", "arbitrary"),
        name="nsa_sel_sample",
    )(page_table, top_idx, q_r3, *([cache2] * (NS_KV * SEL_PER_STEP)), rows3)


def _win_step_body(pos0, qpos, q_ref, buf_ref, new_ref, o_ref):
    q = q_ref[...]
    qb = q.astype(BF16)
    rows_per = 2 * NS_KV
    wb = buf_ref.shape[0] // rows_per

    def slab(c):
        return buf_ref[pl.ds(c, wb, stride=rows_per), :].astype(BF16)

    grp_k = _row_group((NS_H, wb))
    grp_d = _row_group((NS_H, NS_DH))
    s = jnp.where(grp_k == 0, _dot_nt(qb, slab(0)), _dot_nt(qb, slab(1)))
    kpos = pos0 + lax.broadcasted_iota(jnp.int32, (NS_H, wb), 1)
    valid = (kpos <= qpos) & (kpos > qpos - WINDOW) & (kpos >= 0)
    kw_new = jnp.where(grp_d == 0, new_ref[0:1, :], new_ref[1:2, :])
    vw_new = jnp.where(grp_d == 0, new_ref[2:3, :], new_ref[3:4, :])
    s_new = jnp.sum(q * kw_new, axis=-1, keepdims=True)
    mx = jnp.maximum(jnp.max(jnp.where(valid, s, NEG_INF), axis=-1, keepdims=True), s_new)
    p = jnp.where(valid, jnp.exp2(s - mx), 0.0)
    p_new = jnp.exp2(s_new - mx)
    den = jnp.sum(p, axis=-1, keepdims=True) + p_new
    pb = p.astype(BF16)
    pv = jnp.where(grp_d == 0, _dot(pb, slab(2)), _dot(pb, slab(3)))
    o_ref[...] = (pv + p_new * vw_new) / den


def nsa_win_sample(q_r3, win_buf2, win3, layer, wb, past_len, qpos):
    bs = q_r3.shape[0]
    return pl.pallas_call(
        functools.partial(_win_step_body, past_len - wb, qpos),
        grid=(bs,),
        in_specs=[pl.BlockSpec((None, NS_H, NS_DH), lambda b: (b, 0, 0)),
                  pl.BlockSpec((wb * 2 * NS_KV, NS_DH), lambda b: (layer * bs + b, 0)),
                  pl.BlockSpec((None, 2 * NS_KV, NS_DH), lambda b: (b, 0, 0))],
        out_specs=pl.BlockSpec((None, NS_H, NS_DH), lambda b: (b, 0, 0)),
        out_shape=jax.ShapeDtypeStruct((bs, NS_H, NS_DH), F32),
        compiler_params=_cparams("parallel"),
        name="nsa_win_sample",
    )(q_r3, win_buf2, win3)


def _project(h, w_a, w_g, w_b):
    return (matmul(h, w_a, name="in_proj_a"), matmul(h, w_g, name="in_proj_gate"), matmul(h, w_b, name="in_proj_b"))


def _merge_out(x, branches, z_b, w_up_b, w_out_b, next_g, last):
    d = x.shape[1]
    gated = merge_up(branches, z_b, w_up_b, d)
    return out_proj_norm(gated, w_out_b, x, next_g, F32 if last else BF16, keep_x=not last)


def kernel(x_prompt, x_sample, mem_prompt, cache_nsa_kv, state_win_kv, state_hgrn, state_conv, cache_mem_kv,
           page_table, norm_g, final_norm_g, w_in, hg_lb_logits, hg_norm_g, cv_w, ns_pe, ns_cw1, ns_cw2,
           mem_norm_g, w_mem_kv, w_up, w_out):
    depth = w_in.shape[0]
    b_p, t, d = x_prompt.shape
    bs, ts = x_sample.shape[:2]
    assert b_p == 1 and ts == 1
    n_pages = page_table.shape[1]
    past_len = n_pages * PAGE_SIZE
    wb = state_win_kv.shape[2]
    mem_len = mem_prompt.shape[1]
    kvw = NS_KV * NS_DH

    s_lb = jax.nn.softmax(hg_lb_logits.astype(F32), axis=0)
    lower = jnp.cumsum(s_lb, axis=0) - s_lb[0]

    pos_p = jnp.arange(t, dtype=jnp.int32)
    qpos_s = past_len
    pos_s = jnp.full((bs,), qpos_s, jnp.int32)

    n_pool = cache_nsa_kv.shape[1]
    cache2 = cache_nsa_kv.reshape(depth * n_pool * PAGE_SIZE * ROWS_PER_TOKEN, NS_DH)
    win_buf2 = state_win_kv.reshape(depth * bs * wb * 2 * NS_KV, NS_DH)
    mem_cache4 = cache_mem_kv.reshape(depth, bs, mem_len, 2 * MEM_H * MEM_DH)

    nch_p = t // CMP_STRIDE
    nch_s = (past_len + 1) // CMP_STRIDE
    n_sel_s = -(-(past_len + 1) // SEL_BLOCK)

    xp = x_prompt.reshape(t, d)
    xs = x_sample.reshape(bs, d)
    rows_p, win_p, hg_p, cv_p, mkv_p = [], [], [], [], []
    rows_s, win_s, hg_s, cv_s = [], [], [], []
    hp = rmsnorm_rows(xp, norm_g[0], BF16)
    hs = rmsnorm_rows(xs, norm_g[0], BF16)
    for l in range(depth):
        last = l == depth - 1
        next_g = final_norm_g if last else norm_g[l + 1]
        w_l = w_in[l]
        w_a = w_l[:, :COL_A].astype(BF16)
        w_g = jnp.pad(w_l[:, COL_A:COL_B0], ((0, 0), (0, LANE - COL_GATE))).astype(BF16)
        w_b = w_l[:, COL_B0:].astype(BF16)
        w_up_b = w_up[l].astype(BF16)
        w_out_b = w_out[l].astype(BF16)

        z_a, z_g, z_b = _project(hp, w_a, w_g, w_b)
        o_hg, st_hg = hgrn_prompt(z_a, lower[l], hg_norm_g[l])
        o_cv, st_cv = conv_prompt(z_a, cv_w[l], jnp.zeros((CV_K - 1, BR_W), F32))
        q_c, q_r, rows, win, ks_ext, kw, vs_ext, vw_t = nsa_prep(z_a, pos_p, BF16, attn_layouts=True)
        pe_term = compress_pe_term(ns_pe[l], ns_cw1[l])
        cmp = compress_finish(compress_partial_rows(z_a, ns_cw1[l]), pe_term, ns_cw2[l], nch_p)
        kcmp = cmp[0].astype(BF16)
        vcmp_t = jnp.swapaxes(cmp[1], 1, 2).astype(BF16)
        o_cmp, sel_bias = nsa_cmp_select(q_c, kcmp, vcmp_t, nch_p - 1)
        o_sel = nsa_sel_attn(q_r, sel_bias, ks_ext, vs_ext)
        o_win = nsa_win_attn(q_r, kw, vw_t)
        o_ns = nsa_combine(o_cmp, o_sel, o_win, z_g, z_b)
        mem_h = rmsnorm_rows(mem_prompt.reshape(mem_len, d), mem_norm_g[l], BF16)
        mkv = matmul(mem_h, w_mem_kv[l].astype(BF16), name="mem_kv")
        o_mm = mem_attn_prompt(z_b, mkv)
        xp, hp = _merge_out(xp, (o_hg, o_cv, o_ns, o_mm), z_b, w_up_b, w_out_b, next_g, last)
        rows_p.append(rows)
        win_p.append(win[t - min(WINDOW, t):])
        hg_p.append(st_hg.reshape(1, HG_H, HG_DK, HG_DV))
        cv_p.append(st_cv.reshape(1, CV_K - 1, BR_W))
        mkv_p.append(mkv.reshape(1, mem_len, 2, MEM_H, MEM_DH))

        z_a, z_g, z_b = _project(hs, w_a, w_g, w_b)
        z_a3 = z_a.reshape(bs, COL_A // LANE, LANE)
        z_b3 = z_b.reshape(bs, z_b.shape[1] // LANE, LANE)
        o_hg, st_hg = hgrn_sample(z_a3, lower[l], hg_norm_g[l], state_hgrn[l])
        o_cv, st_cv = conv_sample(z_a, cv_w[l], state_conv[l])
        q_c, q_r, rows, win = nsa_prep(z_a, pos_s, F32)
        part = compress_partial_paged(cache2, page_table, ns_cw1[l], l, n_pool)
        cmp = compress_finish(part, pe_term, ns_cw2[l], nch_s)
        cmp = cmp.reshape(2, NS_KV, bs, nch_s, NS_DH)
        q_c3 = q_c.reshape(bs, NS_H, NS_DH)
        q_r3 = q_r.reshape(bs, NS_H, NS_DH)
        o_cmp, psum = nsa_cmp_sample(q_c3, cmp[0], cmp[1], nch_s - 1, qpos_s)
        top = nsa_select_sample(psum.reshape(bs * NS_KV, nch_s), nch_s - 1, n_sel_s, qpos_s)
        top = top[:, :SEL_TOPN].reshape(bs, NS_KV * SEL_TOPN)
        o_sel = nsa_sel_sample(q_r3, rows.reshape(bs, 4 * NS_KV, NS_DH), cache2, page_table, top, l, n_pool, qpos_s)
        o_win = nsa_win_sample(q_r3, win_buf2, win.reshape(bs, 2 * NS_KV, NS_DH), l, wb, past_len, qpos_s)
        o_ns = nsa_combine(o_cmp.reshape(bs, BR_W), o_sel.reshape(bs, BR_W), o_win.reshape(bs, BR_W), z_g, z_b)
        o_mm = mem_attn_sample(z_b3, mem_cache4, l)
        branches = (o_hg.reshape(bs, BR_W).astype(BF16), o_cv.astype(BF16), o_ns, o_mm.reshape(bs, BR_W).astype(BF16))
        xs, hs = _merge_out(xs, branches, z_b, w_up_b, w_out_b, next_g, last)
        rows_s.append(rows)
        win_s.append(win)
        hg_s.append(st_hg)
        cv_s.append(st_cv)

    y_prompt = hp.reshape(1, t, d)
    y_sample = hs.reshape(bs, 1, d)
    wlen = min(WINDOW, t)
    new_rows_p = jnp.stack(rows_p).reshape(depth, 1, t, 4, NS_KV, NS_DH)
    new_win_p = jnp.stack(win_p).reshape(depth, 1, wlen, 2, NS_KV, NS_DH)
    new_rows_s = jnp.stack(rows_s).reshape(depth, bs, 1, 4, NS_KV, NS_DH)
    new_win_s = jnp.concatenate(
        [state_win_kv[:, :, 1:], jnp.stack(win_s).reshape(depth, bs, 1, 2, NS_KV, NS_DH)], axis=2)
    return (y_prompt, y_sample, new_rows_p, new_win_p, jnp.stack(hg_p), jnp.stack(cv_p),
            jnp.stack(mkv_p), new_rows_s, new_win_s, jnp.stack(hg_s), jnp.stack(cv_s))
```

```python
import functools

import numpy as np
import jax
import jax.numpy as jnp
from jax import lax
from jax.experimental import pallas as pl
from jax.experimental.pallas import tpu as pltpu

F32 = jnp.float32
BF16 = jnp.bfloat16

BR_W = 1024
N_BRANCH = 4
HG_H = 8
HG_DK = 128
HG_DV = 128
F_MIN = 1e-30
CV_K = 3
NS_H = 8
NS_KV = 2
NS_G = NS_H // NS_KV
NS_DH = 128
CMP_LEN = 32
CMP_STRIDE = 16
SEL_BLOCK = 64
SEL_TOPN = 16
WINDOW = 512
FORCE_BONUS = 100.0
ROPE_THETA = 500000.0
ROT_DIM = NS_DH // 4
MEM_H = 4
MEM_DH = 256
NORM_EPS = 1e-6
NEG_INF = -1e30
REMOVED = -3e38
LOG2_E = 1.4426950408889634
PAGE_SIZE = 128
ROWS_PER_TOKEN = 4 * NS_KV

COL_A = 10752
COL_GATE = 24
COL_B0 = COL_A + COL_GATE

LANE = 128
HG_CHUNK = 128
HG_HEADS_PER_STEP = 4
Q_TILE = 128
K_TILE = 512
PAGES_PER_STEP = 16
SEL_PER_STEP = 4
VMEM_LIMIT = 56 * 1024 * 1024


def _cparams(*sem):
    return pltpu.CompilerParams(dimension_semantics=sem, vmem_limit_bytes=VMEM_LIMIT)


def _pick(n, cands):
    for c in cands:
        if n % c == 0:
            return c
    return n


def _silu(x):
    return x * jax.nn.sigmoid(x)


def _dot(a, b):
    return jnp.dot(a, b, preferred_element_type=F32)


def _dot_nt(a, b):
    return lax.dot_general(a, b, (((1,), (1,)), ((), ())), preferred_element_type=F32)


def _split3(x):
    hi = x.astype(BF16)
    r1 = x - hi.astype(F32)
    mid = r1.astype(BF16)
    lo = (r1 - mid.astype(F32)).astype(BF16)
    return hi, mid, lo


def _norm_body(x_ref, g_ref, o_ref):
    x = x_ref[...]
    ms = jnp.mean(x * x, axis=-1, keepdims=True)
    o_ref[...] = (x * lax.rsqrt(ms + NORM_EPS) * g_ref[...]).astype(o_ref.dtype)


def rmsnorm_rows(x, g, out_dtype):
    m, d = x.shape
    tm = _pick(m, (256, 128, 64, 32, 16, 8))
    return pl.pallas_call(
        _norm_body,
        grid=(m // tm,),
        in_specs=[pl.BlockSpec((tm, d), lambda i: (i, 0)), pl.BlockSpec((1, d), lambda i: (0, 0))],
        out_specs=pl.BlockSpec((tm, d), lambda i: (i, 0)),
        out_shape=jax.ShapeDtypeStruct((m, d), out_dtype),
        compiler_params=_cparams("parallel"),
        name="rmsnorm",
    )(x, g.reshape(1, d).astype(F32))


def _mm_body(a_ref, b_ref, o_ref):
    o_ref[...] = _dot(a_ref[...].astype(BF16), b_ref[...]).astype(o_ref.dtype)


def _mm_res_body(a_ref, b_ref, r_ref, o_ref):
    o_ref[...] = r_ref[...] + _dot(a_ref[...].astype(BF16), b_ref[...])


def matmul(a, b, res=None, out_dtype=F32, name="matmul", b_layer=None, n=None):
    m, k = a.shape
    n = b.shape[-1] if n is None else n
    tm = _pick(m, (1024, 512, 256, 128, 64, 32, 16, 8))
    tn = _pick(n, (1536, 1024, 512, 256, 128))
    if b_layer is None:
        b_spec = pl.BlockSpec((k, tn), lambda i, j: (0, j))
    else:
        b_spec = pl.BlockSpec((None, k, tn), lambda i, j: (b_layer, 0, j))
    in_specs = [pl.BlockSpec((tm, k), lambda i, j: (i, 0)), b_spec]
    args = [a, b]
    body = _mm_body
    if res is not None:
        in_specs.append(pl.BlockSpec((tm, tn), lambda i, j: (i, j)))
        args.append(res)
        body = _mm_res_body
    return pl.pallas_call(
        body,
        grid=(m // tm, n // tn),
        in_specs=in_specs,
        out_specs=pl.BlockSpec((tm, tn), lambda i, j: (i, j)),
        out_shape=jax.ShapeDtypeStruct((m, n), out_dtype),
        compiler_params=_cparams("parallel", "parallel"),
        name=name,
    )(*args)


def _out_norm_body(keep_x, a_ref, b_ref, r_ref, g_ref, *out_refs):
    x = r_ref[...] + _dot(a_ref[...], b_ref[...])
    if keep_x:
        out_refs[0][...] = x
    ms = jnp.mean(x * x, axis=-1, keepdims=True)
    h_ref = out_refs[-1]
    h_ref[...] = (x * lax.rsqrt(ms + NORM_EPS) * g_ref[...]).astype(h_ref.dtype)


def out_proj_norm(a, b, res, g, h_dtype, keep_x):
    m, k = a.shape
    n = b.shape[1]
    tm = _pick(m, (512, 256, 128, 64, 32, 16, 8))
    row = pl.BlockSpec((tm, n), lambda i: (i, 0))
    out_specs = [row, row] if keep_x else [row]
    out_shape = [jax.ShapeDtypeStruct((m, n), h_dtype)]
    if keep_x:
        out_shape = [jax.ShapeDtypeStruct((m, n), F32)] + out_shape
    outs = pl.pallas_call(
        functools.partial(_out_norm_body, keep_x),
        grid=(m // tm,),
        in_specs=[pl.BlockSpec((tm, k), lambda i: (i, 0)), pl.BlockSpec((k, n), lambda i: (0, 0)), row,
                  pl.BlockSpec((1, n), lambda i: (0, 0))],
        out_specs=out_specs,
        out_shape=out_shape,
        compiler_params=_cparams("parallel"),
        name="out_proj_norm",
    )(a, b, res, g.reshape(1, n).astype(F32))
    return outs if keep_x else (None, outs[0])


def _merge_body(b0, b1, b2, b3, g0, g1, g2, g3, w0, w1, w2, w3, o_ref):
    acc = jax.nn.sigmoid(g0[...]) * _dot(b0[...], w0[...])
    acc += jax.nn.sigmoid(g1[...]) * _dot(b1[...], w1[...])
    acc += jax.nn.sigmoid(g2[...]) * _dot(b2[...], w2[...])
    acc += jax.nn.sigmoid(g3[...]) * _dot(b3[...], w3[...])
    o_ref[...] = acc.astype(o_ref.dtype)


def merge_up(branches, z_b, w_up_bf16, d_model):
    m = branches[0].shape[0]
    tm = _pick(m, (512, 256, 128, 64, 32, 16))
    tn = 512
    nj = d_model // tn
    gate_col0 = (z_b.shape[1] - N_BRANCH * d_model) // tn
    br_specs = [pl.BlockSpec((tm, BR_W), lambda i, j: (i, 0)) for _ in range(N_BRANCH)]
    g_specs = [pl.BlockSpec((tm, tn), functools.partial(lambda i, j, n: (i, gate_col0 + n * nj + j), n=n))
               for n in range(N_BRANCH)]
    w_specs = [pl.BlockSpec((None, BR_W, tn), functools.partial(lambda i, j, n: (n, 0, j), n=n))
               for n in range(N_BRANCH)]
    return pl.pallas_call(
        _merge_body,
        grid=(m // tm, nj),
        in_specs=br_specs + g_specs + w_specs,
        out_specs=pl.BlockSpec((tm, tn), lambda i, j: (i, j)),
        out_shape=jax.ShapeDtypeStruct((m, d_model), BF16),
        compiler_params=_cparams("parallel", "parallel"),
        name="merge_up",
    )(*branches, z_b, z_b, z_b, z_b, w_up_bf16, w_up_bf16, w_up_bf16, w_up_bf16)


def _hgrn_consts(c):
    nlev = int(round(np.log2(c)))
    t = np.arange(c)[:, None]
    r = np.arange(c)[None, :]
    blocks = [r <= t]
    masks = [t == r]
    for lv in range(nlev):
        h = c >> (lv + 1)
        mid = (t // (2 * h)) * (2 * h) + h
        blocks.append(np.where(t >= mid, (r >= mid) & (r <= t), (r > t) & (r <= mid - 1)))
        same = (t // (2 * h)) == (r // (2 * h))
        masks.append(same & (t % (2 * h) >= h) & (r % (2 * h) < h))
    blocks.append(r > t)
    l_all = np.concatenate(blocks, axis=0).astype(np.float32)
    return jnp.asarray(l_all, BF16), jnp.asarray(np.stack(masks).astype(np.float32)), nlev


def _hgrn_body(nlev, q_ref, z_ref, v_ref, g_ref, lb_ref, ng_ref, l_ref, mask_ref, o_ref, s_ref, st_scr):
    c_idx = pl.program_id(1)
    c = q_ref.shape[0]

    @pl.when(c_idx == 0)
    def _():
        st_scr[...] = jnp.zeros_like(st_scr)

    for hh in range(HG_HEADS_PER_STEP):
        sl = slice(hh * HG_DK, (hh + 1) * HG_DK)
        lb = lb_ref[:, sl]
        q = _silu(q_ref[:, sl])
        z = z_ref[:, sl]
        f = lb + (1.0 - lb) * jax.nn.sigmoid(z)
        logf = jnp.log(jnp.maximum(f, F_MIN))
        k = (1.0 - lb) * jax.nn.sigmoid(-z)
        v = v_ref[:, sl]

        hi, mid, _ = _split3(logf)
        e2 = _dot(l_ref[...], jnp.concatenate([hi, mid], axis=1))
        dk = HG_DK
        x = jnp.exp(e2[:, :dk] + e2[:, dk:])
        eb = x[0:c]
        est = x[(nlev + 1) * c:(nlev + 2) * c]

        st = st_scr[hh]
        inter = _dot_nt((q * eb).astype(BF16), st.astype(BF16))
        att = mask_ref[0] * _dot_nt(q.astype(BF16), k.astype(BF16))
        for lv in range(nlev):
            fac = x[(1 + lv) * c:(2 + lv) * c]
            att += mask_ref[1 + lv] * _dot_nt((q * fac).astype(BF16), (k * fac).astype(BF16))
        o = inter + _dot(att.astype(BF16), v.astype(BF16))
        o = o * lax.rsqrt(jnp.mean(o * o, axis=-1, keepdims=True) + NORM_EPS) * ng_ref[:, sl]
        o_ref[:, sl] = (o * _silu(g_ref[:, sl])).astype(o_ref.dtype)

        st_scr[hh] = st * eb[c - 1:c, :] + _dot(v.T.astype(BF16), (k * est).astype(BF16))

    @pl.when(c_idx == pl.num_programs(1) - 1)
    def _():
        for hh in range(HG_HEADS_PER_STEP):
            s_ref[hh] = st_scr[hh].T


def hgrn_prompt(z_a, lb, norm_g):
    t = z_a.shape[0]
    c = HG_CHUNK
    l_all, masks, nlev = _hgrn_consts(c)
    hp = HG_HEADS_PER_STEP
    steps = HG_H // hp
    width = hp * HG_DK

    def col(k):
        return pl.BlockSpec((c, width), functools.partial(lambda h, i, k: (i, k * steps + h), k=k))

    vec = pl.BlockSpec((None, 1, width), lambda h, i: (h, 0, 0))
    return pl.pallas_call(
        functools.partial(_hgrn_body, nlev),
        grid=(steps, t // c),
        in_specs=[col(0), col(1), col(2), col(3), vec, vec,
                  pl.BlockSpec(l_all.shape, lambda h, i: (0, 0)),
                  pl.BlockSpec(masks.shape, lambda h, i: (0, 0, 0))],
        out_specs=[pl.BlockSpec((c, width), lambda h, i: (i, h)),
                   pl.BlockSpec((hp, HG_DK, HG_DV), lambda h, i: (h, 0, 0))],
        out_shape=[jax.ShapeDtypeStruct((t, BR_W), BF16),
                   jax.ShapeDtypeStruct((HG_H, HG_DK, HG_DV), F32)],
        scratch_shapes=[pltpu.VMEM((hp, HG_DV, HG_DK), F32)],
        compiler_params=_cparams("parallel", "arbitrary"),
        name="hgrn_prompt",
    )(z_a, z_a, z_a, z_a, lb.reshape(steps, 1, width), norm_g.reshape(steps, 1, width), l_all, masks)


def _hgrn_step_body(q_ref, z_ref, v_ref, g_ref, lb_ref, ng_ref, s0_ref, o_ref, s_ref):
    lb = lb_ref[...]
    q = _silu(q_ref[...])
    z = z_ref[...]
    f = jnp.maximum(lb + (1.0 - lb) * jax.nn.sigmoid(z), F_MIN)
    k = (1.0 - lb) * jax.nn.sigmoid(-z)
    v = v_ref[...]
    rows = []
    for h in range(HG_H):
        def colb(a):
            return jnp.broadcast_to(a[h:h + 1, :], (HG_DK, HG_DK)).T
        s_new = colb(f) * s0_ref[h] + colb(k) * v[h:h + 1, :]
        s_ref[h] = s_new
        rows.append(jnp.sum(colb(q) * s_new, axis=0, keepdims=True))
    o = jnp.concatenate(rows, axis=0)
    o = o * lax.rsqrt(jnp.mean(o * o, axis=-1, keepdims=True) + NORM_EPS) * ng_ref[...]
    o_ref[...] = o * _silu(g_ref[...])


def hgrn_sample(z_a3, lb, norm_g, s0):
    bs = z_a3.shape[0]

    def grp(k):
        return pl.BlockSpec((None, HG_H, LANE), functools.partial(lambda b, k: (b, k, 0), k=k))

    vec = pl.BlockSpec((HG_H, LANE), lambda b: (0, 0))
    st = pl.BlockSpec((None, HG_H, HG_DK, HG_DV), lambda b: (b, 0, 0, 0))
    return pl.pallas_call(
        _hgrn_step_body,
        grid=(bs,),
        in_specs=[grp(0), grp(1), grp(2), grp(3), vec, vec, st],
        out_specs=[pl.BlockSpec((None, HG_H, LANE), lambda b: (b, 0, 0)), st],
        out_shape=[jax.ShapeDtypeStruct((bs, HG_H, LANE), F32),
                   jax.ShapeDtypeStruct((bs, HG_H, HG_DK, HG_DV), F32)],
        compiler_params=_cparams("parallel"),
        name="hgrn_sample",
    )(z_a3, z_a3, z_a3, z_a3, lb.reshape(HG_H, LANE), norm_g.reshape(HG_H, LANE), s0)


def _conv_body(u_ref, b_ref, c_ref, g_ref, w_ref, prev_ref, o_ref, last_ref, carry):
    @pl.when(pl.program_id(0) == 0)
    def _():
        carry[...] = prev_ref[...]

    v = c_ref[...] * u_ref[...]
    tm = v.shape[0]
    row = lax.broadcasted_iota(jnp.int32, v.shape, 0)
    p1 = carry[7:8, :]
    p2 = carry[6:7, :]
    v1 = jnp.where(row == 0, p1, pltpu.roll(v, 1, 0))
    v2 = jnp.where(row == 0, p2, jnp.where(row == 1, p1, pltpu.roll(v, 2, 0)))
    w = w_ref[...]
    y = w[0:1, :] * v2 + w[1:2, :] * v1 + w[2:3, :] * v
    o_ref[...] = (b_ref[...] * y * _silu(g_ref[...])).astype(o_ref.dtype)
    tail = v[tm - 8:tm, :]
    carry[...] = tail
    last_ref[...] = tail


def conv_prompt(z_a, w, prev):
    t = z_a.shape[0]
    tm = _pick(t, (256, 128, 64, 32, 16, 8))

    def col(k):
        return pl.BlockSpec((tm, BR_W), functools.partial(lambda i, k: (i, 4 + k), k=k))

    w8 = jnp.zeros((8, BR_W), F32).at[:CV_K].set(w.astype(F32))
    prev8 = jnp.zeros((8, BR_W), F32).at[8 - (CV_K - 1):].set(prev.astype(F32))
    full8 = pl.BlockSpec((8, BR_W), lambda i: (0, 0))
    o, last = pl.pallas_call(
        _conv_body,
        grid=(t // tm,),
        in_specs=[col(0), col(1), col(2), col(3), full8, full8],
        out_specs=[pl.BlockSpec((tm, BR_W), lambda i: (i, 0)), full8],
        out_shape=[jax.ShapeDtypeStruct((t, BR_W), BF16), jax.ShapeDtypeStruct((8, BR_W), F32)],
        scratch_shapes=[pltpu.VMEM((8, BR_W), F32)],
        compiler_params=_cparams("arbitrary"),
        name="conv_prompt",
    )(z_a, z_a, z_a, z_a, w8, prev8)
    return o, last[8 - (CV_K - 1):]


def _conv_step_body(u_ref, b_ref, c_ref, g_ref, w_ref, p0_ref, p1_ref, o_ref, v_ref):
    v = c_ref[...] * u_ref[...]
    w = w_ref[...]
    y = w[0:1, :] * p0_ref[...] + w[1:2, :] * p1_ref[...] + w[2:3, :] * v
    o_ref[...] = b_ref[...] * y * _silu(g_ref[...])
    v_ref[...] = v


def conv_sample(z_a, w, prev):
    bs = z_a.shape[0]

    def col(k):
        return pl.BlockSpec((bs, BR_W), functools.partial(lambda i, k: (0, 4 + k), k=k))

    w8 = jnp.zeros((8, BR_W), F32).at[:CV_K].set(w.astype(F32))
    full = pl.BlockSpec((bs, BR_W), lambda i: (0, 0))
    o, v = pl.pallas_call(
        _conv_step_body,
        grid=(1,),
        in_specs=[col(0), col(1), col(2), col(3), pl.BlockSpec((8, BR_W), lambda i: (0, 0)), full, full],
        out_specs=[full, full],
        out_shape=[jax.ShapeDtypeStruct((bs, BR_W), F32), jax.ShapeDtypeStruct((bs, BR_W), F32)],
        compiler_params=_cparams("arbitrary"),
        name="conv_sample",
    )(z_a, z_a, z_a, z_a, w8, prev[:, 0], prev[:, 1])
    return o, jnp.stack([prev[:, 1], v], axis=1)


def _rope_tables(pos):
    half = ROT_DIM // 2
    inv = ROPE_THETA ** (-2.0 * jnp.arange(half, dtype=F32) / ROT_DIM)
    ang = pos.astype(F32)[:, None] * inv[None, :]
    cos, sin = jnp.cos(ang), jnp.sin(ang)
    m = pos.shape[0]
    ones = jnp.ones((m, NS_DH - ROT_DIM), F32)
    zeros = jnp.zeros((m, NS_DH - ROT_DIM), F32)
    zh = jnp.zeros((m, half), F32)
    cos_t = jnp.concatenate([cos, cos, ones], axis=1)
    sin_a = jnp.concatenate([zh, sin, zeros], axis=1)
    sin_b = jnp.concatenate([-sin, zh, zeros], axis=1)
    return cos_t, sin_a, sin_b


def _rope(x, cos_t, sin_a, sin_b):
    n = x.shape[1] // NS_DH
    half = ROT_DIM // 2

    def tile(a):
        return a if n == 1 else jnp.concatenate([a] * n, axis=1)

    return (x * tile(cos_t) + pltpu.roll(x, half, 1) * tile(sin_a)
            + pltpu.roll(x, x.shape[1] - half, 1) * tile(sin_b))


def _nsa_prep_body(q_ref, cv_ref, ks_ref, vs_ref, kw_ref, vw_ref, cos_ref, sa_ref, sb_ref,
                   qc_ref, qr_ref, rows_ref, win_ref, *attn_refs):
    cos_t, sin_a, sin_b = cos_ref[...], sa_ref[...], sb_ref[...]
    scale = NS_DH ** -0.5 * LOG2_E
    q = q_ref[...]
    qc_ref[...] = (q * scale).astype(qc_ref.dtype)
    qr_ref[...] = (_rope(q, cos_t, sin_a, sin_b) * scale).astype(qr_ref.dtype)
    kvw = NS_KV * NS_DH
    ks = _rope(ks_ref[...], cos_t, sin_a, sin_b)
    kw = _rope(kw_ref[...], cos_t, sin_a, sin_b)
    vs = vs_ref[...]
    vw = vw_ref[...]
    rows_ref[:, 0:2 * kvw] = cv_ref[...]
    rows_ref[:, 2 * kvw:3 * kvw] = ks
    rows_ref[:, 3 * kvw:4 * kvw] = vs
    win_ref[:, 0:kvw] = kw
    win_ref[:, kvw:2 * kvw] = vw
    if attn_refs:
        ksx_ref, kwb_ref, vst_ref, vwt_ref = attn_refs
        tm = q.shape[0]
        tok = pl.program_id(0) * tm + lax.broadcasted_iota(jnp.int32, (tm, LANE), 0)
        onehot = jnp.where(tok // SEL_BLOCK == lax.broadcasted_iota(jnp.int32, (tm, LANE), 1),
                           1.0, 0.0).astype(ksx_ref.dtype)
        ones_rows = jnp.where(lax.broadcasted_iota(jnp.int32, (V_EXT_ROWS - NS_DH, tm), 0) == 0,
                              1.0, 0.0).astype(vst_ref.dtype)
        for g in range(NS_KV):
            sl = slice(g * NS_DH, (g + 1) * NS_DH)
            ksx_ref[g, :, 0:NS_DH] = ks[:, sl].astype(ksx_ref.dtype)
            ksx_ref[g, :, NS_DH:NS_DH + LANE] = onehot
            kwb_ref[g] = kw[:, sl].astype(kwb_ref.dtype)
            vst_ref[g, 0:NS_DH, :] = vs[:, sl].T.astype(vst_ref.dtype)
            vst_ref[g, NS_DH:V_EXT_ROWS, :] = ones_rows
            vwt_ref[g] = vw[:, sl].T.astype(vwt_ref.dtype)


def nsa_prep(z_a, pos, q_dtype, attn_layouts=False):
    m = z_a.shape[0]
    tm = _pick(m, (256, 128, 64, 32, 16, 8))
    kvw = NS_KV * NS_DH
    c0 = 8 * BR_W // kvw
    tabs = _rope_tables(pos)

    def col(k, width=kvw):
        return pl.BlockSpec((tm, width), lambda i: (i, k))

    tab = pl.BlockSpec((tm, NS_DH), lambda i: (i, 0))
    out_specs = [pl.BlockSpec((tm, BR_W), lambda i: (i, 0)), pl.BlockSpec((tm, BR_W), lambda i: (i, 0)),
                 pl.BlockSpec((tm, 4 * kvw), lambda i: (i, 0)), pl.BlockSpec((tm, 2 * kvw), lambda i: (i, 0))]
    out_shape = [jax.ShapeDtypeStruct((m, BR_W), q_dtype), jax.ShapeDtypeStruct((m, BR_W), q_dtype),
                 jax.ShapeDtypeStruct((m, 4 * kvw), F32), jax.ShapeDtypeStruct((m, 2 * kvw), F32)]
    if attn_layouts:
        out_specs += [pl.BlockSpec((NS_KV, tm, NS_DH + LANE), lambda i: (0, i, 0)),
                      pl.BlockSpec((NS_KV, tm, NS_DH), lambda i: (0, i, 0)),
                      pl.BlockSpec((NS_KV, V_EXT_ROWS, tm), lambda i: (0, 0, i)),
                      pl.BlockSpec((NS_KV, NS_DH, tm), lambda i: (0, 0, i))]
        out_shape += [jax.ShapeDtypeStruct((NS_KV, m, NS_DH + LANE), BF16),
                      jax.ShapeDtypeStruct((NS_KV, m, NS_DH), BF16),
                      jax.ShapeDtypeStruct((NS_KV, V_EXT_ROWS, m), BF16),
                      jax.ShapeDtypeStruct((NS_KV, NS_DH, m), BF16)]
    return pl.pallas_call(
        _nsa_prep_body,
        grid=(m // tm,),
        in_specs=[col(8, BR_W), col((c0 + 4) // 2, 2 * kvw), col(c0 + 6), col(c0 + 7), col(c0 + 8), col(c0 + 9),
                  tab, tab, tab],
        out_specs=out_specs,
        out_shape=out_shape,
        compiler_params=_cparams("parallel"),
        name="nsa_prep",
    )(z_a, z_a, z_a, z_a, z_a, z_a, *tabs)


def _cmp_finish_body(nch, p_ref, pe_ref, w2_ref, o_ref):
    p = p_ref[...]
    n = p.shape[0]
    nxt = pltpu.roll(p[:, NS_DH:], n - 1, 0)
    hid = _silu(p[:, :NS_DH] + nxt + pe_ref[0:1, :])
    out = _dot(hid.astype(BF16), w2_ref[...])
    row = lax.broadcasted_iota(jnp.int32, out.shape, 0)
    o_ref[...] = jnp.where(row % nch < nch - 1, out, 0.0)


def _cmp_weights(w1):
    kdim = CMP_STRIDE * NS_DH
    w1b = w1.astype(BF16)
    return jnp.concatenate([w1b[:, :kdim], w1b[:, kdim:]], axis=2)


def _partial_rows_body(x_ref, w_ref, o_ref):
    kv = pl.program_id(0) // NS_KV
    nch = o_ref.shape[0]
    acc = jnp.zeros(o_ref.shape, F32)
    for s in range(CMP_STRIDE):
        acc += _dot(x_ref[pl.ds(s, nch, stride=CMP_STRIDE), :].astype(BF16), w_ref[kv, s])
    o_ref[...] = acc


def compress_partial_rows(z_a, w1):
    t = z_a.shape[0]
    nch = t // CMP_STRIDE
    col0 = (8 * BR_W + BR_W) // NS_DH
    w = _cmp_weights(w1).reshape(2, CMP_STRIDE, NS_DH, 2 * NS_DH)
    out = pl.pallas_call(
        _partial_rows_body,
        grid=(2 * NS_KV,),
        in_specs=[pl.BlockSpec((t, NS_DH), lambda c: (0, col0 + c)), pl.BlockSpec(w.shape, lambda c: (0, 0, 0, 0))],
        out_specs=pl.BlockSpec((None, nch, 2 * NS_DH), lambda c: (c, 0, 0)),
        out_shape=jax.ShapeDtypeStruct((2 * NS_KV, nch, 2 * NS_DH), F32),
        compiler_params=_cparams("parallel"),
        name="cmp_partial_rows",
    )(z_a, w)
    return out.reshape(2, NS_KV * nch, 2 * NS_DH)


def compress_pe_term(pe, w1):
    w1b = w1.astype(BF16)
    pe8 = jnp.broadcast_to(pe.reshape(2, 1, CMP_LEN * NS_DH), (2, 8, CMP_LEN * NS_DH)).astype(BF16)
    return jnp.stack([matmul(pe8[a], w1b[a], name="cmp_pe_term") for a in range(2)])


def compress_finish(p, pe_term, w2, nch):
    groups = p.shape[1] // nch
    gps = _pick(groups, (8, 4, 2, 1))
    out = pl.pallas_call(
        functools.partial(_cmp_finish_body, nch),
        grid=(2, groups // gps),
        in_specs=[pl.BlockSpec((None, gps * nch, 2 * NS_DH), lambda a, g: (a, g, 0)),
                  pl.BlockSpec((None, 8, NS_DH), lambda a, g: (a, 0, 0)),
                  pl.BlockSpec((None, NS_DH, NS_DH), lambda a, g: (a, 0, 0))],
        out_specs=pl.BlockSpec((None, gps * nch, NS_DH), lambda a, g: (a, g, 0)),
        out_shape=jax.ShapeDtypeStruct((2, groups * nch, NS_DH), F32),
        compiler_params=_cparams("parallel", "parallel"),
        name="cmp_finish",
    )(p, pe_term, w2.astype(BF16))
    return out.reshape(2, groups, nch, NS_DH)


def _cmp_to_sel(n_cmp_pad, n_sel_pad, n_cmp, n_sel):
    cs = np.arange(n_cmp_pad)[:, None] * CMP_STRIDE
    ss = np.arange(n_sel_pad)[None, :] * SEL_BLOCK
    ov = np.minimum(cs + CMP_LEN, ss + SEL_BLOCK) - np.maximum(cs, ss)
    m = np.clip(ov, 0, None).astype(np.float32) / CMP_LEN
    m[n_cmp:, :] = 0.0
    m[:, n_sel:] = 0.0
    return m


def _stack_heads(q):
    return jnp.concatenate([q[:, r * NS_DH:(r + 1) * NS_DH] for r in range(NS_G)], axis=0)


def _tile_heads(a):
    return jnp.concatenate([a] * NS_G, axis=1)


def _store_heads_t(o_ref, o_t):
    tq = o_t.shape[1] // NS_G
    for r in range(NS_G):
        o_ref[:, r * NS_DH:(r + 1) * NS_DH] = o_t[:, r * tq:(r + 1) * tq].T


def _cmp_sel_body(n_cmp, q_ref, kc_ref, vct_ref, msel_ref, o_ref, bias_ref):
    gw = NS_G * NS_DH
    for g in range(NS_KV):
        _cmp_sel_group(n_cmp, q_ref.at[:, g * gw:(g + 1) * gw], kc_ref.at[g], vct_ref.at[g], msel_ref,
                       o_ref.at[:, g * gw:(g + 1) * gw], bias_ref.at[g])


def _cmp_sel_group(n_cmp, q_ref, kc_ref, vct_ref, msel_ref, o_ref, bias_ref):
    tq = q_ref.shape[0]
    t0 = pl.program_id(0) * tq
    qs = _stack_heads(q_ref[...])
    s_t = _dot_nt(kc_ref[...], qs)
    ncp = s_t.shape[0]
    n_i = lax.broadcasted_iota(jnp.int32, (ncp, tq), 0)
    qpos = t0 + lax.broadcasted_iota(jnp.int32, (ncp, tq), 1)
    valid = _tile_heads((n_i * CMP_STRIDE + CMP_LEN - 1 <= qpos) & (n_i < n_cmp))
    sm = jnp.where(valid, s_t, NEG_INF)
    mx = jnp.max(sm, axis=0, keepdims=True)
    e = jnp.exp2(sm - mx)
    den = jnp.sum(e, axis=0, keepdims=True)
    p = e * jnp.where(mx > 0.5 * NEG_INF, 1.0 / den, 0.0)
    _store_heads_t(o_ref, _dot(vct_ref[...], p.astype(BF16)))

    psum = p[:, 0:tq]
    for r in range(1, NS_G):
        psum += p[:, r * tq:(r + 1) * tq]
    hi, mid, lo = _split3(psum)
    i3 = _dot(msel_ref[...], jnp.concatenate([hi, mid, lo], axis=1))
    imp = i3[:, :tq] + i3[:, tq:2 * tq] + i3[:, 2 * tq:]
    nb = imp.shape[0]
    j_i = lax.broadcasted_iota(jnp.int32, (nb, tq), 0)
    cur = (t0 + lax.broadcasted_iota(jnp.int32, (nb, tq), 1)) // SEL_BLOCK
    forced = (j_i == 0) | (j_i == cur) | (j_i == cur - 1)
    imp = jnp.where(forced, imp + FORCE_BONUS, imp)
    imp = jnp.where(j_i <= cur, imp, NEG_INF)
    j_f = j_i.astype(F32)
    sel = jnp.zeros((nb, tq), F32)
    for _ in range(min(SEL_TOPN, nb)):
        mx = jnp.max(imp, axis=0, keepdims=True)
        first = jnp.min(jnp.where(imp == mx, j_f, float(nb)), axis=0, keepdims=True)
        hit = j_f == first
        sel = jnp.where(hit, 1.0, sel)
        imp = jnp.where(hit, REMOVED, imp)
    bias_t = jnp.where((sel > 0.5) & (j_i <= cur), 0.0, NEG_INF)
    if nb < LANE:
        bias_t = jnp.concatenate([bias_t, jnp.full((LANE - nb, tq), NEG_INF, F32)], axis=0)
    bias_ref[...] = bias_t.T.astype(bias_ref.dtype)


def nsa_cmp_select(q_c, kcmp, vcmp_t, n_cmp):
    t = q_c.shape[0]
    ncp = kcmp.shape[1]
    n_sel = t // SEL_BLOCK
    assert n_sel <= LANE
    msel_t = jnp.asarray(_cmp_to_sel(ncp, n_sel, n_cmp, n_sel).T, BF16)
    return pl.pallas_call(
        functools.partial(_cmp_sel_body, n_cmp),
        grid=(t // Q_TILE,),
        in_specs=[pl.BlockSpec((Q_TILE, BR_W), lambda i: (i, 0)),
                  pl.BlockSpec((NS_KV, ncp, NS_DH), lambda i: (0, 0, 0)),
                  pl.BlockSpec((NS_KV, NS_DH, ncp), lambda i: (0, 0, 0)),
                  pl.BlockSpec((n_sel, ncp), lambda i: (0, 0))],
        out_specs=[pl.BlockSpec((Q_TILE, BR_W), lambda i: (i, 0)),
                   pl.BlockSpec((NS_KV, Q_TILE, LANE), lambda i: (0, i, 0))],
        out_shape=[jax.ShapeDtypeStruct((t, BR_W), F32), jax.ShapeDtypeStruct((NS_KV, t, LANE), BF16)],
        compiler_params=_cparams("parallel"),
        name="nsa_cmp_select",
    )(q_c, kcmp, vcmp_t, msel_t)


V_EXT_ROWS = NS_DH + 16


def _sel_attn_body(q_ref, bias_ref, k_ref, vt_ref, o_ref, m_scr, acc_scr, s_scr):
    tq = q_ref.shape[0]
    t0 = pl.program_id(0) * tq
    gw = NS_G * NS_DH
    qx = [jnp.concatenate([_stack_heads(q_ref[:, g * gw:(g + 1) * gw]),
                           jnp.concatenate([bias_ref[g]] * NS_G, axis=0)], axis=1) for g in range(NS_KV)]
    m_scr[...] = jnp.full_like(m_scr, NEG_INF)
    acc_scr[...] = jnp.zeros_like(acc_scr)
    n_tiles = (t0 + tq + K_TILE - 1) // K_TILE

    def scores(kt, slot):
        k0 = pl.multiple_of(kt * K_TILE, K_TILE)
        for g in range(NS_KV):
            s_scr[slot, g] = _dot_nt(k_ref[g, pl.ds(k0, K_TILE), :], qx[g])

    def consume(kt, slot, causal):
        k0 = pl.multiple_of(kt * K_TILE, K_TILE)
        for g in range(NS_KV):
            s_m = s_scr[slot, g]
            if causal:
                key = k0 + lax.broadcasted_iota(jnp.int32, (K_TILE, tq), 0)
                qpos = t0 + lax.broadcasted_iota(jnp.int32, (K_TILE, tq), 1)
                s_m = s_m + _tile_heads(jnp.where(key <= qpos, 0.0, NEG_INF))
            m_old = m_scr[g]
            m_new = jnp.maximum(m_old, jnp.max(s_m, axis=0, keepdims=True))
            alpha = jnp.exp2(m_old - m_new)
            p = jnp.exp2(s_m - m_new)
            acc_scr[g] = alpha * acc_scr[g] + _dot(vt_ref[g, :, pl.ds(k0, K_TILE)], p.astype(BF16))
            m_scr[g] = m_new

    scores(0, 0)
    n_before = n_tiles - 1

    def pair(i, carry):
        kt = 2 * i
        scores(kt + 1, 1)
        consume(kt, 0, False)
        scores(kt + 2, 0)
        consume(kt + 1, 1, False)
        return carry

    lax.fori_loop(0, n_before // 2, pair, 0)
    odd = n_before % 2

    @pl.when(odd == 1)
    def _():
        scores(n_tiles - 1, 1)
        consume(n_tiles - 2, 0, False)

    consume(n_tiles - 1, odd, True)
    for g in range(NS_KV):
        acc = acc_scr[g]
        _store_heads_t(o_ref.at[:, g * gw:(g + 1) * gw], acc[0:NS_DH] * (1.0 / acc[NS_DH:NS_DH + 1]))


def nsa_sel_attn(q_r, bias, ks_ext, vs_ext):
    t = q_r.shape[0]
    gw = NS_G * NS_DH
    return pl.pallas_call(
        _sel_attn_body,
        grid=(t // Q_TILE,),
        in_specs=[pl.BlockSpec((Q_TILE, NS_KV * gw), lambda i: (i, 0)),
                  pl.BlockSpec((NS_KV, Q_TILE, LANE), lambda i: (0, i, 0)),
                  pl.BlockSpec((NS_KV, t, NS_DH + LANE), lambda i: (0, 0, 0)),
                  pl.BlockSpec((NS_KV, V_EXT_ROWS, t), lambda i: (0, 0, 0))],
        out_specs=pl.BlockSpec((Q_TILE, NS_KV * gw), lambda i: (i, 0)),
        out_shape=jax.ShapeDtypeStruct((t, BR_W), F32),
        scratch_shapes=[pltpu.VMEM((NS_KV, 1, NS_G * Q_TILE), F32),
                        pltpu.VMEM((NS_KV, V_EXT_ROWS, NS_G * Q_TILE), F32),
                        pltpu.VMEM((2, NS_KV, K_TILE, NS_G * Q_TILE), F32)],
        compiler_params=_cparams("parallel"),
        name="nsa_sel_attn",
    )(q_r, bias, ks_ext, vs_ext)


def _win_attn_body(span, q_ref, k_ref, vt_ref, o_ref):
    tq = q_ref.shape[0]
    t0 = pl.program_id(0) * tq
    gw = NS_G * NS_DH
    start = pl.multiple_of(jnp.maximum(t0 - WINDOW, 0), Q_TILE)
    key = start + lax.broadcasted_iota(jnp.int32, (span, tq), 0)
    qpos = t0 + lax.broadcasted_iota(jnp.int32, (span, tq), 1)
    bias = _tile_heads(jnp.where((key <= qpos) & (key > qpos - WINDOW), 0.0, NEG_INF))
    for g in range(NS_KV):
        qs = _stack_heads(q_ref[:, g * gw:(g + 1) * gw])
        sm = _dot_nt(k_ref[g, pl.ds(start, span), :], qs) + bias
        mx = jnp.max(sm, axis=0, keepdims=True)
        e = jnp.exp2(sm - mx)
        den = jnp.sum(e, axis=0, keepdims=True)
        p = e * (1.0 / den)
        _store_heads_t(o_ref.at[:, g * gw:(g + 1) * gw], _dot(vt_ref[g, :, pl.ds(start, span)], p.astype(BF16)))


def nsa_win_attn(q_r, kw, vw_t):
    t = q_r.shape[0]
    span = min(WINDOW + Q_TILE, t)
    return pl.pallas_call(
        functools.partial(_win_attn_body, span),
        grid=(t // Q_TILE,),
        in_specs=[pl.BlockSpec((Q_TILE, BR_W), lambda i: (i, 0)),
                  pl.BlockSpec((NS_KV, t, NS_DH), lambda i: (0, 0, 0)),
                  pl.BlockSpec((NS_KV, NS_DH, t), lambda i: (0, 0, 0))],
        out_specs=pl.BlockSpec((Q_TILE, BR_W), lambda i: (i, 0)),
        out_shape=jax.ShapeDtypeStruct((t, BR_W), F32),
        compiler_params=_cparams("parallel"),
        name="nsa_win_attn",
    )(q_r, kw, vw_t)


def _gate_expand():
    e = np.zeros((3, LANE, BR_W), np.float32)
    for h in range(NS_H):
        for c in range(3):
            e[c, h * 3 + c, h * NS_DH:(h + 1) * NS_DH] = 1.0
    return jnp.asarray(e, BF16)


def _combine_body(oc_ref, os_ref, ow_ref, gate_ref, g_ref, e_ref, o_ref):
    gs = jax.nn.sigmoid(gate_ref[...])
    hi = gs.astype(BF16)
    lo = (gs - hi.astype(F32)).astype(BF16)

    def expand(c):
        return _dot(hi, e_ref[c]) + _dot(lo, e_ref[c])

    o = expand(0) * oc_ref[...] + expand(1) * os_ref[...] + expand(2) * ow_ref[...]
    o_ref[...] = (o * _silu(g_ref[...])).astype(o_ref.dtype)


def nsa_combine(o_cmp, o_sel, o_win, z_g, z_b):
    m = o_cmp.shape[0]
    tm = _pick(m, (256, 128, 64, 32, 16))
    row = pl.BlockSpec((tm, BR_W), lambda i: (i, 0))
    return pl.pallas_call(
        _combine_body,
        grid=(m // tm,),
        in_specs=[row, row, row, pl.BlockSpec((tm, LANE), lambda i: (i, 0)), row,
                  pl.BlockSpec((3, LANE, BR_W), lambda i: (0, 0, 0))],
        out_specs=row,
        out_shape=jax.ShapeDtypeStruct((m, BR_W), BF16),
        compiler_params=_cparams("parallel"),
        name="nsa_combine",
    )(o_cmp, o_sel, o_win, z_g, z_b, _gate_expand())


def _mem_attn_body(q_ref, g_ref, kv_ref, o_ref):
    scale = MEM_DH ** -0.5
    q = q_ref[...]
    for h in range(MEM_H):
        k = kv_ref[:, h * MEM_DH:(h + 1) * MEM_DH].astype(BF16)
        v = kv_ref[:, (MEM_H + h) * MEM_DH:(MEM_H + h + 1) * MEM_DH].astype(BF16)
        s = _dot_nt(q[:, h * MEM_DH:(h + 1) * MEM_DH].astype(BF16), k) * scale
        e = jnp.exp(s - jnp.max(s, axis=-1, keepdims=True))
        p = e * (1.0 / jnp.sum(e, axis=-1, keepdims=True))
        o = _dot(p.astype(BF16), v)
        sl = slice(h * MEM_DH, (h + 1) * MEM_DH)
        o_ref[:, sl] = (o * _silu(g_ref[:, sl])).astype(o_ref.dtype)


def mem_attn_prompt(z_b, mkv):
    t = z_b.shape[0]
    tq = _pick(t, (512, 256, 128, 64, 32, 16))
    return pl.pallas_call(
        _mem_attn_body,
        grid=(t // tq,),
        in_specs=[pl.BlockSpec((tq, BR_W), lambda i: (i, 1)), pl.BlockSpec((tq, BR_W), lambda i: (i, 2)),
                  pl.BlockSpec(mkv.shape, lambda i: (0, 0))],
        out_specs=pl.BlockSpec((tq, BR_W), lambda i: (i, 0)),
        out_shape=jax.ShapeDtypeStruct((t, BR_W), BF16),
        compiler_params=_cparams("parallel"),
        name="mem_attn_prompt",
    )(z_b, z_b, mkv)


def _mem_attn_step_body(q_ref, g_ref, kv_ref, o_ref):
    scale = MEM_DH ** -0.5
    q = q_ref[...]
    rows_per = MEM_DH // LANE
    out_rows = []
    for h in range(MEM_H):
        qh = jnp.concatenate([q[h * rows_per + i:h * rows_per + i + 1, :] for i in range(rows_per)], axis=1)
        qh = jnp.broadcast_to(qh, (8, MEM_DH)).astype(BF16)
        k = kv_ref[:, h * MEM_DH:(h + 1) * MEM_DH].astype(BF16)
        v = kv_ref[:, (MEM_H + h) * MEM_DH:(MEM_H + h + 1) * MEM_DH].astype(BF16)
        s = _dot_nt(qh, k) * scale
        e = jnp.exp(s - jnp.max(s, axis=-1, keepdims=True))
        p = e * (1.0 / jnp.sum(e, axis=-1, keepdims=True))
        o = _dot(p.astype(BF16), v)[0:1, :]
        out_rows += [o[:, i * LANE:(i + 1) * LANE] for i in range(rows_per)]
    o_ref[...] = jnp.concatenate(out_rows, axis=0) * _silu(g_ref[...])


def mem_attn_sample(z_b3, kv_cache, layer):
    bs = z_b3.shape[0]
    mem_len, width = kv_cache.shape[2], kv_cache.shape[3]
    return pl.pallas_call(
        _mem_attn_step_body,
        grid=(bs,),
        in_specs=[pl.BlockSpec((None, 8, LANE), lambda b: (b, 1, 0)), pl.BlockSpec((None, 8, LANE), lambda b: (b, 2, 0)),
                  pl.BlockSpec((None, None, mem_len, width), lambda b: (layer, b, 0, 0))],
        out_specs=pl.BlockSpec((None, 8, LANE), lambda b: (b, 0, 0)),
        out_shape=jax.ShapeDtypeStruct((bs, 8, LANE), F32),
        compiler_params=_cparams("parallel"),
        name="mem_attn_sample",
    )(z_b3, z_b3, kv_cache)


def _paged_partial_body(npg, pt_ref, *refs):
    del pt_ref
    page_refs, w_ref, o_ref = refs[:npg], refs[npg], refs[npg + 1]
    chunks = PAGE_SIZE // CMP_STRIDE
    o_ref[...] = jnp.zeros_like(o_ref)

    unroll = 8

    def step(i, carry):
        for c in range(2 * NS_KV):
            acc = o_ref[c]
            for u in range(unroll):
                s = i * unroll + u
                xs = jnp.concatenate(
                    [r[pl.ds(s * ROWS_PER_TOKEN + c, chunks, stride=CMP_STRIDE * ROWS_PER_TOKEN), :]
                     for r in page_refs], axis=0)
                acc += _dot(xs.astype(BF16), w_ref[c // NS_KV, s])
            o_ref[c] = acc
        return carry

    lax.fori_loop(0, CMP_STRIDE // unroll, step, 0)


def compress_partial_paged(cache2, page_table, w1, layer, n_pool):
    bs, n_pages = page_table.shape
    npg = min(PAGES_PER_STEP, n_pages)
    assert n_pages % npg == 0
    chunks = PAGE_SIZE // CMP_STRIDE
    w = _cmp_weights(w1).reshape(2, CMP_STRIDE, NS_DH, 2 * NS_DH)

    def page_spec(i):
        return pl.BlockSpec((PAGE_SIZE * ROWS_PER_TOKEN, NS_DH),
                            lambda b, j, pt: (layer * n_pool + pt[b, j * npg + i], 0))

    grid_spec = pltpu.PrefetchScalarGridSpec(
        num_scalar_prefetch=1,
        grid=(bs, n_pages // npg),
        in_specs=[page_spec(i) for i in range(npg)] + [pl.BlockSpec(w.shape, lambda b, j, pt: (0, 0, 0, 0))],
        out_specs=pl.BlockSpec((2 * NS_KV, None, npg * chunks, 2 * NS_DH), lambda b, j, pt: (0, b, j, 0)),
    )
    out = pl.pallas_call(
        functools.partial(_paged_partial_body, npg),
        grid_spec=grid_spec,
        out_shape=jax.ShapeDtypeStruct((2 * NS_KV, bs, n_pages * chunks, 2 * NS_DH), F32),
        compiler_params=_cparams("parallel", "parallel"),
        name="cmp_partial_paged",
    )(page_table, *([cache2] * npg), w)
    return out.reshape(2, NS_KV * bs * n_pages * chunks, 2 * NS_DH)


def _row_group(shape):
    return lax.broadcasted_iota(jnp.int32, shape, 0) // NS_G


def _cmp_step_body(n_cmp, qpos, q_ref, kc_ref, vc_ref, o_ref, ps_ref):
    q = q_ref[...].astype(BF16)
    ncp = kc_ref.shape[1]
    s = [_dot_nt(q, kc_ref[g].astype(BF16)) for g in range(NS_KV)]
    grp = _row_group((NS_H, ncp))
    s = jnp.where(grp == 0, s[0], s[1])
    n_i = lax.broadcasted_iota(jnp.int32, (NS_H, ncp), 1)
    valid = (n_i * CMP_STRIDE + CMP_LEN - 1 <= qpos) & (n_i < n_cmp)
    sm = jnp.where(valid, s, NEG_INF)
    mx = jnp.max(sm, axis=-1, keepdims=True)
    e = jnp.where(valid, jnp.exp2(sm - mx), 0.0)
    den = jnp.sum(e, axis=-1, keepdims=True)
    p = e * (1.0 / jnp.where(den > 0.0, den, 1.0))
    pb = p.astype(BF16)
    o = [_dot(pb, vc_ref[g].astype(BF16)) for g in range(NS_KV)]
    o_ref[...] = jnp.where(_row_group((NS_H, NS_DH)) == 0, o[0], o[1])
    ps_ref[...] = jnp.concatenate(
        [jnp.sum(jnp.where(grp == g, p, 0.0), axis=0, keepdims=True) for g in range(NS_KV)], axis=0)


def nsa_cmp_sample(q_c3, kcmp, vcmp, n_cmp, qpos):
    bs = q_c3.shape[0]
    ncp = kcmp.shape[2]
    kv = pl.BlockSpec((NS_KV, None, ncp, NS_DH), lambda b: (0, b, 0, 0))
    return pl.pallas_call(
        functools.partial(_cmp_step_body, n_cmp, qpos),
        grid=(bs,),
        in_specs=[pl.BlockSpec((None, NS_H, NS_DH), lambda b: (b, 0, 0)), kv, kv],
        out_specs=[pl.BlockSpec((None, NS_H, NS_DH), lambda b: (b, 0, 0)),
                   pl.BlockSpec((None, NS_KV, ncp), lambda b: (b, 0, 0))],
        out_shape=[jax.ShapeDtypeStruct((bs, NS_H, NS_DH), F32), jax.ShapeDtypeStruct((bs, NS_KV, ncp), F32)],
        compiler_params=_cparams("parallel"),
        name="nsa_cmp_sample",
    )(q_c3, kcmp, vcmp)


def _select_body(n_sel, qpos, ps_ref, msel_ref, idx_ref):
    hi, mid, lo = _split3(ps_ref[...])
    imp = _dot(hi, msel_ref[...]) + _dot(mid, msel_ref[...]) + _dot(lo, msel_ref[...])
    j_i = lax.broadcasted_iota(jnp.int32, imp.shape, 1)
    cur = qpos // SEL_BLOCK
    forced = (j_i == 0) | (j_i == cur) | (j_i == cur - 1)
    imp = jnp.where(forced, imp + FORCE_BONUS, imp)
    imp = jnp.where(j_i <= cur, imp, NEG_INF)
    imp = jnp.where(j_i < n_sel, imp, REMOVED)
    j_f = j_i.astype(F32)
    col = lax.broadcasted_iota(jnp.int32, idx_ref.shape, 1)
    out = jnp.zeros(idx_ref.shape, F32)
    for it in range(min(SEL_TOPN, n_sel)):
        mx = jnp.max(imp, axis=-1, keepdims=True)
        first = jnp.min(jnp.where(imp == mx, j_f, float(imp.shape[1])), axis=-1, keepdims=True)
        out = jnp.where(col == it, first, out)
        imp = jnp.where(j_f == first, REMOVED, imp)
    idx_ref[...] = out.astype(jnp.int32)


def nsa_select_sample(psum2, n_cmp, n_sel, qpos):
    rows, ncp = psum2.shape
    nsp = -(-n_sel // LANE) * LANE
    msel = jnp.asarray(_cmp_to_sel(ncp, nsp, n_cmp, n_sel), BF16)
    return pl.pallas_call(
        functools.partial(_select_body, n_sel, qpos),
        grid=(1,),
        in_specs=[pl.BlockSpec((rows, ncp), lambda i: (0, 0)), pl.BlockSpec((ncp, nsp), lambda i: (0, 0))],
        out_specs=pl.BlockSpec((rows, LANE), lambda i: (0, 0)),
        out_shape=jax.ShapeDtypeStruct((rows, LANE), jnp.int32),
        compiler_params=_cparams("arbitrary"),
        name="nsa_select_sample",
    )(psum2, msel)


def _sel_step_body(n_past_blocks, qpos, pt_ref, top_ref, q_ref, *refs):
    del pt_ref
    nblk = NS_KV * SEL_PER_STEP
    blk_refs, new_ref, o_ref, m_scr, l_scr, acc_scr = refs[:nblk], *refs[nblk:nblk + 5]
    b = pl.program_id(0)
    j = pl.program_id(1)

    @pl.when(j == 0)
    def _():
        m_scr[...] = jnp.full_like(m_scr, NEG_INF)
        l_scr[...] = jnp.zeros_like(l_scr)
        acc_scr[...] = jnp.zeros_like(acc_scr)

    def slab(ref, slot, g):
        return ref[pl.ds(slot * NS_KV + g, SEL_BLOCK, stride=ROWS_PER_TOKEN), :].astype(BF16)

    q = q_ref[...]
    qb = q.astype(BF16)
    nk = SEL_PER_STEP * SEL_BLOCK
    grp_k = _row_group((NS_H, nk))
    grp_d = _row_group((NS_H, NS_DH))
    lane_blk = lax.broadcasted_iota(jnp.int32, (NS_H, nk), 1) // SEL_BLOCK
    s_g, idx_g, new_g = [], [], []
    for g in range(NS_KV):
        k_all = jnp.concatenate([slab(blk_refs[g * SEL_PER_STEP + k], 2, g) for k in range(SEL_PER_STEP)], axis=0)
        s_g.append(_dot_nt(qb, k_all))
        ids = [top_ref[b, g * SEL_TOPN + j * SEL_PER_STEP + k] for k in range(SEL_PER_STEP)]
        idx = jnp.zeros((NS_H, nk), jnp.int32)
        n_new = jnp.int32(0)
        for k, i in enumerate(ids):
            idx = jnp.where(lane_blk == k, i, idx)
            n_new = n_new + (i == n_past_blocks).astype(jnp.int32)
        idx_g.append(idx)
        new_g.append(n_new)
    s = jnp.where(grp_k == 0, s_g[0], s_g[1])
    idx = jnp.where(grp_k == 0, idx_g[0], idx_g[1])
    tok = idx * SEL_BLOCK + lax.broadcasted_iota(jnp.int32, (NS_H, nk), 1) % SEL_BLOCK
    valid = (idx < n_past_blocks) & (tok <= qpos)
    ks_new = jnp.where(grp_d == 0, new_ref[4:5, :], new_ref[5:6, :])
    vs_new = jnp.where(grp_d == 0, new_ref[6:7, :], new_ref[7:8, :])
    is_new = jnp.where(grp_d == 0, new_g[0], new_g[1]) > 0
    s_new = jnp.where(is_new, jnp.sum(q * ks_new, axis=-1, keepdims=True), NEG_INF)
    m_old = m_scr[...]
    m_new = jnp.maximum(jnp.maximum(m_old, jnp.max(jnp.where(valid, s, NEG_INF), axis=-1, keepdims=True)), s_new)
    alpha = jnp.exp2(m_old - m_new)
    p = jnp.where(valid, jnp.exp2(s - m_new[:, 0:1]), 0.0)
    p_new = jnp.where(is_new, jnp.exp2(s_new - m_new), 0.0)
    pb = p.astype(BF16)
    pv_g = []
    for g in range(NS_KV):
        v_all = jnp.concatenate([slab(blk_refs[g * SEL_PER_STEP + k], 3, g) for k in range(SEL_PER_STEP)], axis=0)
        pv_g.append(_dot(pb, v_all))
    pv = jnp.where(grp_d == 0, pv_g[0], pv_g[1])
    l_scr[...] = alpha * l_scr[...] + jnp.sum(p, axis=-1, keepdims=True) + p_new
    acc_scr[...] = alpha * acc_scr[...] + pv + p_new * vs_new
    m_scr[...] = m_new

    @pl.when(j == pl.num_programs(1) - 1)
    def _():
        o_ref[...] = acc_scr[...] / l_scr[...]


def nsa_sel_sample(q_r3, rows3, cache2, page_table, top_idx, layer, n_pool, qpos):
    bs, n_pages = page_table.shape
    halves = PAGE_SIZE // SEL_BLOCK
    n_past_blocks = n_pages * halves

    def blk(g, k):
        def imap(b, j, pt, top):
            i = top[b, g * SEL_TOPN + j * SEL_PER_STEP + k]
            page = pt[b, jnp.minimum(i // halves, n_pages - 1)]
            return ((layer * n_pool + page) * halves + i % halves, 0)
        return pl.BlockSpec((SEL_BLOCK * ROWS_PER_TOKEN, NS_DH), imap)

    vec = pl.BlockSpec((None, NS_H, NS_DH), lambda b, j, pt, top: (b, 0, 0))
    grid_spec = pltpu.PrefetchScalarGridSpec(
        num_scalar_prefetch=2,
        grid=(bs, SEL_TOPN // SEL_PER_STEP),
        in_specs=[vec] + [blk(g, k) for g in range(NS_KV) for k in range(SEL_PER_STEP)] + [vec],
        out_specs=vec,
        scratch_shapes=[pltpu.VMEM((NS_H, NS_DH), F32)] * 3,
    )
    return pl.pallas_call(
        functools.partial(_sel_step_body, n_past_blocks, qpos),
        grid_spec=grid_spec,
        out_shape=jax.ShapeDtypeStruct((bs, NS_H, NS_DH), F32),
        compiler_params=_cparams("parallel", "arbitrary"),
        name="nsa_sel_sample",
    )(page_table, top_idx, q_r3, *([cache2] * (NS_KV * SEL_PER_STEP)), rows3)


def _win_step_body(pos0, qpos, q_ref, buf_ref, new_ref, o_ref):
    q = q_ref[...]
    qb = q.astype(BF16)
    rows_per = 2 * NS_KV
    wb = buf_ref.shape[0] // rows_per

    def slab(c):
        return buf_ref[pl.ds(c, wb, stride=rows_per), :].astype(BF16)

    grp_k = _row_group((NS_H, wb))
    grp_d = _row_group((NS_H, NS_DH))
    s = jnp.where(grp_k == 0, _dot_nt(qb, slab(0)), _dot_nt(qb, slab(1)))
    kpos = pos0 + lax.broadcasted_iota(jnp.int32, (NS_H, wb), 1)
    valid = (kpos <= qpos) & (kpos > qpos - WINDOW) & (kpos >= 0)
    kw_new = jnp.where(grp_d == 0, new_ref[0:1, :], new_ref[1:2, :])
    vw_new = jnp.where(grp_d == 0, new_ref[2:3, :], new_ref[3:4, :])
    s_new = jnp.sum(q * kw_new, axis=-1, keepdims=True)
    mx = jnp.maximum(jnp.max(jnp.where(valid, s, NEG_INF), axis=-1, keepdims=True), s_new)
    p = jnp.where(valid, jnp.exp2(s - mx), 0.0)
    p_new = jnp.exp2(s_new - mx)
    den = jnp.sum(p, axis=-1, keepdims=True) + p_new
    pb = p.astype(BF16)
    pv = jnp.where(grp_d == 0, _dot(pb, slab(2)), _dot(pb, slab(3)))
    o_ref[...] = (pv + p_new * vw_new) / den


def nsa_win_sample(q_r3, win_buf2, win3, layer, wb, past_len, qpos):
    bs = q_r3.shape[0]
    return pl.pallas_call(
        functools.partial(_win_step_body, past_len - wb, qpos),
        grid=(bs,),
        in_specs=[pl.BlockSpec((None, NS_H, NS_DH), lambda b: (b, 0, 0)),
                  pl.BlockSpec((wb * 2 * NS_KV, NS_DH), lambda b: (layer * bs + b, 0)),
                  pl.BlockSpec((None, 2 * NS_KV, NS_DH), lambda b: (b, 0, 0))],
        out_specs=pl.BlockSpec((None, NS_H, NS_DH), lambda b: (b, 0, 0)),
        out_shape=jax.ShapeDtypeStruct((bs, NS_H, NS_DH), F32),
        compiler_params=_cparams("parallel"),
        name="nsa_win_sample",
    )(q_r3, win_buf2, win3)


def _project(h, w_in_b, layer, w_g, w_b):
    return (matmul(h, w_in_b, name="in_proj_a", b_layer=layer, n=COL_A), matmul(h, w_g, name="in_proj_gate"),
            matmul(h, w_b, name="in_proj_b"))


def _merge_out(x, branches, z_b, w_up_b, w_out_b, next_g, last):
    d = x.shape[1]
    gated = merge_up(branches, z_b, w_up_b, d)
    return out_proj_norm(gated, w_out_b, x, next_g, F32 if last else BF16, keep_x=not last)


def kernel(x_prompt, x_sample, mem_prompt, cache_nsa_kv, state_win_kv, state_hgrn, state_conv, cache_mem_kv,
           page_table, norm_g, final_norm_g, w_in, hg_lb_logits, hg_norm_g, cv_w, ns_pe, ns_cw1, ns_cw2,
           mem_norm_g, w_mem_kv, w_up, w_out):
    depth = w_in.shape[0]
    b_p, t, d = x_prompt.shape
    bs, ts = x_sample.shape[:2]
    assert b_p == 1 and ts == 1
    n_pages = page_table.shape[1]
    past_len = n_pages * PAGE_SIZE
    wb = state_win_kv.shape[2]
    mem_len = mem_prompt.shape[1]
    kvw = NS_KV * NS_DH

    s_lb = jax.nn.softmax(hg_lb_logits.astype(F32), axis=0)
    lower = jnp.cumsum(s_lb, axis=0) - s_lb[0]

    pos_p = jnp.arange(t, dtype=jnp.int32)
    qpos_s = past_len
    pos_s = jnp.full((bs,), qpos_s, jnp.int32)

    n_pool = cache_nsa_kv.shape[1]
    cache2 = cache_nsa_kv.reshape(depth * n_pool * PAGE_SIZE * ROWS_PER_TOKEN, NS_DH)
    win_buf2 = state_win_kv.reshape(depth * bs * wb * 2 * NS_KV, NS_DH)
    mem_cache4 = cache_mem_kv.reshape(depth, bs, mem_len, 2 * MEM_H * MEM_DH)

    nch_p = t // CMP_STRIDE
    nch_s = (past_len + 1) // CMP_STRIDE
    n_sel_s = -(-(past_len + 1) // SEL_BLOCK)

    xp = x_prompt.reshape(t, d)
    xs = x_sample.reshape(bs, d)
    rows_p, win_p, hg_p, cv_p, mkv_p = [], [], [], [], []
    rows_s, win_s, hg_s, cv_s = [], [], [], []
    hp = rmsnorm_rows(xp, norm_g[0], BF16)
    hs = rmsnorm_rows(xs, norm_g[0], BF16)
    w_in_b = w_in.astype(BF16)
    for l in range(depth):
        last = l == depth - 1
        next_g = final_norm_g if last else norm_g[l + 1]
        w_l = w_in_b[l]
        w_g = jnp.pad(w_l[:, COL_A:COL_B0], ((0, 0), (0, LANE - COL_GATE)))
        w_b = w_l[:, COL_B0:]
        w_up_b = w_up[l].astype(BF16)
        w_out_b = w_out[l].astype(BF16)

        z_a, z_g, z_b = _project(hp, w_in_b, l, w_g, w_b)
        o_hg, st_hg = hgrn_prompt(z_a, lower[l], hg_norm_g[l])
        o_cv, st_cv = conv_prompt(z_a, cv_w[l], jnp.zeros((CV_K - 1, BR_W), F32))
        q_c, q_r, rows, win, ks_ext, kw, vs_ext, vw_t = nsa_prep(z_a, pos_p, BF16, attn_layouts=True)
        pe_term = compress_pe_term(ns_pe[l], ns_cw1[l])
        cmp = compress_finish(compress_partial_rows(z_a, ns_cw1[l]), pe_term, ns_cw2[l], nch_p)
        kcmp = cmp[0].astype(BF16)
        vcmp_t = jnp.swapaxes(cmp[1], 1, 2).astype(BF16)
        o_cmp, sel_bias = nsa_cmp_select(q_c, kcmp, vcmp_t, nch_p - 1)
        o_sel = nsa_sel_attn(q_r, sel_bias, ks_ext, vs_ext)
        o_win = nsa_win_attn(q_r, kw, vw_t)
        o_ns = nsa_combine(o_cmp, o_sel, o_win, z_g, z_b)
        mem_h = rmsnorm_rows(mem_prompt.reshape(mem_len, d), mem_norm_g[l], BF16)
        mkv = matmul(mem_h, w_mem_kv[l].astype(BF16), name="mem_kv")
        o_mm = mem_attn_prompt(z_b, mkv)
        xp, hp = _merge_out(xp, (o_hg, o_cv, o_ns, o_mm), z_b, w_up_b, w_out_b, next_g, last)
        rows_p.append(rows)
        win_p.append(win[t - min(WINDOW, t):])
        hg_p.append(st_hg.reshape(1, HG_H, HG_DK, HG_DV))
        cv_p.append(st_cv.reshape(1, CV_K - 1, BR_W))
        mkv_p.append(mkv.reshape(1, mem_len, 2, MEM_H, MEM_DH))

        z_a, z_g, z_b = _project(hs, w_in_b, l, w_g, w_b)
        z_a3 = z_a.reshape(bs, COL_A // LANE, LANE)
        z_b3 = z_b.reshape(bs, z_b.shape[1] // LANE, LANE)
        o_hg, st_hg = hgrn_sample(z_a3, lower[l], hg_norm_g[l], state_hgrn[l])
        o_cv, st_cv = conv_sample(z_a, cv_w[l], state_conv[l])
        q_c, q_r, rows, win = nsa_prep(z_a, pos_s, F32)
        part = compress_partial_paged(cache2, page_table, ns_cw1[l], l, n_pool)
        cmp = compress_finish(part, pe_term, ns_cw2[l], nch_s)
        cmp = cmp.reshape(2, NS_KV, bs, nch_s, NS_DH)
        q_c3 = q_c.reshape(bs, NS_H, NS_DH)
        q_r3 = q_r.reshape(bs, NS_H, NS_DH)
        o_cmp, psum = nsa_cmp_sample(q_c3, cmp[0], cmp[1], nch_s - 1, qpos_s)
        top = nsa_select_sample(psum.reshape(bs * NS_KV, nch_s), nch_s - 1, n_sel_s, qpos_s)
        top = top[:, :SEL_TOPN].reshape(bs, NS_KV * SEL_TOPN)
        o_sel = nsa_sel_sample(q_r3, rows.reshape(bs, 4 * NS_KV, NS_DH), cache2, page_table, top, l, n_pool, qpos_s)
        o_win = nsa_win_sample(q_r3, win_buf2, win.reshape(bs, 2 * NS_KV, NS_DH), l, wb, past_len, qpos_s)
        o_ns = nsa_combine(o_cmp.reshape(bs, BR_W), o_sel.reshape(bs, BR_W), o_win.reshape(bs, BR_W), z_g, z_b)
        o_mm = mem_attn_sample(z_b3, mem_cache4, l)
        branches = (o_hg.reshape(bs, BR_W).astype(BF16), o_cv.astype(BF16), o_ns, o_mm.reshape(bs, BR_W).astype(BF16))
        xs, hs = _merge_out(xs, branches, z_b, w_up_b, w_out_b, next_g, last)
        rows_s.append(rows)
        win_s.append(win)
        hg_s.append(st_hg)
        cv_s.append(st_cv)

    y_prompt = hp.reshape(1, t, d)
    y_sample = hs.reshape(bs, 1, d)
    wlen = min(WINDOW, t)
    new_rows_p = jnp.stack(rows_p).reshape(depth, 1, t, 4, NS_KV, NS_DH)
    new_win_p = jnp.stack(win_p).reshape(depth, 1, wlen, 2, NS_KV, NS_DH)
    new_rows_s = jnp.stack(rows_s).reshape(depth, bs, 1, 4, NS_KV, NS_DH)
    new_win_s = jnp.concatenate(
        [state_win_kv[:, :, 1:], jnp.stack(win_s).reshape(depth, bs, 1, 2, NS_KV, NS_DH)], axis=2)
    return (y_prompt, y_sample, new_rows_p, new_win_p, jnp.stack(hg_p), jnp.stack(cv_p),
            jnp.stack(mkv_p), new_rows_s, new_win_s, jnp.stack(hg_s), jnp.stack(cv_s))
```
